```python
import jax
import jax.numpy as jnp
from jax import lax
import numpy as np

D_MODEL = 1024
BATCH = 16
SEQ = 2048
DEPTH = 2

CTX_LEN = 256
GRID_W = 64
ROPE_THETA = 10000.0
NORM_EPS = 1e-6

A_HEADS = 8
A_KV_HEADS = 2
A_HEAD_DIM = 64
A_WIDTH = A_HEADS * A_HEAD_DIM
WINDOW = 128
WBLK = 128
B_WIDTH = 512
CONV_W = 3
C_HEADS = 8
C_NOPE = 64
C_ROPE = 32
C_VDIM = 64
C_KV_LORA = 256
C_Q_LORA = 768
C_WIDTH = C_HEADS * C_VDIM
QBLK = 128
M_HEADS = 4
M_QK = 64
M_V = 128
M_WIDTH = M_HEADS * M_V
M_CHUNK = 64

E_GATE = A_WIDTH + B_WIDTH
O_GATE = C_WIDTH + M_WIDTH
E_COLS = (A_KV_HEADS * A_HEAD_DIM, A_KV_HEADS * A_HEAD_DIM,
          A_WIDTH,
          B_WIDTH, B_WIDTH, B_WIDTH,
          E_GATE)
E_CTX_COLS = sum(E_COLS[:2])
E_IN = sum(E_COLS)
O_COLS = (C_KV_LORA, C_ROPE,
          M_HEADS * M_QK, M_WIDTH, 4 * M_HEADS,
          C_Q_LORA,
          M_HEADS * M_QK, M_WIDTH,
          O_GATE)
O_CTX_COLS = sum(O_COLS[:5])
O_IN = sum(O_COLS)
N_EVEN = (DEPTH + 1) // 2
N_ODD = DEPTH // 2

kernel_name = 'hybrid_swa_conv_mla_mlstm_dit'


def split_cols(y, sizes):
    idx = np.cumsum(sizes)[:-1].tolist()
    return jnp.split(y, idx, axis=-1)


def rms_norm(x, w):
    xf = x.astype(jnp.float32)
    y = xf * lax.rsqrt(jnp.mean(xf * xf, axis=-1, keepdims=True) + NORM_EPS)
    return (y * w.astype(jnp.float32)).astype(x.dtype)


def axial_rope(rows, rot_dim):
    row = jnp.repeat(jnp.arange(rows), GRID_W).astype(jnp.float32)
    col = jnp.tile(jnp.arange(GRID_W), rows).astype(jnp.float32)
    n_freq = rot_dim // 4
    inv = ROPE_THETA ** (-jnp.arange(n_freq, dtype=jnp.float32) / n_freq)
    ang = jnp.concatenate([row[:, None] * inv, col[:, None] * inv], axis=-1)
    return jnp.cos(ang), jnp.sin(ang)


def apply_rope(x, cos, sin):
    x1, x2 = jnp.split(x.astype(jnp.float32), 2, axis=-1)
    cs, sn = cos[:, None, :], sin[:, None, :]
    return jnp.concatenate([x1 * cs - x2 * sn, x1 * sn + x2 * cs], axis=-1).astype(x.dtype)


def window_attention_latent(q, k, v, k_c, v_c, sink):
    bsz, n_lat = q.shape[:2]
    n_ctx = k_c.shape[1]
    nb = n_lat // WBLK
    grp = A_HEADS // A_KV_HEADS
    span = WBLK + 2 * WINDOW
    scale = A_HEAD_DIM ** -0.5
    qb = q.reshape(bsz, nb, WBLK, A_KV_HEADS, grp, A_HEAD_DIM)
    start = jnp.arange(nb) * WBLK
    idx = start[:, None] + jnp.arange(span)[None, :]
    pad = ((0, 0), (WINDOW, WINDOW), (0, 0), (0, 0))
    kb = jnp.pad(k, pad)[:, idx]
    vb = jnp.pad(v, pad)[:, idx]
    qpos = start[:, None] + jnp.arange(WBLK)[None, :]
    kpos = (idx - WINDOW)[:, None, :]
    valid = (jnp.abs(kpos - qpos[:, :, None]) <= WINDOW) & (kpos >= 0) & (kpos < n_lat)
    s_win = jnp.einsum('bnqhgd,bnkhd->bnhgqk', qb, kb).astype(jnp.float32) * scale
    s_win = jnp.where(valid[None, :, None, None], s_win, -jnp.inf)
    s_ctx = jnp.einsum('bnqhgd,bkhd->bnhgqk', qb, k_c).astype(jnp.float32) * scale
    s_sink = jnp.broadcast_to(sink.astype(jnp.float32).reshape(A_KV_HEADS, grp, 1, 1), s_ctx.shape[:-1] + (1,))
    p = jax.nn.softmax(jnp.concatenate([s_sink, s_ctx, s_win], axis=-1), axis=-1)
    p_ctx = p[..., 1:1 + n_ctx].astype(v.dtype)
    p_win = p[..., 1 + n_ctx:].astype(v.dtype)
    o = jnp.einsum('bnhgqk,bkhd->bnqhgd', p_ctx, v_c) + jnp.einsum('bnhgqk,bnkhd->bnqhgd', p_win, vb)
    return o.reshape(bsz, n_lat, A_WIDTH)


def context_attention_sink(q_c, k_c, v_c, sink):
    bsz, n_ctx = q_c.shape[:2]
    grp = A_HEADS // A_KV_HEADS
    qg = q_c.reshape(bsz, n_ctx, A_KV_HEADS, grp, A_HEAD_DIM)
    s = jnp.einsum('bqhgd,bkhd->bhgqk', qg, k_c).astype(jnp.float32) * (A_HEAD_DIM ** -0.5)
    s_sink = jnp.broadcast_to(sink.astype(jnp.float32).reshape(A_KV_HEADS, grp, 1, 1), s.shape[:-1] + (1,))
    p = jax.nn.softmax(jnp.concatenate([s_sink, s], axis=-1), axis=-1)[..., 1:].astype(v_c.dtype)
    return jnp.einsum('bhgqk,bkhd->bqhgd', p, v_c).reshape(bsz, n_ctx, A_WIDTH)


def short_conv(u, w):
    n = u.shape[1]
    half = CONV_W // 2
    up = jnp.pad(u, ((0, 0), (half, half), (0, 0)))
    return sum(up[:, j:j + n] * w[j] for j in range(CONV_W))


def even_mixer(h, h_c, w_in, sink, conv_w, w_out, rope, ctx_out):
    bsz, n_lat = h.shape[:2]
    n_ctx = h_c.shape[1]
    ak, av, aq, bb, bc, bx, z = split_cols(h @ w_in, E_COLS)
    if ctx_out:
        parts_c = split_cols(h_c @ w_in, E_COLS)
    else:
        parts_c = split_cols(h_c @ w_in[:, :E_CTX_COLS], E_COLS[:2])
    k_c = parts_c[0].reshape(bsz, n_ctx, A_KV_HEADS, A_HEAD_DIM)
    v_c = parts_c[1].reshape(bsz, n_ctx, A_KV_HEADS, A_HEAD_DIM)
    q = apply_rope(aq.reshape(bsz, n_lat, A_HEADS, A_HEAD_DIM), *rope)
    k = apply_rope(ak.reshape(bsz, n_lat, A_KV_HEADS, A_HEAD_DIM), *rope)
    v = av.reshape(bsz, n_lat, A_KV_HEADS, A_HEAD_DIM)
    a_out = window_attention_latent(q, k, v, k_c, v_c, sink)
    b_out = bb * short_conv(bc * bx, conv_w)
    out = (jnp.concatenate([a_out, b_out], axis=-1) * jax.nn.silu(z)) @ w_out
    if not ctx_out:
        return out, None
    q_c = parts_c[2].reshape(bsz, n_ctx, A_HEADS, A_HEAD_DIM)
    a_c = context_attention_sink(q_c, k_c, v_c, sink)
    b_c = parts_c[3] * short_conv(parts_c[4] * parts_c[5], conv_w)
    out_c = (jnp.concatenate([a_c, b_c], axis=-1) * jax.nn.silu(parts_c[6])) @ w_out
    return out, out_c


def mla_keys_values(ckv, kr, kv_norm_w, w_ukv, rope):
    bsz, n = ckv.shape[:2]
    kv = (rms_norm(ckv, kv_norm_w) @ w_ukv).reshape(bsz, n, C_HEADS, C_NOPE + C_VDIM)
    k_nope, v = jnp.split(kv, [C_NOPE], axis=-1)
    kr = kr.reshape(bsz, n, 1, C_ROPE)
    if rope is not None:
        kr = apply_rope(kr, *rope)
    k = jnp.concatenate([k_nope, jnp.broadcast_to(kr, (bsz, n, C_HEADS, C_ROPE))], axis=-1)
    return k, v


def mla_queries(cq, q_norm_w, w_uq, rope):
    bsz, n = cq.shape[:2]
    q = (rms_norm(cq, q_norm_w) @ w_uq).reshape(bsz, n, C_HEADS, C_NOPE + C_ROPE)
    q_nope, q_rope = jnp.split(q, [C_NOPE], axis=-1)
    if rope is not None:
        q_rope = apply_rope(q_rope, *rope)
    return jnp.concatenate([q_nope, q_rope], axis=-1)


def dense_attention_blocks(q, k, v):
    bsz, n_q, nh, dqk = q.shape
    nb = n_q // QBLK
    scale = dqk ** -0.5
    qb = jnp.moveaxis(q.reshape(bsz, nb, QBLK, nh, dqk), 1, 0)

    def one_block(qi):
        s = jnp.einsum('bqhd,bkhd->bhqk', qi, k).astype(jnp.float32) * scale
        p = jax.nn.softmax(s, axis=-1).astype(v.dtype)
        return jnp.einsum('bhqk,bkhd->bqhd', p, v)

    o = lax.map(one_block, qb)
    return jnp.moveaxis(o, 0, 1).reshape(bsz, n_q, nh * v.shape[-1])


def mlstm_chunkwise(q, k, v, log_i, log_f, state, need_h):
    bsz, nh, n, dk = k.shape
    dv = v.shape[-1]
    nc = n // M_CHUNK
    kc = k.reshape(bsz, nh, nc, M_CHUNK, dk)
    vc = v.reshape(bsz, nh, nc, M_CHUNK, dv)
    ic = log_i.reshape(bsz, nh, nc, M_CHUNK)
    b = jnp.cumsum(log_f.reshape(bsz, nh, nc, M_CHUNK), axis=-1)
    g = b[..., -1]
    a = g[..., None] - b + ic
    m_loc = jnp.max(a, axis=-1)
    w = jnp.exp(a - m_loc[..., None])
    c_loc = jnp.einsum('bhcs,bhcsv,bhcsk->bhcvk', w, vc, kc)
    n_loc = jnp.einsum('bhcs,bhcsk->bhck', w, kc)

    def step(carry, inp):
        c_prev, n_prev, m_prev = carry
        g_c, m_l, c_l, n_l = inp
        m_new = jnp.maximum(g_c + m_prev, m_l)
        a_prev = jnp.exp(g_c + m_prev - m_new)
        a_loc = jnp.exp(m_l - m_new)
        c_new = a_prev[..., None, None] * c_prev + a_loc[..., None, None] * c_l
        n_new = a_prev[..., None] * n_prev + a_loc[..., None] * n_l
        return (c_new, n_new, m_new), (c_prev, n_prev, m_prev)

    xs = tuple(jnp.moveaxis(t, 2, 0) for t in (g, m_loc, c_loc, n_loc))
    final, entering = lax.scan(step, state, xs)
    if not need_h:
        return None, final
    c_in, n_in, m_in = (jnp.moveaxis(t, 0, 2) for t in entering)
    qc = q.reshape(bsz, nh, nc, M_CHUNK, dk)
    causal = jnp.tril(jnp.ones((M_CHUNK, M_CHUNK), dtype=bool))
    dmat = jnp.where(causal, b[..., :, None] - b[..., None, :] + ic[..., None, :], -jnp.inf)
    inter = b + m_in[..., None]
    m_t = jnp.maximum(inter, jnp.max(dmat, axis=-1))
    sc = jnp.einsum('bhctk,bhcsk->bhcts', qc, kc) * jnp.exp(dmat - m_t[..., None])
    w_inter = jnp.exp(inter - m_t)
    num = w_inter[..., None] * jnp.einsum('bhcvk,bhctk->bhctv', c_in, qc) + jnp.einsum('bhcts,bhcsv->bhctv', sc, vc)
    den = w_inter * jnp.einsum('bhck,bhctk->bhct', n_in, qc) + jnp.sum(sc, axis=-1)
    h = num / jnp.maximum(jnp.abs(den), jnp.exp(-m_t))[..., None]
    return h.reshape(bsz, nh, n, dv), final


def mlstm_direction(q, k, v, gate_pre, i_bias, f_bias, state, need_h, reverse):
    heads_first = lambda a: jnp.moveaxis(a, 2, 1).astype(jnp.float32)
    kh = heads_first(k) * (M_QK ** -0.5)
    vh = heads_first(v)
    log_i = heads_first(gate_pre[:, :, 0] + i_bias)
    log_f = jax.nn.log_sigmoid(heads_first(gate_pre[:, :, 1] + f_bias))
    qh = heads_first(q) if q is not None else None
    if reverse:
        kh, vh, log_i, log_f = (jnp.flip(t, 2) for t in (kh, vh, log_i, log_f))
        qh = jnp.flip(qh, 2) if qh is not None else None
    h, final = mlstm_chunkwise(qh, kh, vh, log_i, log_f, state, need_h)
    if h is not None and reverse:
        h = jnp.flip(h, 2)
    return h, final


def mlstm_readout(h_sum, o_pre, head_norm_w):
    bsz, nh, n, dv = h_sum.shape
    hs = rms_norm(jnp.moveaxis(h_sum, 1, 2), head_norm_w.reshape(M_HEADS, M_V)).reshape(bsz, n, M_WIDTH)
    return (jax.nn.sigmoid(o_pre.astype(jnp.float32)) * hs).astype(o_pre.dtype)


def odd_mixer(h, h_c, w_in, q_norm_w, kv_norm_w, w_uq, w_ukv, i_bias, f_bias, head_norm_w, w_out, rope, ctx_out):
    bsz, n_lat = h.shape[:2]
    n_ctx = h_c.shape[1]
    ckv, kr, mk, mv, mg, cq, mq, mo, z = split_cols(h @ w_in, O_COLS)
    if ctx_out:
        parts_c = split_cols(h_c @ w_in, O_COLS)
    else:
        parts_c = split_cols(h_c @ w_in[:, :O_CTX_COLS], O_COLS[:5])
    ckv_c, kr_c, mk_c, mv_c, mg_c = parts_c[:5]
    k_c, v_c = mla_keys_values(ckv_c, kr_c, kv_norm_w, w_ukv, None)
    k_x, v_x = mla_keys_values(ckv, kr, kv_norm_w, w_ukv, rope)
    q_x = mla_queries(cq, q_norm_w, w_uq, rope)
    c_out = dense_attention_blocks(q_x, jnp.concatenate([k_c, k_x], axis=1), jnp.concatenate([v_c, v_x], axis=1))
    heads = lambda a, n, dim: a.reshape(bsz, n, M_HEADS, dim)
    gates = lambda a, n: a.reshape(bsz, n, 2, 2, M_HEADS)
    qm, km, vm, gm = heads(mq, n_lat, M_QK), heads(mk, n_lat, M_QK), heads(mv, n_lat, M_V), gates(mg, n_lat)
    qm_c = heads(parts_c[6], n_ctx, M_QK) if ctx_out else None
    km_c, vm_c, gm_c = heads(mk_c, n_ctx, M_QK), heads(mv_c, n_ctx, M_V), gates(mg_c, n_ctx)
    zero = (jnp.zeros((bsz, M_HEADS, M_V, M_QK), jnp.float32),
            jnp.zeros((bsz, M_HEADS, M_QK), jnp.float32),
            jnp.zeros((bsz, M_HEADS), jnp.float32))
    h_lat, h_ctx = [], []
    for d in range(2):
        hc_d, st_d = mlstm_direction(qm_c, km_c, vm_c, gm_c[:, :, d], i_bias[d], f_bias[d], zero, ctx_out, d == 1)
        hx_d, _ = mlstm_direction(qm, km, vm, gm[:, :, d], i_bias[d], f_bias[d], st_d, True, d == 1)
        h_lat.append(hx_d)
        h_ctx.append(hc_d)
    m_out = mlstm_readout(h_lat[0] + h_lat[1], mo, head_norm_w)
    out = (jnp.concatenate([c_out, m_out], axis=-1) * jax.nn.silu(z)) @ w_out
    if not ctx_out:
        return out, None
    q_c = mla_queries(parts_c[5], q_norm_w, w_uq, None)
    c_out_c = dense_attention_blocks(q_c, k_c, v_c)
    m_out_c = mlstm_readout(h_ctx[0] + h_ctx[1], parts_c[7], head_norm_w)
    out_c = (jnp.concatenate([c_out_c, m_out_c], axis=-1) * jax.nn.silu(parts_c[8])) @ w_out
    return out, out_c


def setup_inputs(seed: int = 0) -> dict:
    key = jax.random.key(seed)
    ks = jax.random.split(key, 24)
    nrm = lambda k, shape, s: jax.random.normal(k, shape, jnp.float32) * s
    d = D_MODEL
    return {
        'x': nrm(ks[0], (BATCH, SEQ, d), 1.0),
        'c': nrm(ks[1], (BATCH, d), 1.0),
        'ctx': nrm(ks[2], (BATCH, CTX_LEN, d), 1.0),
        'c_ctx': nrm(ks[3], (d,), 1.0),
        'mod_w': nrm(ks[4], (DEPTH, d, 3 * d), 0.5 * d ** -0.5),
        'mod_b': nrm(ks[5], (DEPTH, 3 * d), 0.02),
        'pre_norm_w': 1.0 + nrm(ks[6], (DEPTH, d), 0.05),
        'post_norm_w': 1.0 + nrm(ks[7], (DEPTH, d), 0.05),
        'e_w_in': nrm(ks[8], (N_EVEN, d, E_IN), d ** -0.5),
        'e_sink': nrm(ks[9], (N_EVEN, A_HEADS), 0.5),
        'e_conv_w': nrm(ks[10], (N_EVEN, CONV_W, B_WIDTH), CONV_W ** -0.5),
        'e_w_out': nrm(ks[11], (N_EVEN, E_GATE, d), E_GATE ** -0.5),
        'o_w_in': nrm(ks[12], (N_ODD, d, O_IN), d ** -0.5),
        'o_q_norm_w': 1.0 + nrm(ks[13], (N_ODD, C_Q_LORA), 0.05),
        'o_kv_norm_w': 1.0 + nrm(ks[14], (N_ODD, C_KV_LORA), 0.05),
        'o_w_uq': nrm(ks[15], (N_ODD, C_Q_LORA, C_HEADS * (C_NOPE + C_ROPE)), C_Q_LORA ** -0.5),
        'o_w_ukv': nrm(ks[16], (N_ODD, C_KV_LORA, C_HEADS * (C_NOPE + C_VDIM)), C_KV_LORA ** -0.5),
        'o_i_bias': nrm(ks[17], (N_ODD, 2, M_HEADS), 0.1),
        'o_f_bias': jnp.linspace(3.0, 6.0, M_HEADS, dtype=jnp.float32) + nrm(ks[18], (N_ODD, 2, M_HEADS), 0.1),
        'o_head_norm_w': 1.0 + nrm(ks[19], (N_ODD, M_WIDTH), 0.05),
        'o_w_out': nrm(ks[20], (N_ODD, O_GATE, d), O_GATE ** -0.5),
    }


def reference(x, c, ctx, c_ctx, mod_w, mod_b, pre_norm_w, post_norm_w, e_w_in, e_sink, e_conv_w, e_w_out,
              o_w_in, o_q_norm_w, o_kv_norm_w, o_w_uq, o_w_ukv, o_i_bias, o_f_bias, o_head_norm_w, o_w_out):
    n_lat = x.shape[1]
    ROWS = n_lat // GRID_W
    rope_a = axial_rope(ROWS, A_HEAD_DIM)
    rope_c = axial_rope(ROWS, C_ROPE)
    for layer in range(DEPTH):
        ctx_out = layer < DEPTH - 1
        mod_x = jax.nn.silu(c) @ mod_w[layer] + mod_b[layer]
        mod_c = jax.nn.silu(c_ctx) @ mod_w[layer] + mod_b[layer]
        sh_x, sc_x, g_x = (m[:, None, :] for m in jnp.split(mod_x, 3, axis=-1))
        sh_c, sc_c, g_c = jnp.split(mod_c, 3, axis=-1)
        h = rms_norm(x, pre_norm_w[layer]) * (1.0 + sc_x) + sh_x
        h_c = rms_norm(ctx, pre_norm_w[layer]) * (1.0 + sc_c) + sh_c
        j = layer // 2
        if layer % 2 == 0:
            y, y_c = even_mixer(h, h_c, e_w_in[j], e_sink[j], e_conv_w[j], e_w_out[j], rope_a, ctx_out)
        else:
            y, y_c = odd_mixer(h, h_c, o_w_in[j], o_q_norm_w[j], o_kv_norm_w[j], o_w_uq[j], o_w_ukv[j],
                               o_i_bias[j], o_f_bias[j], o_head_norm_w[j], o_w_out[j], rope_c, ctx_out)
        x = x + g_x * rms_norm(y, post_norm_w[layer])
        if ctx_out:
            ctx = ctx + g_c * rms_norm(y_c, post_norm_w[layer])
    return x
```

```python
import functools

import numpy as np
import jax
import jax.numpy as jnp
from jax import lax
from jax.experimental import pallas as pl
from jax.experimental.pallas import tpu as pltpu

F32 = jnp.float32
BF16 = jnp.bfloat16
HIGHEST = lax.Precision.HIGHEST

LANES = 128
VMEM_LIMIT = 56 * 1024 * 1024

D_MODEL = 1024
GRID_W = 64
ROPE_THETA = 10000.0
NORM_EPS = 1e-6
NEG = -1e30

A_HEADS, A_KV_HEADS, A_HEAD_DIM = 8, 2, 64
A_WIDTH = A_HEADS * A_HEAD_DIM
WINDOW = 128
B_WIDTH = 512
CONV_W = 3
C_HEADS, C_NOPE, C_ROPE, C_VDIM = 8, 64, 32, 64
C_KV_LORA, C_Q_LORA = 256, 768
C_WIDTH = C_HEADS * C_VDIM
M_HEADS, M_QK, M_V = 4, 64, 128
M_WIDTH = M_HEADS * M_V
E_GATE = A_WIDTH + B_WIDTH
O_GATE = C_WIDTH + M_WIDTH
E_COLS = (128, 128, A_WIDTH, B_WIDTH, B_WIDTH, B_WIDTH, E_GATE)
E_IN = sum(E_COLS)
O_COLS = (C_KV_LORA, C_ROPE, M_HEADS * M_QK, M_WIDTH, 4 * M_HEADS, C_Q_LORA, M_HEADS * M_QK, M_WIDTH, O_GATE)

OC_CKV = 0
OC_G2 = 256
OC_MK = 384
OC_MV = 640
OC_CTX_END = 1152
OC_CQ = 1152
OC_MQ = 1920
OC_MO = 2176
OC_Z = 2688
OC_END = 3712
GATE_LANE0 = 32
MLSTM_CHUNK = 256
ATT_BLK = 128


def _nt(a, b):
    return lax.dot_general(a, b, (((1,), (1,)), ((), ())), preferred_element_type=F32)


def _dot(a, b):
    return jnp.dot(a, b, preferred_element_type=F32)


def _silu(z):
    return z * jax.nn.sigmoid(z)


def _rms(x, w):
    ms = jnp.mean(x * x, axis=-1, keepdims=True)
    return x * lax.rsqrt(ms + NORM_EPS) * w


def _rope_slab(x, cos, sin_signed, half, take_up):
    up = pltpu.roll(x, LANES - half, axis=1)
    dn = pltpu.roll(x, half, axis=1)
    return x * cos + jnp.where(take_up, up, dn) * sin_signed


def _cparams(sem):
    return pltpu.CompilerParams(dimension_semantics=sem, vmem_limit_bytes=VMEM_LIMIT)


def _mod_kernel(cc_ref, w_ref, b_ref, o_ref):
    s = _silu(cc_ref[...])
    o_ref[...] = jnp.dot(s, w_ref[...], preferred_element_type=F32, precision=HIGHEST) + b_ref[...]


def _modulation(cc, mod_w, mod_b):
    depth, d, d3 = mod_w.shape
    r = cc.shape[0]
    return pl.pallas_call(
        _mod_kernel,
        grid=(depth, d3 // d),
        in_specs=[
            pl.BlockSpec((r, d), lambda l, j: (0, 0)),
            pl.BlockSpec((None, d, d), lambda l, j: (l, 0, j)),
            pl.BlockSpec((None, 1, d), lambda l, j: (l, 0, j)),
        ],
        out_specs=pl.BlockSpec((None, r, d), lambda l, j: (l, 0, j)),
        out_shape=jax.ShapeDtypeStruct((depth, r, d3), F32),
        compiler_params=_cparams(("arbitrary", "arbitrary")),
        name="modulation",
    )(cc, mod_w, mod_b.reshape(depth, 1, d3))


def _even_in_kernel(x_ref, sh_ref, sc_ref, nw_ref, w_ref, cos_ref, sin_ref,
                    kv_ref, q_ref, bb_ref, u_ref, z_ref, *, rope):
    h = (_rms(x_ref[...], nw_ref[...]) * (1.0 + sc_ref[...]) + sh_ref[...]).astype(BF16)

    def proj(a, b):
        return _dot(h, w_ref[:, a:b])

    if rope:
        cos, sin = cos_ref[...], sin_ref[...]
        lane = lax.broadcasted_iota(jnp.int32, (1, LANES), 1)
        take_up = (lane % A_HEAD_DIM) < (A_HEAD_DIM // 2)
        rot = lambda t: _rope_slab(t, cos, sin, A_HEAD_DIM // 2, take_up)
    else:
        rot = lambda t: t
    kv = proj(0, 256)
    kv_ref[:, 0:128] = rot(kv[:, 0:128]).astype(BF16)
    kv_ref[:, 128:256] = kv[:, 128:256].astype(BF16)
    q = proj(256, 768)
    for j in range(A_WIDTH // LANES):
        sl = slice(j * LANES, (j + 1) * LANES)
        q_ref[:, sl] = (rot(q[:, sl]) * (A_HEAD_DIM ** -0.5)).astype(BF16)
    bb_ref[...] = proj(768, 1280).astype(BF16)
    u_ref[...] = (proj(1280, 1792) * proj(1792, 2304)).astype(BF16)
    z_ref[...] = _silu(proj(2304, 3328)).astype(BF16)


def _even_in(x, sh, sc, nw, w, cos, sin, *, rope, tm):
    bsz, t, d = x.shape
    nb = sh.shape[0]
    mod_map = (lambda b, i: (b, 0, 0)) if nb > 1 else (lambda b, i: (0, 0, 0))
    tok = lambda width: pl.BlockSpec((None, tm, width), lambda b, i: (b, i, 0))
    widths = (256, A_WIDTH, B_WIDTH, B_WIDTH, E_GATE)
    return pl.pallas_call(
        functools.partial(_even_in_kernel, rope=rope),
        grid=(bsz, t // tm),
        in_specs=[
            tok(d),
            pl.BlockSpec((None, 1, d), mod_map),
            pl.BlockSpec((None, 1, d), mod_map),
            pl.BlockSpec((1, d), lambda b, i: (0, 0)),
            pl.BlockSpec((d, E_IN), lambda b, i: (0, 0)),
            pl.BlockSpec((tm, LANES), lambda b, i: (i, 0)),
            pl.BlockSpec((tm, LANES), lambda b, i: (i, 0)),
        ],
        out_specs=[tok(wd) for wd in widths],
        out_shape=[jax.ShapeDtypeStruct((bsz, t, wd), BF16) for wd in widths],
        compiler_params=_cparams(("parallel", "parallel")),
        name="even_in_rope" if rope else "even_in",
    )(x, sh, sc, nw, w, cos, sin)


def _even_attn_kernel(*refs, window, n_lat):
    if window:
        sink_ref, q_ref, kvp_ref, kvm_ref, kvn_ref, kvc_ref, o_ref = refs
    else:
        sink_ref, q_ref, kvc_ref, o_ref = refs
    blk = q_ref.shape[0]
    lane = lax.broadcasted_iota(jnp.int32, (1, LANES), 1)
    lo = lane < A_HEAD_DIM
    kc, vc = kvc_ref[:, 0:128], kvc_ref[:, 128:256]
    if window:
        i = pl.program_id(1)
        kw = jnp.concatenate([kvp_ref[:, 0:128], kvm_ref[:, 0:128], kvn_ref[:, 0:128]], axis=0)
        vw = jnp.concatenate([kvp_ref[:, 128:256], kvm_ref[:, 128:256], kvn_ref[:, 128:256]], axis=0)
        qpos = i * blk + lax.broadcasted_iota(jnp.int32, (blk, 3 * blk), 0)
        kpos = (i - 1) * blk + lax.broadcasted_iota(jnp.int32, (blk, 3 * blk), 1)
        valid = (jnp.abs(kpos - qpos) <= WINDOW) & (kpos >= 0) & (kpos < n_lat)
    for j in range(A_WIDTH // LANES):
        qs = q_ref[:, j * LANES:(j + 1) * LANES]
        halves = []
        for half in range(2):
            head = j + (A_HEADS // 2) * half
            qm = jnp.where(lo if half == 0 else jnp.logical_not(lo), qs, jnp.zeros_like(qs))
            sink = sink_ref[head]
            s_c = _nt(qm, kc)
            m = jnp.maximum(jnp.max(s_c, axis=-1, keepdims=True), sink)
            if window:
                s_w = jnp.where(valid, _nt(qm, kw), NEG)
                m = jnp.maximum(m, jnp.max(s_w, axis=-1, keepdims=True))
            p_c = jnp.exp(s_c - m)
            l = jnp.sum(p_c, axis=-1, keepdims=True) + jnp.exp(sink - m)
            o = _dot(p_c.astype(BF16), vc)
            if window:
                p_w = jnp.exp(s_w - m)
                l = l + jnp.sum(p_w, axis=-1, keepdims=True)
                o = o + _dot(p_w.astype(BF16), vw)
            halves.append(o / l)
        o_ref[:, j * LANES:(j + 1) * LANES] = jnp.where(lo, halves[0], halves[1]).astype(BF16)


def _even_attn(sink, q, kv, kvc, *, window):
    bsz, t, _ = q.shape
    n_ctx = kvc.shape[1]
    blk = ATT_BLK
    nb = t // blk
    smem = pl.BlockSpec(memory_space=pltpu.SMEM)
    qspec = pl.BlockSpec((None, blk, A_WIDTH), lambda b, i: (b, i, 0))
    cspec = pl.BlockSpec((None, n_ctx, 256), lambda b, i: (b, 0, 0))
    if window:
        specs = [smem, qspec,
                 pl.BlockSpec((None, blk, 256), lambda b, i: (b, jnp.maximum(i - 1, 0), 0)),
                 pl.BlockSpec((None, blk, 256), lambda b, i: (b, i, 0)),
                 pl.BlockSpec((None, blk, 256), lambda b, i: (b, jnp.minimum(i + 1, nb - 1), 0)),
                 cspec]
        args = (sink, q, kv, kv, kv, kvc)
    else:
        specs = [smem, qspec, cspec]
        args = (sink, q, kvc)
    return pl.pallas_call(
        functools.partial(_even_attn_kernel, window=window, n_lat=t),
        grid=(bsz, nb),
        in_specs=specs,
        out_specs=qspec,
        out_shape=jax.ShapeDtypeStruct((bsz, t, A_WIDTH), BF16),
        compiler_params=_cparams(("parallel", "parallel")),
        name="even_attn_window" if window else "even_attn_ctx",
    )(*args)


def _out_kernel(*refs, conv):
    if conv:
        a_ref, bb_ref, u_ref, up_ref, un_ref, z_ref, cw_ref, w_ref, pw_ref, g_ref, x_ref, o_ref = refs
        tm = u_ref.shape[0]
        i = pl.program_id(1)
        u = u_ref[...].astype(F32)
        prev_row = jnp.where(i > 0, up_ref[15:16, :].astype(F32), 0.0)
        next_row = jnp.where(i < pl.num_programs(1) - 1, un_ref[0:1, :].astype(F32), 0.0)
        row = lax.broadcasted_iota(jnp.int32, (tm, 1), 0)
        u_m1 = jnp.where(row == 0, prev_row, pltpu.roll(u, 1, axis=0))
        u_p1 = jnp.where(row == tm - 1, next_row, pltpu.roll(u, tm - 1, axis=0))
        cw = cw_ref[...]
        mix_b = bb_ref[...].astype(F32) * (u_m1 * cw[0:1, :] + u * cw[1:2, :] + u_p1 * cw[2:3, :])
    else:
        a_ref, b_ref, z_ref, w_ref, pw_ref, g_ref, x_ref, o_ref = refs
        mix_b = b_ref[...].astype(F32)
    wa = a_ref.shape[1]
    ga = (a_ref[...].astype(F32) * z_ref[:, 0:wa].astype(F32)).astype(BF16)
    gb = (mix_b * z_ref[:, wa:].astype(F32)).astype(BF16)
    y = _dot(ga, w_ref[0:wa, :]) + _dot(gb, w_ref[wa:, :])
    o_ref[...] = x_ref[...] + g_ref[...] * _rms(y, pw_ref[...])


def _out_proj(mix, z, cw, w, pw, g, x, *, tm):
    conv = cw is not None
    bsz, t, d = x.shape
    nb = g.shape[0]
    mod_map = (lambda b, i: (b, 0, 0)) if nb > 1 else (lambda b, i: (0, 0, 0))
    tok = lambda width: pl.BlockSpec((None, tm, width), lambda b, i: (b, i, 0))
    const = lambda shape: pl.BlockSpec(shape, lambda b, i: (0,) * len(shape))
    r16 = tm // 16
    if conv:
        a, bb, u = mix
        specs = [tok(a.shape[2]), tok(bb.shape[2]), tok(u.shape[2]),
                 pl.BlockSpec((None, 16, u.shape[2]), lambda b, i: (b, jnp.maximum(i * r16 - 1, 0), 0)),
                 pl.BlockSpec((None, 16, u.shape[2]), lambda b, i: (b, jnp.minimum((i + 1) * r16, t // 16 - 1), 0)),
                 tok(z.shape[2]), const(cw.shape)]
        args = (a, bb, u, u, u, z, cw)
    else:
        a, b2 = mix
        specs = [tok(a.shape[2]), tok(b2.shape[2]), tok(z.shape[2])]
        args = (a, b2, z)
    specs += [const(w.shape), const((1, d)), pl.BlockSpec((None, 1, d), mod_map), tok(d)]
    return pl.pallas_call(
        functools.partial(_out_kernel, conv=conv),
        grid=(bsz, t // tm),
        in_specs=specs,
        out_specs=tok(d),
        out_shape=jax.ShapeDtypeStruct((bsz, t, d), F32),
        compiler_params=_cparams(("parallel", "parallel")),
        name="out_proj_conv" if conv else "out_proj",
    )(*args, w, pw, g, x)


def _odd_in_kernel(*refs, lat):
    if lat:
        (x_ref, sh_ref, sc_ref, nw_ref, w_ref, kvn_ref, wk_ref, wv_ref, cosk_ref, sink_ref,
         qn_ref, wq_ref, cosq_ref, sinq_ref,
         kf_ref, v_ref, mk_ref, mv_ref, g2_ref, qf_ref, mq_ref, mo_ref, z_ref) = refs
    else:
        (x_ref, sh_ref, sc_ref, nw_ref, w_ref, kvn_ref, wk_ref, wv_ref,
         kf_ref, v_ref, mk_ref, mv_ref, g2_ref) = refs
    h = (_rms(x_ref[...], nw_ref[...]) * (1.0 + sc_ref[...]) + sh_ref[...]).astype(BF16)

    def proj(a, b):
        return _dot(h, w_ref[:, a:b])

    lane = lax.broadcasted_iota(jnp.int32, (1, LANES), 1)
    ckv = _rms(proj(OC_CKV, OC_CKV + C_KV_LORA), kvn_ref[...]).astype(BF16)
    g2 = proj(OC_G2, OC_G2 + LANES)
    g2_ref[...] = g2
    if lat:
        g2 = _rope_slab(g2, cosk_ref[...], sink_ref[...], C_ROPE // 2, lane < C_ROPE // 2)
    kf_ref[...] = _dot(jnp.concatenate([ckv, g2.astype(BF16)], axis=1), wk_ref[...]).astype(BF16)
    v_ref[...] = _dot(ckv, wv_ref[...]).astype(BF16)
    mk_ref[...] = (proj(OC_MK, OC_MK + 256) * (M_QK ** -0.5)).astype(BF16)
    mv_ref[...] = proj(OC_MV, OC_MV + M_WIDTH).astype(BF16)
    if lat:
        cq = _rms(proj(OC_CQ, OC_CQ + C_Q_LORA), qn_ref[...]).astype(BF16)
        qf = _dot(cq, wq_ref[...])
        cos, sin = cosq_ref[...], sinq_ref[...]
        take_up = (lane >= C_NOPE) & (lane < C_NOPE + C_ROPE // 2)
        scale = (C_NOPE + C_ROPE) ** -0.5
        for hd in range(C_HEADS):
            sl = slice(hd * LANES, (hd + 1) * LANES)
            qf_ref[:, sl] = (_rope_slab(qf[:, sl], cos, sin, C_ROPE // 2, take_up) * scale).astype(BF16)
        mq_ref[...] = proj(OC_MQ, OC_MQ + 256).astype(BF16)
        mo_ref[...] = proj(OC_MO, OC_MO + M_WIDTH).astype(BF16)
        z_ref[...] = _silu(proj(OC_Z, OC_Z + O_GATE)).astype(BF16)


def _odd_in(x, sh, sc, nw, w, kvn, wk, wv, lat_args, *, tm):
    lat = lat_args is not None
    bsz, t, d = x.shape
    nb = sh.shape[0]
    mod_map = (lambda b, i: (b, 0, 0)) if nb > 1 else (lambda b, i: (0, 0, 0))
    tok = lambda width: pl.BlockSpec((None, tm, width), lambda b, i: (b, i, 0))
    const = lambda shape: pl.BlockSpec(shape, lambda b, i: (0,) * len(shape))
    tab = pl.BlockSpec((tm, LANES), lambda b, i: (i, 0))
    specs = [tok(d), pl.BlockSpec((None, 1, d), mod_map), pl.BlockSpec((None, 1, d), mod_map),
             const((1, d)), const(w.shape), const(kvn.shape), const(wk.shape), const(wv.shape)]
    args = [x, sh, sc, nw, w, kvn, wk, wv]
    outs = [(C_HEADS * LANES, BF16), (C_WIDTH, BF16), (256, BF16), (M_WIDTH, BF16), (LANES, F32)]
    if lat:
        cosk, sink, qn, wq, cosq, sinq = lat_args
        specs += [tab, tab, const(qn.shape), const(wq.shape), tab, tab]
        args += [cosk, sink, qn, wq, cosq, sinq]
        outs += [(C_HEADS * LANES, BF16), (256, BF16), (M_WIDTH, BF16), (O_GATE, BF16)]
    return pl.pallas_call(
        functools.partial(_odd_in_kernel, lat=lat),
        grid=(bsz, t // tm),
        in_specs=specs,
        out_specs=[tok(wd) for wd, _ in outs],
        out_shape=[jax.ShapeDtypeStruct((bsz, t, wd), dt) for wd, dt in outs],
        compiler_params=_cparams(("parallel", "parallel")),
        name="odd_in_lat" if lat else "odd_in_ctx",
    )(*args)


def _mla_kernel(q_ref, kc_ref, kx_ref, vc_ref, vx_ref, o_ref):
    lane = lax.broadcasted_iota(jnp.int32, (1, LANES), 1)
    lo = lane < C_VDIM
    for j in range(C_HEADS // 2):
        pair = slice(j * LANES, (j + 1) * LANES)
        vcp, vxp = vc_ref[:, pair], vx_ref[:, pair]
        halves = []
        for half in range(2):
            hd = 2 * j + half
            sl = slice(hd * LANES, (hd + 1) * LANES)
            qh = q_ref[:, sl]
            s_c = _nt(qh, kc_ref[:, sl])
            s_x = _nt(qh, kx_ref[:, sl])
            m = jnp.maximum(jnp.max(s_c, axis=-1, keepdims=True), jnp.max(s_x, axis=-1, keepdims=True))
            p_c = jnp.exp(s_c - m)
            p_x = jnp.exp(s_x - m)
            l = jnp.sum(p_c, axis=-1, keepdims=True) + jnp.sum(p_x, axis=-1, keepdims=True)
            o = _dot(p_c.astype(BF16), vcp) + _dot(p_x.astype(BF16), vxp)
            halves.append(o / l)
        o_ref[:, pair] = jnp.where(lo, halves[0], halves[1]).astype(BF16)


def _mla(qf, kfc, kfx, vc, vx):
    bsz, t, wq = qf.shape
    n_ctx = kfc.shape[1]
    blk = ATT_BLK
    return pl.pallas_call(
        _mla_kernel,
        grid=(bsz, t // blk),
        in_specs=[
            pl.BlockSpec((None, blk, wq), lambda b, i: (b, i, 0)),
            pl.BlockSpec((None, n_ctx, wq), lambda b, i: (b, 0, 0)),
            pl.BlockSpec((None, t, wq), lambda b, i: (b, 0, 0)),
            pl.BlockSpec((None, n_ctx, C_WIDTH), lambda b, i: (b, 0, 0)),
            pl.BlockSpec((None, t, C_WIDTH), lambda b, i: (b, 0, 0)),
        ],
        out_specs=pl.BlockSpec((None, blk, C_WIDTH), lambda b, i: (b, i, 0)),
        out_shape=jax.ShapeDtypeStruct((bsz, t, C_WIDTH), BF16),
        compiler_params=_cparams(("parallel", "arbitrary")),
        name="mla_attention",
    )(qf, kfc, kfx, vc, vx)


def _mlstm_kernel(mq_ref, mk_ref, mv_ref, g_ref, mo_ref, mkc_ref, mvc_ref, gc_ref, bias_ref, hnw_ref,
                  out_ref, s_ref, m_ref, hf_ref, hr_ref, *, chunk):
    L = chunk
    n_lat, n_ctx = mq_ref.shape[0], mkc_ref.shape[0]
    nc, ncc = n_lat // L, n_ctx // L
    row = lax.broadcasted_iota(jnp.int32, (L, L), 0)
    col = lax.broadcasted_iota(jnp.int32, (L, L), 1)
    tri = (col <= row, col >= row)
    tri_f32 = tri[0].astype(F32)
    lane = lax.broadcasted_iota(jnp.int32, (1, LANES), 1)
    lo = lane < M_QK
    ones_col = (lax.broadcasted_iota(jnp.int32, (L, LANES), 1) == 0).astype(BF16)
    fwd_lanes = lane < GATE_LANE0 + 2 * M_HEADS

    def gates(g):
        gb = g + bias_ref[...]
        ls = jnp.minimum(gb, 0.0) - jnp.log1p(jnp.exp(-jnp.abs(gb)))
        cf = jnp.dot(tri_f32, ls, preferred_element_type=F32, precision=HIGHEST)
        tot = cf[L - 1:L, :]
        b = jnp.where(fwd_lanes, cf, tot - cf + ls)
        li = pltpu.roll(gb, M_HEADS, axis=1)
        r_t = jnp.transpose(b - li)
        return b, li, tot, r_t

    def unit(d, hd, kpair, vh, gq, qpair, with_out):
        b, li, tot, r_t = gq
        u = d * M_HEADS + hd
        fl = GATE_LANE0 + d * 2 * M_HEADS + M_HEADS + hd
        msk = lo if hd % 2 == 0 else jnp.logical_not(lo)
        bcol, licol, g = b[:, fl:fl + 1], li[:, fl:fl + 1], tot[:, fl:fl + 1]
        kh = jnp.where(msk, kpair, jnp.zeros_like(kpair))
        vaug = jnp.concatenate([vh, ones_col], axis=1)
        s_in = s_ref[u]
        m_in = m_ref[u][0:1, 0:1]
        h_out = None
        if with_out:
            qh = jnp.where(msk, qpair, jnp.zeros_like(qpair))
            inter = bcol + m_in
            dm = jnp.where(tri[d], bcol - r_t[fl:fl + 1, :], NEG)
            m_t = jnp.maximum(inter, jnp.max(dm, axis=-1, keepdims=True))
            sc = _nt(qh, kh) * jnp.exp(dm - m_t)
            res = jnp.exp(inter - m_t) * _dot(qh, s_in.astype(BF16)) + _dot(sc.astype(BF16), vaug)
            den = jnp.maximum(jnp.abs(res[:, LANES:LANES + 1]), jnp.exp(-m_t))
            h_out = res[:, 0:LANES] / den
        acol = g - bcol + licol
        m_loc = jnp.max(acol, axis=0, keepdims=True)
        wv = (jnp.exp(acol - m_loc) * vaug.astype(F32)).astype(BF16)
        s_loc = lax.dot_general(kh, wv, (((0,), (0,)), ((), ())), preferred_element_type=F32)
        m_new = jnp.maximum(g + m_in, m_loc)
        s_ref[u] = jnp.exp(g + m_in - m_new) * s_in + jnp.exp(m_loc - m_new) * s_loc
        m_ref[u] = jnp.broadcast_to(m_new, (8, LANES))
        return h_out

    s_ref[...] = jnp.zeros_like(s_ref)
    m_ref[...] = jnp.zeros_like(m_ref)
    for d in range(2):
        for cc in range(ncc):
            r0 = (cc if d == 0 else ncc - 1 - cc) * L
            gq = gates(gc_ref[r0:r0 + L, :])
            for hd in range(M_HEADS):
                pr = slice((hd // 2) * LANES, (hd // 2 + 1) * LANES)
                unit(d, hd, mkc_ref[r0:r0 + L, pr], mvc_ref[r0:r0 + L, hd * M_V:(hd + 1) * M_V], gq, None, False)

    def body(j, carry):
        for d in range(2):
            r0 = pl.multiple_of((j if d == 0 else nc - 1 - j) * L, L)
            rows = pl.ds(r0, L)
            gq = gates(g_ref[rows, :])
            dst = hf_ref if d == 0 else hr_ref
            for hd in range(M_HEADS):
                pr = slice((hd // 2) * LANES, (hd // 2 + 1) * LANES)
                vs = slice(hd * M_V, (hd + 1) * M_V)
                dst[rows, vs] = unit(d, hd, mk_ref[rows, pr], mv_ref[rows, vs], gq, mq_ref[rows, pr], True)
        return carry

    lax.fori_loop(0, nc, body, 0)

    def readout(j, carry):
        rows = pl.ds(pl.multiple_of(j * L, L), L)
        for hd in range(M_HEADS):
            vs = slice(hd * M_V, (hd + 1) * M_V)
            hs = _rms(hf_ref[rows, vs] + hr_ref[rows, vs], hnw_ref[:, vs])
            out_ref[rows, vs] = (jax.nn.sigmoid(mo_ref[rows, vs].astype(F32)) * hs).astype(BF16)
        return carry

    lax.fori_loop(0, nc, readout, 0)


def _mlstm(mq, mk, mv, g2, mo, mkc, mvc, g2c, bias_row, hnw):
    bsz, t, _ = mq.shape
    n_ctx = mkc.shape[1]
    chunk = min(MLSTM_CHUNK, n_ctx)
    per_b = lambda rows, width: pl.BlockSpec((None, rows, width), lambda b: (b, 0, 0))
    const = lambda shape: pl.BlockSpec(shape, lambda b: (0,) * len(shape))
    return pl.pallas_call(
        functools.partial(_mlstm_kernel, chunk=chunk),
        grid=(bsz,),
        in_specs=[per_b(t, 256), per_b(t, 256), per_b(t, M_WIDTH), per_b(t, LANES), per_b(t, M_WIDTH),
                  per_b(n_ctx, 256), per_b(n_ctx, M_WIDTH), per_b(n_ctx, LANES),
                  const((1, LANES)), const((1, M_WIDTH))],
        out_specs=per_b(t, M_WIDTH),
        out_shape=jax.ShapeDtypeStruct((bsz, t, M_WIDTH), BF16),
        scratch_shapes=[pltpu.VMEM((2 * M_HEADS, LANES, 2 * LANES), F32),
                        pltpu.VMEM((2 * M_HEADS, 8, LANES), F32),
                        pltpu.VMEM((t, M_WIDTH), F32),
                        pltpu.VMEM((t, M_WIDTH), F32)],
        compiler_params=_cparams(("parallel",)),
        name="mlstm",
    )(mq, mk, mv, g2, mo, mkc, mvc, g2c, bias_row, hnw)


def _axial_angles(n_lat, rot_dim):
    rows = n_lat // GRID_W
    row = jnp.repeat(jnp.arange(rows), GRID_W).astype(F32)
    col = jnp.tile(jnp.arange(GRID_W), rows).astype(F32)
    n_freq = rot_dim // 4
    inv = ROPE_THETA ** (-jnp.arange(n_freq, dtype=F32) / n_freq)
    ang = jnp.concatenate([row[:, None] * inv, col[:, None] * inv], axis=-1)
    return jnp.cos(ang), jnp.sin(ang)


def _rope_tables(n_lat):
    cos, sin = _axial_angles(n_lat, A_HEAD_DIM)
    cos_a = jnp.tile(jnp.concatenate([cos, cos], axis=-1), (1, 2))
    sin_a = jnp.tile(jnp.concatenate([-sin, sin], axis=-1), (1, 2))
    cos, sin = _axial_angles(n_lat, C_ROPE)
    one = lambda w: jnp.ones((n_lat, w), F32)
    zero = lambda w: jnp.zeros((n_lat, w), F32)
    cos_k = jnp.concatenate([cos, cos, one(LANES - C_ROPE)], axis=-1)
    sin_k = jnp.concatenate([-sin, sin, zero(LANES - C_ROPE)], axis=-1)
    cos_q = jnp.concatenate([one(C_NOPE), cos, cos, one(LANES - C_NOPE - C_ROPE)], axis=-1)
    sin_q = jnp.concatenate([zero(C_NOPE), -sin, sin, zero(LANES - C_NOPE - C_ROPE)], axis=-1)
    return (cos_a, sin_a), (cos_k, sin_k), (cos_q, sin_q)


_A_HEAD_ORDER = (0, 4, 1, 5, 2, 6, 3, 7)


def _perm_heads(w, axis):
    shape = w.shape
    w = w.reshape(shape[:axis] + (A_HEADS, A_HEAD_DIM) + shape[axis + 1:])
    w = jnp.take(w, jnp.array(_A_HEAD_ORDER), axis=axis)
    return w.reshape(shape)


def _even_weights(w_in, w_out):
    offs = np.cumsum((0,) + E_COLS)
    parts = [w_in[:, offs[i]:offs[i + 1]] for i in range(len(E_COLS))]
    parts[2] = _perm_heads(parts[2], 1)
    z = parts[6]
    parts[6] = jnp.concatenate([_perm_heads(z[:, :A_WIDTH], 1), z[:, A_WIDTH:]], axis=1)
    w_out_p = jnp.concatenate([_perm_heads(w_out[:A_WIDTH], 0), w_out[A_WIDTH:]], axis=0)
    return jnp.concatenate(parts, axis=1).astype(BF16), w_out_p.astype(BF16)


def _odd_weights(w_in, w_uq, w_ukv):
    d = w_in.shape[0]
    offs = np.cumsum((0,) + O_COLS)
    ckv, kr, mk, mv, mg, cq, mq, mo, z = [w_in[:, offs[i]:offs[i + 1]] for i in range(len(O_COLS))]
    g2 = jnp.concatenate([kr, mg, jnp.zeros((d, LANES - C_ROPE - 4 * M_HEADS), w_in.dtype)], axis=1)
    w_p = jnp.concatenate([ckv, g2, mk, mv, cq, mq, mo, z], axis=1).astype(BF16)
    ukv = w_ukv.reshape(C_KV_LORA, C_HEADS, C_NOPE + C_VDIM)
    wk_top = jnp.pad(ukv[:, :, :C_NOPE], ((0, 0), (0, 0), (0, LANES - C_NOPE)))
    eye = jnp.pad(jnp.eye(C_ROPE, dtype=w_in.dtype), ((0, LANES - C_ROPE), (C_NOPE, LANES - C_NOPE - C_ROPE)))
    wk_bot = jnp.broadcast_to(eye[:, None, :], (LANES, C_HEADS, LANES))
    wk = jnp.concatenate([wk_top, wk_bot], axis=0).reshape(C_KV_LORA + LANES, C_HEADS * LANES).astype(BF16)
    wv = ukv[:, :, C_NOPE:].reshape(C_KV_LORA, C_WIDTH).astype(BF16)
    uq = w_uq.reshape(C_Q_LORA, C_HEADS, C_NOPE + C_ROPE)
    wq = jnp.pad(uq, ((0, 0), (0, 0), (0, LANES - C_NOPE - C_ROPE))).reshape(C_Q_LORA, C_HEADS * LANES).astype(BF16)
    return w_p, wk, wv, wq


def kernel(x, c, ctx, c_ctx, mod_w, mod_b, pre_norm_w, post_norm_w, e_w_in, e_sink, e_conv_w, e_w_out,
           o_w_in, o_q_norm_w, o_kv_norm_w, o_w_uq, o_w_ukv, o_i_bias, o_f_bias, o_head_norm_w, o_w_out):
    bsz, n_lat, d = x.shape
    n_ctx = ctx.shape[1]
    assert mod_w.shape[0] == 2 and d == D_MODEL, "built for one even + one odd layer"
    assert n_lat % 512 == 0 and n_ctx % 128 == 0
    tm_lat, tm_ctx = 512, min(256, n_ctx)

    pad = (-(bsz + 1)) % 8
    cc = jnp.concatenate([c, c_ctx[None, :], jnp.zeros((pad, d), F32)], axis=0)
    mod = _modulation(cc, mod_w, mod_b)
    split = lambda l, r0, r1: [mod[l, r0:r1, k * d:(k + 1) * d][:, None, :] for k in range(3)]
    (rope_a, rope_k, rope_q) = _rope_tables(n_lat)
    dummy_tab = jnp.zeros((n_ctx, LANES), F32)

    sh_x, sc_x, g_x = split(0, 0, bsz)
    sh_c, sc_c, g_c = split(0, bsz, bsz + 1)
    w_in, w_out = _even_weights(e_w_in[0], e_w_out[0])
    nw, pw = pre_norm_w[0][None, :], post_norm_w[0][None, :]
    kv, q, bb, u, z = _even_in(x, sh_x, sc_x, nw, w_in, *rope_a, rope=True, tm=tm_lat)
    kv_c, q_c, bb_c, u_c, z_c = _even_in(ctx, sh_c, sc_c, nw, w_in, dummy_tab, dummy_tab, rope=False, tm=tm_ctx)
    a = _even_attn(e_sink[0], q, kv, kv_c, window=True)
    a_c = _even_attn(e_sink[0], q_c, None, kv_c, window=False)
    x1 = _out_proj((a, bb, u), z, e_conv_w[0], w_out, pw, g_x, x, tm=tm_lat)
    ctx1 = _out_proj((a_c, bb_c, u_c), z_c, e_conv_w[0], w_out, pw, g_c, ctx, tm=tm_ctx)

    sh_x, sc_x, g_x = split(1, 0, bsz)
    sh_c, sc_c, _ = split(1, bsz, bsz + 1)
    w_p, wk, wv, wq = _odd_weights(o_w_in[0], o_w_uq[0], o_w_ukv[0])
    nw, pw = pre_norm_w[1][None, :], post_norm_w[1][None, :]
    kvn, qn = o_kv_norm_w[0][None, :], o_q_norm_w[0][None, :]
    kf, v, mk, mv, g2, qf, mq, mo, z = _odd_in(x1, sh_x, sc_x, nw, w_p, kvn, wk, wv,
                                               (*rope_k, qn, wq, *rope_q), tm=tm_lat)
    kf_c, v_c, mk_c, mv_c, g2_c = _odd_in(ctx1, sh_c, sc_c, nw, w_p[:, :OC_CTX_END], kvn, wk, wv, None, tm=tm_ctx)
    c_out = _mla(qf, kf_c, kf, v_c, v)
    gate_bias = jnp.stack([o_i_bias[0], o_f_bias[0]], axis=1).reshape(1, 4 * M_HEADS)
    bias_row = jnp.pad(gate_bias, ((0, 0), (GATE_LANE0, LANES - GATE_LANE0 - 4 * M_HEADS)))
    m_out = _mlstm(mq, mk, mv, g2, mo, mk_c, mv_c, g2_c, bias_row, o_head_norm_w[0][None, :])
    return _out_proj((c_out, m_out), z, None, o_w_out[0].astype(BF16), pw, g_x, x1, tm=tm_lat)
```

```python
import collections
import functools

import numpy as np
import jax
import jax.numpy as jnp
from jax import lax
from jax.experimental import pallas as pl
from jax.experimental.pallas import tpu as pltpu

F32 = jnp.float32
BF16 = jnp.bfloat16
HIGHEST = lax.Precision.HIGHEST

LANES = 128
VMEM_LIMIT = 56 * 1024 * 1024

D_MODEL = 1024
GRID_W = 64
ROPE_THETA = 10000.0
NORM_EPS = 1e-6
NEG = -1e30
LOG2E = 1.4426950408889634

A_HEADS, A_KV_HEADS, A_HEAD_DIM = 8, 2, 64
A_WIDTH = A_HEADS * A_HEAD_DIM
WINDOW = 128
B_WIDTH = 512
CONV_W = 3
C_HEADS, C_NOPE, C_ROPE, C_VDIM = 8, 64, 32, 64
C_KV_LORA, C_Q_LORA = 256, 768
C_WIDTH = C_HEADS * C_VDIM
M_HEADS, M_QK, M_V = 4, 64, 128
M_WIDTH = M_HEADS * M_V
E_GATE = A_WIDTH + B_WIDTH
O_GATE = C_WIDTH + M_WIDTH
E_COLS = (128, 128, A_WIDTH, B_WIDTH, B_WIDTH, B_WIDTH, E_GATE)
E_IN = sum(E_COLS)
O_COLS = (C_KV_LORA, C_ROPE, M_HEADS * M_QK, M_WIDTH, 4 * M_HEADS, C_Q_LORA, M_HEADS * M_QK, M_WIDTH, O_GATE)

EC_K, EC_Q, EC_BB, EC_BC, EC_BX, EC_Z, EC_END = 0, 128, 640, 1152, 1664, 2176, 3200
OC_CKV = 0
OC_G2 = 256
OC_MK = 384
OC_MV = 640
OC_CTX_END = 1152
OC_CQ = 1152
OC_MQ = 1920
OC_MO = 2176
OC_Z = 2688
OC_END = 3712
GATE_LANE0 = 32
MLSTM_CHUNK = 256
ATT_Q = 256
ATT_KEYS = 256
ATT_VDIM = 64
MLA_BLK = 512


def _nt(a, b):
    return lax.dot_general(a, b, (((1,), (1,)), ((), ())), preferred_element_type=F32)


def _dot(a, b):
    return jnp.dot(a, b, preferred_element_type=F32)


def _silu(z):
    return z * jax.nn.sigmoid(z)


def _rms(x, w):
    ms = jnp.mean(x * x, axis=-1, keepdims=True)
    return x * lax.rsqrt(ms + NORM_EPS) * w


def _rope_slab(x, cos, sin_signed, half, take_up):
    up = pltpu.roll(x, LANES - half, axis=1)
    dn = pltpu.roll(x, half, axis=1)
    return x * cos + jnp.where(take_up, up, dn) * sin_signed


def _cparams(sem):
    return pltpu.CompilerParams(dimension_semantics=sem, vmem_limit_bytes=VMEM_LIMIT)


def _mod_kernel(cc_ref, w_ref, b_ref, o_ref):
    s = _silu(cc_ref[...])
    o_ref[...] = jnp.dot(s, w_ref[...], preferred_element_type=F32, precision=HIGHEST) + b_ref[...]


def _modulation(cc, mod_w, mod_b):
    depth, d, d3 = mod_w.shape
    r = cc.shape[0]
    return pl.pallas_call(
        _mod_kernel,
        grid=(depth, d3 // d),
        in_specs=[
            pl.BlockSpec((r, d), lambda l, j: (0, 0)),
            pl.BlockSpec((None, d, d), lambda l, j: (l, 0, j)),
            pl.BlockSpec((None, 1, d), lambda l, j: (l, 0, j)),
        ],
        out_specs=pl.BlockSpec((None, r, d), lambda l, j: (l, 0, j)),
        out_shape=jax.ShapeDtypeStruct((depth, r, d3), F32),
        compiler_params=_cparams(("arbitrary", "arbitrary")),
        name="modulation",
    )(cc, mod_w, mod_b.reshape(depth, 1, d3))


def _even_in_kernel(x_ref, sh_ref, sc_ref, nw_ref, w_ref, wvt_ref, cos_ref, sin_ref,
                    k_ref, vt_ref, q_ref, bb_ref, u_ref, z_ref, *, rope):
    h = (_rms(x_ref[...], nw_ref[...]) * (1.0 + sc_ref[...]) + sh_ref[...]).astype(BF16)

    def proj(a, b):
        return _dot(h, w_ref[:, a:b])

    if rope:
        cos, sin = cos_ref[...], sin_ref[...]
        lane = lax.broadcasted_iota(jnp.int32, (1, LANES), 1)
        take_up = (lane % A_HEAD_DIM) < (A_HEAD_DIM // 2)
        rot = lambda t: _rope_slab(t, cos, sin, A_HEAD_DIM // 2, take_up)
    else:
        rot = lambda t: t
    k_ref[...] = rot(proj(EC_K, EC_Q)).astype(BF16)
    vt_ref[...] = _nt(wvt_ref[...], h).astype(BF16)
    q = proj(EC_Q, EC_BB)
    for j in range(A_WIDTH // LANES):
        sl = slice(j * LANES, (j + 1) * LANES)
        q_ref[:, sl] = (rot(q[:, sl]) * (A_HEAD_DIM ** -0.5 * LOG2E)).astype(BF16)
    bb_ref[...] = proj(EC_BB, EC_BC).astype(BF16)
    u_ref[...] = (proj(EC_BC, EC_BX) * proj(EC_BX, EC_Z)).astype(BF16)
    z_ref[...] = _silu(proj(EC_Z, EC_END)).astype(BF16)


def _even_in(x, sh, sc, nw, w, wvt, cos, sin, *, rope, tm):
    bsz, t, d = x.shape
    nb = sh.shape[0]
    mod_map = (lambda b, i: (b, 0, 0)) if nb > 1 else (lambda b, i: (0, 0, 0))
    tok = lambda width: pl.BlockSpec((None, tm, width), lambda b, i: (b, i, 0))
    const = lambda shape: pl.BlockSpec(shape, lambda b, i: (0,) * len(shape))
    widths = (LANES, None, A_WIDTH, B_WIDTH, B_WIDTH, E_GATE)
    vt_spec = pl.BlockSpec((None, LANES, tm), lambda b, i: (b, 0, i))
    vt_shape = jax.ShapeDtypeStruct((bsz, LANES, t), BF16)
    return pl.pallas_call(
        functools.partial(_even_in_kernel, rope=rope),
        grid=(bsz, t // tm),
        in_specs=[
            tok(d),
            pl.BlockSpec((None, 1, d), mod_map),
            pl.BlockSpec((None, 1, d), mod_map),
            const((1, d)),
            const(w.shape),
            const(wvt.shape),
            pl.BlockSpec((tm, LANES), lambda b, i: (i, 0)),
            pl.BlockSpec((tm, LANES), lambda b, i: (i, 0)),
        ],
        out_specs=[vt_spec if wd is None else tok(wd) for wd in widths],
        out_shape=[vt_shape if wd is None else jax.ShapeDtypeStruct((bsz, t, wd), BF16) for wd in widths],
        compiler_params=_cparams(("parallel", "parallel")),
        name="even_in_rope" if rope else "even_in",
    )(x, sh, sc, nw, w, wvt, cos, sin)


AttnStage = collections.namedtuple("AttnStage", "q k vt mask sink")


def _tree_max(parts):
    while len(parts) > 1:
        parts = [jnp.maximum(parts[i], parts[i + 1]) if i + 1 < len(parts) else parts[i]
                 for i in range(0, len(parts), 2)]
    return parts[0]


def _attention_pipeline(stages, n_chunks, s_ref, write_pair):
    kc = s_ref.shape[2]
    ones = jnp.ones((16, kc), BF16)
    m8 = None
    halves = []
    for i in range(len(stages) + 1):
        if i >= 1:
            prev = stages[i - 1]
            m_prev = jnp.max(m8, axis=0, keepdims=True)
            if prev.sink is not None:
                m_prev = jnp.maximum(m_prev, prev.sink)
            acc = None
        if i < len(stages):
            cur = stages[i]
            q = cur.q()
        for c in range(n_chunks):
            if i < len(stages):
                st = _nt(cur.k(c), q)
                mask = cur.mask(c)
                if mask is not None:
                    st = jnp.where(mask, st, NEG)
                s_ref[i % 2, c] = st
                part = _tree_max([st[r:r + 8, :] for r in range(0, kc, 8)])
                m8 = part if c == 0 else jnp.maximum(m8, part)
            if i >= 1:
                pt = jnp.exp2(s_ref[(i - 1) % 2, c] - m_prev).astype(BF16)
                upd = _dot(jnp.concatenate([prev.vt(c), ones], axis=0), pt)
                acc = upd if c == 0 else acc + upd
        if i >= 1:
            denom = acc[ATT_VDIM:ATT_VDIM + 1]
            if prev.sink is not None:
                denom = denom + jnp.exp2(prev.sink - m_prev)
            halves.append(acc[0:ATT_VDIM] / denom)
            if len(halves) == 2:
                write_pair((i - 1) // 2, jnp.concatenate(halves, axis=0))
                halves = []


def _even_attn_kernel(*refs, window, n_lat):
    if window:
        (sink_ref, q_ref, kp_ref, km_ref, kn_ref, kc_ref, vtp_ref, vtm_ref, vtn_ref, vtc_ref,
         o_ref, s_ref) = refs
    else:
        sink_ref, q_ref, kc_ref, vtc_ref, o_ref, s_ref = refs
    nq, kc = ATT_Q, ATT_KEYS
    lane = lax.broadcasted_iota(jnp.int32, (1, LANES), 1)
    lo = lane < A_HEAD_DIM
    k_chunks = [kc_ref[r:r + kc, :] for r in range(0, kc_ref.shape[0], kc)]
    vt_chunks = [vtc_ref[:, r:r + kc] for r in range(0, kc_ref.shape[0], kc)]
    masks = [None] * len(k_chunks)
    if window:
        half = kc // 2
        k_chunks += [jnp.concatenate([kp_ref[...], km_ref[0:half, :]], axis=0),
                     jnp.concatenate([km_ref[half:kc, :], kn_ref[...]], axis=0)]
        vt_chunks += [jnp.concatenate([vtp_ref[...], vtm_ref[:, 0:half]], axis=1),
                      jnp.concatenate([vtm_ref[:, half:kc], vtn_ref[...]], axis=1)]
        q0 = pl.program_id(1) * nq
        r = lax.broadcasted_iota(jnp.int32, (kc, nq), 0)
        c = lax.broadcasted_iota(jnp.int32, (kc, nq), 1)
        masks += [(jnp.abs(r - half - c) <= WINDOW) & (r + (q0 - half) >= 0),
                  (jnp.abs(r + half - c) <= WINDOW) & (r + (q0 + half) < n_lat)]

    def stage(head):
        j, grp = head % (A_HEADS // 2), head // (A_HEADS // 2)
        keep = lo if grp == 0 else jnp.logical_not(lo)

        def q():
            qs = q_ref[:, j * LANES:(j + 1) * LANES]
            return jnp.where(keep, qs, jnp.zeros_like(qs))

        return AttnStage(q=q, k=lambda ci: k_chunks[ci],
                         vt=lambda ci: vt_chunks[ci][grp * A_HEAD_DIM:(grp + 1) * A_HEAD_DIM, :],
                         mask=lambda ci: masks[ci], sink=sink_ref[head] * LOG2E)

    def write_pair(p, o_t):
        o_ref[:, p * LANES:(p + 1) * LANES] = jnp.transpose(o_t).astype(BF16)

    _attention_pipeline([stage(h) for h in _A_HEAD_ORDER], len(k_chunks), s_ref, write_pair)


def _even_attn(sink, q, k, vt, kc, vtc, *, window):
    bsz, t, _ = q.shape
    n_ctx = kc.shape[1]
    nq, keys = ATT_Q, ATT_KEYS
    assert nq == keys and t % nq == 0 and n_ctx % keys == 0
    half = keys // 2
    last = t // half - 1
    smem = pl.BlockSpec(memory_space=pltpu.SMEM)
    qspec = pl.BlockSpec((None, nq, A_WIDTH), lambda b, i: (b, i, 0))
    kcspec = pl.BlockSpec((None, n_ctx, LANES), lambda b, i: (b, 0, 0))
    vtcspec = pl.BlockSpec((None, LANES, n_ctx), lambda b, i: (b, 0, 0))
    if window:
        prev = lambda i: jnp.maximum(2 * i - 1, 0)
        nxt = lambda i: jnp.minimum(2 * i + 2, last)
        specs = [smem, qspec,
                 pl.BlockSpec((None, half, LANES), lambda b, i: (b, prev(i), 0)),
                 pl.BlockSpec((None, keys, LANES), lambda b, i: (b, i, 0)),
                 pl.BlockSpec((None, half, LANES), lambda b, i: (b, nxt(i), 0)),
                 kcspec,
                 pl.BlockSpec((None, LANES, half), lambda b, i: (b, 0, prev(i))),
                 pl.BlockSpec((None, LANES, keys), lambda b, i: (b, 0, i)),
                 pl.BlockSpec((None, LANES, half), lambda b, i: (b, 0, nxt(i))),
                 vtcspec]
        args = (sink, q, k, k, k, kc, vt, vt, vt, vtc)
    else:
        specs = [smem, qspec, kcspec, vtcspec]
        args = (sink, q, kc, vtc)
    n_chunks = n_ctx // keys + (2 if window else 0)
    return pl.pallas_call(
        functools.partial(_even_attn_kernel, window=window, n_lat=t),
        grid=(bsz, t // nq),
        in_specs=specs,
        out_specs=qspec,
        out_shape=jax.ShapeDtypeStruct((bsz, t, A_WIDTH), BF16),
        scratch_shapes=[pltpu.VMEM((2, n_chunks, keys, nq), F32)],
        compiler_params=_cparams(("parallel", "parallel")),
        name="even_attn_window" if window else "even_attn_ctx",
    )(*args)


def _out_kernel(*refs, conv):
    if conv:
        a_ref, bb_ref, u_ref, up_ref, un_ref, z_ref, cw_ref, w_ref, pw_ref, g_ref, x_ref, o_ref = refs
        tm = u_ref.shape[0]
        i = pl.program_id(1)
        u = u_ref[...].astype(F32)
        prev_row = jnp.where(i > 0, up_ref[15:16, :].astype(F32), 0.0)
        next_row = jnp.where(i < pl.num_programs(1) - 1, un_ref[0:1, :].astype(F32), 0.0)
        row = lax.broadcasted_iota(jnp.int32, (tm, 1), 0)
        u_m1 = jnp.where(row == 0, prev_row, pltpu.roll(u, 1, axis=0))
        u_p1 = jnp.where(row == tm - 1, next_row, pltpu.roll(u, tm - 1, axis=0))
        cw = cw_ref[...]
        mix_b = bb_ref[...].astype(F32) * (u_m1 * cw[0:1, :] + u * cw[1:2, :] + u_p1 * cw[2:3, :])
    else:
        a_ref, b_ref, z_ref, w_ref, pw_ref, g_ref, x_ref, o_ref = refs
        mix_b = b_ref[...].astype(F32)
    wa = a_ref.shape[1]
    ga = (a_ref[...].astype(F32) * z_ref[:, 0:wa].astype(F32)).astype(BF16)
    gb = (mix_b * z_ref[:, wa:].astype(F32)).astype(BF16)
    y = _dot(ga, w_ref[0:wa, :]) + _dot(gb, w_ref[wa:, :])
    o_ref[...] = x_ref[...] + g_ref[...] * _rms(y, pw_ref[...])


def _out_proj(mix, z, cw, w, pw, g, x, *, tm):
    conv = cw is not None
    bsz, t, d = x.shape
    nb = g.shape[0]
    mod_map = (lambda b, i: (b, 0, 0)) if nb > 1 else (lambda b, i: (0, 0, 0))
    tok = lambda width: pl.BlockSpec((None, tm, width), lambda b, i: (b, i, 0))
    const = lambda shape: pl.BlockSpec(shape, lambda b, i: (0,) * len(shape))
    r16 = tm // 16
    if conv:
        a, bb, u = mix
        specs = [tok(a.shape[2]), tok(bb.shape[2]), tok(u.shape[2]),
                 pl.BlockSpec((None, 16, u.shape[2]), lambda b, i: (b, jnp.maximum(i * r16 - 1, 0), 0)),
                 pl.BlockSpec((None, 16, u.shape[2]), lambda b, i: (b, jnp.minimum((i + 1) * r16, t // 16 - 1), 0)),
                 tok(z.shape[2]), const(cw.shape)]
        args = (a, bb, u, u, u, z, cw)
    else:
        a, b2 = mix
        specs = [tok(a.shape[2]), tok(b2.shape[2]), tok(z.shape[2])]
        args = (a, b2, z)
    specs += [const(w.shape), const((1, d)), pl.BlockSpec((None, 1, d), mod_map), tok(d)]
    return pl.pallas_call(
        functools.partial(_out_kernel, conv=conv),
        grid=(bsz, t // tm),
        in_specs=specs,
        out_specs=tok(d),
        out_shape=jax.ShapeDtypeStruct((bsz, t, d), F32),
        compiler_params=_cparams(("parallel", "parallel")),
        name="out_proj_conv" if conv else "out_proj",
    )(*args, w, pw, g, x)


def _odd_in_kernel(*refs, lat):
    if lat:
        (x_ref, sh_ref, sc_ref, nw_ref, w_ref, kvn_ref, wk_ref, wvt_ref, cosk_ref, sink_ref,
         qn_ref, wq_ref, cosq_ref, sinq_ref,
         kf_ref, vt_ref, mk_ref, mv_ref, g2_ref, qf_ref, mq_ref, mo_ref, z_ref) = refs
    else:
        (x_ref, sh_ref, sc_ref, nw_ref, w_ref, kvn_ref, wk_ref, wvt_ref,
         kf_ref, vt_ref, mk_ref, mv_ref, g2_ref) = refs
    h = (_rms(x_ref[...], nw_ref[...]) * (1.0 + sc_ref[...]) + sh_ref[...]).astype(BF16)

    def proj(a, b):
        return _dot(h, w_ref[:, a:b])

    lane = lax.broadcasted_iota(jnp.int32, (1, LANES), 1)
    ckv = _rms(proj(OC_CKV, OC_CKV + C_KV_LORA), kvn_ref[...]).astype(BF16)
    g2 = proj(OC_G2, OC_G2 + LANES)
    g2_ref[...] = g2
    if lat:
        g2 = _rope_slab(g2, cosk_ref[...], sink_ref[...], C_ROPE // 2, lane < C_ROPE // 2)
    kf_ref[...] = _dot(jnp.concatenate([ckv, g2.astype(BF16)], axis=1), wk_ref[...]).astype(BF16)
    vt_ref[...] = _nt(wvt_ref[...], ckv).astype(BF16)
    mk_ref[...] = (proj(OC_MK, OC_MK + 256) * (M_QK ** -0.5)).astype(BF16)
    mv_ref[...] = proj(OC_MV, OC_MV + M_WIDTH).astype(BF16)
    if lat:
        cq = _rms(proj(OC_CQ, OC_CQ + C_Q_LORA), qn_ref[...]).astype(BF16)
        qf = _dot(cq, wq_ref[...])
        cos, sin = cosq_ref[...], sinq_ref[...]
        take_up = (lane >= C_NOPE) & (lane < C_NOPE + C_ROPE // 2)
        scale = (C_NOPE + C_ROPE) ** -0.5 * LOG2E
        for hd in range(C_HEADS):
            sl = slice(hd * LANES, (hd + 1) * LANES)
            qf_ref[:, sl] = (_rope_slab(qf[:, sl], cos, sin, C_ROPE // 2, take_up) * scale).astype(BF16)
        mq_ref[...] = proj(OC_MQ, OC_MQ + 256).astype(BF16)
        mo_ref[...] = proj(OC_MO, OC_MO + M_WIDTH).astype(BF16)
        z_ref[...] = _silu(proj(OC_Z, OC_Z + O_GATE)).astype(BF16)


def _odd_in(x, sh, sc, nw, w, kvn, wk, wv, lat_args, *, tm):
    lat = lat_args is not None
    bsz, t, d = x.shape
    nb = sh.shape[0]
    mod_map = (lambda b, i: (b, 0, 0)) if nb > 1 else (lambda b, i: (0, 0, 0))
    tok = lambda width: pl.BlockSpec((None, tm, width), lambda b, i: (b, i, 0))
    const = lambda shape: pl.BlockSpec(shape, lambda b, i: (0,) * len(shape))
    tab = pl.BlockSpec((tm, LANES), lambda b, i: (i, 0))
    specs = [tok(d), pl.BlockSpec((None, 1, d), mod_map), pl.BlockSpec((None, 1, d), mod_map),
             const((1, d)), const(w.shape), const(kvn.shape), const(wk.shape), const(wv.shape)]
    args = [x, sh, sc, nw, w, kvn, wk, wv]
    outs = [(C_HEADS * LANES, BF16), None, (256, BF16), (M_WIDTH, BF16), (LANES, F32)]
    if lat:
        cosk, sink, qn, wq, cosq, sinq = lat_args
        specs += [tab, tab, const(qn.shape), const(wq.shape), tab, tab]
        args += [cosk, sink, qn, wq, cosq, sinq]
        outs += [(C_HEADS * LANES, BF16), (256, BF16), (M_WIDTH, BF16), (O_GATE, BF16)]
    vt_spec = pl.BlockSpec((None, C_WIDTH, tm), lambda b, i: (b, 0, i))
    vt_shape = jax.ShapeDtypeStruct((bsz, C_WIDTH, t), BF16)
    return pl.pallas_call(
        functools.partial(_odd_in_kernel, lat=lat),
        grid=(bsz, t // tm),
        in_specs=specs,
        out_specs=[vt_spec if o is None else tok(o[0]) for o in outs],
        out_shape=[vt_shape if o is None else jax.ShapeDtypeStruct((bsz, t, o[0]), o[1]) for o in outs],
        compiler_params=_cparams(("parallel", "parallel")),
        name="odd_in_lat" if lat else "odd_in_ctx",
    )(*args)


def _mla_kernel(q_ref, kc_ref, kx_ref, vtc_ref, vtx_ref, o_ref, s_ref):
    nq, kc = ATT_Q, ATT_KEYS
    chunks = ([(kc_ref, vtc_ref, r) for r in range(0, kc_ref.shape[0], kc)]
              + [(kx_ref, vtx_ref, r) for r in range(0, kx_ref.shape[0], kc)])

    def stage(qb, hd):
        sl = slice(hd * LANES, (hd + 1) * LANES)

        def k(ci):
            k_ref, _, r0 = chunks[ci]
            return k_ref[r0:r0 + kc, sl]

        def vt(ci):
            _, vt_ref, r0 = chunks[ci]
            return vt_ref[hd * C_VDIM:(hd + 1) * C_VDIM, r0:r0 + kc]

        return AttnStage(q=lambda: q_ref[qb * nq:(qb + 1) * nq, sl], k=k, vt=vt, mask=lambda ci: None, sink=None)

    def write_pair(p, o_t):
        qb, j = divmod(p, C_HEADS // 2)
        o_ref[qb * nq:(qb + 1) * nq, j * LANES:(j + 1) * LANES] = jnp.transpose(o_t).astype(BF16)

    stages = [stage(qb, hd) for qb in range(q_ref.shape[0] // nq) for hd in range(C_HEADS)]
    _attention_pipeline(stages, len(chunks), s_ref, write_pair)


def _mla(qf, kfc, kfx, vtc, vtx):
    bsz, t, wq = qf.shape
    n_ctx = kfc.shape[1]
    blk = MLA_BLK
    return pl.pallas_call(
        _mla_kernel,
        grid=(bsz, t // blk),
        in_specs=[
            pl.BlockSpec((None, blk, wq), lambda b, i: (b, i, 0)),
            pl.BlockSpec((None, n_ctx, wq), lambda b, i: (b, 0, 0)),
            pl.BlockSpec((None, t, wq), lambda b, i: (b, 0, 0)),
            pl.BlockSpec((None, C_WIDTH, n_ctx), lambda b, i: (b, 0, 0)),
            pl.BlockSpec((None, C_WIDTH, t), lambda b, i: (b, 0, 0)),
        ],
        out_specs=pl.BlockSpec((None, blk, C_WIDTH), lambda b, i: (b, i, 0)),
        out_shape=jax.ShapeDtypeStruct((bsz, t, C_WIDTH), BF16),
        scratch_shapes=[pltpu.VMEM((2, (n_ctx + t) // ATT_KEYS, ATT_KEYS, ATT_Q), F32)],
        compiler_params=_cparams(("parallel", "arbitrary")),
        name="mla_attention",
    )(qf, kfc, kfx, vtc, vtx)


def _mlstm_kernel(mq_ref, mk_ref, mv_ref, g_ref, mo_ref, mkc_ref, mvc_ref, gc_ref, bias_ref, hnw_ref,
                  out_ref, s_ref, m_ref, hf_ref, hr_ref, *, chunk):
    L = chunk
    n_lat, n_ctx = mq_ref.shape[0], mkc_ref.shape[0]
    nc, ncc = n_lat // L, n_ctx // L
    row = lax.broadcasted_iota(jnp.int32, (L, L), 0)
    col = lax.broadcasted_iota(jnp.int32, (L, L), 1)
    tri = (col <= row, col >= row)
    tri_f32 = tri[0].astype(F32)
    lane = lax.broadcasted_iota(jnp.int32, (1, LANES), 1)
    lo = lane < M_QK
    ones_col = (lax.broadcasted_iota(jnp.int32, (L, LANES), 1) == 0).astype(BF16)
    fwd_lanes = lane < GATE_LANE0 + 2 * M_HEADS

    def gates(g):
        gb = g + bias_ref[...]
        ls = jnp.minimum(gb, 0.0) - jnp.log1p(jnp.exp(-jnp.abs(gb)))
        cf = jnp.dot(tri_f32, ls, preferred_element_type=F32, precision=HIGHEST)
        tot = cf[L - 1:L, :]
        b = jnp.where(fwd_lanes, cf, tot - cf + ls)
        li = pltpu.roll(gb, M_HEADS, axis=1)
        r_t = jnp.transpose(b - li)
        return b, li, tot, r_t

    def unit(d, hd, kpair, vh, gq, qpair, with_out):
        b, li, tot, r_t = gq
        u = d * M_HEADS + hd
        fl = GATE_LANE0 + d * 2 * M_HEADS + M_HEADS + hd
        msk = lo if hd % 2 == 0 else jnp.logical_not(lo)
        bcol, licol, g = b[:, fl:fl + 1], li[:, fl:fl + 1], tot[:, fl:fl + 1]
        kh = jnp.where(msk, kpair, jnp.zeros_like(kpair))
        vaug = jnp.concatenate([vh, ones_col], axis=1)
        s_in = s_ref[u]
        m_in = m_ref[u][0:1, 0:1]
        h_out = None
        if with_out:
            qh = jnp.where(msk, qpair, jnp.zeros_like(qpair))
            inter = bcol + m_in
            dm = jnp.where(tri[d], bcol - r_t[fl:fl + 1, :], NEG)
            m_t = jnp.maximum(inter, jnp.max(dm, axis=-1, keepdims=True))
            sc = _nt(qh, kh) * jnp.exp(dm - m_t)
            res = jnp.exp(inter - m_t) * _dot(qh, s_in.astype(BF16)) + _dot(sc.astype(BF16), vaug)
            den = jnp.maximum(jnp.abs(res[:, LANES:LANES + 1]), jnp.exp(-m_t))
            h_out = res[:, 0:LANES] / den
        acol = g - bcol + licol
        m_loc = jnp.max(acol, axis=0, keepdims=True)
        wv = (jnp.exp(acol - m_loc) * vaug.astype(F32)).astype(BF16)
        s_loc = lax.dot_general(kh, wv, (((0,), (0,)), ((), ())), preferred_element_type=F32)
        m_new = jnp.maximum(g + m_in, m_loc)
        s_ref[u] = jnp.exp(g + m_in - m_new) * s_in + jnp.exp(m_loc - m_new) * s_loc
        m_ref[u] = jnp.broadcast_to(m_new, (8, LANES))
        return h_out

    s_ref[...] = jnp.zeros_like(s_ref)
    m_ref[...] = jnp.zeros_like(m_ref)
    for d in range(2):
        for cc in range(ncc):
            r0 = (cc if d == 0 else ncc - 1 - cc) * L
            gq = gates(gc_ref[r0:r0 + L, :])
            for hd in range(M_HEADS):
                pr = slice((hd // 2) * LANES, (hd // 2 + 1) * LANES)
                unit(d, hd, mkc_ref[r0:r0 + L, pr], mvc_ref[r0:r0 + L, hd * M_V:(hd + 1) * M_V], gq, None, False)

    def body(j, carry):
        for d in range(2):
            r0 = pl.multiple_of((j if d == 0 else nc - 1 - j) * L, L)
            rows = pl.ds(r0, L)
            gq = gates(g_ref[rows, :])
            dst = hf_ref if d == 0 else hr_ref
            for hd in range(M_HEADS):
                pr = slice((hd // 2) * LANES, (hd // 2 + 1) * LANES)
                vs = slice(hd * M_V, (hd + 1) * M_V)
                dst[rows, vs] = unit(d, hd, mk_ref[rows, pr], mv_ref[rows, vs], gq, mq_ref[rows, pr], True)
        return carry

    lax.fori_loop(0, nc, body, 0)

    def readout(j, carry):
        rows = pl.ds(pl.multiple_of(j * L, L), L)
        for hd in range(M_HEADS):
            vs = slice(hd * M_V, (hd + 1) * M_V)
            hs = _rms(hf_ref[rows, vs] + hr_ref[rows, vs], hnw_ref[:, vs])
            out_ref[rows, vs] = (jax.nn.sigmoid(mo_ref[rows, vs].astype(F32)) * hs).astype(BF16)
        return carry

    lax.fori_loop(0, nc, readout, 0)


def _mlstm(mq, mk, mv, g2, mo, mkc, mvc, g2c, bias_row, hnw):
    bsz, t, _ = mq.shape
    n_ctx = mkc.shape[1]
    chunk = min(MLSTM_CHUNK, n_ctx)
    per_b = lambda rows, width: pl.BlockSpec((None, rows, width), lambda b: (b, 0, 0))
    const = lambda shape: pl.BlockSpec(shape, lambda b: (0,) * len(shape))
    return pl.pallas_call(
        functools.partial(_mlstm_kernel, chunk=chunk),
        grid=(bsz,),
        in_specs=[per_b(t, 256), per_b(t, 256), per_b(t, M_WIDTH), per_b(t, LANES), per_b(t, M_WIDTH),
                  per_b(n_ctx, 256), per_b(n_ctx, M_WIDTH), per_b(n_ctx, LANES),
                  const((1, LANES)), const((1, M_WIDTH))],
        out_specs=per_b(t, M_WIDTH),
        out_shape=jax.ShapeDtypeStruct((bsz, t, M_WIDTH), BF16),
        scratch_shapes=[pltpu.VMEM((2 * M_HEADS, LANES, 2 * LANES), F32),
                        pltpu.VMEM((2 * M_HEADS, 8, LANES), F32),
                        pltpu.VMEM((t, M_WIDTH), F32),
                        pltpu.VMEM((t, M_WIDTH), F32)],
        compiler_params=_cparams(("parallel",)),
        name="mlstm",
    )(mq, mk, mv, g2, mo, mkc, mvc, g2c, bias_row, hnw)


def _axial_angles(n_lat, rot_dim):
    rows = n_lat // GRID_W
    row = jnp.repeat(jnp.arange(rows), GRID_W).astype(F32)
    col = jnp.tile(jnp.arange(GRID_W), rows).astype(F32)
    n_freq = rot_dim // 4
    inv = ROPE_THETA ** (-jnp.arange(n_freq, dtype=F32) / n_freq)
    ang = jnp.concatenate([row[:, None] * inv, col[:, None] * inv], axis=-1)
    return jnp.cos(ang), jnp.sin(ang)


def _rope_tables(n_lat):
    cos, sin = _axial_angles(n_lat, A_HEAD_DIM)
    cos_a = jnp.tile(jnp.concatenate([cos, cos], axis=-1), (1, 2))
    sin_a = jnp.tile(jnp.concatenate([-sin, sin], axis=-1), (1, 2))
    cos, sin = _axial_angles(n_lat, C_ROPE)
    one = lambda w: jnp.ones((n_lat, w), F32)
    zero = lambda w: jnp.zeros((n_lat, w), F32)
    cos_k = jnp.concatenate([cos, cos, one(LANES - C_ROPE)], axis=-1)
    sin_k = jnp.concatenate([-sin, sin, zero(LANES - C_ROPE)], axis=-1)
    cos_q = jnp.concatenate([one(C_NOPE), cos, cos, one(LANES - C_NOPE - C_ROPE)], axis=-1)
    sin_q = jnp.concatenate([zero(C_NOPE), -sin, sin, zero(LANES - C_NOPE - C_ROPE)], axis=-1)
    return (cos_a, sin_a), (cos_k, sin_k), (cos_q, sin_q)


_A_HEAD_ORDER = (0, 4, 1, 5, 2, 6, 3, 7)


def _perm_heads(w, axis):
    shape = w.shape
    w = w.reshape(shape[:axis] + (A_HEADS, A_HEAD_DIM) + shape[axis + 1:])
    w = jnp.take(w, jnp.array(_A_HEAD_ORDER), axis=axis)
    return w.reshape(shape)


def _even_weights(w_in, w_out):
    offs = np.cumsum((0,) + E_COLS)
    parts = [w_in[:, offs[i]:offs[i + 1]] for i in range(len(E_COLS))]
    parts[2] = _perm_heads(parts[2], 1)
    z = parts[6]
    parts[6] = jnp.concatenate([_perm_heads(z[:, :A_WIDTH], 1), z[:, A_WIDTH:]], axis=1)
    w_out_p = jnp.concatenate([_perm_heads(w_out[:A_WIDTH], 0), w_out[A_WIDTH:]], axis=0)
    wvt = jnp.transpose(parts.pop(1)).astype(BF16)
    return jnp.concatenate(parts, axis=1).astype(BF16), wvt, w_out_p.astype(BF16)


def _odd_weights(w_in, w_uq, w_ukv):
    d = w_in.shape[0]
    offs = np.cumsum((0,) + O_COLS)
    ckv, kr, mk, mv, mg, cq, mq, mo, z = [w_in[:, offs[i]:offs[i + 1]] for i in range(len(O_COLS))]
    g2 = jnp.concatenate([kr, mg, jnp.zeros((d, LANES - C_ROPE - 4 * M_HEADS), w_in.dtype)], axis=1)
    w_p = jnp.concatenate([ckv, g2, mk, mv, cq, mq, mo, z], axis=1).astype(BF16)
    ukv = w_ukv.reshape(C_KV_LORA, C_HEADS, C_NOPE + C_VDIM)
    wk_top = jnp.pad(ukv[:, :, :C_NOPE], ((0, 0), (0, 0), (0, LANES - C_NOPE)))
    eye = jnp.pad(jnp.eye(C_ROPE, dtype=w_in.dtype), ((0, LANES - C_ROPE), (C_NOPE, LANES - C_NOPE - C_ROPE)))
    wk_bot = jnp.broadcast_to(eye[:, None, :], (LANES, C_HEADS, LANES))
    wk = jnp.concatenate([wk_top, wk_bot], axis=0).reshape(C_KV_LORA + LANES, C_HEADS * LANES).astype(BF16)
    wv = jnp.transpose(ukv[:, :, C_NOPE:].reshape(C_KV_LORA, C_WIDTH)).astype(BF16)
    uq = w_uq.reshape(C_Q_LORA, C_HEADS, C_NOPE + C_ROPE)
    wq = jnp.pad(uq, ((0, 0), (0, 0), (0, LANES - C_NOPE - C_ROPE))).reshape(C_Q_LORA, C_HEADS * LANES).astype(BF16)
    return w_p, wk, wv, wq


def kernel(x, c, ctx, c_ctx, mod_w, mod_b, pre_norm_w, post_norm_w, e_w_in, e_sink, e_conv_w, e_w_out,
           o_w_in, o_q_norm_w, o_kv_norm_w, o_w_uq, o_w_ukv, o_i_bias, o_f_bias, o_head_norm_w, o_w_out):
    bsz, n_lat, d = x.shape
    n_ctx = ctx.shape[1]
    assert mod_w.shape[0] == 2 and d == D_MODEL, "built for one even + one odd layer"
    assert n_lat % 512 == 0 and n_ctx % 128 == 0
    tm_lat, tm_ctx = 512, min(256, n_ctx)

    pad = (-(bsz + 1)) % 8
    cc = jnp.concatenate([c, c_ctx[None, :], jnp.zeros((pad, d), F32)], axis=0)
    mod = _modulation(cc, mod_w, mod_b)
    split = lambda l, r0, r1: [mod[l, r0:r1, k * d:(k + 1) * d][:, None, :] for k in range(3)]
    (rope_a, rope_k, rope_q) = _rope_tables(n_lat)
    dummy_tab = jnp.zeros((n_ctx, LANES), F32)

    sh_x, sc_x, g_x = split(0, 0, bsz)
    sh_c, sc_c, g_c = split(0, bsz, bsz + 1)
    w_in, wvt, w_out = _even_weights(e_w_in[0], e_w_out[0])
    nw, pw = pre_norm_w[0][None, :], post_norm_w[0][None, :]
    k, vt, q, bb, u, z = _even_in(x, sh_x, sc_x, nw, w_in, wvt, *rope_a, rope=True, tm=tm_lat)
    k_c, vt_c, q_c, bb_c, u_c, z_c = _even_in(ctx, sh_c, sc_c, nw, w_in, wvt, dummy_tab, dummy_tab,
                                              rope=False, tm=tm_ctx)
    a = _even_attn(e_sink[0], q, k, vt, k_c, vt_c, window=True)
    a_c = _even_attn(e_sink[0], q_c, None, None, k_c, vt_c, window=False)
    x1 = _out_proj((a, bb, u), z, e_conv_w[0], w_out, pw, g_x, x, tm=tm_lat)
    ctx1 = _out_proj((a_c, bb_c, u_c), z_c, e_conv_w[0], w_out, pw, g_c, ctx, tm=tm_ctx)

    sh_x, sc_x, g_x = split(1, 0, bsz)
    sh_c, sc_c, _ = split(1, bsz, bsz + 1)
    w_p, wk, wv, wq = _odd_weights(o_w_in[0], o_w_uq[0], o_w_ukv[0])
    nw, pw = pre_norm_w[1][None, :], post_norm_w[1][None, :]
    kvn, qn = o_kv_norm_w[0][None, :], o_q_norm_w[0][None, :]
    kf, v, mk, mv, g2, qf, mq, mo, z = _odd_in(x1, sh_x, sc_x, nw, w_p, kvn, wk, wv,
                                               (*rope_k, qn, wq, *rope_q), tm=tm_lat)
    kf_c, v_c, mk_c, mv_c, g2_c = _odd_in(ctx1, sh_c, sc_c, nw, w_p[:, :OC_CTX_END], kvn, wk, wv, None, tm=tm_ctx)
    c_out = _mla(qf, kf_c, kf, v_c, v)
    gate_bias = jnp.stack([o_i_bias[0], o_f_bias[0]], axis=1).reshape(1, 4 * M_HEADS)
    bias_row = jnp.pad(gate_bias, ((0, 0), (GATE_LANE0, LANES - GATE_LANE0 - 4 * M_HEADS)))
    m_out = _mlstm(mq, mk, mv, g2, mo, mk_c, mv_c, g2_c, bias_row, o_head_norm_w[0][None, :])
    return _out_proj((c_out, m_out), z, None, o_w_out[0].astype(BF16), pw, g_x, x1, tm=tm_lat)
```

```python
import collections
import functools

import numpy as np
import jax
import jax.numpy as jnp
from jax import lax
from jax.experimental import pallas as pl
from jax.experimental.pallas import tpu as pltpu

F32 = jnp.float32
BF16 = jnp.bfloat16
HIGHEST = lax.Precision.HIGHEST

LANES = 128
VMEM_LIMIT = 56 * 1024 * 1024

D_MODEL = 1024
GRID_W = 64
ROPE_THETA = 10000.0
NORM_EPS = 1e-6
NEG = -1e30
LOG2E = 1.4426950408889634

A_HEADS, A_KV_HEADS, A_HEAD_DIM = 8, 2, 64
A_WIDTH = A_HEADS * A_HEAD_DIM
WINDOW = 128
B_WIDTH = 512
CONV_W = 3
C_HEADS, C_NOPE, C_ROPE, C_VDIM = 8, 64, 32, 64
C_KV_LORA, C_Q_LORA = 256, 768
C_WIDTH = C_HEADS * C_VDIM
M_HEADS, M_QK, M_V = 4, 64, 128
M_WIDTH = M_HEADS * M_V
E_GATE = A_WIDTH + B_WIDTH
O_GATE = C_WIDTH + M_WIDTH
E_COLS = (128, 128, A_WIDTH, B_WIDTH, B_WIDTH, B_WIDTH, E_GATE)
E_IN = sum(E_COLS)
O_COLS = (C_KV_LORA, C_ROPE, M_HEADS * M_QK, M_WIDTH, 4 * M_HEADS, C_Q_LORA, M_HEADS * M_QK, M_WIDTH, O_GATE)

EC_K, EC_Q, EC_BB, EC_BC, EC_BX, EC_Z, EC_END = 0, 128, 640, 1152, 1664, 2176, 3200
OC_CKV = 0
OC_G2 = 256
OC_MK = 384
OC_CTX_END = 640
OC_CQ = 640
OC_MQ = 1408
OC_MO = 1664
OC_Z = 2176
OC_END = 3200
M_QKW = M_HEADS * M_QK
M_ONES = 16
GATE_LANE0 = 32
MLSTM_CHUNK = 256
ATT_Q = 256
ATT_KEYS = 256
ATT_VDIM = 64
MLA_BLK = 512


def _nt(a, b):
    return lax.dot_general(a, b, (((1,), (1,)), ((), ())), preferred_element_type=F32)


def _dot(a, b):
    return jnp.dot(a, b, preferred_element_type=F32)


def _silu(z):
    return z * jax.nn.sigmoid(z)


def _rms(x, w):
    ms = jnp.mean(x * x, axis=-1, keepdims=True)
    return x * lax.rsqrt(ms + NORM_EPS) * w


def _rope_slab(x, cos, sin_signed, half, take_up):
    up = pltpu.roll(x, LANES - half, axis=1)
    dn = pltpu.roll(x, half, axis=1)
    return x * cos + jnp.where(take_up, up, dn) * sin_signed


def _cparams(sem):
    return pltpu.CompilerParams(dimension_semantics=sem, vmem_limit_bytes=VMEM_LIMIT)


def _mod_kernel(cc_ref, w_ref, b_ref, o_ref):
    s = _silu(cc_ref[...])
    o_ref[...] = jnp.dot(s, w_ref[...], preferred_element_type=F32, precision=HIGHEST) + b_ref[...]


def _modulation(cc, mod_w, mod_b):
    depth, d, d3 = mod_w.shape
    r = cc.shape[0]
    return pl.pallas_call(
        _mod_kernel,
        grid=(depth, d3 // d),
        in_specs=[
            pl.BlockSpec((r, d), lambda l, j: (0, 0)),
            pl.BlockSpec((None, d, d), lambda l, j: (l, 0, j)),
            pl.BlockSpec((None, 1, d), lambda l, j: (l, 0, j)),
        ],
        out_specs=pl.BlockSpec((None, r, d), lambda l, j: (l, 0, j)),
        out_shape=jax.ShapeDtypeStruct((depth, r, d3), F32),
        compiler_params=_cparams(("arbitrary", "arbitrary")),
        name="modulation",
    )(cc, mod_w, mod_b.reshape(depth, 1, d3))


def _even_in_kernel(x_ref, sh_ref, sc_ref, nw_ref, w_ref, wvt_ref, cos_ref, sin_ref,
                    k_ref, vt_ref, q_ref, bb_ref, u_ref, z_ref, *, rope):
    h = (_rms(x_ref[...], nw_ref[...]) * (1.0 + sc_ref[...]) + sh_ref[...]).astype(BF16)

    def proj(a, b):
        return _dot(h, w_ref[:, a:b])

    if rope:
        cos, sin = cos_ref[...], sin_ref[...]
        lane = lax.broadcasted_iota(jnp.int32, (1, LANES), 1)
        take_up = (lane % A_HEAD_DIM) < (A_HEAD_DIM // 2)
        rot = lambda t: _rope_slab(t, cos, sin, A_HEAD_DIM // 2, take_up)
    else:
        rot = lambda t: t
    k_ref[...] = rot(proj(EC_K, EC_Q)).astype(BF16)
    vt_ref[...] = _nt(wvt_ref[...], h).astype(BF16)
    q = proj(EC_Q, EC_BB)
    for j in range(A_WIDTH // LANES):
        sl = slice(j * LANES, (j + 1) * LANES)
        q_ref[:, sl] = (rot(q[:, sl]) * (A_HEAD_DIM ** -0.5 * LOG2E)).astype(BF16)
    bb_ref[...] = proj(EC_BB, EC_BC).astype(BF16)
    u_ref[...] = (proj(EC_BC, EC_BX) * proj(EC_BX, EC_Z)).astype(BF16)
    z_ref[...] = _silu(proj(EC_Z, EC_END)).astype(BF16)


def _even_in(x, sh, sc, nw, w, wvt, cos, sin, *, rope, tm):
    bsz, t, d = x.shape
    nb = sh.shape[0]
    mod_map = (lambda b, i: (b, 0, 0)) if nb > 1 else (lambda b, i: (0, 0, 0))
    tok = lambda width: pl.BlockSpec((None, tm, width), lambda b, i: (b, i, 0))
    const = lambda shape: pl.BlockSpec(shape, lambda b, i: (0,) * len(shape))
    widths = (LANES, None, A_WIDTH, B_WIDTH, B_WIDTH, E_GATE)
    vt_spec = pl.BlockSpec((None, LANES, tm), lambda b, i: (b, 0, i))
    vt_shape = jax.ShapeDtypeStruct((bsz, LANES, t), BF16)
    return pl.pallas_call(
        functools.partial(_even_in_kernel, rope=rope),
        grid=(bsz, t // tm),
        in_specs=[
            tok(d),
            pl.BlockSpec((None, 1, d), mod_map),
            pl.BlockSpec((None, 1, d), mod_map),
            const((1, d)),
            const(w.shape),
            const(wvt.shape),
            pl.BlockSpec((tm, LANES), lambda b, i: (i, 0)),
            pl.BlockSpec((tm, LANES), lambda b, i: (i, 0)),
        ],
        out_specs=[vt_spec if wd is None else tok(wd) for wd in widths],
        out_shape=[vt_shape if wd is None else jax.ShapeDtypeStruct((bsz, t, wd), BF16) for wd in widths],
        compiler_params=_cparams(("parallel", "parallel")),
        name="even_in_rope" if rope else "even_in",
    )(x, sh, sc, nw, w, wvt, cos, sin)


AttnStage = collections.namedtuple("AttnStage", "q k vt mask sink")


def _tree_max(parts):
    while len(parts) > 1:
        parts = [jnp.maximum(parts[i], parts[i + 1]) if i + 1 < len(parts) else parts[i]
                 for i in range(0, len(parts), 2)]
    return parts[0]


def _attention_pipeline(stages, n_chunks, s_ref, write_pair):
    kc = s_ref.shape[2]
    ones = jnp.ones((16, kc), BF16)
    m8 = None
    halves = []
    for i in range(len(stages) + 1):
        if i >= 1:
            prev = stages[i - 1]
            m_prev = jnp.max(m8, axis=0, keepdims=True)
            if prev.sink is not None:
                m_prev = jnp.maximum(m_prev, prev.sink)
            acc = None
        if i < len(stages):
            cur = stages[i]
            q = cur.q()
        for c in range(n_chunks):
            if i < len(stages):
                st = _nt(cur.k(c), q)
                mask = cur.mask(c)
                if mask is not None:
                    st = jnp.where(mask, st, NEG)
                s_ref[i % 2, c] = st
                part = _tree_max([st[r:r + 8, :] for r in range(0, kc, 8)])
                m8 = part if c == 0 else jnp.maximum(m8, part)
            if i >= 1:
                pt = jnp.exp2(s_ref[(i - 1) % 2, c] - m_prev).astype(BF16)
                upd = _dot(jnp.concatenate([prev.vt(c), ones], axis=0), pt)
                acc = upd if c == 0 else acc + upd
        if i >= 1:
            denom = acc[ATT_VDIM:ATT_VDIM + 1]
            if prev.sink is not None:
                denom = denom + jnp.exp2(prev.sink - m_prev)
            halves.append(acc[0:ATT_VDIM] / denom)
            if len(halves) == 2:
                write_pair((i - 1) // 2, jnp.concatenate(halves, axis=0))
                halves = []


def _even_attn_kernel(*refs, window, n_lat):
    if window:
        (sink_ref, q_ref, kp_ref, km_ref, kn_ref, kc_ref, vtp_ref, vtm_ref, vtn_ref, vtc_ref,
         o_ref, s_ref) = refs
    else:
        sink_ref, q_ref, kc_ref, vtc_ref, o_ref, s_ref = refs
    nq, kc = ATT_Q, ATT_KEYS
    lane = lax.broadcasted_iota(jnp.int32, (1, LANES), 1)
    lo = lane < A_HEAD_DIM
    k_chunks = [kc_ref[r:r + kc, :] for r in range(0, kc_ref.shape[0], kc)]
    vt_chunks = [vtc_ref[:, r:r + kc] for r in range(0, kc_ref.shape[0], kc)]
    masks = [None] * len(k_chunks)
    if window:
        half = kc // 2
        k_chunks += [jnp.concatenate([kp_ref[...], km_ref[0:half, :]], axis=0),
                     jnp.concatenate([km_ref[half:kc, :], kn_ref[...]], axis=0)]
        vt_chunks += [jnp.concatenate([vtp_ref[...], vtm_ref[:, 0:half]], axis=1),
                      jnp.concatenate([vtm_ref[:, half:kc], vtn_ref[...]], axis=1)]
        q0 = pl.program_id(1) * nq
        r = lax.broadcasted_iota(jnp.int32, (kc, nq), 0)
        c = lax.broadcasted_iota(jnp.int32, (kc, nq), 1)
        masks += [(jnp.abs(r - half - c) <= WINDOW) & (r + (q0 - half) >= 0),
                  (jnp.abs(r + half - c) <= WINDOW) & (r + (q0 + half) < n_lat)]

    def stage(head):
        j, grp = head % (A_HEADS // 2), head // (A_HEADS // 2)
        keep = lo if grp == 0 else jnp.logical_not(lo)

        def q():
            qs = q_ref[:, j * LANES:(j + 1) * LANES]
            return jnp.where(keep, qs, jnp.zeros_like(qs))

        return AttnStage(q=q, k=lambda ci: k_chunks[ci],
                         vt=lambda ci: vt_chunks[ci][grp * A_HEAD_DIM:(grp + 1) * A_HEAD_DIM, :],
                         mask=lambda ci: masks[ci], sink=sink_ref[head] * LOG2E)

    def write_pair(p, o_t):
        o_ref[:, p * LANES:(p + 1) * LANES] = jnp.transpose(o_t).astype(BF16)

    _attention_pipeline([stage(h) for h in _A_HEAD_ORDER], len(k_chunks), s_ref, write_pair)


def _even_attn(sink, q, k, vt, kc, vtc, *, window):
    bsz, t, _ = q.shape
    n_ctx = kc.shape[1]
    nq, keys = ATT_Q, ATT_KEYS
    assert nq == keys and t % nq == 0 and n_ctx % keys == 0
    half = keys // 2
    last = t // half - 1
    smem = pl.BlockSpec(memory_space=pltpu.SMEM)
    qspec = pl.BlockSpec((None, nq, A_WIDTH), lambda b, i: (b, i, 0))
    kcspec = pl.BlockSpec((None, n_ctx, LANES), lambda b, i: (b, 0, 0))
    vtcspec = pl.BlockSpec((None, LANES, n_ctx), lambda b, i: (b, 0, 0))
    if window:
        prev = lambda i: jnp.maximum(2 * i - 1, 0)
        nxt = lambda i: jnp.minimum(2 * i + 2, last)
        specs = [smem, qspec,
                 pl.BlockSpec((None, half, LANES), lambda b, i: (b, prev(i), 0)),
                 pl.BlockSpec((None, keys, LANES), lambda b, i: (b, i, 0)),
                 pl.BlockSpec((None, half, LANES), lambda b, i: (b, nxt(i), 0)),
                 kcspec,
                 pl.BlockSpec((None, LANES, half), lambda b, i: (b, 0, prev(i))),
                 pl.BlockSpec((None, LANES, keys), lambda b, i: (b, 0, i)),
                 pl.BlockSpec((None, LANES, half), lambda b, i: (b, 0, nxt(i))),
                 vtcspec]
        args = (sink, q, k, k, k, kc, vt, vt, vt, vtc)
    else:
        specs = [smem, qspec, kcspec, vtcspec]
        args = (sink, q, kc, vtc)
    n_chunks = n_ctx // keys + (2 if window else 0)
    return pl.pallas_call(
        functools.partial(_even_attn_kernel, window=window, n_lat=t),
        grid=(bsz, t // nq),
        in_specs=specs,
        out_specs=qspec,
        out_shape=jax.ShapeDtypeStruct((bsz, t, A_WIDTH), BF16),
        scratch_shapes=[pltpu.VMEM((2, n_chunks, keys, nq), F32)],
        compiler_params=_cparams(("parallel", "parallel")),
        name="even_attn_window" if window else "even_attn_ctx",
    )(*args)


def _out_kernel(*refs, conv):
    if conv:
        a_ref, bb_ref, u_ref, up_ref, un_ref, z_ref, cw_ref, w_ref, pw_ref, g_ref, x_ref, o_ref = refs
        tm = u_ref.shape[0]
        i = pl.program_id(1)
        u = u_ref[...].astype(F32)
        prev_row = jnp.where(i > 0, up_ref[15:16, :].astype(F32), 0.0)
        next_row = jnp.where(i < pl.num_programs(1) - 1, un_ref[0:1, :].astype(F32), 0.0)
        row = lax.broadcasted_iota(jnp.int32, (tm, 1), 0)
        u_m1 = jnp.where(row == 0, prev_row, pltpu.roll(u, 1, axis=0))
        u_p1 = jnp.where(row == tm - 1, next_row, pltpu.roll(u, tm - 1, axis=0))
        cw = cw_ref[...]
        mix_b = bb_ref[...].astype(F32) * (u_m1 * cw[0:1, :] + u * cw[1:2, :] + u_p1 * cw[2:3, :])
    else:
        a_ref, b_ref, z_ref, w_ref, pw_ref, g_ref, x_ref, o_ref = refs
        mix_b = b_ref[...].astype(F32)
    wa = a_ref.shape[1]
    ga = (a_ref[...].astype(F32) * z_ref[:, 0:wa].astype(F32)).astype(BF16)
    gb = (mix_b * z_ref[:, wa:].astype(F32)).astype(BF16)
    y = _dot(ga, w_ref[0:wa, :]) + _dot(gb, w_ref[wa:, :])
    o_ref[...] = x_ref[...] + g_ref[...] * _rms(y, pw_ref[...])


def _out_proj(mix, z, cw, w, pw, g, x, *, tm):
    conv = cw is not None
    bsz, t, d = x.shape
    nb = g.shape[0]
    mod_map = (lambda b, i: (b, 0, 0)) if nb > 1 else (lambda b, i: (0, 0, 0))
    tok = lambda width: pl.BlockSpec((None, tm, width), lambda b, i: (b, i, 0))
    const = lambda shape: pl.BlockSpec(shape, lambda b, i: (0,) * len(shape))
    r16 = tm // 16
    if conv:
        a, bb, u = mix
        specs = [tok(a.shape[2]), tok(bb.shape[2]), tok(u.shape[2]),
                 pl.BlockSpec((None, 16, u.shape[2]), lambda b, i: (b, jnp.maximum(i * r16 - 1, 0), 0)),
                 pl.BlockSpec((None, 16, u.shape[2]), lambda b, i: (b, jnp.minimum((i + 1) * r16, t // 16 - 1), 0)),
                 tok(z.shape[2]), const(cw.shape)]
        args = (a, bb, u, u, u, z, cw)
    else:
        a, b2 = mix
        specs = [tok(a.shape[2]), tok(b2.shape[2]), tok(z.shape[2])]
        args = (a, b2, z)
    specs += [const(w.shape), const((1, d)), pl.BlockSpec((None, 1, d), mod_map), tok(d)]
    return pl.pallas_call(
        functools.partial(_out_kernel, conv=conv),
        grid=(bsz, t // tm),
        in_specs=specs,
        out_specs=tok(d),
        out_shape=jax.ShapeDtypeStruct((bsz, t, d), F32),
        compiler_params=_cparams(("parallel", "parallel")),
        name="out_proj_conv" if conv else "out_proj",
    )(*args, w, pw, g, x)


def _odd_in_kernel(*refs, lat):
    if lat:
        (x_ref, sh_ref, sc_ref, nw_ref, w_ref, kvn_ref, wk_ref, wvt_ref, wmvt_ref, cosk_ref, sink_ref,
         qn_ref, wq_ref, cosq_ref, sinq_ref,
         kf_ref, vt_ref, mk_ref, mvt_ref, g2_ref, qf_ref, mq_ref, mo_ref, z_ref) = refs
    else:
        (x_ref, sh_ref, sc_ref, nw_ref, w_ref, kvn_ref, wk_ref, wvt_ref, wmvt_ref,
         kf_ref, vt_ref, mk_ref, mvt_ref, g2_ref) = refs
    h = (_rms(x_ref[...], nw_ref[...]) * (1.0 + sc_ref[...]) + sh_ref[...]).astype(BF16)

    def proj(a, b):
        return _dot(h, w_ref[:, a:b])

    lane = lax.broadcasted_iota(jnp.int32, (1, LANES), 1)
    ckv = _rms(proj(OC_CKV, OC_CKV + C_KV_LORA), kvn_ref[...]).astype(BF16)
    g2 = proj(OC_G2, OC_G2 + LANES)
    g2_ref[...] = g2
    if lat:
        g2 = _rope_slab(g2, cosk_ref[...], sink_ref[...], C_ROPE // 2, lane < C_ROPE // 2)
    kf_ref[...] = _dot(jnp.concatenate([ckv, g2.astype(BF16)], axis=1), wk_ref[...]).astype(BF16)
    vt_ref[...] = _nt(wvt_ref[...], ckv).astype(BF16)
    mk_ref[...] = (proj(OC_MK, OC_MK + M_QKW) * (M_QK ** -0.5)).astype(BF16)
    chunk = mvt_ref.shape[2]
    for cc in range(mvt_ref.shape[0]):
        mvt_ref[cc] = _nt(wmvt_ref[...], h[cc * chunk:(cc + 1) * chunk, :]).astype(BF16)
    if lat:
        cq = _rms(proj(OC_CQ, OC_CQ + C_Q_LORA), qn_ref[...]).astype(BF16)
        qf = _dot(cq, wq_ref[...])
        cos, sin = cosq_ref[...], sinq_ref[...]
        take_up = (lane >= C_NOPE) & (lane < C_NOPE + C_ROPE // 2)
        scale = (C_NOPE + C_ROPE) ** -0.5 * LOG2E
        for hd in range(C_HEADS):
            sl = slice(hd * LANES, (hd + 1) * LANES)
            qf_ref[:, sl] = (_rope_slab(qf[:, sl], cos, sin, C_ROPE // 2, take_up) * scale).astype(BF16)
        mq_ref[...] = proj(OC_MQ, OC_MQ + M_QKW).astype(BF16)
        mo_ref[...] = proj(OC_MO, OC_MO + M_WIDTH).astype(BF16)
        z_ref[...] = _silu(proj(OC_Z, OC_Z + O_GATE)).astype(BF16)


def _odd_in(x, sh, sc, nw, w, kvn, wk, wv, wmvt, lat_args, *, tm, chunk):
    lat = lat_args is not None
    bsz, t, d = x.shape
    nb = sh.shape[0]
    mod_map = (lambda b, i: (b, 0, 0)) if nb > 1 else (lambda b, i: (0, 0, 0))
    tok = lambda width: (pl.BlockSpec((None, tm, width), lambda b, i: (b, i, 0)), (bsz, t, width))
    const = lambda shape: pl.BlockSpec(shape, lambda b, i: (0,) * len(shape))
    tab = pl.BlockSpec((tm, LANES), lambda b, i: (i, 0))
    specs = [tok(d)[0], pl.BlockSpec((None, 1, d), mod_map), pl.BlockSpec((None, 1, d), mod_map),
             const((1, d)), const(w.shape), const(kvn.shape), const(wk.shape), const(wv.shape), const(wmvt.shape)]
    args = [x, sh, sc, nw, w, kvn, wk, wv, wmvt]
    vt = (pl.BlockSpec((None, C_WIDTH, tm), lambda b, i: (b, 0, i)), (bsz, C_WIDTH, t))
    mvt = (pl.BlockSpec((None, tm // chunk, M_WIDTH, chunk), lambda b, i: (b, i, 0, 0)),
           (bsz, t // chunk, M_WIDTH, chunk))
    outs = [(tok(C_HEADS * LANES), BF16), (vt, BF16), (tok(M_QKW), BF16), (mvt, BF16), (tok(LANES), F32)]
    if lat:
        cosk, sink, qn, wq, cosq, sinq = lat_args
        specs += [tab, tab, const(qn.shape), const(wq.shape), tab, tab]
        args += [cosk, sink, qn, wq, cosq, sinq]
        outs += [(tok(C_HEADS * LANES), BF16), (tok(M_QKW), BF16), (tok(M_WIDTH), BF16), (tok(O_GATE), BF16)]
    return pl.pallas_call(
        functools.partial(_odd_in_kernel, lat=lat),
        grid=(bsz, t // tm),
        in_specs=specs,
        out_specs=[o[0][0] for o in outs],
        out_shape=[jax.ShapeDtypeStruct(o[0][1], o[1]) for o in outs],
        compiler_params=_cparams(("parallel", "parallel")),
        name="odd_in_lat" if lat else "odd_in_ctx",
    )(*args)


def _mla_kernel(q_ref, kc_ref, kx_ref, vtc_ref, vtx_ref, o_ref, s_ref):
    nq, kc = ATT_Q, ATT_KEYS
    chunks = ([(kc_ref, vtc_ref, r) for r in range(0, kc_ref.shape[0], kc)]
              + [(kx_ref, vtx_ref, r) for r in range(0, kx_ref.shape[0], kc)])

    def stage(qb, hd):
        sl = slice(hd * LANES, (hd + 1) * LANES)

        def k(ci):
            k_ref, _, r0 = chunks[ci]
            return k_ref[r0:r0 + kc, sl]

        def vt(ci):
            _, vt_ref, r0 = chunks[ci]
            return vt_ref[hd * C_VDIM:(hd + 1) * C_VDIM, r0:r0 + kc]

        return AttnStage(q=lambda: q_ref[qb * nq:(qb + 1) * nq, sl], k=k, vt=vt, mask=lambda ci: None, sink=None)

    def write_pair(p, o_t):
        qb, j = divmod(p, C_HEADS // 2)
        o_ref[qb * nq:(qb + 1) * nq, j * LANES:(j + 1) * LANES] = jnp.transpose(o_t).astype(BF16)

    stages = [stage(qb, hd) for qb in range(q_ref.shape[0] // nq) for hd in range(C_HEADS)]
    _attention_pipeline(stages, len(chunks), s_ref, write_pair)


def _mla(qf, kfc, kfx, vtc, vtx):
    bsz, t, wq = qf.shape
    n_ctx = kfc.shape[1]
    blk = MLA_BLK
    return pl.pallas_call(
        _mla_kernel,
        grid=(bsz, t // blk),
        in_specs=[
            pl.BlockSpec((None, blk, wq), lambda b, i: (b, i, 0)),
            pl.BlockSpec((None, n_ctx, wq), lambda b, i: (b, 0, 0)),
            pl.BlockSpec((None, t, wq), lambda b, i: (b, 0, 0)),
            pl.BlockSpec((None, C_WIDTH, n_ctx), lambda b, i: (b, 0, 0)),
            pl.BlockSpec((None, C_WIDTH, t), lambda b, i: (b, 0, 0)),
        ],
        out_specs=pl.BlockSpec((None, blk, C_WIDTH), lambda b, i: (b, i, 0)),
        out_shape=jax.ShapeDtypeStruct((bsz, t, C_WIDTH), BF16),
        scratch_shapes=[pltpu.VMEM((2, (n_ctx + t) // ATT_KEYS, ATT_KEYS, ATT_Q), F32)],
        compiler_params=_cparams(("parallel", "arbitrary")),
        name="mla_attention",
    )(qf, kfc, kfx, vtc, vtx)


def _mlstm_kernel(mq_ref, mk_ref, mvt_ref, g_ref, mo_ref, mkc_ref, mvtc_ref, gc_ref, bias_ref, hnw_ref,
                  out_ref, s_ref, m_ref, hf_ref, hr_ref):
    nc, ncc, L = mvt_ref.shape[0], mvtc_ref.shape[0], mvt_ref.shape[2]
    row = lax.broadcasted_iota(jnp.int32, (L, L), 0)
    col = lax.broadcasted_iota(jnp.int32, (L, L), 1)
    vis = (row <= col, row >= col)
    tri_f32 = (col <= row).astype(F32)
    lane = lax.broadcasted_iota(jnp.int32, (1, LANES), 1)
    lo = lane < M_QK
    ones_rows = jnp.ones((M_ONES, L), BF16)
    fwd_lanes = lane < GATE_LANE0 + 2 * M_HEADS
    pair = lambda hd: slice((hd // 2) * LANES, (hd // 2 + 1) * LANES)
    vrows = lambda hd: slice(hd * M_V, (hd + 1) * M_V)

    def gates(g):
        gb = g + bias_ref[...]
        ls = jnp.minimum(gb, 0.0) - jnp.log1p(jnp.exp(-jnp.abs(gb)))
        cf = jnp.dot(tri_f32, ls, preferred_element_type=F32, precision=HIGHEST)
        tot = cf[L - 1:L, :]
        b = jnp.where(fwd_lanes, cf, tot - cf + ls)
        li = pltpu.roll(gb, M_HEADS, axis=1)
        return b - li, tot, jnp.transpose(b), jnp.transpose(li)

    def head_lanes(hd, pair_slab):
        keep = lo if hd % 2 == 0 else jnp.logical_not(lo)
        return jnp.where(keep, pair_slab, jnp.zeros_like(pair_slab))

    def scores(u, kh, qh):
        return _nt(jnp.concatenate([kh, s_ref[u].astype(BF16)], axis=0), qh)

    def finish(d, hd, gq, kh, vt_h, big):
        r, tot, b_t, li_t = gq
        u = d * M_HEADS + hd
        fl = GATE_LANE0 + d * 2 * M_HEADS + M_HEADS + hd
        b_row, li_row = b_t[fl:fl + 1, :], li_t[fl:fl + 1, :]
        g, r_col = tot[:, fl:fl + 1], r[:, fl:fl + 1]
        vaug = jnp.concatenate([vt_h, ones_rows], axis=0)
        s_in = s_ref[u]
        m_in = m_ref[u][0:1, 0:1]
        h_t = None
        if big is not None:
            inter = b_row + m_in
            dm = jnp.where(vis[d], b_row - r_col, NEG)
            m_t = jnp.maximum(inter, jnp.max(dm, axis=0, keepdims=True))
            sc = big[0:L] * jnp.exp(dm - m_t)
            res = jnp.exp(inter - m_t) * big[L:] + _dot(vaug, sc.astype(BF16))
            den = jnp.maximum(jnp.abs(res[M_V:M_V + 1]), jnp.exp(-m_t))
            h_t = res[0:M_V] / den
        a_row = g - b_row + li_row
        m_loc = jnp.max(a_row, axis=1, keepdims=True)
        wv = (jnp.exp(a_row - m_loc) * vaug.astype(F32)).astype(BF16)
        s_loc = _dot(wv, kh)
        m_new = jnp.maximum(g + m_in, m_loc)
        s_ref[u] = jnp.exp(g + m_in - m_new) * s_in + jnp.exp(m_loc - m_new) * s_loc
        m_ref[u] = jnp.broadcast_to(m_new, (8, LANES))
        return h_t

    s_ref[...] = jnp.zeros_like(s_ref)
    m_ref[...] = jnp.zeros_like(m_ref)
    for d in range(2):
        for cc in range(ncc):
            jj = cc if d == 0 else ncc - 1 - cc
            gq = gates(gc_ref[jj * L:(jj + 1) * L, :])
            for hd in range(M_HEADS):
                kh = head_lanes(hd, mkc_ref[jj * L:(jj + 1) * L, pair(hd)])
                finish(d, hd, gq, kh, mvtc_ref[jj, vrows(hd), :], None)

    def body(j, carry):
        chunk_of = (j, nc - 1 - j)
        rows = [pl.ds(pl.multiple_of(jj * L, L), L) for jj in chunk_of]
        gq = [gates(g_ref[rows[d], :]) for d in range(2)]
        pending = None
        for d in range(2):
            for hd in range(M_HEADS):
                kh = head_lanes(hd, mk_ref[rows[d], pair(hd)])
                big = scores(d * M_HEADS + hd, kh, head_lanes(hd, mq_ref[rows[d], pair(hd)]))
                if pending is not None:
                    pending()
                dst = hf_ref if d == 0 else hr_ref

                def pending(d=d, hd=hd, kh=kh, big=big, dst=dst):
                    dst[chunk_of[d], hd] = finish(d, hd, gq[d], kh, mvt_ref[chunk_of[d], vrows(hd), :], big)
        pending()
        return carry

    lax.fori_loop(0, nc, body, 0)

    def readout(j, carry):
        rows = pl.ds(pl.multiple_of(j * L, L), L)
        for hd in range(M_HEADS):
            hs_t = hf_ref[j, hd] + hr_ref[j, hd]
            ms = jnp.mean(hs_t * hs_t, axis=0, keepdims=True)
            hs = jnp.transpose(hs_t * lax.rsqrt(ms + NORM_EPS)) * hnw_ref[:, vrows(hd)]
            out_ref[rows, vrows(hd)] = (jax.nn.sigmoid(mo_ref[rows, vrows(hd)].astype(F32)) * hs).astype(BF16)
        return carry

    lax.fori_loop(0, nc, readout, 0)


def _mlstm(mq, mk, mvt, g2, mo, mkc, mvtc, g2c, bias_row, hnw):
    bsz, t, _ = mq.shape
    n_ctx = mkc.shape[1]
    nc, _, chunk = mvt.shape[1:]
    ncc = mvtc.shape[1]
    per_b = lambda rows, width: pl.BlockSpec((None, rows, width), lambda b: (b, 0, 0))
    chunked = lambda n: pl.BlockSpec((None, n, M_WIDTH, chunk), lambda b: (b, 0, 0, 0))
    const = lambda shape: pl.BlockSpec(shape, lambda b: (0,) * len(shape))
    return pl.pallas_call(
        _mlstm_kernel,
        grid=(bsz,),
        in_specs=[per_b(t, M_QKW), per_b(t, M_QKW), chunked(nc), per_b(t, LANES), per_b(t, M_WIDTH),
                  per_b(n_ctx, M_QKW), chunked(ncc), per_b(n_ctx, LANES),
                  const((1, LANES)), const((1, M_WIDTH))],
        out_specs=per_b(t, M_WIDTH),
        out_shape=jax.ShapeDtypeStruct((bsz, t, M_WIDTH), BF16),
        scratch_shapes=[pltpu.VMEM((2 * M_HEADS, M_V + M_ONES, LANES), F32),
                        pltpu.VMEM((2 * M_HEADS, 8, LANES), F32),
                        pltpu.VMEM((nc, M_HEADS, M_V, chunk), F32),
                        pltpu.VMEM((nc, M_HEADS, M_V, chunk), F32)],
        compiler_params=_cparams(("parallel",)),
        name="mlstm",
    )(mq, mk, mvt, g2, mo, mkc, mvtc, g2c, bias_row, hnw)


def _axial_angles(n_lat, rot_dim):
    rows = n_lat // GRID_W
    row = jnp.repeat(jnp.arange(rows), GRID_W).astype(F32)
    col = jnp.tile(jnp.arange(GRID_W), rows).astype(F32)
    n_freq = rot_dim // 4
    inv = ROPE_THETA ** (-jnp.arange(n_freq, dtype=F32) / n_freq)
    ang = jnp.concatenate([row[:, None] * inv, col[:, None] * inv], axis=-1)
    return jnp.cos(ang), jnp.sin(ang)


def _rope_tables(n_lat):
    cos, sin = _axial_angles(n_lat, A_HEAD_DIM)
    cos_a = jnp.tile(jnp.concatenate([cos, cos], axis=-1), (1, 2))
    sin_a = jnp.tile(jnp.concatenate([-sin, sin], axis=-1), (1, 2))
    cos, sin = _axial_angles(n_lat, C_ROPE)
    one = lambda w: jnp.ones((n_lat, w), F32)
    zero = lambda w: jnp.zeros((n_lat, w), F32)
    cos_k = jnp.concatenate([cos, cos, one(LANES - C_ROPE)], axis=-1)
    sin_k = jnp.concatenate([-sin, sin, zero(LANES - C_ROPE)], axis=-1)
    cos_q = jnp.concatenate([one(C_NOPE), cos, cos, one(LANES - C_NOPE - C_ROPE)], axis=-1)
    sin_q = jnp.concatenate([zero(C_NOPE), -sin, sin, zero(LANES - C_NOPE - C_ROPE)], axis=-1)
    return (cos_a, sin_a), (cos_k, sin_k), (cos_q, sin_q)


_A_HEAD_ORDER = (0, 4, 1, 5, 2, 6, 3, 7)


def _perm_heads(w, axis):
    shape = w.shape
    w = w.reshape(shape[:axis] + (A_HEADS, A_HEAD_DIM) + shape[axis + 1:])
    w = jnp.take(w, jnp.array(_A_HEAD_ORDER), axis=axis)
    return w.reshape(shape)


def _even_weights(w_in, w_out):
    offs = np.cumsum((0,) + E_COLS)
    parts = [w_in[:, offs[i]:offs[i + 1]] for i in range(len(E_COLS))]
    parts[2] = _perm_heads(parts[2], 1)
    z = parts[6]
    parts[6] = jnp.concatenate([_perm_heads(z[:, :A_WIDTH], 1), z[:, A_WIDTH:]], axis=1)
    w_out_p = jnp.concatenate([_perm_heads(w_out[:A_WIDTH], 0), w_out[A_WIDTH:]], axis=0)
    wvt = jnp.transpose(parts.pop(1)).astype(BF16)
    return jnp.concatenate(parts, axis=1).astype(BF16), wvt, w_out_p.astype(BF16)


def _odd_weights(w_in, w_uq, w_ukv):
    d = w_in.shape[0]
    offs = np.cumsum((0,) + O_COLS)
    ckv, kr, mk, mv, mg, cq, mq, mo, z = [w_in[:, offs[i]:offs[i + 1]] for i in range(len(O_COLS))]
    g2 = jnp.concatenate([kr, mg, jnp.zeros((d, LANES - C_ROPE - 4 * M_HEADS), w_in.dtype)], axis=1)
    w_p = jnp.concatenate([ckv, g2, mk, cq, mq, mo, z], axis=1).astype(BF16)
    wmvt = jnp.transpose(mv).astype(BF16)
    ukv = w_ukv.reshape(C_KV_LORA, C_HEADS, C_NOPE + C_VDIM)
    wk_top = jnp.pad(ukv[:, :, :C_NOPE], ((0, 0), (0, 0), (0, LANES - C_NOPE)))
    eye = jnp.pad(jnp.eye(C_ROPE, dtype=w_in.dtype), ((0, LANES - C_ROPE), (C_NOPE, LANES - C_NOPE - C_ROPE)))
    wk_bot = jnp.broadcast_to(eye[:, None, :], (LANES, C_HEADS, LANES))
    wk = jnp.concatenate([wk_top, wk_bot], axis=0).reshape(C_KV_LORA + LANES, C_HEADS * LANES).astype(BF16)
    wv = jnp.transpose(ukv[:, :, C_NOPE:].reshape(C_KV_LORA, C_WIDTH)).astype(BF16)
    uq = w_uq.reshape(C_Q_LORA, C_HEADS, C_NOPE + C_ROPE)
    wq = jnp.pad(uq, ((0, 0), (0, 0), (0, LANES - C_NOPE - C_ROPE))).reshape(C_Q_LORA, C_HEADS * LANES).astype(BF16)
    return w_p, wk, wv, wmvt, wq


def kernel(x, c, ctx, c_ctx, mod_w, mod_b, pre_norm_w, post_norm_w, e_w_in, e_sink, e_conv_w, e_w_out,
           o_w_in, o_q_norm_w, o_kv_norm_w, o_w_uq, o_w_ukv, o_i_bias, o_f_bias, o_head_norm_w, o_w_out):
    bsz, n_lat, d = x.shape
    n_ctx = ctx.shape[1]
    assert mod_w.shape[0] == 2 and d == D_MODEL, "built for one even + one odd layer"
    assert n_lat % 512 == 0 and n_ctx % 128 == 0
    tm_lat, tm_ctx = 512, min(256, n_ctx)

    pad = (-(bsz + 1)) % 8
    cc = jnp.concatenate([c, c_ctx[None, :], jnp.zeros((pad, d), F32)], axis=0)
    mod = _modulation(cc, mod_w, mod_b)
    split = lambda l, r0, r1: [mod[l, r0:r1, k * d:(k + 1) * d][:, None, :] for k in range(3)]
    (rope_a, rope_k, rope_q) = _rope_tables(n_lat)
    dummy_tab = jnp.zeros((n_ctx, LANES), F32)

    sh_x, sc_x, g_x = split(0, 0, bsz)
    sh_c, sc_c, g_c = split(0, bsz, bsz + 1)
    w_in, wvt, w_out = _even_weights(e_w_in[0], e_w_out[0])
    nw, pw = pre_norm_w[0][None, :], post_norm_w[0][None, :]
    k, vt, q, bb, u, z = _even_in(x, sh_x, sc_x, nw, w_in, wvt, *rope_a, rope=True, tm=tm_lat)
    k_c, vt_c, q_c, bb_c, u_c, z_c = _even_in(ctx, sh_c, sc_c, nw, w_in, wvt, dummy_tab, dummy_tab,
                                              rope=False, tm=tm_ctx)
    a = _even_attn(e_sink[0], q, k, vt, k_c, vt_c, window=True)
    a_c = _even_attn(e_sink[0], q_c, None, None, k_c, vt_c, window=False)
    x1 = _out_proj((a, bb, u), z, e_conv_w[0], w_out, pw, g_x, x, tm=tm_lat)
    ctx1 = _out_proj((a_c, bb_c, u_c), z_c, e_conv_w[0], w_out, pw, g_c, ctx, tm=tm_ctx)

    sh_x, sc_x, g_x = split(1, 0, bsz)
    sh_c, sc_c, _ = split(1, bsz, bsz + 1)
    w_p, wk, wv, wmvt, wq = _odd_weights(o_w_in[0], o_w_uq[0], o_w_ukv[0])
    nw, pw = pre_norm_w[1][None, :], post_norm_w[1][None, :]
    kvn, qn = o_kv_norm_w[0][None, :], o_q_norm_w[0][None, :]
    chunk = min(MLSTM_CHUNK, n_ctx)
    kf, vt, mk, mvt, g2, qf, mq, mo, z = _odd_in(x1, sh_x, sc_x, nw, w_p, kvn, wk, wv, wmvt,
                                                 (*rope_k, qn, wq, *rope_q), tm=tm_lat, chunk=chunk)
    kf_c, vt_c, mk_c, mvt_c, g2_c = _odd_in(ctx1, sh_c, sc_c, nw, w_p[:, :OC_CTX_END], kvn, wk, wv, wmvt, None,
                                            tm=tm_ctx, chunk=chunk)
    c_out = _mla(qf, kf_c, kf, vt_c, vt)
    gate_bias = jnp.stack([o_i_bias[0], o_f_bias[0]], axis=1).reshape(1, 4 * M_HEADS)
    bias_row = jnp.pad(gate_bias, ((0, 0), (GATE_LANE0, LANES - GATE_LANE0 - 4 * M_HEADS)))
    m_out = _mlstm(mq, mk, mvt, g2, mo, mk_c, mvt_c, g2_c, bias_row, o_head_norm_w[0][None, :])
    return _out_proj((c_out, m_out), z, None, o_w_out[0].astype(BF16), pw, g_x, x1, tm=tm_lat)
```

```python
import collections
import functools

import numpy as np
import jax
import jax.numpy as jnp
from jax import lax
from jax.experimental import pallas as pl
from jax.experimental.pallas import tpu as pltpu

F32 = jnp.float32
BF16 = jnp.bfloat16
HIGHEST = lax.Precision.HIGHEST

LANES = 128
VMEM_LIMIT = 56 * 1024 * 1024

D_MODEL = 1024
GRID_W = 64
ROPE_THETA = 10000.0
NORM_EPS = 1e-6
NEG = -1e30
LOG2E = 1.4426950408889634

A_HEADS, A_KV_HEADS, A_HEAD_DIM = 8, 2, 64
A_WIDTH = A_HEADS * A_HEAD_DIM
WINDOW = 128
B_WIDTH = 512
CONV_W = 3
C_HEADS, C_NOPE, C_ROPE, C_VDIM = 8, 64, 32, 64
C_KV_LORA, C_Q_LORA = 256, 768
C_WIDTH = C_HEADS * C_VDIM
M_HEADS, M_QK, M_V = 4, 64, 128
M_WIDTH = M_HEADS * M_V
E_GATE = A_WIDTH + B_WIDTH
O_GATE = C_WIDTH + M_WIDTH
E_COLS = (128, 128, A_WIDTH, B_WIDTH, B_WIDTH, B_WIDTH, E_GATE)
E_IN = sum(E_COLS)
O_COLS = (C_KV_LORA, C_ROPE, M_HEADS * M_QK, M_WIDTH, 4 * M_HEADS, C_Q_LORA, M_HEADS * M_QK, M_WIDTH, O_GATE)

EC_K, EC_Q, EC_BB, EC_BC, EC_BX, EC_Z, EC_END = 0, 128, 640, 1152, 1664, 2176, 3200
OC_CKV = 0
OC_G2 = 256
OC_MK = 384
OC_CTX_END = 640
OC_CQ = 640
OC_MQ = 1408
OC_MO = 1664
OC_Z = 2176
OC_END = 3200
M_QKW = M_HEADS * M_QK
M_ONES = 16
GATE_LANE0 = 32
MLSTM_CHUNK = 256
ATT_Q = 256
ATT_KEYS = 256
SCORE_ROWS = 128
ATT_VDIM = 64
MLA_BLK = 512


def _nt(a, b):
    return lax.dot_general(a, b, (((1,), (1,)), ((), ())), preferred_element_type=F32)


def _dot(a, b):
    return jnp.dot(a, b, preferred_element_type=F32)


def _silu(z):
    return z * jax.nn.sigmoid(z)


def _rms(x, w):
    ms = jnp.mean(x * x, axis=-1, keepdims=True)
    return x * lax.rsqrt(ms + NORM_EPS) * w


def _rope_slab(x, cos, sin_signed, half, take_up):
    up = pltpu.roll(x, LANES - half, axis=1)
    dn = pltpu.roll(x, half, axis=1)
    return x * cos + jnp.where(take_up, up, dn) * sin_signed


def _cparams(sem):
    return pltpu.CompilerParams(dimension_semantics=sem, vmem_limit_bytes=VMEM_LIMIT)


def _mod_kernel(cc_ref, w_ref, b_ref, o_ref):
    s = _silu(cc_ref[...])
    o_ref[...] = jnp.dot(s, w_ref[...], preferred_element_type=F32, precision=HIGHEST) + b_ref[...]


def _modulation(cc, mod_w, mod_b):
    depth, d, d3 = mod_w.shape
    r = cc.shape[0]
    return pl.pallas_call(
        _mod_kernel,
        grid=(depth, d3 // d),
        in_specs=[
            pl.BlockSpec((r, d), lambda l, j: (0, 0)),
            pl.BlockSpec((None, d, d), lambda l, j: (l, 0, j)),
            pl.BlockSpec((None, 1, d), lambda l, j: (l, 0, j)),
        ],
        out_specs=pl.BlockSpec((None, r, d), lambda l, j: (l, 0, j)),
        out_shape=jax.ShapeDtypeStruct((depth, r, d3), F32),
        compiler_params=_cparams(("arbitrary", "arbitrary")),
        name="modulation",
    )(cc, mod_w, mod_b.reshape(depth, 1, d3))


def _even_in_kernel(x_ref, sh_ref, sc_ref, nw_ref, w_ref, wvt_ref, cos_ref, sin_ref,
                    k_ref, vt_ref, q_ref, bb_ref, u_ref, z_ref, *, rope):
    h = (_rms(x_ref[...], nw_ref[...]) * (1.0 + sc_ref[...]) + sh_ref[...]).astype(BF16)

    def proj(a, b):
        return _dot(h, w_ref[:, a:b])

    if rope:
        cos, sin = cos_ref[...], sin_ref[...]
        lane = lax.broadcasted_iota(jnp.int32, (1, LANES), 1)
        take_up = (lane % A_HEAD_DIM) < (A_HEAD_DIM // 2)
        rot = lambda t: _rope_slab(t, cos, sin, A_HEAD_DIM // 2, take_up)
    else:
        rot = lambda t: t
    k_ref[...] = rot(proj(EC_K, EC_Q)).astype(BF16)
    vt_ref[...] = _nt(wvt_ref[...], h).astype(BF16)
    q = proj(EC_Q, EC_BB)
    for j in range(A_WIDTH // LANES):
        sl = slice(j * LANES, (j + 1) * LANES)
        q_ref[:, sl] = (rot(q[:, sl]) * (A_HEAD_DIM ** -0.5 * LOG2E)).astype(BF16)
    bb_ref[...] = proj(EC_BB, EC_BC).astype(BF16)
    u_ref[...] = (proj(EC_BC, EC_BX) * proj(EC_BX, EC_Z)).astype(BF16)
    z_ref[...] = _silu(proj(EC_Z, EC_END)).astype(BF16)


def _even_in(x, sh, sc, nw, w, wvt, cos, sin, *, rope, tm):
    bsz, t, d = x.shape
    nb = sh.shape[0]
    mod_map = (lambda b, i: (b, 0, 0)) if nb > 1 else (lambda b, i: (0, 0, 0))
    tok = lambda width: pl.BlockSpec((None, tm, width), lambda b, i: (b, i, 0))
    const = lambda shape: pl.BlockSpec(shape, lambda b, i: (0,) * len(shape))
    widths = (LANES, None, A_WIDTH, B_WIDTH, B_WIDTH, E_GATE)
    vt_spec = pl.BlockSpec((None, LANES, tm), lambda b, i: (b, 0, i))
    vt_shape = jax.ShapeDtypeStruct((bsz, LANES, t), BF16)
    return pl.pallas_call(
        functools.partial(_even_in_kernel, rope=rope),
        grid=(bsz, t // tm),
        in_specs=[
            tok(d),
            pl.BlockSpec((None, 1, d), mod_map),
            pl.BlockSpec((None, 1, d), mod_map),
            const((1, d)),
            const(w.shape),
            const(wvt.shape),
            pl.BlockSpec((tm, LANES), lambda b, i: (i, 0)),
            pl.BlockSpec((tm, LANES), lambda b, i: (i, 0)),
        ],
        out_specs=[vt_spec if wd is None else tok(wd) for wd in widths],
        out_shape=[vt_shape if wd is None else jax.ShapeDtypeStruct((bsz, t, wd), BF16) for wd in widths],
        compiler_params=_cparams(("parallel", "parallel")),
        name="even_in_rope" if rope else "even_in",
    )(x, sh, sc, nw, w, wvt, cos, sin)


AttnStage = collections.namedtuple("AttnStage", "q k vt mask sink")


def _tree_max(parts):
    while len(parts) > 1:
        parts = [jnp.maximum(parts[i], parts[i + 1]) if i + 1 < len(parts) else parts[i]
                 for i in range(0, len(parts), 2)]
    return parts[0]


def _attention_pipeline(stages, n_chunks, s_ref, write_pair):
    kc, sr = ATT_KEYS, SCORE_ROWS
    n_keys = s_ref.shape[1]
    ones = jnp.ones((16, n_keys), BF16)
    m8 = None
    halves = []
    for i in range(len(stages) + 1):
        if i >= 1:
            prev = stages[i - 1]
            m_prev = jnp.max(m8, axis=0, keepdims=True)
            if prev.sink is not None:
                m_prev = jnp.maximum(m_prev, prev.sink)
        if i < len(stages):
            cur = stages[i]
            q = cur.q()
            parts = []
            for c in range(n_chunks):
                kch, mask = cur.k(c), cur.mask(c)
                for r in range(0, kc, sr):
                    st = _nt(kch[r:r + sr], q)
                    if mask is not None:
                        st = jnp.where(mask[r:r + sr], st, NEG)
                    s_ref[i % 2, c * kc + r:c * kc + r + sr, :] = st
                    parts += [st[t:t + 8, :] for t in range(0, sr, 8)]
            m8 = _tree_max(parts)
        if i >= 1:
            pt = jnp.exp2(s_ref[(i - 1) % 2] - m_prev).astype(BF16)
            vta = jnp.concatenate([prev.vt(c) for c in range(n_chunks)], axis=1)
            acc = _dot(jnp.concatenate([vta, ones], axis=0), pt)
            denom = acc[ATT_VDIM:ATT_VDIM + 1]
            if prev.sink is not None:
                denom = denom + jnp.exp2(prev.sink - m_prev)
            halves.append(acc[0:ATT_VDIM] / denom)
            if len(halves) == 2:
                write_pair((i - 1) // 2, jnp.concatenate(halves, axis=0))
                halves = []


def _even_attn_kernel(*refs, window, n_lat):
    if window:
        (sink_ref, q_ref, kp_ref, km_ref, kn_ref, kc_ref, vtp_ref, vtm_ref, vtn_ref, vtc_ref,
         o_ref, s_ref) = refs
    else:
        sink_ref, q_ref, kc_ref, vtc_ref, o_ref, s_ref = refs
    nq, kc = ATT_Q, ATT_KEYS
    lane = lax.broadcasted_iota(jnp.int32, (1, LANES), 1)
    lo = lane < A_HEAD_DIM
    k_chunks = [kc_ref[r:r + kc, :] for r in range(0, kc_ref.shape[0], kc)]
    vt_chunks = [vtc_ref[:, r:r + kc] for r in range(0, kc_ref.shape[0], kc)]
    masks = [None] * len(k_chunks)
    if window:
        half = kc // 2
        k_chunks += [jnp.concatenate([kp_ref[...], km_ref[0:half, :]], axis=0),
                     jnp.concatenate([km_ref[half:kc, :], kn_ref[...]], axis=0)]
        vt_chunks += [jnp.concatenate([vtp_ref[...], vtm_ref[:, 0:half]], axis=1),
                      jnp.concatenate([vtm_ref[:, half:kc], vtn_ref[...]], axis=1)]
        q0 = pl.program_id(1) * nq
        r = lax.broadcasted_iota(jnp.int32, (kc, nq), 0)
        c = lax.broadcasted_iota(jnp.int32, (kc, nq), 1)
        masks += [(jnp.abs(r - half - c) <= WINDOW) & (r + (q0 - half) >= 0),
                  (jnp.abs(r + half - c) <= WINDOW) & (r + (q0 + half) < n_lat)]

    def stage(head):
        j, grp = head % (A_HEADS // 2), head // (A_HEADS // 2)
        keep = lo if grp == 0 else jnp.logical_not(lo)

        def q():
            qs = q_ref[:, j * LANES:(j + 1) * LANES]
            return jnp.where(keep, qs, jnp.zeros_like(qs))

        return AttnStage(q=q, k=lambda ci: k_chunks[ci],
                         vt=lambda ci: vt_chunks[ci][grp * A_HEAD_DIM:(grp + 1) * A_HEAD_DIM, :],
                         mask=lambda ci: masks[ci], sink=sink_ref[head] * LOG2E)

    def write_pair(p, o_t):
        o_ref[:, p * LANES:(p + 1) * LANES] = jnp.transpose(o_t).astype(BF16)

    _attention_pipeline([stage(h) for h in _A_HEAD_ORDER], len(k_chunks), s_ref, write_pair)


def _even_attn(sink, q, k, vt, kc, vtc, *, window):
    bsz, t, _ = q.shape
    n_ctx = kc.shape[1]
    nq, keys = ATT_Q, ATT_KEYS
    assert nq == keys and t % nq == 0 and n_ctx % keys == 0
    half = keys // 2
    last = t // half - 1
    smem = pl.BlockSpec(memory_space=pltpu.SMEM)
    qspec = pl.BlockSpec((None, nq, A_WIDTH), lambda b, i: (b, i, 0))
    kcspec = pl.BlockSpec((None, n_ctx, LANES), lambda b, i: (b, 0, 0))
    vtcspec = pl.BlockSpec((None, LANES, n_ctx), lambda b, i: (b, 0, 0))
    if window:
        prev = lambda i: jnp.maximum(2 * i - 1, 0)
        nxt = lambda i: jnp.minimum(2 * i + 2, last)
        specs = [smem, qspec,
                 pl.BlockSpec((None, half, LANES), lambda b, i: (b, prev(i), 0)),
                 pl.BlockSpec((None, keys, LANES), lambda b, i: (b, i, 0)),
                 pl.BlockSpec((None, half, LANES), lambda b, i: (b, nxt(i), 0)),
                 kcspec,
                 pl.BlockSpec((None, LANES, half), lambda b, i: (b, 0, prev(i))),
                 pl.BlockSpec((None, LANES, keys), lambda b, i: (b, 0, i)),
                 pl.BlockSpec((None, LANES, half), lambda b, i: (b, 0, nxt(i))),
                 vtcspec]
        args = (sink, q, k, k, k, kc, vt, vt, vt, vtc)
    else:
        specs = [smem, qspec, kcspec, vtcspec]
        args = (sink, q, kc, vtc)
    n_chunks = n_ctx // keys + (2 if window else 0)
    return pl.pallas_call(
        functools.partial(_even_attn_kernel, window=window, n_lat=t),
        grid=(bsz, t // nq),
        in_specs=specs,
        out_specs=qspec,
        out_shape=jax.ShapeDtypeStruct((bsz, t, A_WIDTH), BF16),
        scratch_shapes=[pltpu.VMEM((2, n_chunks * keys, nq), F32)],
        compiler_params=_cparams(("parallel", "parallel")),
        name="even_attn_window" if window else "even_attn_ctx",
    )(*args)


def _out_kernel(*refs, conv):
    if conv:
        a_ref, bb_ref, u_ref, up_ref, un_ref, z_ref, cw_ref, w_ref, pw_ref, g_ref, x_ref, o_ref = refs
        tm = u_ref.shape[0]
        i = pl.program_id(1)
        u = u_ref[...].astype(F32)
        prev_row = jnp.where(i > 0, up_ref[15:16, :].astype(F32), 0.0)
        next_row = jnp.where(i < pl.num_programs(1) - 1, un_ref[0:1, :].astype(F32), 0.0)
        row = lax.broadcasted_iota(jnp.int32, (tm, 1), 0)
        u_m1 = jnp.where(row == 0, prev_row, pltpu.roll(u, 1, axis=0))
        u_p1 = jnp.where(row == tm - 1, next_row, pltpu.roll(u, tm - 1, axis=0))
        cw = cw_ref[...]
        mix_b = bb_ref[...].astype(F32) * (u_m1 * cw[0:1, :] + u * cw[1:2, :] + u_p1 * cw[2:3, :])
    else:
        a_ref, b_ref, z_ref, w_ref, pw_ref, g_ref, x_ref, o_ref = refs
        mix_b = b_ref[...].astype(F32)
    wa = a_ref.shape[1]
    ga = (a_ref[...].astype(F32) * z_ref[:, 0:wa].astype(F32)).astype(BF16)
    gb = (mix_b * z_ref[:, wa:].astype(F32)).astype(BF16)
    y = _dot(ga, w_ref[0:wa, :]) + _dot(gb, w_ref[wa:, :])
    o_ref[...] = x_ref[...] + g_ref[...] * _rms(y, pw_ref[...])


def _out_proj(mix, z, cw, w, pw, g, x, *, tm):
    conv = cw is not None
    bsz, t, d = x.shape
    nb = g.shape[0]
    mod_map = (lambda b, i: (b, 0, 0)) if nb > 1 else (lambda b, i: (0, 0, 0))
    tok = lambda width: pl.BlockSpec((None, tm, width), lambda b, i: (b, i, 0))
    const = lambda shape: pl.BlockSpec(shape, lambda b, i: (0,) * len(shape))
    r16 = tm // 16
    if conv:
        a, bb, u = mix
        specs = [tok(a.shape[2]), tok(bb.shape[2]), tok(u.shape[2]),
                 pl.BlockSpec((None, 16, u.shape[2]), lambda b, i: (b, jnp.maximum(i * r16 - 1, 0), 0)),
                 pl.BlockSpec((None, 16, u.shape[2]), lambda b, i: (b, jnp.minimum((i + 1) * r16, t // 16 - 1), 0)),
                 tok(z.shape[2]), const(cw.shape)]
        args = (a, bb, u, u, u, z, cw)
    else:
        a, b2 = mix
        specs = [tok(a.shape[2]), tok(b2.shape[2]), tok(z.shape[2])]
        args = (a, b2, z)
    specs += [const(w.shape), const((1, d)), pl.BlockSpec((None, 1, d), mod_map), tok(d)]
    return pl.pallas_call(
        functools.partial(_out_kernel, conv=conv),
        grid=(bsz, t // tm),
        in_specs=specs,
        out_specs=tok(d),
        out_shape=jax.ShapeDtypeStruct((bsz, t, d), F32),
        compiler_params=_cparams(("parallel", "parallel")),
        name="out_proj_conv" if conv else "out_proj",
    )(*args, w, pw, g, x)


def _odd_in_kernel(*refs, lat):
    if lat:
        (x_ref, sh_ref, sc_ref, nw_ref, w_ref, kvn_ref, wk_ref, wvt_ref, wmvt_ref, cosk_ref, sink_ref,
         qn_ref, wq_ref, cosq_ref, sinq_ref,
         kf_ref, vt_ref, mk_ref, mvt_ref, g2_ref, qf_ref, mq_ref, mo_ref, z_ref) = refs
    else:
        (x_ref, sh_ref, sc_ref, nw_ref, w_ref, kvn_ref, wk_ref, wvt_ref, wmvt_ref,
         kf_ref, vt_ref, mk_ref, mvt_ref, g2_ref) = refs
    h = (_rms(x_ref[...], nw_ref[...]) * (1.0 + sc_ref[...]) + sh_ref[...]).astype(BF16)

    def proj(a, b):
        return _dot(h, w_ref[:, a:b])

    lane = lax.broadcasted_iota(jnp.int32, (1, LANES), 1)
    ckv = _rms(proj(OC_CKV, OC_CKV + C_KV_LORA), kvn_ref[...]).astype(BF16)
    g2 = proj(OC_G2, OC_G2 + LANES)
    g2_ref[...] = g2
    if lat:
        g2 = _rope_slab(g2, cosk_ref[...], sink_ref[...], C_ROPE // 2, lane < C_ROPE // 2)
    kf_ref[...] = _dot(jnp.concatenate([ckv, g2.astype(BF16)], axis=1), wk_ref[...]).astype(BF16)
    vt_ref[...] = _nt(wvt_ref[...], ckv).astype(BF16)
    mk_ref[...] = (proj(OC_MK, OC_MK + M_QKW) * (M_QK ** -0.5)).astype(BF16)
    chunk = mvt_ref.shape[2]
    for cc in range(mvt_ref.shape[0]):
        mvt_ref[cc] = _nt(wmvt_ref[...], h[cc * chunk:(cc + 1) * chunk, :]).astype(BF16)
    if lat:
        cq = _rms(proj(OC_CQ, OC_CQ + C_Q_LORA), qn_ref[...]).astype(BF16)
        qf = _dot(cq, wq_ref[...])
        cos, sin = cosq_ref[...], sinq_ref[...]
        take_up = (lane >= C_NOPE) & (lane < C_NOPE + C_ROPE // 2)
        scale = (C_NOPE + C_ROPE) ** -0.5 * LOG2E
        for hd in range(C_HEADS):
            sl = slice(hd * LANES, (hd + 1) * LANES)
            qf_ref[:, sl] = (_rope_slab(qf[:, sl], cos, sin, C_ROPE // 2, take_up) * scale).astype(BF16)
        mq_ref[...] = proj(OC_MQ, OC_MQ + M_QKW).astype(BF16)
        mo_ref[...] = proj(OC_MO, OC_MO + M_WIDTH).astype(BF16)
        z_ref[...] = _silu(proj(OC_Z, OC_Z + O_GATE)).astype(BF16)


def _odd_in(x, sh, sc, nw, w, kvn, wk, wv, wmvt, lat_args, *, tm, chunk):
    lat = lat_args is not None
    bsz, t, d = x.shape
    nb = sh.shape[0]
    mod_map = (lambda b, i: (b, 0, 0)) if nb > 1 else (lambda b, i: (0, 0, 0))
    tok = lambda width: (pl.BlockSpec((None, tm, width), lambda b, i: (b, i, 0)), (bsz, t, width))
    const = lambda shape: pl.BlockSpec(shape, lambda b, i: (0,) * len(shape))
    tab = pl.BlockSpec((tm, LANES), lambda b, i: (i, 0))
    specs = [tok(d)[0], pl.BlockSpec((None, 1, d), mod_map), pl.BlockSpec((None, 1, d), mod_map),
             const((1, d)), const(w.shape), const(kvn.shape), const(wk.shape), const(wv.shape), const(wmvt.shape)]
    args = [x, sh, sc, nw, w, kvn, wk, wv, wmvt]
    vt = (pl.BlockSpec((None, C_WIDTH, tm), lambda b, i: (b, 0, i)), (bsz, C_WIDTH, t))
    mvt = (pl.BlockSpec((None, tm // chunk, M_WIDTH, chunk), lambda b, i: (b, i, 0, 0)),
           (bsz, t // chunk, M_WIDTH, chunk))
    outs = [(tok(C_HEADS * LANES), BF16), (vt, BF16), (tok(M_QKW), BF16), (mvt, BF16), (tok(LANES), F32)]
    if lat:
        cosk, sink, qn, wq, cosq, sinq = lat_args
        specs += [tab, tab, const(qn.shape), const(wq.shape), tab, tab]
        args += [cosk, sink, qn, wq, cosq, sinq]
        outs += [(tok(C_HEADS * LANES), BF16), (tok(M_QKW), BF16), (tok(M_WIDTH), BF16), (tok(O_GATE), BF16)]
    return pl.pallas_call(
        functools.partial(_odd_in_kernel, lat=lat),
        grid=(bsz, t // tm),
        in_specs=specs,
        out_specs=[o[0][0] for o in outs],
        out_shape=[jax.ShapeDtypeStruct(o[0][1], o[1]) for o in outs],
        compiler_params=_cparams(("parallel", "parallel")),
        name="odd_in_lat" if lat else "odd_in_ctx",
    )(*args)


def _mla_kernel(q_ref, kc_ref, kx_ref, vtc_ref, vtx_ref, o_ref, s_ref):
    nq, kc = ATT_Q, ATT_KEYS
    chunks = ([(kc_ref, vtc_ref, r) for r in range(0, kc_ref.shape[0], kc)]
              + [(kx_ref, vtx_ref, r) for r in range(0, kx_ref.shape[0], kc)])

    def stage(qb, hd):
        sl = slice(hd * LANES, (hd + 1) * LANES)

        def k(ci):
            k_ref, _, r0 = chunks[ci]
            return k_ref[r0:r0 + kc, sl]

        def vt(ci):
            _, vt_ref, r0 = chunks[ci]
            return vt_ref[hd * C_VDIM:(hd + 1) * C_VDIM, r0:r0 + kc]

        return AttnStage(q=lambda: q_ref[qb * nq:(qb + 1) * nq, sl], k=k, vt=vt, mask=lambda ci: None, sink=None)

    def write_pair(p, o_t):
        qb, j = divmod(p, C_HEADS // 2)
        o_ref[qb * nq:(qb + 1) * nq, j * LANES:(j + 1) * LANES] = jnp.transpose(o_t).astype(BF16)

    stages = [stage(qb, hd) for qb in range(q_ref.shape[0] // nq) for hd in range(C_HEADS)]
    _attention_pipeline(stages, len(chunks), s_ref, write_pair)


def _mla(qf, kfc, kfx, vtc, vtx):
    bsz, t, wq = qf.shape
    n_ctx = kfc.shape[1]
    blk = MLA_BLK
    return pl.pallas_call(
        _mla_kernel,
        grid=(bsz, t // blk),
        in_specs=[
            pl.BlockSpec((None, blk, wq), lambda b, i: (b, i, 0)),
            pl.BlockSpec((None, n_ctx, wq), lambda b, i: (b, 0, 0)),
            pl.BlockSpec((None, t, wq), lambda b, i: (b, 0, 0)),
            pl.BlockSpec((None, C_WIDTH, n_ctx), lambda b, i: (b, 0, 0)),
            pl.BlockSpec((None, C_WIDTH, t), lambda b, i: (b, 0, 0)),
        ],
        out_specs=pl.BlockSpec((None, blk, C_WIDTH), lambda b, i: (b, i, 0)),
        out_shape=jax.ShapeDtypeStruct((bsz, t, C_WIDTH), BF16),
        scratch_shapes=[pltpu.VMEM((2, n_ctx + t, ATT_Q), F32)],
        compiler_params=_cparams(("parallel", "arbitrary")),
        name="mla_attention",
    )(qf, kfc, kfx, vtc, vtx)


def _mlstm_kernel(mq_ref, mk_ref, mvt_ref, g_ref, mo_ref, mkc_ref, mvtc_ref, gc_ref, bias_ref, hnw_ref,
                  out_ref, s_ref, m_ref, hf_ref, hr_ref):
    nc, ncc, L = mvt_ref.shape[0], mvtc_ref.shape[0], mvt_ref.shape[2]
    row = lax.broadcasted_iota(jnp.int32, (L, L), 0)
    col = lax.broadcasted_iota(jnp.int32, (L, L), 1)
    vis = (row <= col, row >= col)
    tri_f32 = (col <= row).astype(F32)
    lane = lax.broadcasted_iota(jnp.int32, (1, LANES), 1)
    lo = lane < M_QK
    ones_rows = jnp.ones((M_ONES, L), BF16)
    fwd_lanes = lane < GATE_LANE0 + 2 * M_HEADS
    pair = lambda hd: slice((hd // 2) * LANES, (hd // 2 + 1) * LANES)
    vrows = lambda hd: slice(hd * M_V, (hd + 1) * M_V)

    def gates(g):
        gb = g + bias_ref[...]
        ls = jnp.minimum(gb, 0.0) - jnp.log1p(jnp.exp(-jnp.abs(gb)))
        cf = jnp.dot(tri_f32, ls, preferred_element_type=F32, precision=HIGHEST)
        tot = cf[L - 1:L, :]
        b = jnp.where(fwd_lanes, cf, tot - cf + ls)
        li = pltpu.roll(gb, M_HEADS, axis=1)
        return b - li, tot, jnp.transpose(b), jnp.transpose(li)

    def head_lanes(hd, pair_slab):
        keep = lo if hd % 2 == 0 else jnp.logical_not(lo)
        return jnp.where(keep, pair_slab, jnp.zeros_like(pair_slab))

    def scores(u, kh, qh):
        return _nt(jnp.concatenate([kh, s_ref[u].astype(BF16)], axis=0), qh)

    def finish(d, hd, gq, kh, vt_h, big):
        r, tot, b_t, li_t = gq
        u = d * M_HEADS + hd
        fl = GATE_LANE0 + d * 2 * M_HEADS + M_HEADS + hd
        b_row, li_row = b_t[fl:fl + 1, :], li_t[fl:fl + 1, :]
        g, r_col = tot[:, fl:fl + 1], r[:, fl:fl + 1]
        vaug = jnp.concatenate([vt_h, ones_rows], axis=0)
        s_in = s_ref[u]
        m_in = m_ref[u][0:1, 0:1]
        h_t = None
        if big is not None:
            inter = b_row + m_in
            dm = jnp.where(vis[d], b_row - r_col, NEG)
            m_t = jnp.maximum(inter, jnp.max(dm, axis=0, keepdims=True))
            sc = big[0:L] * jnp.exp(dm - m_t)
            res = jnp.exp(inter - m_t) * big[L:] + _dot(vaug, sc.astype(BF16))
            den = jnp.maximum(jnp.abs(res[M_V:M_V + 1]), jnp.exp(-m_t))
            h_t = res[0:M_V] / den
        a_row = g - b_row + li_row
        m_loc = jnp.max(a_row, axis=1, keepdims=True)
        wv = (jnp.exp(a_row - m_loc) * vaug.astype(F32)).astype(BF16)
        s_loc = _dot(wv, kh)
        m_new = jnp.maximum(g + m_in, m_loc)
        s_ref[u] = jnp.exp(g + m_in - m_new) * s_in + jnp.exp(m_loc - m_new) * s_loc
        m_ref[u] = jnp.broadcast_to(m_new, (8, LANES))
        return h_t

    s_ref[...] = jnp.zeros_like(s_ref)
    m_ref[...] = jnp.zeros_like(m_ref)
    for d in range(2):
        for cc in range(ncc):
            jj = cc if d == 0 else ncc - 1 - cc
            gq = gates(gc_ref[jj * L:(jj + 1) * L, :])
            for hd in range(M_HEADS):
                kh = head_lanes(hd, mkc_ref[jj * L:(jj + 1) * L, pair(hd)])
                finish(d, hd, gq, kh, mvtc_ref[jj, vrows(hd), :], None)

    def body(j, carry):
        chunk_of = (j, nc - 1 - j)
        rows = [pl.ds(pl.multiple_of(jj * L, L), L) for jj in chunk_of]
        gq = [gates(g_ref[rows[d], :]) for d in range(2)]
        pending = None
        for d in range(2):
            for hd in range(M_HEADS):
                kh = head_lanes(hd, mk_ref[rows[d], pair(hd)])
                big = scores(d * M_HEADS + hd, kh, head_lanes(hd, mq_ref[rows[d], pair(hd)]))
                if pending is not None:
                    pending()
                dst = hf_ref if d == 0 else hr_ref

                def pending(d=d, hd=hd, kh=kh, big=big, dst=dst):
                    dst[chunk_of[d], hd] = finish(d, hd, gq[d], kh, mvt_ref[chunk_of[d], vrows(hd), :], big)
        pending()
        return carry

    lax.fori_loop(0, nc, body, 0)

    def readout(j, carry):
        rows = pl.ds(pl.multiple_of(j * L, L), L)
        for hd in range(M_HEADS):
            hs_t = hf_ref[j, hd] + hr_ref[j, hd]
            ms = jnp.mean(hs_t * hs_t, axis=0, keepdims=True)
            hs = jnp.transpose(hs_t * lax.rsqrt(ms + NORM_EPS)) * hnw_ref[:, vrows(hd)]
            out_ref[rows, vrows(hd)] = (jax.nn.sigmoid(mo_ref[rows, vrows(hd)].astype(F32)) * hs).astype(BF16)
        return carry

    lax.fori_loop(0, nc, readout, 0)


def _mlstm(mq, mk, mvt, g2, mo, mkc, mvtc, g2c, bias_row, hnw):
    bsz, t, _ = mq.shape
    n_ctx = mkc.shape[1]
    nc, _, chunk = mvt.shape[1:]
    ncc = mvtc.shape[1]
    per_b = lambda rows, width: pl.BlockSpec((None, rows, width), lambda b: (b, 0, 0))
    chunked = lambda n: pl.BlockSpec((None, n, M_WIDTH, chunk), lambda b: (b, 0, 0, 0))
    const = lambda shape: pl.BlockSpec(shape, lambda b: (0,) * len(shape))
    return pl.pallas_call(
        _mlstm_kernel,
        grid=(bsz,),
        in_specs=[per_b(t, M_QKW), per_b(t, M_QKW), chunked(nc), per_b(t, LANES), per_b(t, M_WIDTH),
                  per_b(n_ctx, M_QKW), chunked(ncc), per_b(n_ctx, LANES),
                  const((1, LANES)), const((1, M_WIDTH))],
        out_specs=per_b(t, M_WIDTH),
        out_shape=jax.ShapeDtypeStruct((bsz, t, M_WIDTH), BF16),
        scratch_shapes=[pltpu.VMEM((2 * M_HEADS, M_V + M_ONES, LANES), F32),
                        pltpu.VMEM((2 * M_HEADS, 8, LANES), F32),
                        pltpu.VMEM((nc, M_HEADS, M_V, chunk), F32),
                        pltpu.VMEM((nc, M_HEADS, M_V, chunk), F32)],
        compiler_params=_cparams(("parallel",)),
        name="mlstm",
    )(mq, mk, mvt, g2, mo, mkc, mvtc, g2c, bias_row, hnw)


def _axial_angles(n_lat, rot_dim):
    rows = n_lat // GRID_W
    row = jnp.repeat(jnp.arange(rows), GRID_W).astype(F32)
    col = jnp.tile(jnp.arange(GRID_W), rows).astype(F32)
    n_freq = rot_dim // 4
    inv = ROPE_THETA ** (-jnp.arange(n_freq, dtype=F32) / n_freq)
    ang = jnp.concatenate([row[:, None] * inv, col[:, None] * inv], axis=-1)
    return jnp.cos(ang), jnp.sin(ang)


def _rope_tables(n_lat):
    cos, sin = _axial_angles(n_lat, A_HEAD_DIM)
    cos_a = jnp.tile(jnp.concatenate([cos, cos], axis=-1), (1, 2))
    sin_a = jnp.tile(jnp.concatenate([-sin, sin], axis=-1), (1, 2))
    cos, sin = _axial_angles(n_lat, C_ROPE)
    one = lambda w: jnp.ones((n_lat, w), F32)
    zero = lambda w: jnp.zeros((n_lat, w), F32)
    cos_k = jnp.concatenate([cos, cos, one(LANES - C_ROPE)], axis=-1)
    sin_k = jnp.concatenate([-sin, sin, zero(LANES - C_ROPE)], axis=-1)
    cos_q = jnp.concatenate([one(C_NOPE), cos, cos, one(LANES - C_NOPE - C_ROPE)], axis=-1)
    sin_q = jnp.concatenate([zero(C_NOPE), -sin, sin, zero(LANES - C_NOPE - C_ROPE)], axis=-1)
    return (cos_a, sin_a), (cos_k, sin_k), (cos_q, sin_q)


_A_HEAD_ORDER = (0, 4, 1, 5, 2, 6, 3, 7)


def _perm_heads(w, axis):
    shape = w.shape
    w = w.reshape(shape[:axis] + (A_HEADS, A_HEAD_DIM) + shape[axis + 1:])
    w = jnp.take(w, jnp.array(_A_HEAD_ORDER), axis=axis)
    return w.reshape(shape)


def _even_weights(w_in, w_out):
    offs = np.cumsum((0,) + E_COLS)
    parts = [w_in[:, offs[i]:offs[i + 1]] for i in range(len(E_COLS))]
    parts[2] = _perm_heads(parts[2], 1)
    z = parts[6]
    parts[6] = jnp.concatenate([_perm_heads(z[:, :A_WIDTH], 1), z[:, A_WIDTH:]], axis=1)
    w_out_p = jnp.concatenate([_perm_heads(w_out[:A_WIDTH], 0), w_out[A_WIDTH:]], axis=0)
    wvt = jnp.transpose(parts.pop(1)).astype(BF16)
    return jnp.concatenate(parts, axis=1).astype(BF16), wvt, w_out_p.astype(BF16)


def _odd_weights(w_in, w_uq, w_ukv):
    d = w_in.shape[0]
    offs = np.cumsum((0,) + O_COLS)
    ckv, kr, mk, mv, mg, cq, mq, mo, z = [w_in[:, offs[i]:offs[i + 1]] for i in range(len(O_COLS))]
    g2 = jnp.concatenate([kr, mg, jnp.zeros((d, LANES - C_ROPE - 4 * M_HEADS), w_in.dtype)], axis=1)
    w_p = jnp.concatenate([ckv, g2, mk, cq, mq, mo, z], axis=1).astype(BF16)
    wmvt = jnp.transpose(mv).astype(BF16)
    ukv = w_ukv.reshape(C_KV_LORA, C_HEADS, C_NOPE + C_VDIM)
    wk_top = jnp.pad(ukv[:, :, :C_NOPE], ((0, 0), (0, 0), (0, LANES - C_NOPE)))
    eye = jnp.pad(jnp.eye(C_ROPE, dtype=w_in.dtype), ((0, LANES - C_ROPE), (C_NOPE, LANES - C_NOPE - C_ROPE)))
    wk_bot = jnp.broadcast_to(eye[:, None, :], (LANES, C_HEADS, LANES))
    wk = jnp.concatenate([wk_top, wk_bot], axis=0).reshape(C_KV_LORA + LANES, C_HEADS * LANES).astype(BF16)
    wv = jnp.transpose(ukv[:, :, C_NOPE:].reshape(C_KV_LORA, C_WIDTH)).astype(BF16)
    uq = w_uq.reshape(C_Q_LORA, C_HEADS, C_NOPE + C_ROPE)
    wq = jnp.pad(uq, ((0, 0), (0, 0), (0, LANES - C_NOPE - C_ROPE))).reshape(C_Q_LORA, C_HEADS * LANES).astype(BF16)
    return w_p, wk, wv, wmvt, wq


def kernel(x, c, ctx, c_ctx, mod_w, mod_b, pre_norm_w, post_norm_w, e_w_in, e_sink, e_conv_w, e_w_out,
           o_w_in, o_q_norm_w, o_kv_norm_w, o_w_uq, o_w_ukv, o_i_bias, o_f_bias, o_head_norm_w, o_w_out):
    bsz, n_lat, d = x.shape
    n_ctx = ctx.shape[1]
    assert mod_w.shape[0] == 2 and d == D_MODEL, "built for one even + one odd layer"
    assert n_lat % 512 == 0 and n_ctx % 128 == 0
    tm_lat, tm_ctx = 512, min(256, n_ctx)

    pad = (-(bsz + 1)) % 8
    cc = jnp.concatenate([c, c_ctx[None, :], jnp.zeros((pad, d), F32)], axis=0)
    mod = _modulation(cc, mod_w, mod_b)
    split = lambda l, r0, r1: [mod[l, r0:r1, k * d:(k + 1) * d][:, None, :] for k in range(3)]
    (rope_a, rope_k, rope_q) = _rope_tables(n_lat)
    dummy_tab = jnp.zeros((n_ctx, LANES), F32)

    sh_x, sc_x, g_x = split(0, 0, bsz)
    sh_c, sc_c, g_c = split(0, bsz, bsz + 1)
    w_in, wvt, w_out = _even_weights(e_w_in[0], e_w_out[0])
    nw, pw = pre_norm_w[0][None, :], post_norm_w[0][None, :]
    k, vt, q, bb, u, z = _even_in(x, sh_x, sc_x, nw, w_in, wvt, *rope_a, rope=True, tm=tm_lat)
    k_c, vt_c, q_c, bb_c, u_c, z_c = _even_in(ctx, sh_c, sc_c, nw, w_in, wvt, dummy_tab, dummy_tab,
                                              rope=False, tm=tm_ctx)
    a = _even_attn(e_sink[0], q, k, vt, k_c, vt_c, window=True)
    a_c = _even_attn(e_sink[0], q_c, None, None, k_c, vt_c, window=False)
    x1 = _out_proj((a, bb, u), z, e_conv_w[0], w_out, pw, g_x, x, tm=tm_lat)
    ctx1 = _out_proj((a_c, bb_c, u_c), z_c, e_conv_w[0], w_out, pw, g_c, ctx, tm=tm_ctx)

    sh_x, sc_x, g_x = split(1, 0, bsz)
    sh_c, sc_c, _ = split(1, bsz, bsz + 1)
    w_p, wk, wv, wmvt, wq = _odd_weights(o_w_in[0], o_w_uq[0], o_w_ukv[0])
    nw, pw = pre_norm_w[1][None, :], post_norm_w[1][None, :]
    kvn, qn = o_kv_norm_w[0][None, :], o_q_norm_w[0][None, :]
    chunk = min(MLSTM_CHUNK, n_ctx)
    kf, vt, mk, mvt, g2, qf, mq, mo, z = _odd_in(x1, sh_x, sc_x, nw, w_p, kvn, wk, wv, wmvt,
                                                 (*rope_k, qn, wq, *rope_q), tm=tm_lat, chunk=chunk)
    kf_c, vt_c, mk_c, mvt_c, g2_c = _odd_in(ctx1, sh_c, sc_c, nw, w_p[:, :OC_CTX_END], kvn, wk, wv, wmvt, None,
                                            tm=tm_ctx, chunk=chunk)
    c_out = _mla(qf, kf_c, kf, vt_c, vt)
    gate_bias = jnp.stack([o_i_bias[0], o_f_bias[0]], axis=1).reshape(1, 4 * M_HEADS)
    bias_row = jnp.pad(gate_bias, ((0, 0), (GATE_LANE0, LANES - GATE_LANE0 - 4 * M_HEADS)))
    m_out = _mlstm(mq, mk, mvt, g2, mo, mk_c, mvt_c, g2_c, bias_row, o_head_norm_w[0][None, :])
    return _out_proj((c_out, m_out), z, None, o_w_out[0].astype(BF16), pw, g_x, x1, tm=tm_lat)
```

```python
import collections
import functools

import numpy as np
import jax
import jax.numpy as jnp
from jax import lax
from jax.experimental import pallas as pl
from jax.experimental.pallas import tpu as pltpu

F32 = jnp.float32
BF16 = jnp.bfloat16
HIGHEST = lax.Precision.HIGHEST

LANES = 128
VMEM_LIMIT = 56 * 1024 * 1024

D_MODEL = 1024
GRID_W = 64
ROPE_THETA = 10000.0
NORM_EPS = 1e-6
NEG = -1e30
LOG2E = 1.4426950408889634

A_HEADS, A_KV_HEADS, A_HEAD_DIM = 8, 2, 64
A_WIDTH = A_HEADS * A_HEAD_DIM
WINDOW = 128
B_WIDTH = 512
CONV_W = 3
C_HEADS, C_NOPE, C_ROPE, C_VDIM = 8, 64, 32, 64
C_KV_LORA, C_Q_LORA = 256, 768
C_WIDTH = C_HEADS * C_VDIM
M_HEADS, M_QK, M_V = 4, 64, 128
M_WIDTH = M_HEADS * M_V
E_GATE = A_WIDTH + B_WIDTH
O_GATE = C_WIDTH + M_WIDTH
E_COLS = (128, 128, A_WIDTH, B_WIDTH, B_WIDTH, B_WIDTH, E_GATE)
E_IN = sum(E_COLS)
O_COLS = (C_KV_LORA, C_ROPE, M_HEADS * M_QK, M_WIDTH, 4 * M_HEADS, C_Q_LORA, M_HEADS * M_QK, M_WIDTH, O_GATE)

EC_K, EC_V, EC_Q, EC_BB, EC_BC, EC_BX, EC_Z, EC_END = 0, 128, 256, 768, 1280, 1792, 2304, 3328
OC_CKV = 0
OC_G2 = 256
OC_MK = 384
OC_CTX_END = 640
OC_CQ = 640
OC_MQ = 1408
OC_MO = 1664
OC_Z = 2176
OC_END = 3200
M_QKW = M_HEADS * M_QK
M_ONES = 16
GATE_LANE0 = 32
SUB_ROWS = 256
MLSTM_CHUNK = 256
ATT_Q = 256
ATT_KEYS = 256
SCORE_ROWS = 128
ATT_VDIM = 64
MLA_BLK = 512
ATT_STEP = 512


def _nt(a, b):
    return lax.dot_general(a, b, (((1,), (1,)), ((), ())), preferred_element_type=F32)


def _dot(a, b):
    return jnp.dot(a, b, preferred_element_type=F32)


def _silu(z):
    return z * jax.nn.sigmoid(z)


def _rms(x, w):
    ms = jnp.mean(x * x, axis=-1, keepdims=True)
    return x * lax.rsqrt(ms + NORM_EPS) * w


def _rope_slab(x, cos, sin_signed, half, take_up):
    up = pltpu.roll(x, LANES - half, axis=1)
    dn = pltpu.roll(x, half, axis=1)
    return x * cos + jnp.where(take_up, up, dn) * sin_signed


def _cparams(sem):
    return pltpu.CompilerParams(dimension_semantics=sem, vmem_limit_bytes=VMEM_LIMIT)


def _resident(shape):
    return pl.BlockSpec(shape, lambda b, i: (0,) * len(shape), pipeline_mode=pl.Buffered(1))


def _mod_kernel(cc_ref, w_ref, b_ref, o_ref):
    s = _silu(cc_ref[...])
    o_ref[...] = jnp.dot(s, w_ref[...], preferred_element_type=F32, precision=HIGHEST) + b_ref[...]


def _modulation(cc, mod_w, mod_b):
    depth, d, d3 = mod_w.shape
    r = cc.shape[0]
    return pl.pallas_call(
        _mod_kernel,
        grid=(depth, d3 // d),
        in_specs=[
            pl.BlockSpec((r, d), lambda l, j: (0, 0)),
            pl.BlockSpec((None, d, d), lambda l, j: (l, 0, j)),
            pl.BlockSpec((None, 1, d), lambda l, j: (l, 0, j)),
        ],
        out_specs=pl.BlockSpec((None, r, d), lambda l, j: (l, 0, j)),
        out_shape=jax.ShapeDtypeStruct((depth, r, d3), F32),
        compiler_params=_cparams(("arbitrary", "arbitrary")),
        name="modulation",
    )(cc, mod_w, mod_b.reshape(depth, 1, d3))


def _even_in_kernel(x_ref, sh_ref, sc_ref, nw_ref, w_ref, cos_ref, sin_ref,
                    k_ref, vt_ref, q_ref, bb_ref, u_ref, z_ref, *, rope):
    lane = lax.broadcasted_iota(jnp.int32, (1, LANES), 1)
    take_up = (lane % A_HEAD_DIM) < (A_HEAD_DIM // 2)
    for r0 in range(0, x_ref.shape[0], SUB_ROWS):
        rows = slice(r0, r0 + SUB_ROWS)
        h = (_rms(x_ref[rows, :], nw_ref[...]) * (1.0 + sc_ref[...]) + sh_ref[...]).astype(BF16)
        y_all = _dot(h, w_ref[...])

        def proj(a, b):
            return y_all[:, a:b]

        if rope:
            cos, sin = cos_ref[rows, :], sin_ref[rows, :]
            rot = lambda t: _rope_slab(t, cos, sin, A_HEAD_DIM // 2, take_up)
        else:
            rot = lambda t: t
        k_ref[rows, :] = rot(proj(EC_K, EC_V)).astype(BF16)
        vt_ref[:, rows] = jnp.transpose(proj(EC_V, EC_Q)).astype(BF16)
        q = proj(EC_Q, EC_BB)
        for j in range(A_WIDTH // LANES):
            sl = slice(j * LANES, (j + 1) * LANES)
            q_ref[rows, sl] = (rot(q[:, sl]) * (A_HEAD_DIM ** -0.5 * LOG2E)).astype(BF16)
        bb_ref[rows, :] = proj(EC_BB, EC_BC).astype(BF16)
        u_ref[rows, :] = (proj(EC_BC, EC_BX) * proj(EC_BX, EC_Z)).astype(BF16)
        z_ref[rows, :] = _silu(proj(EC_Z, EC_END)).astype(BF16)


def _even_in(x, sh, sc, nw, w, cos, sin, *, rope, tm):
    bsz, t, d = x.shape
    nb = sh.shape[0]
    mod_map = (lambda b, i: (b, 0, 0)) if nb > 1 else (lambda b, i: (0, 0, 0))
    tok = lambda width: pl.BlockSpec((None, tm, width), lambda b, i: (b, i, 0))
    const = lambda shape: pl.BlockSpec(shape, lambda b, i: (0,) * len(shape))
    widths = (LANES, None, A_WIDTH, B_WIDTH, B_WIDTH, E_GATE)
    vt_spec = pl.BlockSpec((None, LANES, tm), lambda b, i: (b, 0, i))
    vt_shape = jax.ShapeDtypeStruct((bsz, LANES, t), BF16)
    return pl.pallas_call(
        functools.partial(_even_in_kernel, rope=rope),
        grid=(bsz, t // tm),
        in_specs=[
            tok(d),
            pl.BlockSpec((None, 1, d), mod_map),
            pl.BlockSpec((None, 1, d), mod_map),
            const((1, d)),
            _resident(w.shape),
            pl.BlockSpec((tm, LANES), lambda b, i: (i, 0)),
            pl.BlockSpec((tm, LANES), lambda b, i: (i, 0)),
        ],
        out_specs=[vt_spec if wd is None else tok(wd) for wd in widths],
        out_shape=[vt_shape if wd is None else jax.ShapeDtypeStruct((bsz, t, wd), BF16) for wd in widths],
        compiler_params=_cparams(("parallel", "parallel")),
        name="even_in_rope" if rope else "even_in",
    )(x, sh, sc, nw, w, cos, sin)


AttnStage = collections.namedtuple("AttnStage", "q k vt mask sink")


def _tree_max(parts):
    while len(parts) > 1:
        parts = [jnp.maximum(parts[i], parts[i + 1]) if i + 1 < len(parts) else parts[i]
                 for i in range(0, len(parts), 2)]
    return parts[0]


def _attention_pipeline(stages, n_chunks, s_ref, write_pair):
    kc, sr = ATT_KEYS, SCORE_ROWS
    n_keys = s_ref.shape[1]
    ones = jnp.ones((16, n_keys), BF16)
    m8 = None
    halves = []
    for i in range(len(stages) + 1):
        if i >= 1:
            prev = stages[i - 1]
            m_prev = jnp.max(m8, axis=0, keepdims=True)
            if prev.sink is not None:
                m_prev = jnp.maximum(m_prev, prev.sink)
        if i < len(stages):
            cur = stages[i]
            q = cur.q()
            parts = []
            for c in range(n_chunks):
                kch, mask = cur.k(c), cur.mask(c)
                for r in range(0, kc, sr):
                    st = _nt(kch[r:r + sr], q)
                    if mask is not None:
                        st = jnp.where(mask[r:r + sr], st, NEG)
                    s_ref[i % 2, c * kc + r:c * kc + r + sr, :] = st
                    parts += [st[t:t + 8, :] for t in range(0, sr, 8)]
            m8 = _tree_max(parts)
        if i >= 1:
            pt = jnp.exp2(s_ref[(i - 1) % 2] - m_prev).astype(BF16)
            vta = jnp.concatenate([prev.vt(c) for c in range(n_chunks)], axis=1)
            acc = _dot(jnp.concatenate([vta, ones], axis=0), pt)
            denom = acc[ATT_VDIM:ATT_VDIM + 1]
            if prev.sink is not None:
                denom = denom + jnp.exp2(prev.sink - m_prev)
            halves.append(acc[0:ATT_VDIM] / denom)
            if len(halves) == 2:
                write_pair((i - 1) // 2, jnp.concatenate(halves, axis=0))
                halves = []


def _even_attn_kernel(*refs, window, n_lat):
    if window:
        (sink_ref, q_ref, kp_ref, km_ref, kn_ref, kc_ref, vtp_ref, vtm_ref, vtn_ref, vtc_ref,
         o_ref, s_ref) = refs
    else:
        sink_ref, q_ref, kc_ref, vtc_ref, o_ref, s_ref = refs
    nq, kc = ATT_Q, ATT_KEYS
    step, half = q_ref.shape[0], WINDOW
    lane = lax.broadcasted_iota(jnp.int32, (1, LANES), 1)
    lo = lane < A_HEAD_DIM
    ctx_k = [kc_ref[r:r + kc, :] for r in range(0, kc_ref.shape[0], kc)]
    ctx_vt = [vtc_ref[:, r:r + kc] for r in range(0, kc_ref.shape[0], kc)]
    r = lax.broadcasted_iota(jnp.int32, (kc, nq), 0)
    c = lax.broadcasted_iota(jnp.int32, (kc, nq), 1)

    def sub_block(qb):
        k_chunks, vt_chunks, masks = list(ctx_k), list(ctx_vt), [None] * len(ctx_k)
        if window:
            base, q0 = qb * nq, pl.program_id(1) * step + qb * nq
            if qb == 0:
                k1 = jnp.concatenate([kp_ref[...], km_ref[0:half, :]], axis=0)
                vt1 = jnp.concatenate([vtp_ref[...], vtm_ref[:, 0:half]], axis=1)
            else:
                k1, vt1 = km_ref[base - half:base + half, :], vtm_ref[:, base - half:base + half]
            if base + nq == step:
                k2 = jnp.concatenate([km_ref[base + half:step, :], kn_ref[...]], axis=0)
                vt2 = jnp.concatenate([vtm_ref[:, base + half:step], vtn_ref[...]], axis=1)
            else:
                k2, vt2 = km_ref[base + half:base + half + kc, :], vtm_ref[:, base + half:base + half + kc]
            k_chunks += [k1, k2]
            vt_chunks += [vt1, vt2]
            masks += [(jnp.abs(r - half - c) <= WINDOW) & (r + (q0 - half) >= 0),
                      (jnp.abs(r + half - c) <= WINDOW) & (r + (q0 + half) < n_lat)]
        return k_chunks, vt_chunks, masks

    def stage(qb, head, chunks):
        k_chunks, vt_chunks, masks = chunks
        j, grp = head % (A_HEADS // 2), head // (A_HEADS // 2)
        keep = lo if grp == 0 else jnp.logical_not(lo)

        def q():
            qs = q_ref[qb * nq:(qb + 1) * nq, j * LANES:(j + 1) * LANES]
            return jnp.where(keep, qs, jnp.zeros_like(qs))

        return AttnStage(q=q, k=lambda ci: k_chunks[ci],
                         vt=lambda ci: vt_chunks[ci][grp * A_HEAD_DIM:(grp + 1) * A_HEAD_DIM, :],
                         mask=lambda ci: masks[ci], sink=sink_ref[head] * LOG2E)

    def write_pair(p, o_t):
        qb, j = divmod(p, A_HEADS // 2)
        o_ref[qb * nq:(qb + 1) * nq, j * LANES:(j + 1) * LANES] = jnp.transpose(o_t).astype(BF16)

    stages = []
    for qb in range(step // nq):
        chunks = sub_block(qb)
        stages += [stage(qb, h, chunks) for h in _A_HEAD_ORDER]
    _attention_pipeline(stages, len(ctx_k) + (2 if window else 0), s_ref, write_pair)


def _even_attn(sink, q, k, vt, kc, vtc, *, window):
    bsz, t, _ = q.shape
    n_ctx = kc.shape[1]
    nq, keys, half = ATT_Q, ATT_KEYS, WINDOW
    step = min(ATT_STEP, t)
    assert nq == keys == 2 * half and t % step == 0 and n_ctx % keys == 0
    per_step, last = step // half, t // half - 1
    smem = pl.BlockSpec(memory_space=pltpu.SMEM)
    qspec = pl.BlockSpec((None, step, A_WIDTH), lambda b, i: (b, i, 0))
    kcspec = pl.BlockSpec((None, n_ctx, LANES), lambda b, i: (b, 0, 0))
    vtcspec = pl.BlockSpec((None, LANES, n_ctx), lambda b, i: (b, 0, 0))
    if window:
        prev = lambda i: jnp.maximum(per_step * i - 1, 0)
        nxt = lambda i: jnp.minimum(per_step * (i + 1), last)
        specs = [smem, qspec,
                 pl.BlockSpec((None, half, LANES), lambda b, i: (b, prev(i), 0)),
                 pl.BlockSpec((None, step, LANES), lambda b, i: (b, i, 0)),
                 pl.BlockSpec((None, half, LANES), lambda b, i: (b, nxt(i), 0)),
                 kcspec,
                 pl.BlockSpec((None, LANES, half), lambda b, i: (b, 0, prev(i))),
                 pl.BlockSpec((None, LANES, step), lambda b, i: (b, 0, i)),
                 pl.BlockSpec((None, LANES, half), lambda b, i: (b, 0, nxt(i))),
                 vtcspec]
        args = (sink, q, k, k, k, kc, vt, vt, vt, vtc)
    else:
        specs = [smem, qspec, kcspec, vtcspec]
        args = (sink, q, kc, vtc)
    n_chunks = n_ctx // keys + (2 if window else 0)
    return pl.pallas_call(
        functools.partial(_even_attn_kernel, window=window, n_lat=t),
        grid=(bsz, t // step),
        in_specs=specs,
        out_specs=qspec,
        out_shape=jax.ShapeDtypeStruct((bsz, t, A_WIDTH), BF16),
        scratch_shapes=[pltpu.VMEM((2, n_chunks * keys, nq), F32)],
        compiler_params=_cparams(("parallel", "parallel")),
        name="even_attn_window" if window else "even_attn_ctx",
    )(*args)


def _out_kernel(*refs, conv):
    if conv:
        a_ref, bb_ref, u_ref, up_ref, un_ref, z_ref, cw_ref, w_ref, pw_ref, g_ref, x_ref, o_ref = refs
        tm = u_ref.shape[0]
        i = pl.program_id(1)
        u = u_ref[...].astype(F32)
        prev_row = jnp.where(i > 0, up_ref[15:16, :].astype(F32), 0.0)
        next_row = jnp.where(i < pl.num_programs(1) - 1, un_ref[0:1, :].astype(F32), 0.0)
        row = lax.broadcasted_iota(jnp.int32, (tm, 1), 0)
        u_m1 = jnp.where(row == 0, prev_row, pltpu.roll(u, 1, axis=0))
        u_p1 = jnp.where(row == tm - 1, next_row, pltpu.roll(u, tm - 1, axis=0))
        cw = cw_ref[...]
        mix_b = bb_ref[...].astype(F32) * (u_m1 * cw[0:1, :] + u * cw[1:2, :] + u_p1 * cw[2:3, :])
    else:
        a_ref, b_ref, z_ref, w_ref, pw_ref, g_ref, x_ref, o_ref = refs
        mix_b = b_ref[...].astype(F32)
    wa = a_ref.shape[1]
    ga = (a_ref[...].astype(F32) * z_ref[:, 0:wa].astype(F32)).astype(BF16)
    gb = (mix_b * z_ref[:, wa:].astype(F32)).astype(BF16)
    y = _dot(ga, w_ref[0:wa, :]) + _dot(gb, w_ref[wa:, :])
    o_ref[...] = x_ref[...] + g_ref[...] * _rms(y, pw_ref[...])


def _out_proj(mix, z, cw, w, pw, g, x, *, tm):
    conv = cw is not None
    bsz, t, d = x.shape
    nb = g.shape[0]
    mod_map = (lambda b, i: (b, 0, 0)) if nb > 1 else (lambda b, i: (0, 0, 0))
    tok = lambda width: pl.BlockSpec((None, tm, width), lambda b, i: (b, i, 0))
    const = lambda shape: pl.BlockSpec(shape, lambda b, i: (0,) * len(shape))
    r16 = tm // 16
    if conv:
        a, bb, u = mix
        specs = [tok(a.shape[2]), tok(bb.shape[2]), tok(u.shape[2]),
                 pl.BlockSpec((None, 16, u.shape[2]), lambda b, i: (b, jnp.maximum(i * r16 - 1, 0), 0)),
                 pl.BlockSpec((None, 16, u.shape[2]), lambda b, i: (b, jnp.minimum((i + 1) * r16, t // 16 - 1), 0)),
                 tok(z.shape[2]), const(cw.shape)]
        args = (a, bb, u, u, u, z, cw)
    else:
        a, b2 = mix
        specs = [tok(a.shape[2]), tok(b2.shape[2]), tok(z.shape[2])]
        args = (a, b2, z)
    specs += [const(w.shape), const((1, d)), pl.BlockSpec((None, 1, d), mod_map), tok(d)]
    return pl.pallas_call(
        functools.partial(_out_kernel, conv=conv),
        grid=(bsz, t // tm),
        in_specs=specs,
        out_specs=tok(d),
        out_shape=jax.ShapeDtypeStruct((bsz, t, d), F32),
        compiler_params=_cparams(("parallel", "parallel")),
        name="out_proj_conv" if conv else "out_proj",
    )(*args, w, pw, g, x)


def _odd_in_kernel(*refs, lat):
    if lat:
        (x_ref, sh_ref, sc_ref, nw_ref, w_ref, kvn_ref, wk_ref, wvt_ref, wmvt_ref, cosk_ref, sink_ref,
         qn_ref, wq_ref, cosq_ref, sinq_ref,
         kf_ref, vt_ref, mk_ref, mvt_ref, g2_ref, qf_ref, mq_ref, mo_ref, z_ref) = refs
    else:
        (x_ref, sh_ref, sc_ref, nw_ref, w_ref, kvn_ref, wk_ref, wvt_ref, wmvt_ref,
         kf_ref, vt_ref, mk_ref, mvt_ref, g2_ref) = refs
    lane = lax.broadcasted_iota(jnp.int32, (1, LANES), 1)
    chunk = mvt_ref.shape[2]
    for cc in range(mvt_ref.shape[0]):
        rows = slice(cc * chunk, (cc + 1) * chunk)
        h = (_rms(x_ref[rows, :], nw_ref[...]) * (1.0 + sc_ref[...]) + sh_ref[...]).astype(BF16)
        y_all = _dot(h, w_ref[...])

        def proj(a, b):
            return y_all[:, a:b]

        ckv = _rms(proj(OC_CKV, OC_CKV + C_KV_LORA), kvn_ref[...]).astype(BF16)
        g2 = proj(OC_G2, OC_G2 + LANES)
        g2_ref[rows, :] = g2
        if lat:
            g2 = _rope_slab(g2, cosk_ref[rows, :], sink_ref[rows, :], C_ROPE // 2, lane < C_ROPE // 2)
        kf_ref[rows, :] = _dot(jnp.concatenate([ckv, g2.astype(BF16)], axis=1), wk_ref[...]).astype(BF16)
        vt_ref[:, rows] = _nt(wvt_ref[...], ckv).astype(BF16)
        mk_ref[rows, :] = (proj(OC_MK, OC_MK + M_QKW) * (M_QK ** -0.5)).astype(BF16)
        mvt_ref[cc] = _nt(wmvt_ref[...], h).astype(BF16)
        if lat:
            cq = _rms(proj(OC_CQ, OC_CQ + C_Q_LORA), qn_ref[...]).astype(BF16)
            qf = _dot(cq, wq_ref[...])
            cos, sin = cosq_ref[rows, :], sinq_ref[rows, :]
            take_up = (lane >= C_NOPE) & (lane < C_NOPE + C_ROPE // 2)
            scale = (C_NOPE + C_ROPE) ** -0.5 * LOG2E
            for hd in range(C_HEADS):
                sl = slice(hd * LANES, (hd + 1) * LANES)
                qf_ref[rows, sl] = (_rope_slab(qf[:, sl], cos, sin, C_ROPE // 2, take_up) * scale).astype(BF16)
            mq_ref[rows, :] = proj(OC_MQ, OC_MQ + M_QKW).astype(BF16)
            mo_ref[rows, :] = proj(OC_MO, OC_MO + M_WIDTH).astype(BF16)
            z_ref[rows, :] = _silu(proj(OC_Z, OC_Z + O_GATE)).astype(BF16)


def _odd_in(x, sh, sc, nw, w, kvn, wk, wv, wmvt, lat_args, *, tm, chunk):
    lat = lat_args is not None
    bsz, t, d = x.shape
    nb = sh.shape[0]
    mod_map = (lambda b, i: (b, 0, 0)) if nb > 1 else (lambda b, i: (0, 0, 0))
    tok = lambda width: (pl.BlockSpec((None, tm, width), lambda b, i: (b, i, 0)), (bsz, t, width))
    const = lambda shape: pl.BlockSpec(shape, lambda b, i: (0,) * len(shape))
    tab = pl.BlockSpec((tm, LANES), lambda b, i: (i, 0))
    specs = [tok(d)[0], pl.BlockSpec((None, 1, d), mod_map), pl.BlockSpec((None, 1, d), mod_map),
             const((1, d)), _resident(w.shape), const(kvn.shape), _resident(wk.shape), _resident(wv.shape),
             _resident(wmvt.shape)]
    args = [x, sh, sc, nw, w, kvn, wk, wv, wmvt]
    vt = (pl.BlockSpec((None, C_WIDTH, tm), lambda b, i: (b, 0, i)), (bsz, C_WIDTH, t))
    mvt = (pl.BlockSpec((None, tm // chunk, M_WIDTH, chunk), lambda b, i: (b, i, 0, 0)),
           (bsz, t // chunk, M_WIDTH, chunk))
    outs = [(tok(C_HEADS * LANES), BF16), (vt, BF16), (tok(M_QKW), BF16), (mvt, BF16), (tok(LANES), F32)]
    if lat:
        cosk, sink, qn, wq, cosq, sinq = lat_args
        specs += [tab, tab, const(qn.shape), _resident(wq.shape), tab, tab]
        args += [cosk, sink, qn, wq, cosq, sinq]
        outs += [(tok(C_HEADS * LANES), BF16), (tok(M_QKW), BF16), (tok(M_WIDTH), BF16), (tok(O_GATE), BF16)]
    return pl.pallas_call(
        functools.partial(_odd_in_kernel, lat=lat),
        grid=(bsz, t // tm),
        in_specs=specs,
        out_specs=[o[0][0] for o in outs],
        out_shape=[jax.ShapeDtypeStruct(o[0][1], o[1]) for o in outs],
        compiler_params=_cparams(("parallel", "parallel")),
        name="odd_in_lat" if lat else "odd_in_ctx",
    )(*args)


def _mla_kernel(q_ref, kc_ref, kx_ref, vtc_ref, vtx_ref, o_ref, s_ref):
    nq, kc = ATT_Q, ATT_KEYS
    chunks = ([(kc_ref, vtc_ref, r) for r in range(0, kc_ref.shape[0], kc)]
              + [(kx_ref, vtx_ref, r) for r in range(0, kx_ref.shape[0], kc)])

    def stage(qb, hd):
        sl = slice(hd * LANES, (hd + 1) * LANES)

        def k(ci):
            k_ref, _, r0 = chunks[ci]
            return k_ref[r0:r0 + kc, sl]

        def vt(ci):
            _, vt_ref, r0 = chunks[ci]
            return vt_ref[hd * C_VDIM:(hd + 1) * C_VDIM, r0:r0 + kc]

        return AttnStage(q=lambda: q_ref[qb * nq:(qb + 1) * nq, sl], k=k, vt=vt, mask=lambda ci: None, sink=None)

    def write_pair(p, o_t):
        qb, j = divmod(p, C_HEADS // 2)
        o_ref[qb * nq:(qb + 1) * nq, j * LANES:(j + 1) * LANES] = jnp.transpose(o_t).astype(BF16)

    stages = [stage(qb, hd) for qb in range(q_ref.shape[0] // nq) for hd in range(C_HEADS)]
    _attention_pipeline(stages, len(chunks), s_ref, write_pair)


def _mla(qf, kfc, kfx, vtc, vtx):
    bsz, t, wq = qf.shape
    n_ctx = kfc.shape[1]
    blk = MLA_BLK
    return pl.pallas_call(
        _mla_kernel,
        grid=(bsz, t // blk),
        in_specs=[
            pl.BlockSpec((None, blk, wq), lambda b, i: (b, i, 0)),
            pl.BlockSpec((None, n_ctx, wq), lambda b, i: (b, 0, 0)),
            pl.BlockSpec((None, t, wq), lambda b, i: (b, 0, 0)),
            pl.BlockSpec((None, C_WIDTH, n_ctx), lambda b, i: (b, 0, 0)),
            pl.BlockSpec((None, C_WIDTH, t), lambda b, i: (b, 0, 0)),
        ],
        out_specs=pl.BlockSpec((None, blk, C_WIDTH), lambda b, i: (b, i, 0)),
        out_shape=jax.ShapeDtypeStruct((bsz, t, C_WIDTH), BF16),
        scratch_shapes=[pltpu.VMEM((2, n_ctx + t, ATT_Q), F32)],
        compiler_params=_cparams(("parallel", "arbitrary")),
        name="mla_attention",
    )(qf, kfc, kfx, vtc, vtx)


def _mlstm_kernel(mq_ref, mk_ref, mvt_ref, g_ref, mo_ref, mkc_ref, mvtc_ref, gc_ref, bias_ref, hnw_ref,
                  out_ref, s_ref, m_ref, hf_ref, hr_ref):
    nc, ncc, L = mvt_ref.shape[0], mvtc_ref.shape[0], mvt_ref.shape[2]
    row = lax.broadcasted_iota(jnp.int32, (L, L), 0)
    col = lax.broadcasted_iota(jnp.int32, (L, L), 1)
    vis = (row <= col, row >= col)
    tri_f32 = (col <= row).astype(F32)
    lane = lax.broadcasted_iota(jnp.int32, (1, LANES), 1)
    lo = lane < M_QK
    ones_rows = jnp.ones((M_ONES, L), BF16)
    fwd_lanes = lane < GATE_LANE0 + 2 * M_HEADS
    pair = lambda hd: slice((hd // 2) * LANES, (hd // 2 + 1) * LANES)
    vrows = lambda hd: slice(hd * M_V, (hd + 1) * M_V)

    def gates(g):
        gb = g + bias_ref[...]
        ls = jnp.minimum(gb, 0.0) - jnp.log1p(jnp.exp(-jnp.abs(gb)))
        cf = jnp.dot(tri_f32, ls, preferred_element_type=F32, precision=HIGHEST)
        tot = cf[L - 1:L, :]
        b = jnp.where(fwd_lanes, cf, tot - cf + ls)
        li = pltpu.roll(gb, M_HEADS, axis=1)
        return b - li, tot, jnp.transpose(b), jnp.transpose(li)

    def head_lanes(hd, pair_slab):
        keep = lo if hd % 2 == 0 else jnp.logical_not(lo)
        return jnp.where(keep, pair_slab, jnp.zeros_like(pair_slab))

    def scores(u, kh, qh):
        return _nt(jnp.concatenate([kh, s_ref[u].astype(BF16)], axis=0), qh)

    def finish(d, hd, gq, kh, vt_h, big):
        r, tot, b_t, li_t = gq
        u = d * M_HEADS + hd
        fl = GATE_LANE0 + d * 2 * M_HEADS + M_HEADS + hd
        b_row, li_row = b_t[fl:fl + 1, :], li_t[fl:fl + 1, :]
        g, r_col = tot[:, fl:fl + 1], r[:, fl:fl + 1]
        vaug = jnp.concatenate([vt_h, ones_rows], axis=0)
        s_in = s_ref[u]
        m_in = m_ref[u][0:1, 0:1]
        h_t = None
        if big is not None:
            inter = b_row + m_in
            dm = jnp.where(vis[d], b_row - r_col, NEG)
            m_t = jnp.maximum(inter, jnp.max(dm, axis=0, keepdims=True))
            sc = big[0:L] * jnp.exp(dm - m_t)
            res = jnp.exp(inter - m_t) * big[L:] + _dot(vaug, sc.astype(BF16))
            den = jnp.maximum(jnp.abs(res[M_V:M_V + 1]), jnp.exp(-m_t))
            h_t = res[0:M_V] / den
        a_row = g - b_row + li_row
        m_loc = jnp.max(a_row, axis=1, keepdims=True)
        wv = (jnp.exp(a_row - m_loc) * vaug.astype(F32)).astype(BF16)
        s_loc = _dot(wv, kh)
        m_new = jnp.maximum(g + m_in, m_loc)
        s_ref[u] = jnp.exp(g + m_in - m_new) * s_in + jnp.exp(m_loc - m_new) * s_loc
        m_ref[u] = jnp.broadcast_to(m_new, (8, LANES))
        return h_t

    s_ref[...] = jnp.zeros_like(s_ref)
    m_ref[...] = jnp.zeros_like(m_ref)
    for d in range(2):
        for cc in range(ncc):
            jj = cc if d == 0 else ncc - 1 - cc
            gq = gates(gc_ref[jj * L:(jj + 1) * L, :])
            for hd in range(M_HEADS):
                kh = head_lanes(hd, mkc_ref[jj * L:(jj + 1) * L, pair(hd)])
                finish(d, hd, gq, kh, mvtc_ref[jj, vrows(hd), :], None)

    def body(j, carry):
        chunk_of = (j, nc - 1 - j)
        rows = [pl.ds(pl.multiple_of(jj * L, L), L) for jj in chunk_of]
        gq = [gates(g_ref[rows[d], :]) for d in range(2)]
        pending = None
        for d in range(2):
            for hd in range(M_HEADS):
                kh = head_lanes(hd, mk_ref[rows[d], pair(hd)])
                big = scores(d * M_HEADS + hd, kh, head_lanes(hd, mq_ref[rows[d], pair(hd)]))
                if pending is not None:
                    pending()
                dst = hf_ref if d == 0 else hr_ref

                def pending(d=d, hd=hd, kh=kh, big=big, dst=dst):
                    dst[chunk_of[d], hd] = finish(d, hd, gq[d], kh, mvt_ref[chunk_of[d], vrows(hd), :], big)
        pending()
        return carry

    lax.fori_loop(0, nc, body, 0)

    def readout(j, carry):
        rows = pl.ds(pl.multiple_of(j * L, L), L)
        for hd in range(M_HEADS):
            hs_t = hf_ref[j, hd] + hr_ref[j, hd]
            ms = jnp.mean(hs_t * hs_t, axis=0, keepdims=True)
            hs = jnp.transpose(hs_t * lax.rsqrt(ms + NORM_EPS)) * hnw_ref[:, vrows(hd)]
            out_ref[rows, vrows(hd)] = (jax.nn.sigmoid(mo_ref[rows, vrows(hd)].astype(F32)) * hs).astype(BF16)
        return carry

    lax.fori_loop(0, nc, readout, 0)


def _mlstm(mq, mk, mvt, g2, mo, mkc, mvtc, g2c, bias_row, hnw):
    bsz, t, _ = mq.shape
    n_ctx = mkc.shape[1]
    nc, _, chunk = mvt.shape[1:]
    ncc = mvtc.shape[1]
    per_b = lambda rows, width: pl.BlockSpec((None, rows, width), lambda b: (b, 0, 0))
    chunked = lambda n: pl.BlockSpec((None, n, M_WIDTH, chunk), lambda b: (b, 0, 0, 0))
    const = lambda shape: pl.BlockSpec(shape, lambda b: (0,) * len(shape))
    return pl.pallas_call(
        _mlstm_kernel,
        grid=(bsz,),
        in_specs=[per_b(t, M_QKW), per_b(t, M_QKW), chunked(nc), per_b(t, LANES), per_b(t, M_WIDTH),
                  per_b(n_ctx, M_QKW), chunked(ncc), per_b(n_ctx, LANES),
                  const((1, LANES)), const((1, M_WIDTH))],
        out_specs=per_b(t, M_WIDTH),
        out_shape=jax.ShapeDtypeStruct((bsz, t, M_WIDTH), BF16),
        scratch_shapes=[pltpu.VMEM((2 * M_HEADS, M_V + M_ONES, LANES), F32),
                        pltpu.VMEM((2 * M_HEADS, 8, LANES), F32),
                        pltpu.VMEM((nc, M_HEADS, M_V, chunk), F32),
                        pltpu.VMEM((nc, M_HEADS, M_V, chunk), F32)],
        compiler_params=_cparams(("parallel",)),
        name="mlstm",
    )(mq, mk, mvt, g2, mo, mkc, mvtc, g2c, bias_row, hnw)


def _axial_angles(n_lat, rot_dim):
    rows = n_lat // GRID_W
    row = jnp.repeat(jnp.arange(rows), GRID_W).astype(F32)
    col = jnp.tile(jnp.arange(GRID_W), rows).astype(F32)
    n_freq = rot_dim // 4
    inv = ROPE_THETA ** (-jnp.arange(n_freq, dtype=F32) / n_freq)
    ang = jnp.concatenate([row[:, None] * inv, col[:, None] * inv], axis=-1)
    return jnp.cos(ang), jnp.sin(ang)


def _rope_tables(n_lat):
    cos, sin = _axial_angles(n_lat, A_HEAD_DIM)
    cos_a = jnp.tile(jnp.concatenate([cos, cos], axis=-1), (1, 2))
    sin_a = jnp.tile(jnp.concatenate([-sin, sin], axis=-1), (1, 2))
    cos, sin = _axial_angles(n_lat, C_ROPE)
    one = lambda w: jnp.ones((n_lat, w), F32)
    zero = lambda w: jnp.zeros((n_lat, w), F32)
    cos_k = jnp.concatenate([cos, cos, one(LANES - C_ROPE)], axis=-1)
    sin_k = jnp.concatenate([-sin, sin, zero(LANES - C_ROPE)], axis=-1)
    cos_q = jnp.concatenate([one(C_NOPE), cos, cos, one(LANES - C_NOPE - C_ROPE)], axis=-1)
    sin_q = jnp.concatenate([zero(C_NOPE), -sin, sin, zero(LANES - C_NOPE - C_ROPE)], axis=-1)
    return (cos_a, sin_a), (cos_k, sin_k), (cos_q, sin_q)


_A_HEAD_ORDER = (0, 4, 1, 5, 2, 6, 3, 7)


def _perm_heads(w, axis):
    shape = w.shape
    w = w.reshape(shape[:axis] + (A_HEADS, A_HEAD_DIM) + shape[axis + 1:])
    w = jnp.take(w, jnp.array(_A_HEAD_ORDER), axis=axis)
    return w.reshape(shape)


def _even_weights(w_in, w_out):
    offs = np.cumsum((0,) + E_COLS)
    parts = [w_in[:, offs[i]:offs[i + 1]] for i in range(len(E_COLS))]
    parts[2] = _perm_heads(parts[2], 1)
    z = parts[6]
    parts[6] = jnp.concatenate([_perm_heads(z[:, :A_WIDTH], 1), z[:, A_WIDTH:]], axis=1)
    w_out_p = jnp.concatenate([_perm_heads(w_out[:A_WIDTH], 0), w_out[A_WIDTH:]], axis=0)
    return jnp.concatenate(parts, axis=1).astype(BF16), w_out_p.astype(BF16)


def _odd_weights(w_in, w_uq, w_ukv):
    d = w_in.shape[0]
    offs = np.cumsum((0,) + O_COLS)
    ckv, kr, mk, mv, mg, cq, mq, mo, z = [w_in[:, offs[i]:offs[i + 1]] for i in range(len(O_COLS))]
    g2 = jnp.concatenate([kr, mg, jnp.zeros((d, LANES - C_ROPE - 4 * M_HEADS), w_in.dtype)], axis=1)
    w_p = jnp.concatenate([ckv, g2, mk, cq, mq, mo, z], axis=1).astype(BF16)
    wmvt = jnp.transpose(mv).astype(BF16)
    ukv = w_ukv.reshape(C_KV_LORA, C_HEADS, C_NOPE + C_VDIM)
    wk_top = jnp.pad(ukv[:, :, :C_NOPE], ((0, 0), (0, 0), (0, LANES - C_NOPE)))
    eye = jnp.pad(jnp.eye(C_ROPE, dtype=w_in.dtype), ((0, LANES - C_ROPE), (C_NOPE, LANES - C_NOPE - C_ROPE)))
    wk_bot = jnp.broadcast_to(eye[:, None, :], (LANES, C_HEADS, LANES))
    wk = jnp.concatenate([wk_top, wk_bot], axis=0).reshape(C_KV_LORA + LANES, C_HEADS * LANES).astype(BF16)
    wv = jnp.transpose(ukv[:, :, C_NOPE:].reshape(C_KV_LORA, C_WIDTH)).astype(BF16)
    uq = w_uq.reshape(C_Q_LORA, C_HEADS, C_NOPE + C_ROPE)
    wq = jnp.pad(uq, ((0, 0), (0, 0), (0, LANES - C_NOPE - C_ROPE))).reshape(C_Q_LORA, C_HEADS * LANES).astype(BF16)
    return w_p, wk, wv, wmvt, wq


def kernel(x, c, ctx, c_ctx, mod_w, mod_b, pre_norm_w, post_norm_w, e_w_in, e_sink, e_conv_w, e_w_out,
           o_w_in, o_q_norm_w, o_kv_norm_w, o_w_uq, o_w_ukv, o_i_bias, o_f_bias, o_head_norm_w, o_w_out):
    bsz, n_lat, d = x.shape
    n_ctx = ctx.shape[1]
    assert mod_w.shape[0] == 2 and d == D_MODEL, "built for one even + one odd layer"
    assert n_lat % 1024 == 0 and n_ctx % 256 == 0
    tm_in, tm_lat, tm_ctx = 1024, 512, 256

    pad = (-(bsz + 1)) % 8
    cc = jnp.concatenate([c, c_ctx[None, :], jnp.zeros((pad, d), F32)], axis=0)
    mod = _modulation(cc, mod_w, mod_b)
    split = lambda l, r0, r1: [mod[l, r0:r1, k * d:(k + 1) * d][:, None, :] for k in range(3)]
    (rope_a, rope_k, rope_q) = _rope_tables(n_lat)
    dummy_tab = jnp.zeros((n_ctx, LANES), F32)

    sh_x, sc_x, g_x = split(0, 0, bsz)
    sh_c, sc_c, g_c = split(0, bsz, bsz + 1)
    w_in, w_out = _even_weights(e_w_in[0], e_w_out[0])
    nw, pw = pre_norm_w[0][None, :], post_norm_w[0][None, :]
    k, vt, q, bb, u, z = _even_in(x, sh_x, sc_x, nw, w_in, *rope_a, rope=True, tm=tm_in)
    k_c, vt_c, q_c, bb_c, u_c, z_c = _even_in(ctx, sh_c, sc_c, nw, w_in, dummy_tab, dummy_tab,
                                              rope=False, tm=tm_ctx)
    a = _even_attn(e_sink[0], q, k, vt, k_c, vt_c, window=True)
    a_c = _even_attn(e_sink[0], q_c, None, None, k_c, vt_c, window=False)
    x1 = _out_proj((a, bb, u), z, e_conv_w[0], w_out, pw, g_x, x, tm=tm_lat)
    ctx1 = _out_proj((a_c, bb_c, u_c), z_c, e_conv_w[0], w_out, pw, g_c, ctx, tm=tm_ctx)

    sh_x, sc_x, g_x = split(1, 0, bsz)
    sh_c, sc_c, _ = split(1, bsz, bsz + 1)
    w_p, wk, wv, wmvt, wq = _odd_weights(o_w_in[0], o_w_uq[0], o_w_ukv[0])
    nw, pw = pre_norm_w[1][None, :], post_norm_w[1][None, :]
    kvn, qn = o_kv_norm_w[0][None, :], o_q_norm_w[0][None, :]
    chunk = min(MLSTM_CHUNK, n_ctx)
    kf, vt, mk, mvt, g2, qf, mq, mo, z = _odd_in(x1, sh_x, sc_x, nw, w_p, kvn, wk, wv, wmvt,
                                                 (*rope_k, qn, wq, *rope_q), tm=tm_in, chunk=chunk)
    kf_c, vt_c, mk_c, mvt_c, g2_c = _odd_in(ctx1, sh_c, sc_c, nw, w_p[:, :OC_CTX_END], kvn, wk, wv, wmvt, None,
                                            tm=tm_ctx, chunk=chunk)
    c_out = _mla(qf, kf_c, kf, vt_c, vt)
    gate_bias = jnp.stack([o_i_bias[0], o_f_bias[0]], axis=1).reshape(1, 4 * M_HEADS)
    bias_row = jnp.pad(gate_bias, ((0, 0), (GATE_LANE0, LANES - GATE_LANE0 - 4 * M_HEADS)))
    m_out = _mlstm(mq, mk, mvt, g2, mo, mk_c, mvt_c, g2_c, bias_row, o_head_norm_w[0][None, :])
    return _out_proj((c_out, m_out), z, None, o_w_out[0].astype(BF16), pw, g_x, x1, tm=tm_lat)
```

```python
import collections
import functools

import numpy as np
import jax
import jax.numpy as jnp
from jax import lax
from jax.experimental import pallas as pl
from jax.experimental.pallas import tpu as pltpu

F32 = jnp.float32
BF16 = jnp.bfloat16
HIGHEST = lax.Precision.HIGHEST

LANES = 128
VMEM_LIMIT = 56 * 1024 * 1024

D_MODEL = 1024
GRID_W = 64
ROPE_THETA = 10000.0
NORM_EPS = 1e-6
NEG = -1e30
LOG2E = 1.4426950408889634

A_HEADS, A_KV_HEADS, A_HEAD_DIM = 8, 2, 64
A_WIDTH = A_HEADS * A_HEAD_DIM
WINDOW = 128
B_WIDTH = 512
CONV_W = 3
C_HEADS, C_NOPE, C_ROPE, C_VDIM = 8, 64, 32, 64
C_KV_LORA, C_Q_LORA = 256, 768
C_WIDTH = C_HEADS * C_VDIM
M_HEADS, M_QK, M_V = 4, 64, 128
M_WIDTH = M_HEADS * M_V
E_GATE = A_WIDTH + B_WIDTH
O_GATE = C_WIDTH + M_WIDTH
E_COLS = (128, 128, A_WIDTH, B_WIDTH, B_WIDTH, B_WIDTH, E_GATE)
E_IN = sum(E_COLS)
O_COLS = (C_KV_LORA, C_ROPE, M_HEADS * M_QK, M_WIDTH, 4 * M_HEADS, C_Q_LORA, M_HEADS * M_QK, M_WIDTH, O_GATE)

EC_K, EC_V, EC_Q, EC_BB, EC_BC, EC_BX, EC_Z, EC_END = 0, 128, 256, 768, 1280, 1792, 2304, 3328
OC_CKV = 0
OC_G2 = 256
OC_MK = 384
OC_CTX_END = 640
OC_CQ = 640
OC_MQ = 1408
OC_MO = 1664
OC_Z = 2176
OC_END = 3200
M_QKW = M_HEADS * M_QK
M_ONES = 16
GATE_LANE0 = 32
SUB_ROWS = 256
MLSTM_CHUNK = 256
ATT_Q = 256
ATT_KEYS = 256
SCORE_SLOTS = 3
SCORE_ROWS = 128
ATT_VDIM = 64
MLA_BLK = 512
ATT_STEP = 512


def _nt(a, b):
    return lax.dot_general(a, b, (((1,), (1,)), ((), ())), preferred_element_type=F32)


def _dot(a, b):
    return jnp.dot(a, b, preferred_element_type=F32)


def _silu(z):
    return z * jax.nn.sigmoid(z)


def _rms(x, w):
    ms = jnp.mean(x * x, axis=-1, keepdims=True)
    return x * lax.rsqrt(ms + NORM_EPS) * w


def _rope_slab(x, cos, sin_signed, half, take_up):
    up = pltpu.roll(x, LANES - half, axis=1)
    dn = pltpu.roll(x, half, axis=1)
    return x * cos + jnp.where(take_up, up, dn) * sin_signed


def _cparams(sem):
    return pltpu.CompilerParams(dimension_semantics=sem, vmem_limit_bytes=VMEM_LIMIT)


def _resident(shape):
    return pl.BlockSpec(shape, lambda b, i: (0,) * len(shape), pipeline_mode=pl.Buffered(1))


def _mod_kernel(cc_ref, w_ref, b_ref, o_ref):
    s = _silu(cc_ref[...])
    o_ref[...] = jnp.dot(s, w_ref[...], preferred_element_type=F32, precision=HIGHEST) + b_ref[...]


def _modulation(cc, mod_w, mod_b):
    depth, d, d3 = mod_w.shape
    r = cc.shape[0]
    return pl.pallas_call(
        _mod_kernel,
        grid=(depth, d3 // d),
        in_specs=[
            pl.BlockSpec((r, d), lambda l, j: (0, 0)),
            pl.BlockSpec((None, d, d), lambda l, j: (l, 0, j)),
            pl.BlockSpec((None, 1, d), lambda l, j: (l, 0, j)),
        ],
        out_specs=pl.BlockSpec((None, r, d), lambda l, j: (l, 0, j)),
        out_shape=jax.ShapeDtypeStruct((depth, r, d3), F32),
        compiler_params=_cparams(("arbitrary", "arbitrary")),
        name="modulation",
    )(cc, mod_w, mod_b.reshape(depth, 1, d3))


def _even_in_kernel(x_ref, sh_ref, sc_ref, nw_ref, w_ref, cos_ref, sin_ref,
                    k_ref, vt_ref, q_ref, bb_ref, u_ref, z_ref, *, rope):
    lane = lax.broadcasted_iota(jnp.int32, (1, LANES), 1)
    take_up = (lane % A_HEAD_DIM) < (A_HEAD_DIM // 2)
    for r0 in range(0, x_ref.shape[0], SUB_ROWS):
        rows = slice(r0, r0 + SUB_ROWS)
        h = (_rms(x_ref[rows, :], nw_ref[...]) * (1.0 + sc_ref[...]) + sh_ref[...]).astype(BF16)
        y_all = _dot(h, w_ref[...])

        def proj(a, b):
            return y_all[:, a:b]

        if rope:
            cos, sin = cos_ref[rows, :], sin_ref[rows, :]
            rot = lambda t: _rope_slab(t, cos, sin, A_HEAD_DIM // 2, take_up)
        else:
            rot = lambda t: t
        k_ref[rows, :] = rot(proj(EC_K, EC_V)).astype(BF16)
        vt_ref[:, rows] = jnp.transpose(proj(EC_V, EC_Q)).astype(BF16)
        q = proj(EC_Q, EC_BB)
        for j in range(A_WIDTH // LANES):
            sl = slice(j * LANES, (j + 1) * LANES)
            q_ref[rows, sl] = (rot(q[:, sl]) * (A_HEAD_DIM ** -0.5 * LOG2E)).astype(BF16)
        bb_ref[rows, :] = proj(EC_BB, EC_BC).astype(BF16)
        u_ref[rows, :] = (proj(EC_BC, EC_BX) * proj(EC_BX, EC_Z)).astype(BF16)
        z_ref[rows, :] = _silu(proj(EC_Z, EC_END)).astype(BF16)


def _even_in(x, sh, sc, nw, w, cos, sin, *, rope, tm):
    bsz, t, d = x.shape
    nb = sh.shape[0]
    mod_map = (lambda b, i: (b, 0, 0)) if nb > 1 else (lambda b, i: (0, 0, 0))
    tok = lambda width: pl.BlockSpec((None, tm, width), lambda b, i: (b, i, 0))
    const = lambda shape: pl.BlockSpec(shape, lambda b, i: (0,) * len(shape))
    widths = (LANES, None, A_WIDTH, B_WIDTH, B_WIDTH, E_GATE)
    vt_spec = pl.BlockSpec((None, LANES, tm), lambda b, i: (b, 0, i))
    vt_shape = jax.ShapeDtypeStruct((bsz, LANES, t), BF16)
    return pl.pallas_call(
        functools.partial(_even_in_kernel, rope=rope),
        grid=(bsz, t // tm),
        in_specs=[
            tok(d),
            pl.BlockSpec((None, 1, d), mod_map),
            pl.BlockSpec((None, 1, d), mod_map),
            const((1, d)),
            _resident(w.shape),
            pl.BlockSpec((tm, LANES), lambda b, i: (i, 0)),
            pl.BlockSpec((tm, LANES), lambda b, i: (i, 0)),
        ],
        out_specs=[vt_spec if wd is None else tok(wd) for wd in widths],
        out_shape=[vt_shape if wd is None else jax.ShapeDtypeStruct((bsz, t, wd), BF16) for wd in widths],
        compiler_params=_cparams(("parallel", "parallel")),
        name="even_in_rope" if rope else "even_in",
    )(x, sh, sc, nw, w, cos, sin)


AttnStage = collections.namedtuple("AttnStage", "q k vt mask sink")


def _tree_max(parts):
    while len(parts) > 1:
        parts = [jnp.maximum(parts[i], parts[i + 1]) if i + 1 < len(parts) else parts[i]
                 for i in range(0, len(parts), 2)]
    return parts[0]


def _attention_pipeline(stages, n_chunks, s_ref, write_pair):
    kc, sr = ATT_KEYS, SCORE_ROWS
    slots, n_keys = s_ref.shape[0], s_ref.shape[1]
    lag = slots - 1
    ones = jnp.ones((16, n_keys), BF16)
    m8 = {}
    halves = []
    for i in range(len(stages) + lag):
        if i < len(stages):
            cur = stages[i]
            q = cur.q()
            parts = []
            for c in range(n_chunks):
                kch, mask = cur.k(c), cur.mask(c)
                for r in range(0, kc, sr):
                    st = _nt(kch[r:r + sr], q)
                    if mask is not None:
                        st = jnp.where(mask[r:r + sr], st, NEG)
                    s_ref[i % slots, c * kc + r:c * kc + r + sr, :] = st
                    parts += [st[t:t + 8, :] for t in range(0, sr, 8)]
            m8[i] = _tree_max(parts)
        if i >= lag:
            prev = stages[i - lag]
            m_prev = jnp.max(m8.pop(i - lag), axis=0, keepdims=True)
            if prev.sink is not None:
                m_prev = jnp.maximum(m_prev, prev.sink)
            pt = jnp.exp2(s_ref[(i - lag) % slots] - m_prev).astype(BF16)
            vta = jnp.concatenate([prev.vt(c) for c in range(n_chunks)], axis=1)
            acc = _dot(jnp.concatenate([vta, ones], axis=0), pt)
            denom = acc[ATT_VDIM:ATT_VDIM + 1]
            if prev.sink is not None:
                denom = denom + jnp.exp2(prev.sink - m_prev)
            halves.append(acc[0:ATT_VDIM] / denom)
            if len(halves) == 2:
                write_pair((i - lag) // 2, jnp.concatenate(halves, axis=0))
                halves = []


def _even_attn_kernel(*refs, window, n_lat):
    if window:
        (sink_ref, q_ref, kp_ref, km_ref, kn_ref, kc_ref, vtp_ref, vtm_ref, vtn_ref, vtc_ref,
         o_ref, s_ref) = refs
    else:
        sink_ref, q_ref, kc_ref, vtc_ref, o_ref, s_ref = refs
    nq, kc = ATT_Q, ATT_KEYS
    step, half = q_ref.shape[0], WINDOW
    lane = lax.broadcasted_iota(jnp.int32, (1, LANES), 1)
    lo = lane < A_HEAD_DIM
    ctx_k = [kc_ref[r:r + kc, :] for r in range(0, kc_ref.shape[0], kc)]
    ctx_vt = [vtc_ref[:, r:r + kc] for r in range(0, kc_ref.shape[0], kc)]
    r = lax.broadcasted_iota(jnp.int32, (kc, nq), 0)
    c = lax.broadcasted_iota(jnp.int32, (kc, nq), 1)

    def sub_block(qb):
        k_chunks, vt_chunks, masks = list(ctx_k), list(ctx_vt), [None] * len(ctx_k)
        if window:
            base, q0 = qb * nq, pl.program_id(1) * step + qb * nq
            if qb == 0:
                k1 = jnp.concatenate([kp_ref[...], km_ref[0:half, :]], axis=0)
                vt1 = jnp.concatenate([vtp_ref[...], vtm_ref[:, 0:half]], axis=1)
            else:
                k1, vt1 = km_ref[base - half:base + half, :], vtm_ref[:, base - half:base + half]
            if base + nq == step:
                k2 = jnp.concatenate([km_ref[base + half:step, :], kn_ref[...]], axis=0)
                vt2 = jnp.concatenate([vtm_ref[:, base + half:step], vtn_ref[...]], axis=1)
            else:
                k2, vt2 = km_ref[base + half:base + half + kc, :], vtm_ref[:, base + half:base + half + kc]
            k_chunks += [k1, k2]
            vt_chunks += [vt1, vt2]
            masks += [(jnp.abs(r - half - c) <= WINDOW) & (r + (q0 - half) >= 0),
                      (jnp.abs(r + half - c) <= WINDOW) & (r + (q0 + half) < n_lat)]
        return k_chunks, vt_chunks, masks

    def stage(qb, head, chunks):
        k_chunks, vt_chunks, masks = chunks
        j, grp = head % (A_HEADS // 2), head // (A_HEADS // 2)
        keep = lo if grp == 0 else jnp.logical_not(lo)

        def q():
            qs = q_ref[qb * nq:(qb + 1) * nq, j * LANES:(j + 1) * LANES]
            return jnp.where(keep, qs, jnp.zeros_like(qs))

        return AttnStage(q=q, k=lambda ci: k_chunks[ci],
                         vt=lambda ci: vt_chunks[ci][grp * A_HEAD_DIM:(grp + 1) * A_HEAD_DIM, :],
                         mask=lambda ci: masks[ci], sink=sink_ref[head] * LOG2E)

    def write_pair(p, o_t):
        qb, j = divmod(p, A_HEADS // 2)
        o_ref[qb * nq:(qb + 1) * nq, j * LANES:(j + 1) * LANES] = jnp.transpose(o_t).astype(BF16)

    stages = []
    for qb in range(step // nq):
        chunks = sub_block(qb)
        stages += [stage(qb, h, chunks) for h in _A_HEAD_ORDER]
    _attention_pipeline(stages, len(ctx_k) + (2 if window else 0), s_ref, write_pair)


def _even_attn(sink, q, k, vt, kc, vtc, *, window):
    bsz, t, _ = q.shape
    n_ctx = kc.shape[1]
    nq, keys, half = ATT_Q, ATT_KEYS, WINDOW
    step = min(ATT_STEP, t)
    assert nq == keys == 2 * half and t % step == 0 and n_ctx % keys == 0
    per_step, last = step // half, t // half - 1
    smem = pl.BlockSpec(memory_space=pltpu.SMEM)
    qspec = pl.BlockSpec((None, step, A_WIDTH), lambda b, i: (b, i, 0))
    kcspec = pl.BlockSpec((None, n_ctx, LANES), lambda b, i: (b, 0, 0))
    vtcspec = pl.BlockSpec((None, LANES, n_ctx), lambda b, i: (b, 0, 0))
    if window:
        prev = lambda i: jnp.maximum(per_step * i - 1, 0)
        nxt = lambda i: jnp.minimum(per_step * (i + 1), last)
        specs = [smem, qspec,
                 pl.BlockSpec((None, half, LANES), lambda b, i: (b, prev(i), 0)),
                 pl.BlockSpec((None, step, LANES), lambda b, i: (b, i, 0)),
                 pl.BlockSpec((None, half, LANES), lambda b, i: (b, nxt(i), 0)),
                 kcspec,
                 pl.BlockSpec((None, LANES, half), lambda b, i: (b, 0, prev(i))),
                 pl.BlockSpec((None, LANES, step), lambda b, i: (b, 0, i)),
                 pl.BlockSpec((None, LANES, half), lambda b, i: (b, 0, nxt(i))),
                 vtcspec]
        args = (sink, q, k, k, k, kc, vt, vt, vt, vtc)
    else:
        specs = [smem, qspec, kcspec, vtcspec]
        args = (sink, q, kc, vtc)
    n_chunks = n_ctx // keys + (2 if window else 0)
    return pl.pallas_call(
        functools.partial(_even_attn_kernel, window=window, n_lat=t),
        grid=(bsz, t // step),
        in_specs=specs,
        out_specs=qspec,
        out_shape=jax.ShapeDtypeStruct((bsz, t, A_WIDTH), BF16),
        scratch_shapes=[pltpu.VMEM((SCORE_SLOTS, n_chunks * keys, nq), F32)],
        compiler_params=_cparams(("parallel", "parallel")),
        name="even_attn_window" if window else "even_attn_ctx",
    )(*args)


def _out_kernel(*refs, conv):
    if conv:
        a_ref, bb_ref, u_ref, up_ref, un_ref, z_ref, cw_ref, w_ref, pw_ref, g_ref, x_ref, o_ref = refs
        tm = u_ref.shape[0]
        i = pl.program_id(1)
        u = u_ref[...].astype(F32)
        prev_row = jnp.where(i > 0, up_ref[15:16, :].astype(F32), 0.0)
        next_row = jnp.where(i < pl.num_programs(1) - 1, un_ref[0:1, :].astype(F32), 0.0)
        row = lax.broadcasted_iota(jnp.int32, (tm, 1), 0)
        u_m1 = jnp.where(row == 0, prev_row, pltpu.roll(u, 1, axis=0))
        u_p1 = jnp.where(row == tm - 1, next_row, pltpu.roll(u, tm - 1, axis=0))
        cw = cw_ref[...]
        mix_b = bb_ref[...].astype(F32) * (u_m1 * cw[0:1, :] + u * cw[1:2, :] + u_p1 * cw[2:3, :])
    else:
        a_ref, b_ref, z_ref, w_ref, pw_ref, g_ref, x_ref, o_ref = refs
        mix_b = b_ref[...].astype(F32)
    wa = a_ref.shape[1]
    ga = (a_ref[...].astype(F32) * z_ref[:, 0:wa].astype(F32)).astype(BF16)
    gb = (mix_b * z_ref[:, wa:].astype(F32)).astype(BF16)
    y = _dot(ga, w_ref[0:wa, :]) + _dot(gb, w_ref[wa:, :])
    o_ref[...] = x_ref[...] + g_ref[...] * _rms(y, pw_ref[...])


def _out_proj(mix, z, cw, w, pw, g, x, *, tm):
    conv = cw is not None
    bsz, t, d = x.shape
    nb = g.shape[0]
    mod_map = (lambda b, i: (b, 0, 0)) if nb > 1 else (lambda b, i: (0, 0, 0))
    tok = lambda width: pl.BlockSpec((None, tm, width), lambda b, i: (b, i, 0))
    const = lambda shape: pl.BlockSpec(shape, lambda b, i: (0,) * len(shape))
    r16 = tm // 16
    if conv:
        a, bb, u = mix
        specs = [tok(a.shape[2]), tok(bb.shape[2]), tok(u.shape[2]),
                 pl.BlockSpec((None, 16, u.shape[2]), lambda b, i: (b, jnp.maximum(i * r16 - 1, 0), 0)),
                 pl.BlockSpec((None, 16, u.shape[2]), lambda b, i: (b, jnp.minimum((i + 1) * r16, t // 16 - 1), 0)),
                 tok(z.shape[2]), const(cw.shape)]
        args = (a, bb, u, u, u, z, cw)
    else:
        a, b2 = mix
        specs = [tok(a.shape[2]), tok(b2.shape[2]), tok(z.shape[2])]
        args = (a, b2, z)
    specs += [const(w.shape), const((1, d)), pl.BlockSpec((None, 1, d), mod_map), tok(d)]
    return pl.pallas_call(
        functools.partial(_out_kernel, conv=conv),
        grid=(bsz, t // tm),
        in_specs=specs,
        out_specs=tok(d),
        out_shape=jax.ShapeDtypeStruct((bsz, t, d), F32),
        compiler_params=_cparams(("parallel", "parallel")),
        name="out_proj_conv" if conv else "out_proj",
    )(*args, w, pw, g, x)


def _odd_in_kernel(*refs, lat):
    if lat:
        (x_ref, sh_ref, sc_ref, nw_ref, w_ref, kvn_ref, wk_ref, wvt_ref, wmvt_ref, cosk_ref, sink_ref,
         qn_ref, wq_ref, cosq_ref, sinq_ref,
         kf_ref, vt_ref, mk_ref, mvt_ref, g2_ref, qf_ref, mq_ref, mo_ref, z_ref) = refs
    else:
        (x_ref, sh_ref, sc_ref, nw_ref, w_ref, kvn_ref, wk_ref, wvt_ref, wmvt_ref,
         kf_ref, vt_ref, mk_ref, mvt_ref, g2_ref) = refs
    lane = lax.broadcasted_iota(jnp.int32, (1, LANES), 1)
    chunk = mvt_ref.shape[2]
    for cc in range(mvt_ref.shape[0]):
        rows = slice(cc * chunk, (cc + 1) * chunk)
        h = (_rms(x_ref[rows, :], nw_ref[...]) * (1.0 + sc_ref[...]) + sh_ref[...]).astype(BF16)
        y_all = _dot(h, w_ref[...])

        def proj(a, b):
            return y_all[:, a:b]

        ckv = _rms(proj(OC_CKV, OC_CKV + C_KV_LORA), kvn_ref[...]).astype(BF16)
        g2 = proj(OC_G2, OC_G2 + LANES)
        g2_ref[rows, :] = g2
        if lat:
            g2 = _rope_slab(g2, cosk_ref[rows, :], sink_ref[rows, :], C_ROPE // 2, lane < C_ROPE // 2)
        kf_ref[rows, :] = _dot(jnp.concatenate([ckv, g2.astype(BF16)], axis=1), wk_ref[...]).astype(BF16)
        vt_ref[:, rows] = _nt(wvt_ref[...], ckv).astype(BF16)
        mk_ref[rows, :] = (proj(OC_MK, OC_MK + M_QKW) * (M_QK ** -0.5)).astype(BF16)
        mvt_ref[cc] = _nt(wmvt_ref[...], h).astype(BF16)
        if lat:
            cq = _rms(proj(OC_CQ, OC_CQ + C_Q_LORA), qn_ref[...]).astype(BF16)
            qf = _dot(cq, wq_ref[...])
            cos, sin = cosq_ref[rows, :], sinq_ref[rows, :]
            take_up = (lane >= C_NOPE) & (lane < C_NOPE + C_ROPE // 2)
            scale = (C_NOPE + C_ROPE) ** -0.5 * LOG2E
            for hd in range(C_HEADS):
                sl = slice(hd * LANES, (hd + 1) * LANES)
                qf_ref[rows, sl] = (_rope_slab(qf[:, sl], cos, sin, C_ROPE // 2, take_up) * scale).astype(BF16)
            mq_ref[rows, :] = proj(OC_MQ, OC_MQ + M_QKW).astype(BF16)
            mo_ref[rows, :] = proj(OC_MO, OC_MO + M_WIDTH).astype(BF16)
            z_ref[rows, :] = _silu(proj(OC_Z, OC_Z + O_GATE)).astype(BF16)


def _odd_in(x, sh, sc, nw, w, kvn, wk, wv, wmvt, lat_args, *, tm, chunk):
    lat = lat_args is not None
    bsz, t, d = x.shape
    nb = sh.shape[0]
    mod_map = (lambda b, i: (b, 0, 0)) if nb > 1 else (lambda b, i: (0, 0, 0))
    tok = lambda width: (pl.BlockSpec((None, tm, width), lambda b, i: (b, i, 0)), (bsz, t, width))
    const = lambda shape: pl.BlockSpec(shape, lambda b, i: (0,) * len(shape))
    tab = pl.BlockSpec((tm, LANES), lambda b, i: (i, 0))
    specs = [tok(d)[0], pl.BlockSpec((None, 1, d), mod_map), pl.BlockSpec((None, 1, d), mod_map),
             const((1, d)), _resident(w.shape), const(kvn.shape), _resident(wk.shape), _resident(wv.shape),
             _resident(wmvt.shape)]
    args = [x, sh, sc, nw, w, kvn, wk, wv, wmvt]
    vt = (pl.BlockSpec((None, C_WIDTH, tm), lambda b, i: (b, 0, i)), (bsz, C_WIDTH, t))
    mvt = (pl.BlockSpec((None, tm // chunk, M_WIDTH, chunk), lambda b, i: (b, i, 0, 0)),
           (bsz, t // chunk, M_WIDTH, chunk))
    outs = [(tok(C_HEADS * LANES), BF16), (vt, BF16), (tok(M_QKW), BF16), (mvt, BF16), (tok(LANES), F32)]
    if lat:
        cosk, sink, qn, wq, cosq, sinq = lat_args
        specs += [tab, tab, const(qn.shape), _resident(wq.shape), tab, tab]
        args += [cosk, sink, qn, wq, cosq, sinq]
        outs += [(tok(C_HEADS * LANES), BF16), (tok(M_QKW), BF16), (tok(M_WIDTH), BF16), (tok(O_GATE), BF16)]
    return pl.pallas_call(
        functools.partial(_odd_in_kernel, lat=lat),
        grid=(bsz, t // tm),
        in_specs=specs,
        out_specs=[o[0][0] for o in outs],
        out_shape=[jax.ShapeDtypeStruct(o[0][1], o[1]) for o in outs],
        compiler_params=_cparams(("parallel", "parallel")),
        name="odd_in_lat" if lat else "odd_in_ctx",
    )(*args)


def _mla_kernel(q_ref, kc_ref, vtc_ref, *rest):
    n_pairs = C_HEADS // 2
    kx_refs, vtx_refs, (o_ref, s_ref) = rest[:n_pairs], rest[n_pairs:2 * n_pairs], rest[2 * n_pairs:]
    nq, kc = ATT_Q, ATT_KEYS
    n_ctx_chunks = kc_ref.shape[0] // kc
    n_chunks = n_ctx_chunks + kx_refs[0].shape[0] // kc

    def stage(qb, hd):
        sl = slice(hd * LANES, (hd + 1) * LANES)
        half = hd % 2

        def k(ci):
            if ci < n_ctx_chunks:
                return kc_ref[ci * kc:(ci + 1) * kc, sl]
            r0 = (ci - n_ctx_chunks) * kc
            return kx_refs[hd // 2][r0:r0 + kc, half * LANES:(half + 1) * LANES]

        def vt(ci):
            if ci < n_ctx_chunks:
                return vtc_ref[hd * C_VDIM:(hd + 1) * C_VDIM, ci * kc:(ci + 1) * kc]
            r0 = (ci - n_ctx_chunks) * kc
            return vtx_refs[hd // 2][half * C_VDIM:(half + 1) * C_VDIM, r0:r0 + kc]

        return AttnStage(q=lambda: q_ref[qb * nq:(qb + 1) * nq, sl], k=k, vt=vt, mask=lambda ci: None, sink=None)

    def write_pair(p, o_t):
        qb, j = divmod(p, C_HEADS // 2)
        o_ref[qb * nq:(qb + 1) * nq, j * LANES:(j + 1) * LANES] = jnp.transpose(o_t).astype(BF16)

    stages = [stage(qb, hd) for qb in range(q_ref.shape[0] // nq) for hd in range(C_HEADS)]
    _attention_pipeline(stages, n_chunks, s_ref, write_pair)


def _mla(qf, kfc, kfx, vtc, vtx):
    bsz, t, wq = qf.shape
    n_ctx = kfc.shape[1]
    blk = MLA_BLK
    n_pairs = C_HEADS // 2
    pair_k = lambda j: pl.BlockSpec((None, t, 2 * LANES), lambda b, i: (b, 0, j))
    pair_vt = lambda j: pl.BlockSpec((None, 2 * C_VDIM, t), lambda b, i: (b, j, 0))
    return pl.pallas_call(
        _mla_kernel,
        grid=(bsz, t // blk),
        in_specs=[
            pl.BlockSpec((None, blk, wq), lambda b, i: (b, i, 0)),
            pl.BlockSpec((None, n_ctx, wq), lambda b, i: (b, 0, 0)),
            pl.BlockSpec((None, C_WIDTH, n_ctx), lambda b, i: (b, 0, 0)),
        ] + [pair_k(j) for j in range(n_pairs)] + [pair_vt(j) for j in range(n_pairs)],
        out_specs=pl.BlockSpec((None, blk, C_WIDTH), lambda b, i: (b, i, 0)),
        out_shape=jax.ShapeDtypeStruct((bsz, t, C_WIDTH), BF16),
        scratch_shapes=[pltpu.VMEM((SCORE_SLOTS, n_ctx + t, ATT_Q), F32)],
        compiler_params=_cparams(("parallel", "arbitrary")),
        name="mla_attention",
    )(qf, kfc, vtc, *([kfx] * n_pairs), *([vtx] * n_pairs))


def _mlstm_kernel(mq_ref, mk_ref, mvt_ref, g_ref, mo_ref, mkc_ref, mvtc_ref, gc_ref, bias_ref, hnw_ref,
                  out_ref, s_ref, m_ref, hf_ref, hr_ref):
    nc, ncc, L = mvt_ref.shape[0], mvtc_ref.shape[0], mvt_ref.shape[2]
    row = lax.broadcasted_iota(jnp.int32, (L, L), 0)
    col = lax.broadcasted_iota(jnp.int32, (L, L), 1)
    vis = (row <= col, row >= col)
    tri_f32 = (col <= row).astype(F32)
    lane = lax.broadcasted_iota(jnp.int32, (1, LANES), 1)
    lo = lane < M_QK
    ones_rows = jnp.ones((M_ONES, L), BF16)
    fwd_lanes = lane < GATE_LANE0 + 2 * M_HEADS
    pair = lambda hd: slice((hd // 2) * LANES, (hd // 2 + 1) * LANES)
    vrows = lambda hd: slice(hd * M_V, (hd + 1) * M_V)

    def gates(g):
        gb = g + bias_ref[...]
        ls = jnp.minimum(gb, 0.0) - jnp.log1p(jnp.exp(-jnp.abs(gb)))
        cf = jnp.dot(tri_f32, ls, preferred_element_type=F32, precision=HIGHEST)
        tot = cf[L - 1:L, :]
        b = jnp.where(fwd_lanes, cf, tot - cf + ls)
        li = pltpu.roll(gb, M_HEADS, axis=1)
        return b - li, tot, jnp.transpose(b), jnp.transpose(li)

    def head_lanes(hd, pair_slab):
        keep = lo if hd % 2 == 0 else jnp.logical_not(lo)
        return jnp.where(keep, pair_slab, jnp.zeros_like(pair_slab))

    def scores(u, kh, qh):
        return _nt(jnp.concatenate([kh, s_ref[u].astype(BF16)], axis=0), qh)

    def finish(d, hd, gq, kh, vt_h, big):
        r, tot, b_t, li_t = gq
        u = d * M_HEADS + hd
        fl = GATE_LANE0 + d * 2 * M_HEADS + M_HEADS + hd
        b_row, li_row = b_t[fl:fl + 1, :], li_t[fl:fl + 1, :]
        g, r_col = tot[:, fl:fl + 1], r[:, fl:fl + 1]
        vaug = jnp.concatenate([vt_h, ones_rows], axis=0)
        s_in = s_ref[u]
        m_in = m_ref[u][0:1, 0:1]
        h_t = None
        if big is not None:
            inter = b_row + m_in
            dm = jnp.where(vis[d], b_row - r_col, NEG)
            m_t = jnp.maximum(inter, jnp.max(dm, axis=0, keepdims=True))
            sc = big[0:L] * jnp.exp(dm - m_t)
            res = jnp.exp(inter - m_t) * big[L:] + _dot(vaug, sc.astype(BF16))
            den = jnp.maximum(jnp.abs(res[M_V:M_V + 1]), jnp.exp(-m_t))
            h_t = res[0:M_V] / den
        a_row = g - b_row + li_row
        m_loc = jnp.max(a_row, axis=1, keepdims=True)
        wv = (jnp.exp(a_row - m_loc) * vaug.astype(F32)).astype(BF16)
        s_loc = _dot(wv, kh)
        m_new = jnp.maximum(g + m_in, m_loc)
        s_ref[u] = jnp.exp(g + m_in - m_new) * s_in + jnp.exp(m_loc - m_new) * s_loc
        m_ref[u] = jnp.broadcast_to(m_new, (8, LANES))
        return h_t

    s_ref[...] = jnp.zeros_like(s_ref)
    m_ref[...] = jnp.zeros_like(m_ref)
    for d in range(2):
        for cc in range(ncc):
            jj = cc if d == 0 else ncc - 1 - cc
            gq = gates(gc_ref[jj * L:(jj + 1) * L, :])
            for hd in range(M_HEADS):
                kh = head_lanes(hd, mkc_ref[jj * L:(jj + 1) * L, pair(hd)])
                finish(d, hd, gq, kh, mvtc_ref[jj, vrows(hd), :], None)

    def body(j, carry):
        chunk_of = (j, nc - 1 - j)
        rows = [pl.ds(pl.multiple_of(jj * L, L), L) for jj in chunk_of]
        gq = [gates(g_ref[rows[d], :]) for d in range(2)]
        pending = None
        for d in range(2):
            for hd in range(M_HEADS):
                kh = head_lanes(hd, mk_ref[rows[d], pair(hd)])
                big = scores(d * M_HEADS + hd, kh, head_lanes(hd, mq_ref[rows[d], pair(hd)]))
                if pending is not None:
                    pending()
                dst = hf_ref if d == 0 else hr_ref

                def pending(d=d, hd=hd, kh=kh, big=big, dst=dst):
                    dst[chunk_of[d], hd] = finish(d, hd, gq[d], kh, mvt_ref[chunk_of[d], vrows(hd), :], big)
        pending()
        return carry

    lax.fori_loop(0, nc, body, 0)

    def readout(j, carry):
        rows = pl.ds(pl.multiple_of(j * L, L), L)
        for hd in range(M_HEADS):
            hs_t = hf_ref[j, hd] + hr_ref[j, hd]
            ms = jnp.mean(hs_t * hs_t, axis=0, keepdims=True)
            hs = jnp.transpose(hs_t * lax.rsqrt(ms + NORM_EPS)) * hnw_ref[:, vrows(hd)]
            out_ref[rows, vrows(hd)] = (jax.nn.sigmoid(mo_ref[rows, vrows(hd)].astype(F32)) * hs).astype(BF16)
        return carry

    lax.fori_loop(0, nc, readout, 0)


def _mlstm(mq, mk, mvt, g2, mo, mkc, mvtc, g2c, bias_row, hnw):
    bsz, t, _ = mq.shape
    n_ctx = mkc.shape[1]
    nc, _, chunk = mvt.shape[1:]
    ncc = mvtc.shape[1]
    per_b = lambda rows, width: pl.BlockSpec((None, rows, width), lambda b: (b, 0, 0))
    chunked = lambda n: pl.BlockSpec((None, n, M_WIDTH, chunk), lambda b: (b, 0, 0, 0))
    const = lambda shape: pl.BlockSpec(shape, lambda b: (0,) * len(shape))
    return pl.pallas_call(
        _mlstm_kernel,
        grid=(bsz,),
        in_specs=[per_b(t, M_QKW), per_b(t, M_QKW), chunked(nc), per_b(t, LANES), per_b(t, M_WIDTH),
                  per_b(n_ctx, M_QKW), chunked(ncc), per_b(n_ctx, LANES),
                  const((1, LANES)), const((1, M_WIDTH))],
        out_specs=per_b(t, M_WIDTH),
        out_shape=jax.ShapeDtypeStruct((bsz, t, M_WIDTH), BF16),
        scratch_shapes=[pltpu.VMEM((2 * M_HEADS, M_V + M_ONES, LANES), F32),
                        pltpu.VMEM((2 * M_HEADS, 8, LANES), F32),
                        pltpu.VMEM((nc, M_HEADS, M_V, chunk), F32),
                        pltpu.VMEM((nc, M_HEADS, M_V, chunk), F32)],
        compiler_params=_cparams(("parallel",)),
        name="mlstm",
    )(mq, mk, mvt, g2, mo, mkc, mvtc, g2c, bias_row, hnw)


def _axial_angles(n_lat, rot_dim):
    rows = n_lat // GRID_W
    row = jnp.repeat(jnp.arange(rows), GRID_W).astype(F32)
    col = jnp.tile(jnp.arange(GRID_W), rows).astype(F32)
    n_freq = rot_dim // 4
    inv = ROPE_THETA ** (-jnp.arange(n_freq, dtype=F32) / n_freq)
    ang = jnp.concatenate([row[:, None] * inv, col[:, None] * inv], axis=-1)
    return jnp.cos(ang), jnp.sin(ang)


def _rope_tables(n_lat):
    cos, sin = _axial_angles(n_lat, A_HEAD_DIM)
    cos_a = jnp.tile(jnp.concatenate([cos, cos], axis=-1), (1, 2))
    sin_a = jnp.tile(jnp.concatenate([-sin, sin], axis=-1), (1, 2))
    cos, sin = _axial_angles(n_lat, C_ROPE)
    one = lambda w: jnp.ones((n_lat, w), F32)
    zero = lambda w: jnp.zeros((n_lat, w), F32)
    cos_k = jnp.concatenate([cos, cos, one(LANES - C_ROPE)], axis=-1)
    sin_k = jnp.concatenate([-sin, sin, zero(LANES - C_ROPE)], axis=-1)
    cos_q = jnp.concatenate([one(C_NOPE), cos, cos, one(LANES - C_NOPE - C_ROPE)], axis=-1)
    sin_q = jnp.concatenate([zero(C_NOPE), -sin, sin, zero(LANES - C_NOPE - C_ROPE)], axis=-1)
    return (cos_a, sin_a), (cos_k, sin_k), (cos_q, sin_q)


_A_HEAD_ORDER = (0, 4, 1, 5, 2, 6, 3, 7)


def _perm_heads(w, axis):
    shape = w.shape
    grp = A_HEADS // A_KV_HEADS
    w = w.reshape(shape[:axis] + (A_KV_HEADS, grp, A_HEAD_DIM) + shape[axis + 1:])
    return jnp.swapaxes(w, axis, axis + 1).reshape(shape)


def _even_weights(w_in, w_out):
    offs = np.cumsum((0,) + E_COLS)
    parts = [w_in[:, offs[i]:offs[i + 1]] for i in range(len(E_COLS))]
    parts[2] = _perm_heads(parts[2], 1)
    z = parts[6]
    parts[6] = jnp.concatenate([_perm_heads(z[:, :A_WIDTH], 1), z[:, A_WIDTH:]], axis=1)
    w_out_p = jnp.concatenate([_perm_heads(w_out[:A_WIDTH], 0), w_out[A_WIDTH:]], axis=0)
    return jnp.concatenate(parts, axis=1).astype(BF16), w_out_p.astype(BF16)


def _odd_weights(w_in, w_uq, w_ukv):
    d = w_in.shape[0]
    offs = np.cumsum((0,) + O_COLS)
    ckv, kr, mk, mv, mg, cq, mq, mo, z = [w_in[:, offs[i]:offs[i + 1]] for i in range(len(O_COLS))]
    g2 = jnp.concatenate([kr, mg, jnp.zeros((d, LANES - C_ROPE - 4 * M_HEADS), w_in.dtype)], axis=1)
    w_p = jnp.concatenate([ckv, g2, mk, cq, mq, mo, z], axis=1).astype(BF16)
    wmvt = jnp.transpose(mv).astype(BF16)
    ukv = w_ukv.reshape(C_KV_LORA, C_HEADS, C_NOPE + C_VDIM)
    wk_top = jnp.pad(ukv[:, :, :C_NOPE], ((0, 0), (0, 0), (0, LANES - C_NOPE)))
    eye = jnp.pad(jnp.eye(C_ROPE, dtype=w_in.dtype), ((0, LANES - C_ROPE), (C_NOPE, LANES - C_NOPE - C_ROPE)))
    wk_bot = jnp.broadcast_to(eye[:, None, :], (LANES, C_HEADS, LANES))
    wk = jnp.concatenate([wk_top, wk_bot], axis=0).reshape(C_KV_LORA + LANES, C_HEADS * LANES).astype(BF16)
    wv = jnp.transpose(ukv[:, :, C_NOPE:].reshape(C_KV_LORA, C_WIDTH)).astype(BF16)
    uq = w_uq.reshape(C_Q_LORA, C_HEADS, C_NOPE + C_ROPE)
    wq = jnp.pad(uq, ((0, 0), (0, 0), (0, LANES - C_NOPE - C_ROPE))).reshape(C_Q_LORA, C_HEADS * LANES).astype(BF16)
    return w_p, wk, wv, wmvt, wq


def kernel(x, c, ctx, c_ctx, mod_w, mod_b, pre_norm_w, post_norm_w, e_w_in, e_sink, e_conv_w, e_w_out,
           o_w_in, o_q_norm_w, o_kv_norm_w, o_w_uq, o_w_ukv, o_i_bias, o_f_bias, o_head_norm_w, o_w_out):
    bsz, n_lat, d = x.shape
    n_ctx = ctx.shape[1]
    assert mod_w.shape[0] == 2 and d == D_MODEL, "built for one even + one odd layer"
    assert n_lat % 1024 == 0 and n_ctx % 256 == 0
    tm_in, tm_lat, tm_ctx = 1024, 512, 256

    pad = (-(bsz + 1)) % 8
    cc = jnp.concatenate([c, c_ctx[None, :], jnp.zeros((pad, d), F32)], axis=0)
    mod = _modulation(cc, mod_w, mod_b)
    split = lambda l, r0, r1: [mod[l, r0:r1, k * d:(k + 1) * d][:, None, :] for k in range(3)]
    (rope_a, rope_k, rope_q) = _rope_tables(n_lat)
    dummy_tab = jnp.zeros((n_ctx, LANES), F32)

    sh_x, sc_x, g_x = split(0, 0, bsz)
    sh_c, sc_c, g_c = split(0, bsz, bsz + 1)
    w_in, w_out = _even_weights(e_w_in[0], e_w_out[0])
    nw, pw = pre_norm_w[0][None, :], post_norm_w[0][None, :]
    k, vt, q, bb, u, z = _even_in(x, sh_x, sc_x, nw, w_in, *rope_a, rope=True, tm=tm_in)
    k_c, vt_c, q_c, bb_c, u_c, z_c = _even_in(ctx, sh_c, sc_c, nw, w_in, dummy_tab, dummy_tab,
                                              rope=False, tm=tm_ctx)
    a = _even_attn(e_sink[0], q, k, vt, k_c, vt_c, window=True)
    a_c = _even_attn(e_sink[0], q_c, None, None, k_c, vt_c, window=False)
    x1 = _out_proj((a, bb, u), z, e_conv_w[0], w_out, pw, g_x, x, tm=tm_lat)
    ctx1 = _out_proj((a_c, bb_c, u_c), z_c, e_conv_w[0], w_out, pw, g_c, ctx, tm=tm_ctx)

    sh_x, sc_x, g_x = split(1, 0, bsz)
    sh_c, sc_c, _ = split(1, bsz, bsz + 1)
    w_p, wk, wv, wmvt, wq = _odd_weights(o_w_in[0], o_w_uq[0], o_w_ukv[0])
    nw, pw = pre_norm_w[1][None, :], post_norm_w[1][None, :]
    kvn, qn = o_kv_norm_w[0][None, :], o_q_norm_w[0][None, :]
    chunk = min(MLSTM_CHUNK, n_ctx)
    kf, vt, mk, mvt, g2, qf, mq, mo, z = _odd_in(x1, sh_x, sc_x, nw, w_p, kvn, wk, wv, wmvt,
                                                 (*rope_k, qn, wq, *rope_q), tm=tm_in, chunk=chunk)
    kf_c, vt_c, mk_c, mvt_c, g2_c = _odd_in(ctx1, sh_c, sc_c, nw, w_p[:, :OC_CTX_END], kvn, wk, wv, wmvt, None,
                                            tm=tm_ctx, chunk=chunk)
    c_out = _mla(qf, kf_c, kf, vt_c, vt)
    gate_bias = jnp.stack([o_i_bias[0], o_f_bias[0]], axis=1).reshape(1, 4 * M_HEADS)
    bias_row = jnp.pad(gate_bias, ((0, 0), (GATE_LANE0, LANES - GATE_LANE0 - 4 * M_HEADS)))
    m_out = _mlstm(mq, mk, mvt, g2, mo, mk_c, mvt_c, g2_c, bias_row, o_head_norm_w[0][None, :])
    return _out_proj((c_out, m_out), z, None, o_w_out[0].astype(BF16), pw, g_x, x1, tm=tm_lat)
```

```python
import collections
import functools

import numpy as np
import jax
import jax.numpy as jnp
from jax import lax
from jax.experimental import pallas as pl
from jax.experimental.pallas import tpu as pltpu

F32 = jnp.float32
BF16 = jnp.bfloat16
HIGHEST = lax.Precision.HIGHEST

LANES = 128
VMEM_LIMIT = 56 * 1024 * 1024

D_MODEL = 1024
GRID_W = 64
ROPE_THETA = 10000.0
NORM_EPS = 1e-6
NEG = -1e30
LOG2E = 1.4426950408889634

A_HEADS, A_KV_HEADS, A_HEAD_DIM = 8, 2, 64
A_WIDTH = A_HEADS * A_HEAD_DIM
WINDOW = 128
B_WIDTH = 512
CONV_W = 3
C_HEADS, C_NOPE, C_ROPE, C_VDIM = 8, 64, 32, 64
C_KV_LORA, C_Q_LORA = 256, 768
C_WIDTH = C_HEADS * C_VDIM
M_HEADS, M_QK, M_V = 4, 64, 128
M_WIDTH = M_HEADS * M_V
E_GATE = A_WIDTH + B_WIDTH
O_GATE = C_WIDTH + M_WIDTH
E_COLS = (128, 128, A_WIDTH, B_WIDTH, B_WIDTH, B_WIDTH, E_GATE)
E_IN = sum(E_COLS)
O_COLS = (C_KV_LORA, C_ROPE, M_HEADS * M_QK, M_WIDTH, 4 * M_HEADS, C_Q_LORA, M_HEADS * M_QK, M_WIDTH, O_GATE)

EC_K, EC_V, EC_Q, EC_BB, EC_BC, EC_BX, EC_Z, EC_END = 0, 128, 256, 768, 1280, 1792, 2304, 3328
OC_CKV = 0
OC_G2 = 256
OC_MK = 384
OC_CTX_END = 640
OC_CQ = 640
OC_MQ = 1408
OC_MO = 1664
OC_Z = 2176
OC_END = 3200
M_QKW = M_HEADS * M_QK
M_ONES = 16
GATE_LANE0 = 32
SUB_ROWS = 256
MLSTM_CHUNK = 256
ATT_Q = 256
ATT_KEYS = 256
SCORE_SLOTS = 3
SCORE_ROWS = 128
ATT_VDIM = 64
MLA_BLK = 512
ATT_STEP = 1024


def _nt(a, b):
    return lax.dot_general(a, b, (((1,), (1,)), ((), ())), preferred_element_type=F32)


def _dot(a, b):
    return jnp.dot(a, b, preferred_element_type=F32)


def _silu(z):
    return z * jax.nn.sigmoid(z)


def _rms(x, w):
    ms = jnp.mean(x * x, axis=-1, keepdims=True)
    return x * lax.rsqrt(ms + NORM_EPS) * w


def _rope_slab(x, cos, sin_signed, half, take_up):
    up = pltpu.roll(x, LANES - half, axis=1)
    dn = pltpu.roll(x, half, axis=1)
    return x * cos + jnp.where(take_up, up, dn) * sin_signed


def _cparams(sem):
    return pltpu.CompilerParams(dimension_semantics=sem, vmem_limit_bytes=VMEM_LIMIT)


def _resident(shape):
    return pl.BlockSpec(shape, lambda b, i: (0,) * len(shape), pipeline_mode=pl.Buffered(1))


def _mod_kernel(cc_ref, w_ref, b_ref, o_ref):
    s = _silu(cc_ref[...])
    o_ref[...] = jnp.dot(s, w_ref[...], preferred_element_type=F32, precision=HIGHEST) + b_ref[...]


def _modulation(cc, mod_w, mod_b):
    depth, d, d3 = mod_w.shape
    r = cc.shape[0]
    return pl.pallas_call(
        _mod_kernel,
        grid=(depth, d3 // d),
        in_specs=[
            pl.BlockSpec((r, d), lambda l, j: (0, 0)),
            pl.BlockSpec((None, d, d), lambda l, j: (l, 0, j)),
            pl.BlockSpec((None, 1, d), lambda l, j: (l, 0, j)),
        ],
        out_specs=pl.BlockSpec((None, r, d), lambda l, j: (l, 0, j)),
        out_shape=jax.ShapeDtypeStruct((depth, r, d3), F32),
        compiler_params=_cparams(("arbitrary", "arbitrary")),
        name="modulation",
    )(cc, mod_w, mod_b.reshape(depth, 1, d3))


def _even_in_kernel(x_ref, sh_ref, sc_ref, nw_ref, w_ref, cos_ref, sin_ref,
                    k_ref, vt_ref, q_ref, bb_ref, u_ref, z_ref, *, rope):
    lane = lax.broadcasted_iota(jnp.int32, (1, LANES), 1)
    take_up = (lane % A_HEAD_DIM) < (A_HEAD_DIM // 2)
    for r0 in range(0, x_ref.shape[0], SUB_ROWS):
        rows = slice(r0, r0 + SUB_ROWS)
        h = (_rms(x_ref[rows, :], nw_ref[...]) * (1.0 + sc_ref[...]) + sh_ref[...]).astype(BF16)
        y_all = _dot(h, w_ref[...])

        def proj(a, b):
            return y_all[:, a:b]

        if rope:
            cos, sin = cos_ref[rows, :], sin_ref[rows, :]
            rot = lambda t: _rope_slab(t, cos, sin, A_HEAD_DIM // 2, take_up)
        else:
            rot = lambda t: t
        k_ref[rows, :] = rot(proj(EC_K, EC_V)).astype(BF16)
        vt_ref[:, rows] = jnp.transpose(proj(EC_V, EC_Q)).astype(BF16)
        q = proj(EC_Q, EC_BB)
        for j in range(A_WIDTH // LANES):
            sl = slice(j * LANES, (j + 1) * LANES)
            q_ref[rows, sl] = (rot(q[:, sl]) * (A_HEAD_DIM ** -0.5 * LOG2E)).astype(BF16)
        bb_ref[rows, :] = proj(EC_BB, EC_BC).astype(BF16)
        u_ref[rows, :] = (proj(EC_BC, EC_BX) * proj(EC_BX, EC_Z)).astype(BF16)
        z_ref[rows, :] = _silu(proj(EC_Z, EC_END)).astype(BF16)


def _even_in(x, sh, sc, nw, w, cos, sin, *, rope, tm):
    bsz, t, d = x.shape
    nb = sh.shape[0]
    mod_map = (lambda b, i: (b, 0, 0)) if nb > 1 else (lambda b, i: (0, 0, 0))
    tok = lambda width: pl.BlockSpec((None, tm, width), lambda b, i: (b, i, 0))
    const = lambda shape: pl.BlockSpec(shape, lambda b, i: (0,) * len(shape))
    widths = (LANES, None, A_WIDTH, B_WIDTH, B_WIDTH, E_GATE)
    vt_spec = pl.BlockSpec((None, LANES, tm), lambda b, i: (b, 0, i))
    vt_shape = jax.ShapeDtypeStruct((bsz, LANES, t), BF16)
    return pl.pallas_call(
        functools.partial(_even_in_kernel, rope=rope),
        grid=(bsz, t // tm),
        in_specs=[
            tok(d),
            pl.BlockSpec((None, 1, d), mod_map),
            pl.BlockSpec((None, 1, d), mod_map),
            const((1, d)),
            _resident(w.shape),
            pl.BlockSpec((tm, LANES), lambda b, i: (i, 0)),
            pl.BlockSpec((tm, LANES), lambda b, i: (i, 0)),
        ],
        out_specs=[vt_spec if wd is None else tok(wd) for wd in widths],
        out_shape=[vt_shape if wd is None else jax.ShapeDtypeStruct((bsz, t, wd), BF16) for wd in widths],
        compiler_params=_cparams(("parallel", "parallel")),
        name="even_in_rope" if rope else "even_in",
    )(x, sh, sc, nw, w, cos, sin)


AttnStage = collections.namedtuple("AttnStage", "q k vt mask sink")


def _tree_max(parts):
    while len(parts) > 1:
        parts = [jnp.maximum(parts[i], parts[i + 1]) if i + 1 < len(parts) else parts[i]
                 for i in range(0, len(parts), 2)]
    return parts[0]


def _attention_pipeline(stages, n_chunks, s_ref, write_pair):
    kc, sr = ATT_KEYS, SCORE_ROWS
    slots, n_keys = s_ref.shape[0], s_ref.shape[1]
    lag = slots - 1
    ones = jnp.ones((16, n_keys), BF16)
    m8 = {}
    halves = []
    for i in range(len(stages) + lag):
        if i < len(stages):
            cur = stages[i]
            q = cur.q()
            parts = []
            for c in range(n_chunks):
                kch, mask = cur.k(c), cur.mask(c)
                for r in range(0, kc, sr):
                    st = _nt(kch[r:r + sr], q)
                    if mask is not None:
                        st = jnp.where(mask[r:r + sr], st, NEG)
                    s_ref[i % slots, c * kc + r:c * kc + r + sr, :] = st
                    parts += [st[t:t + 8, :] for t in range(0, sr, 8)]
            m8[i] = _tree_max(parts)
        if i >= lag:
            prev = stages[i - lag]
            m_prev = jnp.max(m8.pop(i - lag), axis=0, keepdims=True)
            if prev.sink is not None:
                m_prev = jnp.maximum(m_prev, prev.sink)
            pt = jnp.exp2(s_ref[(i - lag) % slots] - m_prev).astype(BF16)
            vta = jnp.concatenate([prev.vt(c) for c in range(n_chunks)], axis=1)
            acc = _dot(jnp.concatenate([vta, ones], axis=0), pt)
            denom = acc[ATT_VDIM:ATT_VDIM + 1]
            if prev.sink is not None:
                denom = denom + jnp.exp2(prev.sink - m_prev)
            halves.append(acc[0:ATT_VDIM] / denom)
            if len(halves) == 2:
                write_pair((i - lag) // 2, jnp.concatenate(halves, axis=0))
                halves = []


def _even_attn_kernel(*refs, window, n_lat):
    if window:
        (sink_ref, q_ref, kp_ref, km_ref, kn_ref, kc_ref, vtp_ref, vtm_ref, vtn_ref, vtc_ref,
         o_ref, s_ref) = refs
    else:
        sink_ref, q_ref, kc_ref, vtc_ref, o_ref, s_ref = refs
    nq, kc = ATT_Q, ATT_KEYS
    step, half = q_ref.shape[0], WINDOW
    lane = lax.broadcasted_iota(jnp.int32, (1, LANES), 1)
    lo = lane < A_HEAD_DIM
    ctx_k = [kc_ref[r:r + kc, :] for r in range(0, kc_ref.shape[0], kc)]
    ctx_vt = [vtc_ref[:, r:r + kc] for r in range(0, kc_ref.shape[0], kc)]
    r = lax.broadcasted_iota(jnp.int32, (kc, nq), 0)
    c = lax.broadcasted_iota(jnp.int32, (kc, nq), 1)

    def sub_block(qb):
        k_chunks, vt_chunks, masks = list(ctx_k), list(ctx_vt), [None] * len(ctx_k)
        if window:
            base, q0 = qb * nq, pl.program_id(1) * step + qb * nq
            if qb == 0:
                k1 = jnp.concatenate([kp_ref[...], km_ref[0:half, :]], axis=0)
                vt1 = jnp.concatenate([vtp_ref[...], vtm_ref[:, 0:half]], axis=1)
            else:
                k1, vt1 = km_ref[base - half:base + half, :], vtm_ref[:, base - half:base + half]
            if base + nq == step:
                k2 = jnp.concatenate([km_ref[base + half:step, :], kn_ref[...]], axis=0)
                vt2 = jnp.concatenate([vtm_ref[:, base + half:step], vtn_ref[...]], axis=1)
            else:
                k2, vt2 = km_ref[base + half:base + half + kc, :], vtm_ref[:, base + half:base + half + kc]
            k_chunks += [k1, k2]
            vt_chunks += [vt1, vt2]
            masks += [(jnp.abs(r - half - c) <= WINDOW) & (r + (q0 - half) >= 0),
                      (jnp.abs(r + half - c) <= WINDOW) & (r + (q0 + half) < n_lat)]
        return k_chunks, vt_chunks, masks

    def stage(qb, head, chunks):
        k_chunks, vt_chunks, masks = chunks
        j, grp = head % (A_HEADS // 2), head // (A_HEADS // 2)
        keep = lo if grp == 0 else jnp.logical_not(lo)

        def q():
            qs = q_ref[qb * nq:(qb + 1) * nq, j * LANES:(j + 1) * LANES]
            return jnp.where(keep, qs, jnp.zeros_like(qs))

        return AttnStage(q=q, k=lambda ci: k_chunks[ci],
                         vt=lambda ci: vt_chunks[ci][grp * A_HEAD_DIM:(grp + 1) * A_HEAD_DIM, :],
                         mask=lambda ci: masks[ci], sink=sink_ref[head] * LOG2E)

    def write_pair(p, o_t):
        qb, j = divmod(p, A_HEADS // 2)
        o_ref[qb * nq:(qb + 1) * nq, j * LANES:(j + 1) * LANES] = jnp.transpose(o_t).astype(BF16)

    stages = []
    for qb in range(step // nq):
        chunks = sub_block(qb)
        stages += [stage(qb, h, chunks) for h in _A_HEAD_ORDER]
    _attention_pipeline(stages, len(ctx_k) + (2 if window else 0), s_ref, write_pair)


def _even_attn(sink, q, k, vt, kc, vtc, *, window):
    bsz, t, _ = q.shape
    n_ctx = kc.shape[1]
    nq, keys, half = ATT_Q, ATT_KEYS, WINDOW
    step = min(ATT_STEP, t)
    assert nq == keys == 2 * half and t % step == 0 and n_ctx % keys == 0
    per_step, last = step // half, t // half - 1
    smem = pl.BlockSpec(memory_space=pltpu.SMEM)
    qspec = pl.BlockSpec((None, step, A_WIDTH), lambda b, i: (b, i, 0))
    kcspec = pl.BlockSpec((None, n_ctx, LANES), lambda b, i: (b, 0, 0))
    vtcspec = pl.BlockSpec((None, LANES, n_ctx), lambda b, i: (b, 0, 0))
    if window:
        prev = lambda i: jnp.maximum(per_step * i - 1, 0)
        nxt = lambda i: jnp.minimum(per_step * (i + 1), last)
        specs = [smem, qspec,
                 pl.BlockSpec((None, half, LANES), lambda b, i: (b, prev(i), 0)),
                 pl.BlockSpec((None, step, LANES), lambda b, i: (b, i, 0)),
                 pl.BlockSpec((None, half, LANES), lambda b, i: (b, nxt(i), 0)),
                 kcspec,
                 pl.BlockSpec((None, LANES, half), lambda b, i: (b, 0, prev(i))),
                 pl.BlockSpec((None, LANES, step), lambda b, i: (b, 0, i)),
                 pl.BlockSpec((None, LANES, half), lambda b, i: (b, 0, nxt(i))),
                 vtcspec]
        args = (sink, q, k, k, k, kc, vt, vt, vt, vtc)
    else:
        specs = [smem, qspec, kcspec, vtcspec]
        args = (sink, q, kc, vtc)
    n_chunks = n_ctx // keys + (2 if window else 0)
    return pl.pallas_call(
        functools.partial(_even_attn_kernel, window=window, n_lat=t),
        grid=(bsz, t // step),
        in_specs=specs,
        out_specs=qspec,
        out_shape=jax.ShapeDtypeStruct((bsz, t, A_WIDTH), BF16),
        scratch_shapes=[pltpu.VMEM((SCORE_SLOTS, n_chunks * keys, nq), F32)],
        compiler_params=_cparams(("parallel", "parallel")),
        name="even_attn_window" if window else "even_attn_ctx",
    )(*args)


def _out_kernel(*refs, conv):
    if conv:
        a_ref, bb_ref, u_ref, up_ref, un_ref, z_ref, cw_ref, w_ref, pw_ref, g_ref, x_ref, o_ref = refs
        tm = u_ref.shape[0]
        i = pl.program_id(1)
        u = u_ref[...].astype(F32)
        prev_row = jnp.where(i > 0, up_ref[15:16, :].astype(F32), 0.0)
        next_row = jnp.where(i < pl.num_programs(1) - 1, un_ref[0:1, :].astype(F32), 0.0)
        row = lax.broadcasted_iota(jnp.int32, (tm, 1), 0)
        u_m1 = jnp.where(row == 0, prev_row, pltpu.roll(u, 1, axis=0))
        u_p1 = jnp.where(row == tm - 1, next_row, pltpu.roll(u, tm - 1, axis=0))
        cw = cw_ref[...]
        mix_b = (bb_ref[...].astype(F32) * (u_m1 * cw[0:1, :] + u * cw[1:2, :] + u_p1 * cw[2:3, :])).astype(BF16)
    else:
        a_ref, b_ref, z_ref, w_ref, pw_ref, g_ref, x_ref, o_ref = refs
        mix_b = b_ref[...]
    wa = a_ref.shape[1]
    for r0 in range(0, x_ref.shape[0], SUB_ROWS):
        rows = slice(r0, r0 + SUB_ROWS)
        gated = jnp.concatenate([a_ref[rows, :] * z_ref[rows, 0:wa], mix_b[rows, :] * z_ref[rows, wa:]], axis=1)
        o_ref[rows, :] = x_ref[rows, :] + g_ref[...] * _rms(_dot(gated, w_ref[...]), pw_ref[...])


def _out_proj(mix, z, cw, w, pw, g, x, *, tm):
    conv = cw is not None
    bsz, t, d = x.shape
    nb = g.shape[0]
    mod_map = (lambda b, i: (b, 0, 0)) if nb > 1 else (lambda b, i: (0, 0, 0))
    tok = lambda width: pl.BlockSpec((None, tm, width), lambda b, i: (b, i, 0))
    const = lambda shape: pl.BlockSpec(shape, lambda b, i: (0,) * len(shape))
    r16 = tm // 16
    if conv:
        a, bb, u = mix
        specs = [tok(a.shape[2]), tok(bb.shape[2]), tok(u.shape[2]),
                 pl.BlockSpec((None, 16, u.shape[2]), lambda b, i: (b, jnp.maximum(i * r16 - 1, 0), 0)),
                 pl.BlockSpec((None, 16, u.shape[2]), lambda b, i: (b, jnp.minimum((i + 1) * r16, t // 16 - 1), 0)),
                 tok(z.shape[2]), const(cw.shape)]
        args = (a, bb, u, u, u, z, cw)
    else:
        a, b2 = mix
        specs = [tok(a.shape[2]), tok(b2.shape[2]), tok(z.shape[2])]
        args = (a, b2, z)
    specs += [_resident(w.shape), const((1, d)), pl.BlockSpec((None, 1, d), mod_map), tok(d)]
    return pl.pallas_call(
        functools.partial(_out_kernel, conv=conv),
        grid=(bsz, t // tm),
        in_specs=specs,
        out_specs=tok(d),
        out_shape=jax.ShapeDtypeStruct((bsz, t, d), F32),
        compiler_params=_cparams(("parallel", "parallel")),
        name="out_proj_conv" if conv else "out_proj",
    )(*args, w, pw, g, x)


def _odd_in_kernel(*refs, lat):
    if lat:
        (x_ref, sh_ref, sc_ref, nw_ref, w_ref, kvn_ref, wk_ref, wvt_ref, wmvt_ref, cosk_ref, sink_ref,
         qn_ref, wq_ref, cosq_ref, sinq_ref,
         kf_ref, vt_ref, mk_ref, mvt_ref, g2_ref, qf_ref, mq_ref, mo_ref, z_ref) = refs
    else:
        (x_ref, sh_ref, sc_ref, nw_ref, w_ref, kvn_ref, wk_ref, wvt_ref, wmvt_ref,
         kf_ref, vt_ref, mk_ref, mvt_ref, g2_ref) = refs
    lane = lax.broadcasted_iota(jnp.int32, (1, LANES), 1)
    chunk = mvt_ref.shape[2]
    for cc in range(mvt_ref.shape[0]):
        rows = slice(cc * chunk, (cc + 1) * chunk)
        h = (_rms(x_ref[rows, :], nw_ref[...]) * (1.0 + sc_ref[...]) + sh_ref[...]).astype(BF16)
        y_all = _dot(h, w_ref[...])

        def proj(a, b):
            return y_all[:, a:b]

        ckv = _rms(proj(OC_CKV, OC_CKV + C_KV_LORA), kvn_ref[...]).astype(BF16)
        g2 = proj(OC_G2, OC_G2 + LANES)
        g2_ref[rows, :] = g2
        if lat:
            g2 = _rope_slab(g2, cosk_ref[rows, :], sink_ref[rows, :], C_ROPE // 2, lane < C_ROPE // 2)
        kf_ref[rows, :] = _dot(jnp.concatenate([ckv, g2.astype(BF16)], axis=1), wk_ref[...]).astype(BF16)
        vt_ref[:, rows] = _nt(wvt_ref[...], ckv).astype(BF16)
        mk_ref[rows, :] = (proj(OC_MK, OC_MK + M_QKW) * (M_QK ** -0.5)).astype(BF16)
        mvt_ref[cc] = _nt(wmvt_ref[...], h).astype(BF16)
        if lat:
            cq = _rms(proj(OC_CQ, OC_CQ + C_Q_LORA), qn_ref[...]).astype(BF16)
            qf = _dot(cq, wq_ref[...])
            cos, sin = cosq_ref[rows, :], sinq_ref[rows, :]
            take_up = (lane >= C_NOPE) & (lane < C_NOPE + C_ROPE // 2)
            scale = (C_NOPE + C_ROPE) ** -0.5 * LOG2E
            for hd in range(C_HEADS):
                sl = slice(hd * LANES, (hd + 1) * LANES)
                qf_ref[rows, sl] = (_rope_slab(qf[:, sl], cos, sin, C_ROPE // 2, take_up) * scale).astype(BF16)
            mq_ref[rows, :] = proj(OC_MQ, OC_MQ + M_QKW).astype(BF16)
            mo_ref[rows, :] = proj(OC_MO, OC_MO + M_WIDTH).astype(BF16)
            z_ref[rows, :] = _silu(proj(OC_Z, OC_Z + O_GATE)).astype(BF16)


def _odd_in(x, sh, sc, nw, w, kvn, wk, wv, wmvt, lat_args, *, tm, chunk):
    lat = lat_args is not None
    bsz, t, d = x.shape
    nb = sh.shape[0]
    mod_map = (lambda b, i: (b, 0, 0)) if nb > 1 else (lambda b, i: (0, 0, 0))
    tok = lambda width: (pl.BlockSpec((None, tm, width), lambda b, i: (b, i, 0)), (bsz, t, width))
    const = lambda shape: pl.BlockSpec(shape, lambda b, i: (0,) * len(shape))
    tab = pl.BlockSpec((tm, LANES), lambda b, i: (i, 0))
    specs = [tok(d)[0], pl.BlockSpec((None, 1, d), mod_map), pl.BlockSpec((None, 1, d), mod_map),
             const((1, d)), _resident(w.shape), const(kvn.shape), _resident(wk.shape), _resident(wv.shape),
             _resident(wmvt.shape)]
    args = [x, sh, sc, nw, w, kvn, wk, wv, wmvt]
    vt = (pl.BlockSpec((None, C_WIDTH, tm), lambda b, i: (b, 0, i)), (bsz, C_WIDTH, t))
    mvt = (pl.BlockSpec((None, tm // chunk, M_WIDTH, chunk), lambda b, i: (b, i, 0, 0)),
           (bsz, t // chunk, M_WIDTH, chunk))
    outs = [(tok(C_HEADS * LANES), BF16), (vt, BF16), (tok(M_QKW), BF16), (mvt, BF16), (tok(LANES), F32)]
    if lat:
        cosk, sink, qn, wq, cosq, sinq = lat_args
        specs += [tab, tab, const(qn.shape), _resident(wq.shape), tab, tab]
        args += [cosk, sink, qn, wq, cosq, sinq]
        outs += [(tok(C_HEADS * LANES), BF16), (tok(M_QKW), BF16), (tok(M_WIDTH), BF16), (tok(O_GATE), BF16)]
    return pl.pallas_call(
        functools.partial(_odd_in_kernel, lat=lat),
        grid=(bsz, t // tm),
        in_specs=specs,
        out_specs=[o[0][0] for o in outs],
        out_shape=[jax.ShapeDtypeStruct(o[0][1], o[1]) for o in outs],
        compiler_params=_cparams(("parallel", "parallel")),
        name="odd_in_lat" if lat else "odd_in_ctx",
    )(*args)


def _mla_kernel(q_ref, kc_ref, vtc_ref, *rest):
    n_pairs = C_HEADS // 2
    kx_refs, vtx_refs, (o_ref, s_ref) = rest[:n_pairs], rest[n_pairs:2 * n_pairs], rest[2 * n_pairs:]
    nq, kc = ATT_Q, ATT_KEYS
    n_ctx_chunks = kc_ref.shape[0] // kc
    n_chunks = n_ctx_chunks + kx_refs[0].shape[0] // kc

    def stage(qb, hd):
        sl = slice(hd * LANES, (hd + 1) * LANES)
        half = hd % 2

        def k(ci):
            if ci < n_ctx_chunks:
                return kc_ref[ci * kc:(ci + 1) * kc, sl]
            r0 = (ci - n_ctx_chunks) * kc
            return kx_refs[hd // 2][r0:r0 + kc, half * LANES:(half + 1) * LANES]

        def vt(ci):
            if ci < n_ctx_chunks:
                return vtc_ref[hd * C_VDIM:(hd + 1) * C_VDIM, ci * kc:(ci + 1) * kc]
            r0 = (ci - n_ctx_chunks) * kc
            return vtx_refs[hd // 2][half * C_VDIM:(half + 1) * C_VDIM, r0:r0 + kc]

        return AttnStage(q=lambda: q_ref[qb * nq:(qb + 1) * nq, sl], k=k, vt=vt, mask=lambda ci: None, sink=None)

    def write_pair(p, o_t):
        qb, j = divmod(p, C_HEADS // 2)
        o_ref[qb * nq:(qb + 1) * nq, j * LANES:(j + 1) * LANES] = jnp.transpose(o_t).astype(BF16)

    stages = [stage(qb, hd) for qb in range(q_ref.shape[0] // nq) for hd in range(C_HEADS)]
    _attention_pipeline(stages, n_chunks, s_ref, write_pair)


def _mla(qf, kfc, kfx, vtc, vtx):
    bsz, t, wq = qf.shape
    n_ctx = kfc.shape[1]
    blk = MLA_BLK
    n_pairs = C_HEADS // 2
    pair_k = lambda j: pl.BlockSpec((None, t, 2 * LANES), lambda b, i: (b, 0, j))
    pair_vt = lambda j: pl.BlockSpec((None, 2 * C_VDIM, t), lambda b, i: (b, j, 0))
    return pl.pallas_call(
        _mla_kernel,
        grid=(bsz, t // blk),
        in_specs=[
            pl.BlockSpec((None, blk, wq), lambda b, i: (b, i, 0)),
            pl.BlockSpec((None, n_ctx, wq), lambda b, i: (b, 0, 0)),
            pl.BlockSpec((None, C_WIDTH, n_ctx), lambda b, i: (b, 0, 0)),
        ] + [pair_k(j) for j in range(n_pairs)] + [pair_vt(j) for j in range(n_pairs)],
        out_specs=pl.BlockSpec((None, blk, C_WIDTH), lambda b, i: (b, i, 0)),
        out_shape=jax.ShapeDtypeStruct((bsz, t, C_WIDTH), BF16),
        scratch_shapes=[pltpu.VMEM((SCORE_SLOTS, n_ctx + t, ATT_Q), F32)],
        compiler_params=_cparams(("parallel", "arbitrary")),
        name="mla_attention",
    )(qf, kfc, vtc, *([kfx] * n_pairs), *([vtx] * n_pairs))


def _mlstm_kernel(mq_ref, mk_ref, mvt_ref, g_ref, mo_ref, mkc_ref, mvtc_ref, gc_ref, bias_ref, hnw_ref,
                  out_ref, s_ref, m_ref, hf_ref, hr_ref):
    nc, ncc, L = mvt_ref.shape[0], mvtc_ref.shape[0], mvt_ref.shape[2]
    row = lax.broadcasted_iota(jnp.int32, (L, L), 0)
    col = lax.broadcasted_iota(jnp.int32, (L, L), 1)
    vis = (row <= col, row >= col)
    tri_f32 = (col <= row).astype(F32)
    lane = lax.broadcasted_iota(jnp.int32, (1, LANES), 1)
    lo = lane < M_QK
    ones_rows = jnp.ones((M_ONES, L), BF16)
    fwd_lanes = lane < GATE_LANE0 + 2 * M_HEADS
    pair = lambda hd: slice((hd // 2) * LANES, (hd // 2 + 1) * LANES)
    vrows = lambda hd: slice(hd * M_V, (hd + 1) * M_V)

    def gates(g):
        gb = g + bias_ref[...]
        ls = jnp.minimum(gb, 0.0) - jnp.log1p(jnp.exp(-jnp.abs(gb)))
        cf = jnp.dot(tri_f32, ls, preferred_element_type=F32, precision=HIGHEST)
        tot = cf[L - 1:L, :]
        b = jnp.where(fwd_lanes, cf, tot - cf + ls)
        li = pltpu.roll(gb, M_HEADS, axis=1)
        return b - li, tot, jnp.transpose(b), jnp.transpose(li)

    def head_lanes(hd, pair_slab):
        keep = lo if hd % 2 == 0 else jnp.logical_not(lo)
        return jnp.where(keep, pair_slab, jnp.zeros_like(pair_slab))

    def scores(u, kh, qh):
        return _nt(jnp.concatenate([kh, s_ref[u].astype(BF16)], axis=0), qh)

    def finish(d, hd, gq, kh, vt_h, big):
        r, tot, b_t, li_t = gq
        u = d * M_HEADS + hd
        fl = GATE_LANE0 + d * 2 * M_HEADS + M_HEADS + hd
        b_row, li_row = b_t[fl:fl + 1, :], li_t[fl:fl + 1, :]
        g, r_col = tot[:, fl:fl + 1], r[:, fl:fl + 1]
        vaug = jnp.concatenate([vt_h, ones_rows], axis=0)
        s_in = s_ref[u]
        m_in = m_ref[u][0:1, 0:1]
        h_t = None
        if big is not None:
            inter = b_row + m_in
            dm = jnp.where(vis[d], b_row - r_col, NEG)
            m_t = jnp.maximum(inter, jnp.max(dm, axis=0, keepdims=True))
            sc = big[0:L] * jnp.exp(dm - m_t)
            res = jnp.exp(inter - m_t) * big[L:] + _dot(vaug, sc.astype(BF16))
            den = jnp.maximum(jnp.abs(res[M_V:M_V + 1]), jnp.exp(-m_t))
            h_t = res[0:M_V] / den
        a_row = g - b_row + li_row
        m_loc = jnp.max(a_row, axis=1, keepdims=True)
        wv = (jnp.exp(a_row - m_loc) * vaug.astype(F32)).astype(BF16)
        s_loc = _dot(wv, kh)
        m_new = jnp.maximum(g + m_in, m_loc)
        s_ref[u] = jnp.exp(g + m_in - m_new) * s_in + jnp.exp(m_loc - m_new) * s_loc
        m_ref[u] = jnp.broadcast_to(m_new, (8, LANES))
        return h_t

    s_ref[...] = jnp.zeros_like(s_ref)
    m_ref[...] = jnp.zeros_like(m_ref)
    for d in range(2):
        for cc in range(ncc):
            jj = cc if d == 0 else ncc - 1 - cc
            gq = gates(gc_ref[jj * L:(jj + 1) * L, :])
            for hd in range(M_HEADS):
                kh = head_lanes(hd, mkc_ref[jj * L:(jj + 1) * L, pair(hd)])
                finish(d, hd, gq, kh, mvtc_ref[jj, vrows(hd), :], None)

    def body(j, carry):
        chunk_of = (j, nc - 1 - j)
        rows = [pl.ds(pl.multiple_of(jj * L, L), L) for jj in chunk_of]
        gq = [gates(g_ref[rows[d], :]) for d in range(2)]
        pending = None
        for d in range(2):
            for hd in range(M_HEADS):
                kh = head_lanes(hd, mk_ref[rows[d], pair(hd)])
                big = scores(d * M_HEADS + hd, kh, head_lanes(hd, mq_ref[rows[d], pair(hd)]))
                if pending is not None:
                    pending()
                dst = hf_ref if d == 0 else hr_ref

                def pending(d=d, hd=hd, kh=kh, big=big, dst=dst):
                    dst[chunk_of[d], hd] = finish(d, hd, gq[d], kh, mvt_ref[chunk_of[d], vrows(hd), :], big)
        pending()
        return carry

    lax.fori_loop(0, nc, body, 0)

    def readout(j, carry):
        rows = pl.ds(pl.multiple_of(j * L, L), L)
        for hd in range(M_HEADS):
            hs_t = hf_ref[j, hd] + hr_ref[j, hd]
            ms = jnp.mean(hs_t * hs_t, axis=0, keepdims=True)
            hs = jnp.transpose(hs_t * lax.rsqrt(ms + NORM_EPS)) * hnw_ref[:, vrows(hd)]
            out_ref[rows, vrows(hd)] = (jax.nn.sigmoid(mo_ref[rows, vrows(hd)].astype(F32)) * hs).astype(BF16)
        return carry

    lax.fori_loop(0, nc, readout, 0)


def _mlstm(mq, mk, mvt, g2, mo, mkc, mvtc, g2c, bias_row, hnw):
    bsz, t, _ = mq.shape
    n_ctx = mkc.shape[1]
    nc, _, chunk = mvt.shape[1:]
    ncc = mvtc.shape[1]
    per_b = lambda rows, width: pl.BlockSpec((None, rows, width), lambda b: (b, 0, 0))
    chunked = lambda n: pl.BlockSpec((None, n, M_WIDTH, chunk), lambda b: (b, 0, 0, 0))
    const = lambda shape: pl.BlockSpec(shape, lambda b: (0,) * len(shape))
    return pl.pallas_call(
        _mlstm_kernel,
        grid=(bsz,),
        in_specs=[per_b(t, M_QKW), per_b(t, M_QKW), chunked(nc), per_b(t, LANES), per_b(t, M_WIDTH),
                  per_b(n_ctx, M_QKW), chunked(ncc), per_b(n_ctx, LANES),
                  const((1, LANES)), const((1, M_WIDTH))],
        out_specs=per_b(t, M_WIDTH),
        out_shape=jax.ShapeDtypeStruct((bsz, t, M_WIDTH), BF16),
        scratch_shapes=[pltpu.VMEM((2 * M_HEADS, M_V + M_ONES, LANES), F32),
                        pltpu.VMEM((2 * M_HEADS, 8, LANES), F32),
                        pltpu.VMEM((nc, M_HEADS, M_V, chunk), F32),
                        pltpu.VMEM((nc, M_HEADS, M_V, chunk), F32)],
        compiler_params=_cparams(("parallel",)),
        name="mlstm",
    )(mq, mk, mvt, g2, mo, mkc, mvtc, g2c, bias_row, hnw)


def _axial_angles(n_lat, rot_dim):
    rows = n_lat // GRID_W
    row = jnp.repeat(jnp.arange(rows), GRID_W).astype(F32)
    col = jnp.tile(jnp.arange(GRID_W), rows).astype(F32)
    n_freq = rot_dim // 4
    inv = ROPE_THETA ** (-jnp.arange(n_freq, dtype=F32) / n_freq)
    ang = jnp.concatenate([row[:, None] * inv, col[:, None] * inv], axis=-1)
    return jnp.cos(ang), jnp.sin(ang)


def _rope_tables(n_lat):
    cos, sin = _axial_angles(n_lat, A_HEAD_DIM)
    cos_a = jnp.tile(jnp.concatenate([cos, cos], axis=-1), (1, 2))
    sin_a = jnp.tile(jnp.concatenate([-sin, sin], axis=-1), (1, 2))
    cos, sin = _axial_angles(n_lat, C_ROPE)
    one = lambda w: jnp.ones((n_lat, w), F32)
    zero = lambda w: jnp.zeros((n_lat, w), F32)
    cos_k = jnp.concatenate([cos, cos, one(LANES - C_ROPE)], axis=-1)
    sin_k = jnp.concatenate([-sin, sin, zero(LANES - C_ROPE)], axis=-1)
    cos_q = jnp.concatenate([one(C_NOPE), cos, cos, one(LANES - C_NOPE - C_ROPE)], axis=-1)
    sin_q = jnp.concatenate([zero(C_NOPE), -sin, sin, zero(LANES - C_NOPE - C_ROPE)], axis=-1)
    return (cos_a, sin_a), (cos_k, sin_k), (cos_q, sin_q)


_A_HEAD_ORDER = (0, 4, 1, 5, 2, 6, 3, 7)


def _perm_heads(w, axis):
    shape = w.shape
    grp = A_HEADS // A_KV_HEADS
    w = w.reshape(shape[:axis] + (A_KV_HEADS, grp, A_HEAD_DIM) + shape[axis + 1:])
    return jnp.swapaxes(w, axis, axis + 1).reshape(shape)


def _even_weights(w_in, w_out):
    offs = np.cumsum((0,) + E_COLS)
    parts = [w_in[:, offs[i]:offs[i + 1]] for i in range(len(E_COLS))]
    parts[2] = _perm_heads(parts[2], 1)
    z = parts[6]
    parts[6] = jnp.concatenate([_perm_heads(z[:, :A_WIDTH], 1), z[:, A_WIDTH:]], axis=1)
    w_out_p = jnp.concatenate([_perm_heads(w_out[:A_WIDTH], 0), w_out[A_WIDTH:]], axis=0)
    return jnp.concatenate(parts, axis=1).astype(BF16), w_out_p.astype(BF16)


def _odd_weights(w_in, w_uq, w_ukv):
    d = w_in.shape[0]
    offs = np.cumsum((0,) + O_COLS)
    ckv, kr, mk, mv, mg, cq, mq, mo, z = [w_in[:, offs[i]:offs[i + 1]] for i in range(len(O_COLS))]
    g2 = jnp.concatenate([kr, mg, jnp.zeros((d, LANES - C_ROPE - 4 * M_HEADS), w_in.dtype)], axis=1)
    w_p = jnp.concatenate([ckv, g2, mk, cq, mq, mo, z], axis=1).astype(BF16)
    wmvt = jnp.transpose(mv).astype(BF16)
    ukv = w_ukv.reshape(C_KV_LORA, C_HEADS, C_NOPE + C_VDIM)
    wk_top = jnp.pad(ukv[:, :, :C_NOPE], ((0, 0), (0, 0), (0, LANES - C_NOPE)))
    eye = jnp.pad(jnp.eye(C_ROPE, dtype=w_in.dtype), ((0, LANES - C_ROPE), (C_NOPE, LANES - C_NOPE - C_ROPE)))
    wk_bot = jnp.broadcast_to(eye[:, None, :], (LANES, C_HEADS, LANES))
    wk = jnp.concatenate([wk_top, wk_bot], axis=0).reshape(C_KV_LORA + LANES, C_HEADS * LANES).astype(BF16)
    wv = jnp.transpose(ukv[:, :, C_NOPE:].reshape(C_KV_LORA, C_WIDTH)).astype(BF16)
    uq = w_uq.reshape(C_Q_LORA, C_HEADS, C_NOPE + C_ROPE)
    wq = jnp.pad(uq, ((0, 0), (0, 0), (0, LANES - C_NOPE - C_ROPE))).reshape(C_Q_LORA, C_HEADS * LANES).astype(BF16)
    return w_p, wk, wv, wmvt, wq


def kernel(x, c, ctx, c_ctx, mod_w, mod_b, pre_norm_w, post_norm_w, e_w_in, e_sink, e_conv_w, e_w_out,
           o_w_in, o_q_norm_w, o_kv_norm_w, o_w_uq, o_w_ukv, o_i_bias, o_f_bias, o_head_norm_w, o_w_out):
    bsz, n_lat, d = x.shape
    n_ctx = ctx.shape[1]
    assert mod_w.shape[0] == 2 and d == D_MODEL, "built for one even + one odd layer"
    assert n_lat % 1024 == 0 and n_ctx % 256 == 0
    tm_in, tm_ctx = 1024, 256

    pad = (-(bsz + 1)) % 8
    cc = jnp.concatenate([c, c_ctx[None, :], jnp.zeros((pad, d), F32)], axis=0)
    mod = _modulation(cc, mod_w, mod_b)
    split = lambda l, r0, r1: [mod[l, r0:r1, k * d:(k + 1) * d][:, None, :] for k in range(3)]
    (rope_a, rope_k, rope_q) = _rope_tables(n_lat)
    dummy_tab = jnp.zeros((n_ctx, LANES), F32)

    sh_x, sc_x, g_x = split(0, 0, bsz)
    sh_c, sc_c, g_c = split(0, bsz, bsz + 1)
    w_in, w_out = _even_weights(e_w_in[0], e_w_out[0])
    nw, pw = pre_norm_w[0][None, :], post_norm_w[0][None, :]
    k, vt, q, bb, u, z = _even_in(x, sh_x, sc_x, nw, w_in, *rope_a, rope=True, tm=tm_in)
    k_c, vt_c, q_c, bb_c, u_c, z_c = _even_in(ctx, sh_c, sc_c, nw, w_in, dummy_tab, dummy_tab,
                                              rope=False, tm=tm_ctx)
    a = _even_attn(e_sink[0], q, k, vt, k_c, vt_c, window=True)
    a_c = _even_attn(e_sink[0], q_c, None, None, k_c, vt_c, window=False)
    x1 = _out_proj((a, bb, u), z, e_conv_w[0], w_out, pw, g_x, x, tm=tm_in)
    ctx1 = _out_proj((a_c, bb_c, u_c), z_c, e_conv_w[0], w_out, pw, g_c, ctx, tm=tm_ctx)

    sh_x, sc_x, g_x = split(1, 0, bsz)
    sh_c, sc_c, _ = split(1, bsz, bsz + 1)
    w_p, wk, wv, wmvt, wq = _odd_weights(o_w_in[0], o_w_uq[0], o_w_ukv[0])
    nw, pw = pre_norm_w[1][None, :], post_norm_w[1][None, :]
    kvn, qn = o_kv_norm_w[0][None, :], o_q_norm_w[0][None, :]
    chunk = min(MLSTM_CHUNK, n_ctx)
    kf, vt, mk, mvt, g2, qf, mq, mo, z = _odd_in(x1, sh_x, sc_x, nw, w_p, kvn, wk, wv, wmvt,
                                                 (*rope_k, qn, wq, *rope_q), tm=tm_in, chunk=chunk)
    kf_c, vt_c, mk_c, mvt_c, g2_c = _odd_in(ctx1, sh_c, sc_c, nw, w_p[:, :OC_CTX_END], kvn, wk, wv, wmvt, None,
                                            tm=tm_ctx, chunk=chunk)
    c_out = _mla(qf, kf_c, kf, vt_c, vt)
    gate_bias = jnp.stack([o_i_bias[0], o_f_bias[0]], axis=1).reshape(1, 4 * M_HEADS)
    bias_row = jnp.pad(gate_bias, ((0, 0), (GATE_LANE0, LANES - GATE_LANE0 - 4 * M_HEADS)))
    m_out = _mlstm(mq, mk, mvt, g2, mo, mk_c, mvt_c, g2_c, bias_row, o_head_norm_w[0][None, :])
    return _out_proj((c_out, m_out), z, None, o_w_out[0].astype(BF16), pw, g_x, x1, tm=tm_in)
```

```python
import collections
import functools

import numpy as np
import jax
import jax.numpy as jnp
from jax import lax
from jax.experimental import pallas as pl
from jax.experimental.pallas import tpu as pltpu

F32 = jnp.float32
BF16 = jnp.bfloat16
HIGHEST = lax.Precision.HIGHEST

LANES = 128
VMEM_LIMIT = 56 * 1024 * 1024

D_MODEL = 1024
GRID_W = 64
ROPE_THETA = 10000.0
NORM_EPS = 1e-6
NEG = -1e30
LOG2E = 1.4426950408889634

A_HEADS, A_KV_HEADS, A_HEAD_DIM = 8, 2, 64
A_WIDTH = A_HEADS * A_HEAD_DIM
WINDOW = 128
B_WIDTH = 512
CONV_W = 3
C_HEADS, C_NOPE, C_ROPE, C_VDIM = 8, 64, 32, 64
C_KV_LORA, C_Q_LORA = 256, 768
C_WIDTH = C_HEADS * C_VDIM
M_HEADS, M_QK, M_V = 4, 64, 128
M_WIDTH = M_HEADS * M_V
E_GATE = A_WIDTH + B_WIDTH
O_GATE = C_WIDTH + M_WIDTH
E_COLS = (128, 128, A_WIDTH, B_WIDTH, B_WIDTH, B_WIDTH, E_GATE)
E_IN = sum(E_COLS)
O_COLS = (C_KV_LORA, C_ROPE, M_HEADS * M_QK, M_WIDTH, 4 * M_HEADS, C_Q_LORA, M_HEADS * M_QK, M_WIDTH, O_GATE)

EC_K, EC_V, EC_Q, EC_BB, EC_BC, EC_BX, EC_Z, EC_END = 0, 128, 256, 768, 1280, 1792, 2304, 3328
OC_CKV = 0
OC_G2 = 256
OC_MK = 384
OC_CTX_END = 640
OC_CQ = 640
OC_MQ = 1408
OC_MO = 1664
OC_Z = 2176
OC_END = 3200
M_QKW = M_HEADS * M_QK
M_ONES = 16
GATE_LANE0 = 32
SUB_ROWS = 256
MLSTM_CHUNK = 256
ATT_Q = 256
ATT_KEYS = 256
SCORE_SLOTS = 3
SCORE_PAD_ROWS = 8
SCORE_ROWS = 128
ATT_VDIM = 64
MLA_BLK = 512
ATT_STEP = 1024


def _nt(a, b):
    return lax.dot_general(a, b, (((1,), (1,)), ((), ())), preferred_element_type=F32)


def _dot(a, b):
    return jnp.dot(a, b, preferred_element_type=F32)


def _silu(z):
    return z * jax.nn.sigmoid(z)


def _rms(x, w):
    ms = jnp.mean(x * x, axis=-1, keepdims=True)
    return x * lax.rsqrt(ms + NORM_EPS) * w


def _rope_slab(x, cos, sin_signed, half, take_up):
    up = pltpu.roll(x, LANES - half, axis=1)
    dn = pltpu.roll(x, half, axis=1)
    return x * cos + jnp.where(take_up, up, dn) * sin_signed


def _cparams(sem):
    return pltpu.CompilerParams(dimension_semantics=sem, vmem_limit_bytes=VMEM_LIMIT)


def _resident(shape):
    return pl.BlockSpec(shape, lambda b, i: (0,) * len(shape), pipeline_mode=pl.Buffered(1))


def _mod_kernel(cc_ref, w_ref, b_ref, o_ref):
    s = _silu(cc_ref[...])
    o_ref[...] = jnp.dot(s, w_ref[...], preferred_element_type=F32, precision=HIGHEST) + b_ref[...]


def _modulation(cc, mod_w, mod_b):
    depth, d, d3 = mod_w.shape
    r = cc.shape[0]
    return pl.pallas_call(
        _mod_kernel,
        grid=(depth, d3 // d),
        in_specs=[
            pl.BlockSpec((r, d), lambda l, j: (0, 0)),
            pl.BlockSpec((None, d, d), lambda l, j: (l, 0, j)),
            pl.BlockSpec((None, 1, d), lambda l, j: (l, 0, j)),
        ],
        out_specs=pl.BlockSpec((None, r, d), lambda l, j: (l, 0, j)),
        out_shape=jax.ShapeDtypeStruct((depth, r, d3), F32),
        compiler_params=_cparams(("arbitrary", "arbitrary")),
        name="modulation",
    )(cc, mod_w, mod_b.reshape(depth, 1, d3))


def _even_in_kernel(x_ref, sh_ref, sc_ref, nw_ref, w_ref, cos_ref, sin_ref,
                    k_ref, vt_ref, q_ref, bb_ref, u_ref, z_ref, *, rope):
    lane = lax.broadcasted_iota(jnp.int32, (1, LANES), 1)
    take_up = (lane % A_HEAD_DIM) < (A_HEAD_DIM // 2)
    for r0 in range(0, x_ref.shape[0], SUB_ROWS):
        rows = slice(r0, r0 + SUB_ROWS)
        h = (_rms(x_ref[rows, :], nw_ref[...]) * (1.0 + sc_ref[...]) + sh_ref[...]).astype(BF16)
        y_all = _dot(h, w_ref[...])

        def proj(a, b):
            return y_all[:, a:b]

        if rope:
            cos, sin = cos_ref[rows, :], sin_ref[rows, :]
            rot = lambda t: _rope_slab(t, cos, sin, A_HEAD_DIM // 2, take_up)
        else:
            rot = lambda t: t
        k_ref[rows, :] = rot(proj(EC_K, EC_V)).astype(BF16)
        vt_ref[:, rows] = jnp.transpose(proj(EC_V, EC_Q)).astype(BF16)
        q = proj(EC_Q, EC_BB)
        for j in range(A_WIDTH // LANES):
            sl = slice(j * LANES, (j + 1) * LANES)
            q_ref[rows, sl] = (rot(q[:, sl]) * (A_HEAD_DIM ** -0.5 * LOG2E)).astype(BF16)
        bb_ref[rows, :] = proj(EC_BB, EC_BC).astype(BF16)
        u_ref[rows, :] = (proj(EC_BC, EC_BX) * proj(EC_BX, EC_Z)).astype(BF16)
        z_ref[rows, :] = _silu(proj(EC_Z, EC_END)).astype(BF16)


def _even_in(x, sh, sc, nw, w, cos, sin, *, rope, tm):
    bsz, t, d = x.shape
    nb = sh.shape[0]
    mod_map = (lambda b, i: (b, 0, 0)) if nb > 1 else (lambda b, i: (0, 0, 0))
    tok = lambda width: pl.BlockSpec((None, tm, width), lambda b, i: (b, i, 0))
    const = lambda shape: pl.BlockSpec(shape, lambda b, i: (0,) * len(shape))
    widths = (LANES, None, A_WIDTH, B_WIDTH, B_WIDTH, E_GATE)
    vt_spec = pl.BlockSpec((None, LANES, tm), lambda b, i: (b, 0, i))
    vt_shape = jax.ShapeDtypeStruct((bsz, LANES, t), BF16)
    return pl.pallas_call(
        functools.partial(_even_in_kernel, rope=rope),
        grid=(bsz, t // tm),
        in_specs=[
            tok(d),
            pl.BlockSpec((None, 1, d), mod_map),
            pl.BlockSpec((None, 1, d), mod_map),
            const((1, d)),
            _resident(w.shape),
            pl.BlockSpec((tm, LANES), lambda b, i: (i, 0)),
            pl.BlockSpec((tm, LANES), lambda b, i: (i, 0)),
        ],
        out_specs=[vt_spec if wd is None else tok(wd) for wd in widths],
        out_shape=[vt_shape if wd is None else jax.ShapeDtypeStruct((bsz, t, wd), BF16) for wd in widths],
        compiler_params=_cparams(("parallel", "parallel")),
        name="even_in_rope" if rope else "even_in",
    )(x, sh, sc, nw, w, cos, sin)


AttnStage = collections.namedtuple("AttnStage", "q k vt mask sink")


def _tree_max(parts):
    while len(parts) > 1:
        parts = [jnp.maximum(parts[i], parts[i + 1]) if i + 1 < len(parts) else parts[i]
                 for i in range(0, len(parts), 2)]
    return parts[0]


def _attention_pipeline(stages, n_chunks, s_ref, write_pair):
    kc, sr = ATT_KEYS, SCORE_ROWS
    slots, n_keys = s_ref.shape[0], n_chunks * kc
    lag = slots - 1
    ones = jnp.ones((16, n_keys), BF16)
    m8 = {}
    halves = []
    for i in range(len(stages) + lag):
        if i < len(stages):
            cur = stages[i]
            q = cur.q()
            parts = []
            for c in range(n_chunks):
                kch, mask = cur.k(c), cur.mask(c)
                for r in range(0, kc, sr):
                    st = _nt(kch[r:r + sr], q)
                    if mask is not None:
                        st = jnp.where(mask[r:r + sr], st, NEG)
                    s_ref[i % slots, c * kc + r:c * kc + r + sr, :] = st
                    parts += [st[t:t + 8, :] for t in range(0, sr, 8)]
            m8[i] = _tree_max(parts)
        if i >= lag:
            prev = stages[i - lag]
            m_prev = jnp.max(m8.pop(i - lag), axis=0, keepdims=True)
            if prev.sink is not None:
                m_prev = jnp.maximum(m_prev, prev.sink)
            pt = jnp.exp2(s_ref[(i - lag) % slots, 0:n_keys, :] - m_prev).astype(BF16)
            vta = jnp.concatenate([prev.vt(c) for c in range(n_chunks)], axis=1)
            acc = _dot(jnp.concatenate([vta, ones], axis=0), pt)
            denom = acc[ATT_VDIM:ATT_VDIM + 1]
            if prev.sink is not None:
                denom = denom + jnp.exp2(prev.sink - m_prev)
            halves.append(acc[0:ATT_VDIM] / denom)
            if len(halves) == 2:
                write_pair((i - lag) // 2, jnp.concatenate(halves, axis=0))
                halves = []


def _even_attn_kernel(*refs, window, n_lat):
    if window:
        (sink_ref, q_ref, kp_ref, km_ref, kn_ref, kc_ref, vtp_ref, vtm_ref, vtn_ref, vtc_ref,
         o_ref, s_ref) = refs
    else:
        sink_ref, q_ref, kc_ref, vtc_ref, o_ref, s_ref = refs
    nq, kc = ATT_Q, ATT_KEYS
    step, half = q_ref.shape[0], WINDOW
    lane = lax.broadcasted_iota(jnp.int32, (1, LANES), 1)
    lo = lane < A_HEAD_DIM
    ctx_k = [kc_ref[r:r + kc, :] for r in range(0, kc_ref.shape[0], kc)]
    ctx_vt = [vtc_ref[:, r:r + kc] for r in range(0, kc_ref.shape[0], kc)]
    r = lax.broadcasted_iota(jnp.int32, (kc, nq), 0)
    c = lax.broadcasted_iota(jnp.int32, (kc, nq), 1)

    def sub_block(qb):
        k_chunks, vt_chunks, masks = list(ctx_k), list(ctx_vt), [None] * len(ctx_k)
        if window:
            base, q0 = qb * nq, pl.program_id(1) * step + qb * nq
            if qb == 0:
                k1 = jnp.concatenate([kp_ref[...], km_ref[0:half, :]], axis=0)
                vt1 = jnp.concatenate([vtp_ref[...], vtm_ref[:, 0:half]], axis=1)
            else:
                k1, vt1 = km_ref[base - half:base + half, :], vtm_ref[:, base - half:base + half]
            if base + nq == step:
                k2 = jnp.concatenate([km_ref[base + half:step, :], kn_ref[...]], axis=0)
                vt2 = jnp.concatenate([vtm_ref[:, base + half:step], vtn_ref[...]], axis=1)
            else:
                k2, vt2 = km_ref[base + half:base + half + kc, :], vtm_ref[:, base + half:base + half + kc]
            k_chunks += [k1, k2]
            vt_chunks += [vt1, vt2]
            masks += [(jnp.abs(r - half - c) <= WINDOW) & (r + (q0 - half) >= 0),
                      (jnp.abs(r + half - c) <= WINDOW) & (r + (q0 + half) < n_lat)]
        return k_chunks, vt_chunks, masks

    def stage(qb, head, chunks):
        k_chunks, vt_chunks, masks = chunks
        j, grp = head % (A_HEADS // 2), head // (A_HEADS // 2)
        keep = lo if grp == 0 else jnp.logical_not(lo)

        def q():
            qs = q_ref[qb * nq:(qb + 1) * nq, j * LANES:(j + 1) * LANES]
            return jnp.where(keep, qs, jnp.zeros_like(qs))

        return AttnStage(q=q, k=lambda ci: k_chunks[ci],
                         vt=lambda ci: vt_chunks[ci][grp * A_HEAD_DIM:(grp + 1) * A_HEAD_DIM, :],
                         mask=lambda ci: masks[ci], sink=sink_ref[head] * LOG2E)

    def write_pair(p, o_t):
        qb, j = divmod(p, A_HEADS // 2)
        o_ref[qb * nq:(qb + 1) * nq, j * LANES:(j + 1) * LANES] = jnp.transpose(o_t).astype(BF16)

    stages = []
    for qb in range(step // nq):
        chunks = sub_block(qb)
        stages += [stage(qb, h, chunks) for h in _A_HEAD_ORDER]
    _attention_pipeline(stages, len(ctx_k) + (2 if window else 0), s_ref, write_pair)


def _even_attn(sink, q, k, vt, kc, vtc, *, window):
    bsz, t, _ = q.shape
    n_ctx = kc.shape[1]
    nq, keys, half = ATT_Q, ATT_KEYS, WINDOW
    step = min(ATT_STEP, t)
    assert nq == keys == 2 * half and t % step == 0 and n_ctx % keys == 0
    per_step, last = step // half, t // half - 1
    smem = pl.BlockSpec(memory_space=pltpu.SMEM)
    qspec = pl.BlockSpec((None, step, A_WIDTH), lambda b, i: (b, i, 0))
    kcspec = pl.BlockSpec((None, n_ctx, LANES), lambda b, i: (b, 0, 0))
    vtcspec = pl.BlockSpec((None, LANES, n_ctx), lambda b, i: (b, 0, 0))
    if window:
        prev = lambda i: jnp.maximum(per_step * i - 1, 0)
        nxt = lambda i: jnp.minimum(per_step * (i + 1), last)
        specs = [smem, qspec,
                 pl.BlockSpec((None, half, LANES), lambda b, i: (b, prev(i), 0)),
                 pl.BlockSpec((None, step, LANES), lambda b, i: (b, i, 0)),
                 pl.BlockSpec((None, half, LANES), lambda b, i: (b, nxt(i), 0)),
                 kcspec,
                 pl.BlockSpec((None, LANES, half), lambda b, i: (b, 0, prev(i))),
                 pl.BlockSpec((None, LANES, step), lambda b, i: (b, 0, i)),
                 pl.BlockSpec((None, LANES, half), lambda b, i: (b, 0, nxt(i))),
                 vtcspec]
        args = (sink, q, k, k, k, kc, vt, vt, vt, vtc)
    else:
        specs = [smem, qspec, kcspec, vtcspec]
        args = (sink, q, kc, vtc)
    n_chunks = n_ctx // keys + (2 if window else 0)
    return pl.pallas_call(
        functools.partial(_even_attn_kernel, window=window, n_lat=t),
        grid=(bsz, t // step),
        in_specs=specs,
        out_specs=qspec,
        out_shape=jax.ShapeDtypeStruct((bsz, t, A_WIDTH), BF16),
        scratch_shapes=[pltpu.VMEM((SCORE_SLOTS, n_chunks * keys + SCORE_PAD_ROWS, nq), F32)],
        compiler_params=_cparams(("parallel", "parallel")),
        name="even_attn_window" if window else "even_attn_ctx",
    )(*args)


def _out_kernel(*refs, conv):
    if conv:
        a_ref, bb_ref, u_ref, up_ref, un_ref, z_ref, cw_ref, w_ref, pw_ref, g_ref, x_ref, o_ref = refs
        tm = u_ref.shape[0]
        i = pl.program_id(1)
        u = u_ref[...].astype(F32)
        prev_row = jnp.where(i > 0, up_ref[15:16, :].astype(F32), 0.0)
        next_row = jnp.where(i < pl.num_programs(1) - 1, un_ref[0:1, :].astype(F32), 0.0)
        row = lax.broadcasted_iota(jnp.int32, (tm, 1), 0)
        u_m1 = jnp.where(row == 0, prev_row, pltpu.roll(u, 1, axis=0))
        u_p1 = jnp.where(row == tm - 1, next_row, pltpu.roll(u, tm - 1, axis=0))
        cw = cw_ref[...]
        mix_b = (bb_ref[...].astype(F32) * (u_m1 * cw[0:1, :] + u * cw[1:2, :] + u_p1 * cw[2:3, :])).astype(BF16)
    else:
        a_ref, b_ref, z_ref, w_ref, pw_ref, g_ref, x_ref, o_ref = refs
        mix_b = b_ref[...]
    wa = a_ref.shape[1]
    for r0 in range(0, x_ref.shape[0], SUB_ROWS):
        rows = slice(r0, r0 + SUB_ROWS)
        gated = jnp.concatenate([a_ref[rows, :] * z_ref[rows, 0:wa], mix_b[rows, :] * z_ref[rows, wa:]], axis=1)
        o_ref[rows, :] = x_ref[rows, :] + g_ref[...] * _rms(_dot(gated, w_ref[...]), pw_ref[...])


def _out_proj(mix, z, cw, w, pw, g, x, *, tm):
    conv = cw is not None
    bsz, t, d = x.shape
    nb = g.shape[0]
    mod_map = (lambda b, i: (b, 0, 0)) if nb > 1 else (lambda b, i: (0, 0, 0))
    tok = lambda width: pl.BlockSpec((None, tm, width), lambda b, i: (b, i, 0))
    const = lambda shape: pl.BlockSpec(shape, lambda b, i: (0,) * len(shape))
    r16 = tm // 16
    if conv:
        a, bb, u = mix
        specs = [tok(a.shape[2]), tok(bb.shape[2]), tok(u.shape[2]),
                 pl.BlockSpec((None, 16, u.shape[2]), lambda b, i: (b, jnp.maximum(i * r16 - 1, 0), 0)),
                 pl.BlockSpec((None, 16, u.shape[2]), lambda b, i: (b, jnp.minimum((i + 1) * r16, t // 16 - 1), 0)),
                 tok(z.shape[2]), const(cw.shape)]
        args = (a, bb, u, u, u, z, cw)
    else:
        a, b2 = mix
        specs = [tok(a.shape[2]), tok(b2.shape[2]), tok(z.shape[2])]
        args = (a, b2, z)
    specs += [_resident(w.shape), const((1, d)), pl.BlockSpec((None, 1, d), mod_map), tok(d)]
    return pl.pallas_call(
        functools.partial(_out_kernel, conv=conv),
        grid=(bsz, t // tm),
        in_specs=specs,
        out_specs=tok(d),
        out_shape=jax.ShapeDtypeStruct((bsz, t, d), F32),
        compiler_params=_cparams(("parallel", "parallel")),
        name="out_proj_conv" if conv else "out_proj",
    )(*args, w, pw, g, x)


def _odd_in_kernel(*refs, lat):
    if lat:
        (x_ref, sh_ref, sc_ref, nw_ref, w_ref, kvn_ref, wk_ref, wvt_ref, wmvt_ref, cosk_ref, sink_ref,
         qn_ref, wq_ref, cosq_ref, sinq_ref,
         kf_ref, vt_ref, mk_ref, mvt_ref, g2_ref, qf_ref, mq_ref, mo_ref, z_ref) = refs
    else:
        (x_ref, sh_ref, sc_ref, nw_ref, w_ref, kvn_ref, wk_ref, wvt_ref, wmvt_ref,
         kf_ref, vt_ref, mk_ref, mvt_ref, g2_ref) = refs
    lane = lax.broadcasted_iota(jnp.int32, (1, LANES), 1)
    chunk = mvt_ref.shape[2]
    per_sub = SUB_ROWS // chunk
    for cc in range(x_ref.shape[0] // SUB_ROWS):
        rows = slice(cc * SUB_ROWS, (cc + 1) * SUB_ROWS)
        h = (_rms(x_ref[rows, :], nw_ref[...]) * (1.0 + sc_ref[...]) + sh_ref[...]).astype(BF16)
        y_all = _dot(h, w_ref[...])

        def proj(a, b):
            return y_all[:, a:b]

        ckv = _rms(proj(OC_CKV, OC_CKV + C_KV_LORA), kvn_ref[...]).astype(BF16)
        g2 = proj(OC_G2, OC_G2 + LANES)
        g2_ref[rows, :] = g2
        if lat:
            g2 = _rope_slab(g2, cosk_ref[rows, :], sink_ref[rows, :], C_ROPE // 2, lane < C_ROPE // 2)
        rope_lanes = (lane >= C_NOPE) & (lane < C_NOPE + C_ROPE)
        kr = jnp.where(rope_lanes, pltpu.roll(g2, C_NOPE, axis=1), 0.0)
        k_nope = _dot(ckv, wk_ref[...])
        for hd in range(C_HEADS):
            sl = slice(hd * LANES, (hd + 1) * LANES)
            kf_ref[rows, sl] = (k_nope[:, sl] + kr).astype(BF16)
        vt_ref[:, rows] = _nt(wvt_ref[...], ckv).astype(BF16)
        mk_ref[rows, :] = (proj(OC_MK, OC_MK + M_QKW) * (M_QK ** -0.5)).astype(BF16)
        mvt = _nt(wmvt_ref[...], h).astype(BF16)
        for s in range(per_sub):
            mvt_ref[cc * per_sub + s] = mvt[:, s * chunk:(s + 1) * chunk]
        if lat:
            cq = _rms(proj(OC_CQ, OC_CQ + C_Q_LORA), qn_ref[...]).astype(BF16)
            qf = _dot(cq, wq_ref[...])
            cos, sin = cosq_ref[rows, :], sinq_ref[rows, :]
            take_up = (lane >= C_NOPE) & (lane < C_NOPE + C_ROPE // 2)
            scale = (C_NOPE + C_ROPE) ** -0.5 * LOG2E
            for hd in range(C_HEADS):
                sl = slice(hd * LANES, (hd + 1) * LANES)
                qf_ref[rows, sl] = (_rope_slab(qf[:, sl], cos, sin, C_ROPE // 2, take_up) * scale).astype(BF16)
            mq_ref[rows, :] = proj(OC_MQ, OC_MQ + M_QKW).astype(BF16)
            mo_ref[rows, :] = jax.nn.sigmoid(proj(OC_MO, OC_MO + M_WIDTH)).astype(BF16)
            z_ref[rows, :] = _silu(proj(OC_Z, OC_Z + O_GATE)).astype(BF16)


def _odd_in(x, sh, sc, nw, w, kvn, wk, wv, wmvt, lat_args, *, tm, chunk):
    lat = lat_args is not None
    bsz, t, d = x.shape
    nb = sh.shape[0]
    mod_map = (lambda b, i: (b, 0, 0)) if nb > 1 else (lambda b, i: (0, 0, 0))
    tok = lambda width: (pl.BlockSpec((None, tm, width), lambda b, i: (b, i, 0)), (bsz, t, width))
    const = lambda shape: pl.BlockSpec(shape, lambda b, i: (0,) * len(shape))
    tab = pl.BlockSpec((tm, LANES), lambda b, i: (i, 0))
    specs = [tok(d)[0], pl.BlockSpec((None, 1, d), mod_map), pl.BlockSpec((None, 1, d), mod_map),
             const((1, d)), _resident(w.shape), const(kvn.shape), _resident(wk.shape), _resident(wv.shape),
             _resident(wmvt.shape)]
    args = [x, sh, sc, nw, w, kvn, wk, wv, wmvt]
    vt = (pl.BlockSpec((None, C_WIDTH, tm), lambda b, i: (b, 0, i)), (bsz, C_WIDTH, t))
    mvt = (pl.BlockSpec((None, tm // chunk, M_WIDTH, chunk), lambda b, i: (b, i, 0, 0)),
           (bsz, t // chunk, M_WIDTH, chunk))
    outs = [(tok(C_HEADS * LANES), BF16), (vt, BF16), (tok(M_QKW), BF16), (mvt, BF16), (tok(LANES), F32)]
    if lat:
        cosk, sink, qn, wq, cosq, sinq = lat_args
        specs += [tab, tab, const(qn.shape), _resident(wq.shape), tab, tab]
        args += [cosk, sink, qn, wq, cosq, sinq]
        outs += [(tok(C_HEADS * LANES), BF16), (tok(M_QKW), BF16), (tok(M_WIDTH), BF16), (tok(O_GATE), BF16)]
    return pl.pallas_call(
        functools.partial(_odd_in_kernel, lat=lat),
        grid=(bsz, t // tm),
        in_specs=specs,
        out_specs=[o[0][0] for o in outs],
        out_shape=[jax.ShapeDtypeStruct(o[0][1], o[1]) for o in outs],
        compiler_params=_cparams(("parallel", "parallel")),
        name="odd_in_lat" if lat else "odd_in_ctx",
    )(*args)


def _mla_kernel(q_ref, kc_ref, vtc_ref, *rest):
    n_pairs = C_HEADS // 2
    kx_refs, vtx_refs, (o_ref, s_ref) = rest[:n_pairs], rest[n_pairs:2 * n_pairs], rest[2 * n_pairs:]
    nq, kc = ATT_Q, ATT_KEYS
    n_ctx_chunks = kc_ref.shape[0] // kc
    n_chunks = n_ctx_chunks + kx_refs[0].shape[0] // kc

    def stage(qb, hd):
        sl = slice(hd * LANES, (hd + 1) * LANES)
        half = hd % 2

        def k(ci):
            if ci < n_ctx_chunks:
                return kc_ref[ci * kc:(ci + 1) * kc, sl]
            r0 = (ci - n_ctx_chunks) * kc
            return kx_refs[hd // 2][r0:r0 + kc, half * LANES:(half + 1) * LANES]

        def vt(ci):
            if ci < n_ctx_chunks:
                return vtc_ref[hd * C_VDIM:(hd + 1) * C_VDIM, ci * kc:(ci + 1) * kc]
            r0 = (ci - n_ctx_chunks) * kc
            return vtx_refs[hd // 2][half * C_VDIM:(half + 1) * C_VDIM, r0:r0 + kc]

        return AttnStage(q=lambda: q_ref[qb * nq:(qb + 1) * nq, sl], k=k, vt=vt, mask=lambda ci: None, sink=None)

    def write_pair(p, o_t):
        qb, j = divmod(p, C_HEADS // 2)
        o_ref[qb * nq:(qb + 1) * nq, j * LANES:(j + 1) * LANES] = jnp.transpose(o_t).astype(BF16)

    stages = [stage(qb, hd) for qb in range(q_ref.shape[0] // nq) for hd in range(C_HEADS)]
    _attention_pipeline(stages, n_chunks, s_ref, write_pair)


def _mla(qf, kfc, kfx, vtc, vtx):
    bsz, t, wq = qf.shape
    n_ctx = kfc.shape[1]
    blk = MLA_BLK
    n_pairs = C_HEADS // 2
    pair_k = lambda j: pl.BlockSpec((None, t, 2 * LANES), lambda b, i: (b, 0, j))
    pair_vt = lambda j: pl.BlockSpec((None, 2 * C_VDIM, t), lambda b, i: (b, j, 0))
    return pl.pallas_call(
        _mla_kernel,
        grid=(bsz, t // blk),
        in_specs=[
            pl.BlockSpec((None, blk, wq), lambda b, i: (b, i, 0)),
            pl.BlockSpec((None, n_ctx, wq), lambda b, i: (b, 0, 0)),
            pl.BlockSpec((None, C_WIDTH, n_ctx), lambda b, i: (b, 0, 0)),
        ] + [pair_k(j) for j in range(n_pairs)] + [pair_vt(j) for j in range(n_pairs)],
        out_specs=pl.BlockSpec((None, blk, C_WIDTH), lambda b, i: (b, i, 0)),
        out_shape=jax.ShapeDtypeStruct((bsz, t, C_WIDTH), BF16),
        scratch_shapes=[pltpu.VMEM((SCORE_SLOTS, n_ctx + t + SCORE_PAD_ROWS, ATT_Q), F32)],
        compiler_params=_cparams(("parallel", "arbitrary")),
        name="mla_attention",
    )(qf, kfc, vtc, *([kfx] * n_pairs), *([vtx] * n_pairs))


def _mlstm_kernel(mq_ref, mk_ref, mvt_ref, g_ref, mo_ref, mkc_ref, mvtc_ref, gc_ref, bias_ref, hnw_ref,
                  out_ref, s_ref, m_ref, hf_ref, hr_ref):
    nc, ncc, L = mvt_ref.shape[0], mvtc_ref.shape[0], mvt_ref.shape[2]
    row = lax.broadcasted_iota(jnp.int32, (L, L), 0)
    col = lax.broadcasted_iota(jnp.int32, (L, L), 1)
    vis = (row <= col, row >= col)
    tri_f32 = (col <= row).astype(F32)
    lane = lax.broadcasted_iota(jnp.int32, (1, LANES), 1)
    lo = lane < M_QK
    ones_rows = jnp.ones((M_ONES, L), BF16)
    fwd_lanes = lane < GATE_LANE0 + 2 * M_HEADS
    pair = lambda hd: slice((hd // 2) * LANES, (hd // 2 + 1) * LANES)
    vrows = lambda hd: slice(hd * M_V, (hd + 1) * M_V)

    def gates(g):
        gb = g + bias_ref[...]
        ls = jnp.minimum(gb, 0.0) - jnp.log1p(jnp.exp(-jnp.abs(gb)))
        cf = jnp.dot(tri_f32, ls, preferred_element_type=F32, precision=HIGHEST)
        tot = cf[L - 1:L, :]
        b = jnp.where(fwd_lanes, cf, tot - cf + ls)
        li = pltpu.roll(gb, M_HEADS, axis=1)
        return b - li, tot, jnp.transpose(b), jnp.transpose(li)

    def head_lanes(hd, pair_slab):
        keep = lo if hd % 2 == 0 else jnp.logical_not(lo)
        return jnp.where(keep, pair_slab, jnp.zeros_like(pair_slab))

    def scores(u, kh, qh):
        return _nt(jnp.concatenate([kh, s_ref[u].astype(BF16)], axis=0), qh)

    def finish(d, hd, gq, kh, vt_h, big):
        r, tot, b_t, li_t = gq
        u = d * M_HEADS + hd
        fl = GATE_LANE0 + d * 2 * M_HEADS + M_HEADS + hd
        b_row, li_row = b_t[fl:fl + 1, :], li_t[fl:fl + 1, :]
        g, r_col = tot[:, fl:fl + 1], r[:, fl:fl + 1]
        vaug = jnp.concatenate([vt_h, ones_rows], axis=0)
        s_in = s_ref[u]
        m_in = m_ref[u][0:1, 0:1]
        h_t = None
        if big is not None:
            inter = b_row + m_in
            dm = jnp.where(vis[d], b_row - r_col, NEG)
            m_t = jnp.maximum(inter, jnp.max(dm, axis=0, keepdims=True))
            sc = big[0:L] * jnp.exp(dm - m_t)
            res = jnp.exp(inter - m_t) * big[L:] + _dot(vaug, sc.astype(BF16))
            den = jnp.maximum(jnp.abs(res[M_V:M_V + 1]), jnp.exp(-m_t))
            h_t = res[0:M_V] / den
        a_row = g - b_row + li_row
        m_loc = jnp.max(a_row, axis=1, keepdims=True)
        wv = (jnp.exp(a_row - m_loc) * vaug.astype(F32)).astype(BF16)
        s_loc = _dot(wv, kh)
        m_new = jnp.maximum(g + m_in, m_loc)
        s_ref[u] = jnp.exp(g + m_in - m_new) * s_in + jnp.exp(m_loc - m_new) * s_loc
        m_ref[u] = jnp.broadcast_to(m_new, (8, LANES))
        return h_t

    s_ref[...] = jnp.zeros_like(s_ref)
    m_ref[...] = jnp.zeros_like(m_ref)
    for d in range(2):
        for cc in range(ncc):
            jj = cc if d == 0 else ncc - 1 - cc
            gq = gates(gc_ref[jj * L:(jj + 1) * L, :])
            for hd in range(M_HEADS):
                kh = head_lanes(hd, mkc_ref[jj * L:(jj + 1) * L, pair(hd)])
                finish(d, hd, gq, kh, mvtc_ref[jj, vrows(hd), :], None)

    def body(j, carry):
        chunk_of = (j, nc - 1 - j)
        rows = [pl.ds(pl.multiple_of(jj * L, L), L) for jj in chunk_of]
        gq = [gates(g_ref[rows[d], :]) for d in range(2)]
        pending = None
        for d in range(2):
            for hd in range(M_HEADS):
                kh = head_lanes(hd, mk_ref[rows[d], pair(hd)])
                big = scores(d * M_HEADS + hd, kh, head_lanes(hd, mq_ref[rows[d], pair(hd)]))
                if pending is not None:
                    pending()
                dst = hf_ref if d == 0 else hr_ref

                def pending(d=d, hd=hd, kh=kh, big=big, dst=dst):
                    dst[chunk_of[d], hd] = finish(d, hd, gq[d], kh, mvt_ref[chunk_of[d], vrows(hd), :], big)
        pending()
        return carry

    lax.fori_loop(0, nc, body, 0)

    def readout(j, carry):
        rows = pl.ds(pl.multiple_of(j * L, L), L)
        for hd in range(M_HEADS):
            hs_t = hf_ref[j, hd] + hr_ref[j, hd]
            ms = jnp.mean(hs_t * hs_t, axis=0, keepdims=True)
            hs = jnp.transpose(hs_t * lax.rsqrt(ms + NORM_EPS)) * hnw_ref[:, vrows(hd)]
            out_ref[rows, vrows(hd)] = (mo_ref[rows, vrows(hd)].astype(F32) * hs).astype(BF16)
        return carry

    lax.fori_loop(0, nc, readout, 0)


def _mlstm(mq, mk, mvt, g2, mo, mkc, mvtc, g2c, bias_row, hnw):
    bsz, t, _ = mq.shape
    n_ctx = mkc.shape[1]
    nc, _, chunk = mvt.shape[1:]
    ncc = mvtc.shape[1]
    per_b = lambda rows, width: pl.BlockSpec((None, rows, width), lambda b: (b, 0, 0))
    chunked = lambda n: pl.BlockSpec((None, n, M_WIDTH, chunk), lambda b: (b, 0, 0, 0))
    const = lambda shape: pl.BlockSpec(shape, lambda b: (0,) * len(shape))
    return pl.pallas_call(
        _mlstm_kernel,
        grid=(bsz,),
        in_specs=[per_b(t, M_QKW), per_b(t, M_QKW), chunked(nc), per_b(t, LANES), per_b(t, M_WIDTH),
                  per_b(n_ctx, M_QKW), chunked(ncc), per_b(n_ctx, LANES),
                  const((1, LANES)), const((1, M_WIDTH))],
        out_specs=per_b(t, M_WIDTH),
        out_shape=jax.ShapeDtypeStruct((bsz, t, M_WIDTH), BF16),
        scratch_shapes=[pltpu.VMEM((2 * M_HEADS, M_V + M_ONES, LANES), F32),
                        pltpu.VMEM((2 * M_HEADS, 8, LANES), F32),
                        pltpu.VMEM((nc, M_HEADS, M_V, chunk), F32),
                        pltpu.VMEM((nc, M_HEADS, M_V, chunk), F32)],
        compiler_params=_cparams(("parallel",)),
        name="mlstm",
    )(mq, mk, mvt, g2, mo, mkc, mvtc, g2c, bias_row, hnw)


def _axial_angles(n_lat, rot_dim):
    rows = n_lat // GRID_W
    row = jnp.repeat(jnp.arange(rows), GRID_W).astype(F32)
    col = jnp.tile(jnp.arange(GRID_W), rows).astype(F32)
    n_freq = rot_dim // 4
    inv = ROPE_THETA ** (-jnp.arange(n_freq, dtype=F32) / n_freq)
    ang = jnp.concatenate([row[:, None] * inv, col[:, None] * inv], axis=-1)
    return jnp.cos(ang), jnp.sin(ang)


def _rope_tables(n_lat):
    cos, sin = _axial_angles(n_lat, A_HEAD_DIM)
    cos_a = jnp.tile(jnp.concatenate([cos, cos], axis=-1), (1, 2))
    sin_a = jnp.tile(jnp.concatenate([-sin, sin], axis=-1), (1, 2))
    cos, sin = _axial_angles(n_lat, C_ROPE)
    one = lambda w: jnp.ones((n_lat, w), F32)
    zero = lambda w: jnp.zeros((n_lat, w), F32)
    cos_k = jnp.concatenate([cos, cos, one(LANES - C_ROPE)], axis=-1)
    sin_k = jnp.concatenate([-sin, sin, zero(LANES - C_ROPE)], axis=-1)
    cos_q = jnp.concatenate([one(C_NOPE), cos, cos, one(LANES - C_NOPE - C_ROPE)], axis=-1)
    sin_q = jnp.concatenate([zero(C_NOPE), -sin, sin, zero(LANES - C_NOPE - C_ROPE)], axis=-1)
    return (cos_a, sin_a), (cos_k, sin_k), (cos_q, sin_q)


_A_HEAD_ORDER = (0, 4, 1, 5, 2, 6, 3, 7)


def _perm_heads(w, axis):
    shape = w.shape
    grp = A_HEADS // A_KV_HEADS
    w = w.reshape(shape[:axis] + (A_KV_HEADS, grp, A_HEAD_DIM) + shape[axis + 1:])
    return jnp.swapaxes(w, axis, axis + 1).reshape(shape)


def _even_weights(w_in, w_out):
    offs = np.cumsum((0,) + E_COLS)
    parts = [w_in[:, offs[i]:offs[i + 1]] for i in range(len(E_COLS))]
    parts[2] = _perm_heads(parts[2], 1)
    z = parts[6]
    parts[6] = jnp.concatenate([_perm_heads(z[:, :A_WIDTH], 1), z[:, A_WIDTH:]], axis=1)
    w_out_p = jnp.concatenate([_perm_heads(w_out[:A_WIDTH], 0), w_out[A_WIDTH:]], axis=0)
    return jnp.concatenate(parts, axis=1).astype(BF16), w_out_p.astype(BF16)


def _odd_weights(w_in, w_uq, w_ukv):
    d = w_in.shape[0]
    offs = np.cumsum((0,) + O_COLS)
    ckv, kr, mk, mv, mg, cq, mq, mo, z = [w_in[:, offs[i]:offs[i + 1]] for i in range(len(O_COLS))]
    g2 = jnp.concatenate([kr, mg, jnp.zeros((d, LANES - C_ROPE - 4 * M_HEADS), w_in.dtype)], axis=1)
    w_p = jnp.concatenate([ckv, g2, mk, cq, mq, mo, z], axis=1).astype(BF16)
    wmvt = jnp.transpose(mv).astype(BF16)
    ukv = w_ukv.reshape(C_KV_LORA, C_HEADS, C_NOPE + C_VDIM)
    wk = jnp.pad(ukv[:, :, :C_NOPE], ((0, 0), (0, 0), (0, LANES - C_NOPE)))
    wk = wk.reshape(C_KV_LORA, C_HEADS * LANES).astype(BF16)
    wv = jnp.transpose(ukv[:, :, C_NOPE:].reshape(C_KV_LORA, C_WIDTH)).astype(BF16)
    uq = w_uq.reshape(C_Q_LORA, C_HEADS, C_NOPE + C_ROPE)
    wq = jnp.pad(uq, ((0, 0), (0, 0), (0, LANES - C_NOPE - C_ROPE))).reshape(C_Q_LORA, C_HEADS * LANES).astype(BF16)
    return w_p, wk, wv, wmvt, wq


def kernel(x, c, ctx, c_ctx, mod_w, mod_b, pre_norm_w, post_norm_w, e_w_in, e_sink, e_conv_w, e_w_out,
           o_w_in, o_q_norm_w, o_kv_norm_w, o_w_uq, o_w_ukv, o_i_bias, o_f_bias, o_head_norm_w, o_w_out):
    bsz, n_lat, d = x.shape
    n_ctx = ctx.shape[1]
    assert mod_w.shape[0] == 2 and d == D_MODEL, "built for one even + one odd layer"
    assert n_lat % 1024 == 0 and n_ctx % 256 == 0
    tm_in, tm_ctx = 1024, 256

    pad = (-(bsz + 1)) % 8
    cc = jnp.concatenate([c, c_ctx[None, :], jnp.zeros((pad, d), F32)], axis=0)
    mod = _modulation(cc, mod_w, mod_b)
    split = lambda l, r0, r1: [mod[l, r0:r1, k * d:(k + 1) * d][:, None, :] for k in range(3)]
    (rope_a, rope_k, rope_q) = _rope_tables(n_lat)
    dummy_tab = jnp.zeros((n_ctx, LANES), F32)

    sh_x, sc_x, g_x = split(0, 0, bsz)
    sh_c, sc_c, g_c = split(0, bsz, bsz + 1)
    w_in, w_out = _even_weights(e_w_in[0], e_w_out[0])
    nw, pw = pre_norm_w[0][None, :], post_norm_w[0][None, :]
    k, vt, q, bb, u, z = _even_in(x, sh_x, sc_x, nw, w_in, *rope_a, rope=True, tm=tm_in)
    k_c, vt_c, q_c, bb_c, u_c, z_c = _even_in(ctx, sh_c, sc_c, nw, w_in, dummy_tab, dummy_tab,
                                              rope=False, tm=tm_ctx)
    a = _even_attn(e_sink[0], q, k, vt, k_c, vt_c, window=True)
    a_c = _even_attn(e_sink[0], q_c, None, None, k_c, vt_c, window=False)
    x1 = _out_proj((a, bb, u), z, e_conv_w[0], w_out, pw, g_x, x, tm=tm_in)
    ctx1 = _out_proj((a_c, bb_c, u_c), z_c, e_conv_w[0], w_out, pw, g_c, ctx, tm=tm_ctx)

    sh_x, sc_x, g_x = split(1, 0, bsz)
    sh_c, sc_c, _ = split(1, bsz, bsz + 1)
    w_p, wk, wv, wmvt, wq = _odd_weights(o_w_in[0], o_w_uq[0], o_w_ukv[0])
    nw, pw = pre_norm_w[1][None, :], post_norm_w[1][None, :]
    kvn, qn = o_kv_norm_w[0][None, :], o_q_norm_w[0][None, :]
    chunk = min(MLSTM_CHUNK, n_ctx)
    kf, vt, mk, mvt, g2, qf, mq, mo, z = _odd_in(x1, sh_x, sc_x, nw, w_p, kvn, wk, wv, wmvt,
                                                 (*rope_k, qn, wq, *rope_q), tm=tm_in, chunk=chunk)
    kf_c, vt_c, mk_c, mvt_c, g2_c = _odd_in(ctx1, sh_c, sc_c, nw, w_p[:, :OC_CTX_END], kvn, wk, wv, wmvt, None,
                                            tm=tm_ctx, chunk=chunk)
    c_out = _mla(qf, kf_c, kf, vt_c, vt)
    gate_bias = jnp.stack([o_i_bias[0], o_f_bias[0]], axis=1).reshape(1, 4 * M_HEADS)
    bias_row = jnp.pad(gate_bias, ((0, 0), (GATE_LANE0, LANES - GATE_LANE0 - 4 * M_HEADS)))
    m_out = _mlstm(mq, mk, mvt, g2, mo, mk_c, mvt_c, g2_c, bias_row, o_head_norm_w[0][None, :])
    return _out_proj((c_out, m_out), z, None, o_w_out[0].astype(BF16), pw, g_x, x1, tm=tm_in)
```

```python
import collections
import functools

import numpy as np
import jax
import jax.numpy as jnp
from jax import lax
from jax.experimental import pallas as pl
from jax.experimental.pallas import tpu as pltpu

F32 = jnp.float32
BF16 = jnp.bfloat16
HIGHEST = lax.Precision.HIGHEST

LANES = 128
VMEM_LIMIT = 56 * 1024 * 1024

D_MODEL = 1024
GRID_W = 64
ROPE_THETA = 10000.0
NORM_EPS = 1e-6
NEG = -1e30
LOG2E = 1.4426950408889634

A_HEADS, A_KV_HEADS, A_HEAD_DIM = 8, 2, 64
A_WIDTH = A_HEADS * A_HEAD_DIM
WINDOW = 128
B_WIDTH = 512
CONV_W = 3
C_HEADS, C_NOPE, C_ROPE, C_VDIM = 8, 64, 32, 64
C_KV_LORA, C_Q_LORA = 256, 768
C_WIDTH = C_HEADS * C_VDIM
M_HEADS, M_QK, M_V = 4, 64, 128
M_WIDTH = M_HEADS * M_V
E_GATE = A_WIDTH + B_WIDTH
O_GATE = C_WIDTH + M_WIDTH
E_COLS = (128, 128, A_WIDTH, B_WIDTH, B_WIDTH, B_WIDTH, E_GATE)
E_IN = sum(E_COLS)
O_COLS = (C_KV_LORA, C_ROPE, M_HEADS * M_QK, M_WIDTH, 4 * M_HEADS, C_Q_LORA, M_HEADS * M_QK, M_WIDTH, O_GATE)

EC_K, EC_V, EC_Q, EC_BB, EC_BC, EC_BX, EC_Z, EC_END = 0, 128, 256, 768, 1280, 1792, 2304, 3328
OC_CKV = 0
OC_G2 = 256
OC_MK = 384
OC_CTX_END = 640
OC_CQ = 640
OC_MQ = 1408
OC_MO = 1664
OC_Z = 2176
OC_END = 3200
M_QKW = M_HEADS * M_QK
M_ONES = 16
GATE_LANE0 = 32
SUB_ROWS = 256
MLSTM_CHUNK = 256
ATT_Q = 256
ATT_KEYS = 256
SCORE_SLOTS = 3
SCORE_ROWS = 128
ATT_VDIM = 64
MLA_BLK = 512
ATT_STEP = 1024


def _nt(a, b):
    return lax.dot_general(a, b, (((1,), (1,)), ((), ())), preferred_element_type=F32)


def _dot(a, b):
    return jnp.dot(a, b, preferred_element_type=F32)


def _silu(z):
    return z * jax.nn.sigmoid(z)


def _rms(x, w):
    ms = jnp.mean(x * x, axis=-1, keepdims=True)
    return x * lax.rsqrt(ms + NORM_EPS) * w


def _rope_slab(x, cos, sin_signed, half, take_up):
    up = pltpu.roll(x, LANES - half, axis=1)
    dn = pltpu.roll(x, half, axis=1)
    return x * cos + jnp.where(take_up, up, dn) * sin_signed


def _cparams(sem):
    return pltpu.CompilerParams(dimension_semantics=sem, vmem_limit_bytes=VMEM_LIMIT)


def _resident(shape):
    return pl.BlockSpec(shape, lambda b, i: (0,) * len(shape), pipeline_mode=pl.Buffered(1))


def _mod_kernel(cc_ref, w_ref, b_ref, o_ref):
    s = _silu(cc_ref[...])
    o_ref[...] = jnp.dot(s, w_ref[...], preferred_element_type=F32, precision=HIGHEST) + b_ref[...]


def _modulation(cc, mod_w, mod_b):
    depth, d, d3 = mod_w.shape
    r = cc.shape[0]
    return pl.pallas_call(
        _mod_kernel,
        grid=(depth, d3 // d),
        in_specs=[
            pl.BlockSpec((r, d), lambda l, j: (0, 0)),
            pl.BlockSpec((None, d, d), lambda l, j: (l, 0, j)),
            pl.BlockSpec((None, 1, d), lambda l, j: (l, 0, j)),
        ],
        out_specs=pl.BlockSpec((None, r, d), lambda l, j: (l, 0, j)),
        out_shape=jax.ShapeDtypeStruct((depth, r, d3), F32),
        compiler_params=_cparams(("arbitrary", "arbitrary")),
        name="modulation",
    )(cc, mod_w, mod_b.reshape(depth, 1, d3))


def _even_in_kernel(x_ref, sh_ref, sc_ref, nw_ref, w_ref, cos_ref, sin_ref,
                    k_ref, vt_ref, q_ref, bb_ref, u_ref, z_ref, *, rope):
    lane = lax.broadcasted_iota(jnp.int32, (1, LANES), 1)
    take_up = (lane % A_HEAD_DIM) < (A_HEAD_DIM // 2)
    for r0 in range(0, x_ref.shape[0], SUB_ROWS):
        rows = slice(r0, r0 + SUB_ROWS)
        h = (_rms(x_ref[rows, :], nw_ref[...]) * (1.0 + sc_ref[...]) + sh_ref[...]).astype(BF16)
        y_all = _dot(h, w_ref[...])

        def proj(a, b):
            return y_all[:, a:b]

        if rope:
            cos, sin = cos_ref[rows, :], sin_ref[rows, :]
            rot = lambda t: _rope_slab(t, cos, sin, A_HEAD_DIM // 2, take_up)
        else:
            rot = lambda t: t
        k_ref[rows, :] = rot(proj(EC_K, EC_V)).astype(BF16)
        vt_ref[:, rows] = jnp.transpose(proj(EC_V, EC_Q)).astype(BF16)
        q = proj(EC_Q, EC_BB)
        for j in range(A_WIDTH // LANES):
            sl = slice(j * LANES, (j + 1) * LANES)
            q_ref[rows, sl] = (rot(q[:, sl]) * (A_HEAD_DIM ** -0.5 * LOG2E)).astype(BF16)
        bb_ref[rows, :] = proj(EC_BB, EC_BC).astype(BF16)
        u_ref[rows, :] = (proj(EC_BC, EC_BX) * proj(EC_BX, EC_Z)).astype(BF16)
        z_ref[rows, :] = _silu(proj(EC_Z, EC_END)).astype(BF16)


def _even_in(x, sh, sc, nw, w, cos, sin, *, rope, tm):
    bsz, t, d = x.shape
    nb = sh.shape[0]
    mod_map = (lambda b, i: (b, 0, 0)) if nb > 1 else (lambda b, i: (0, 0, 0))
    tok = lambda width: pl.BlockSpec((None, tm, width), lambda b, i: (b, i, 0))
    const = lambda shape: pl.BlockSpec(shape, lambda b, i: (0,) * len(shape))
    widths = (LANES, None, A_WIDTH, B_WIDTH, B_WIDTH, E_GATE)
    vt_spec = pl.BlockSpec((None, LANES, tm), lambda b, i: (b, 0, i))
    vt_shape = jax.ShapeDtypeStruct((bsz, LANES, t), BF16)
    return pl.pallas_call(
        functools.partial(_even_in_kernel, rope=rope),
        grid=(bsz, t // tm),
        in_specs=[
            tok(d),
            pl.BlockSpec((None, 1, d), mod_map),
            pl.BlockSpec((None, 1, d), mod_map),
            const((1, d)),
            _resident(w.shape),
            pl.BlockSpec((tm, LANES), lambda b, i: (i, 0)),
            pl.BlockSpec((tm, LANES), lambda b, i: (i, 0)),
        ],
        out_specs=[vt_spec if wd is None else tok(wd) for wd in widths],
        out_shape=[vt_shape if wd is None else jax.ShapeDtypeStruct((bsz, t, wd), BF16) for wd in widths],
        compiler_params=_cparams(("parallel", "parallel")),
        name="even_in_rope" if rope else "even_in",
    )(x, sh, sc, nw, w, cos, sin)


AttnStage = collections.namedtuple("AttnStage", "q k vt mask sink")


def _tree_max(parts):
    while len(parts) > 1:
        parts = [jnp.maximum(parts[i], parts[i + 1]) if i + 1 < len(parts) else parts[i]
                 for i in range(0, len(parts), 2)]
    return parts[0]


def _attention_pipeline(stages, n_chunks, s_ref, write_pair):
    kc, sr = ATT_KEYS, SCORE_ROWS
    slots, n_keys = s_ref.shape[0], n_chunks * kc
    lag = slots - 1
    ones = jnp.ones((16, n_keys), BF16)
    m8 = {}
    halves = []
    for i in range(len(stages) + lag):
        if i < len(stages):
            cur = stages[i]
            q = cur.q()
            parts = []
            for c in range(n_chunks):
                kch, mask = cur.k(c), cur.mask(c)
                for r in range(0, kc, sr):
                    st = _nt(kch[r:r + sr], q)
                    if mask is not None:
                        st = jnp.where(mask[r:r + sr], st, NEG)
                    s_ref[i % slots, c * kc + r:c * kc + r + sr, :] = st
                    parts += [st[t:t + 8, :] for t in range(0, sr, 8)]
            m8[i] = _tree_max(parts)
        if i >= lag:
            prev = stages[i - lag]
            m_prev = jnp.max(m8.pop(i - lag), axis=0, keepdims=True)
            if prev.sink is not None:
                m_prev = jnp.maximum(m_prev, prev.sink)
            pt = jnp.exp2(s_ref[(i - lag) % slots, 0:n_keys, :] - m_prev).astype(BF16)
            vta = jnp.concatenate([prev.vt(c) for c in range(n_chunks)], axis=1)
            acc = _dot(jnp.concatenate([vta, ones], axis=0), pt)
            denom = acc[ATT_VDIM:ATT_VDIM + 1]
            if prev.sink is not None:
                denom = denom + jnp.exp2(prev.sink - m_prev)
            halves.append(acc[0:ATT_VDIM] / denom)
            if len(halves) == 2:
                write_pair((i - lag) // 2, jnp.concatenate(halves, axis=0))
                halves = []


def _even_attn_kernel(*refs, window, n_lat):
    if window:
        (sink_ref, q_ref, kp_ref, km_ref, kn_ref, kc_ref, vtp_ref, vtm_ref, vtn_ref, vtc_ref,
         o_ref, s_ref) = refs
    else:
        sink_ref, q_ref, kc_ref, vtc_ref, o_ref, s_ref = refs
    nq, kc = ATT_Q, ATT_KEYS
    step, half = q_ref.shape[0], WINDOW
    lane = lax.broadcasted_iota(jnp.int32, (1, LANES), 1)
    lo = lane < A_HEAD_DIM
    ctx_k = [kc_ref[r:r + kc, :] for r in range(0, kc_ref.shape[0], kc)]
    ctx_vt = [vtc_ref[:, r:r + kc] for r in range(0, kc_ref.shape[0], kc)]
    r = lax.broadcasted_iota(jnp.int32, (kc, nq), 0)
    c = lax.broadcasted_iota(jnp.int32, (kc, nq), 1)

    def sub_block(qb):
        k_chunks, vt_chunks, masks = list(ctx_k), list(ctx_vt), [None] * len(ctx_k)
        if window:
            base, q0 = qb * nq, pl.program_id(1) * step + qb * nq
            if qb == 0:
                k1 = jnp.concatenate([kp_ref[...], km_ref[0:half, :]], axis=0)
                vt1 = jnp.concatenate([vtp_ref[...], vtm_ref[:, 0:half]], axis=1)
            else:
                k1, vt1 = km_ref[base - half:base + half, :], vtm_ref[:, base - half:base + half]
            if base + nq == step:
                k2 = jnp.concatenate([km_ref[base + half:step, :], kn_ref[...]], axis=0)
                vt2 = jnp.concatenate([vtm_ref[:, base + half:step], vtn_ref[...]], axis=1)
            else:
                k2, vt2 = km_ref[base + half:base + half + kc, :], vtm_ref[:, base + half:base + half + kc]
            k_chunks += [k1, k2]
            vt_chunks += [vt1, vt2]
            masks += [(jnp.abs(r - half - c) <= WINDOW) & (r + (q0 - half) >= 0),
                      (jnp.abs(r + half - c) <= WINDOW) & (r + (q0 + half) < n_lat)]
        return k_chunks, vt_chunks, masks

    def stage(qb, head, chunks):
        k_chunks, vt_chunks, masks = chunks
        j, grp = head % (A_HEADS // 2), head // (A_HEADS // 2)
        keep = lo if grp == 0 else jnp.logical_not(lo)

        def q():
            qs = q_ref[qb * nq:(qb + 1) * nq, j * LANES:(j + 1) * LANES]
            return jnp.where(keep, qs, jnp.zeros_like(qs))

        return AttnStage(q=q, k=lambda ci: k_chunks[ci],
                         vt=lambda ci: vt_chunks[ci][grp * A_HEAD_DIM:(grp + 1) * A_HEAD_DIM, :],
                         mask=lambda ci: masks[ci], sink=sink_ref[head] * LOG2E)

    def write_pair(p, o_t):
        qb, j = divmod(p, A_HEADS // 2)
        o_ref[qb * nq:(qb + 1) * nq, j * LANES:(j + 1) * LANES] = jnp.transpose(o_t).astype(BF16)

    stages = []
    for qb in range(step // nq):
        chunks = sub_block(qb)
        stages += [stage(qb, h, chunks) for h in _A_HEAD_ORDER]
    _attention_pipeline(stages, len(ctx_k) + (2 if window else 0), s_ref, write_pair)


def _even_attn(sink, q, k, vt, kc, vtc, *, window):
    bsz, t, _ = q.shape
    n_ctx = kc.shape[1]
    nq, keys, half = ATT_Q, ATT_KEYS, WINDOW
    step = min(ATT_STEP, t)
    assert nq == keys == 2 * half and t % step == 0 and n_ctx % keys == 0
    per_step, last = step // half, t // half - 1
    smem = pl.BlockSpec(memory_space=pltpu.SMEM)
    qspec = pl.BlockSpec((None, step, A_WIDTH), lambda b, i: (b, i, 0))
    kcspec = pl.BlockSpec((None, n_ctx, LANES), lambda b, i: (b, 0, 0))
    vtcspec = pl.BlockSpec((None, LANES, n_ctx), lambda b, i: (b, 0, 0))
    if window:
        prev = lambda i: jnp.maximum(per_step * i - 1, 0)
        nxt = lambda i: jnp.minimum(per_step * (i + 1), last)
        specs = [smem, qspec,
                 pl.BlockSpec((None, half, LANES), lambda b, i: (b, prev(i), 0)),
                 pl.BlockSpec((None, step, LANES), lambda b, i: (b, i, 0)),
                 pl.BlockSpec((None, half, LANES), lambda b, i: (b, nxt(i), 0)),
                 kcspec,
                 pl.BlockSpec((None, LANES, half), lambda b, i: (b, 0, prev(i))),
                 pl.BlockSpec((None, LANES, step), lambda b, i: (b, 0, i)),
                 pl.BlockSpec((None, LANES, half), lambda b, i: (b, 0, nxt(i))),
                 vtcspec]
        args = (sink, q, k, k, k, kc, vt, vt, vt, vtc)
    else:
        specs = [smem, qspec, kcspec, vtcspec]
        args = (sink, q, kc, vtc)
    n_chunks = n_ctx // keys + (2 if window else 0)
    return pl.pallas_call(
        functools.partial(_even_attn_kernel, window=window, n_lat=t),
        grid=(bsz, t // step),
        in_specs=specs,
        out_specs=qspec,
        out_shape=jax.ShapeDtypeStruct((bsz, t, A_WIDTH), BF16),
        scratch_shapes=[pltpu.VMEM((SCORE_SLOTS, n_chunks * keys, nq), F32)],
        compiler_params=_cparams(("parallel", "parallel")),
        name="even_attn_window" if window else "even_attn_ctx",
    )(*args)


def _out_kernel(*refs, conv):
    if conv:
        a_ref, bb_ref, u_ref, up_ref, un_ref, z_ref, cw_ref, w_ref, pw_ref, g_ref, x_ref, o_ref = refs
        tm = u_ref.shape[0]
        i = pl.program_id(1)
        u = u_ref[...].astype(F32)
        prev_row = jnp.where(i > 0, up_ref[15:16, :].astype(F32), 0.0)
        next_row = jnp.where(i < pl.num_programs(1) - 1, un_ref[0:1, :].astype(F32), 0.0)
        row = lax.broadcasted_iota(jnp.int32, (tm, 1), 0)
        u_m1 = jnp.where(row == 0, prev_row, pltpu.roll(u, 1, axis=0))
        u_p1 = jnp.where(row == tm - 1, next_row, pltpu.roll(u, tm - 1, axis=0))
        cw = cw_ref[...]
        mix_b = (bb_ref[...].astype(F32) * (u_m1 * cw[0:1, :] + u * cw[1:2, :] + u_p1 * cw[2:3, :])).astype(BF16)
    else:
        a_ref, b_ref, z_ref, w_ref, pw_ref, g_ref, x_ref, o_ref = refs
        mix_b = b_ref[...]
    wa = a_ref.shape[1]
    for r0 in range(0, x_ref.shape[0], SUB_ROWS):
        rows = slice(r0, r0 + SUB_ROWS)
        gated = jnp.concatenate([a_ref[rows, :] * z_ref[rows, 0:wa], mix_b[rows, :] * z_ref[rows, wa:]], axis=1)
        o_ref[rows, :] = x_ref[rows, :] + g_ref[...] * _rms(_dot(gated, w_ref[...]), pw_ref[...])


def _out_proj(mix, z, cw, w, pw, g, x, *, tm):
    conv = cw is not None
    bsz, t, d = x.shape
    nb = g.shape[0]
    mod_map = (lambda b, i: (b, 0, 0)) if nb > 1 else (lambda b, i: (0, 0, 0))
    tok = lambda width: pl.BlockSpec((None, tm, width), lambda b, i: (b, i, 0))
    const = lambda shape: pl.BlockSpec(shape, lambda b, i: (0,) * len(shape))
    r16 = tm // 16
    if conv:
        a, bb, u = mix
        specs = [tok(a.shape[2]), tok(bb.shape[2]), tok(u.shape[2]),
                 pl.BlockSpec((None, 16, u.shape[2]), lambda b, i: (b, jnp.maximum(i * r16 - 1, 0), 0)),
                 pl.BlockSpec((None, 16, u.shape[2]), lambda b, i: (b, jnp.minimum((i + 1) * r16, t // 16 - 1), 0)),
                 tok(z.shape[2]), const(cw.shape)]
        args = (a, bb, u, u, u, z, cw)
    else:
        a, b2 = mix
        specs = [tok(a.shape[2]), tok(b2.shape[2]), tok(z.shape[2])]
        args = (a, b2, z)
    specs += [_resident(w.shape), const((1, d)), pl.BlockSpec((None, 1, d), mod_map), tok(d)]
    return pl.pallas_call(
        functools.partial(_out_kernel, conv=conv),
        grid=(bsz, t // tm),
        in_specs=specs,
        out_specs=tok(d),
        out_shape=jax.ShapeDtypeStruct((bsz, t, d), F32),
        compiler_params=_cparams(("parallel", "parallel")),
        name="out_proj_conv" if conv else "out_proj",
    )(*args, w, pw, g, x)


def _odd_in_kernel(*refs, lat):
    if lat:
        (x_ref, sh_ref, sc_ref, nw_ref, w_ref, kvn_ref, wk_ref, wvt_ref, wmvt_ref, cosk_ref, sink_ref,
         qn_ref, wq_ref, cosq_ref, sinq_ref,
         kf_ref, vt_ref, mk_ref, mvt_ref, g2_ref, qf_ref, mq_ref, mo_ref, z_ref) = refs
    else:
        (x_ref, sh_ref, sc_ref, nw_ref, w_ref, kvn_ref, wk_ref, wvt_ref, wmvt_ref,
         kf_ref, vt_ref, mk_ref, mvt_ref, g2_ref) = refs
    lane = lax.broadcasted_iota(jnp.int32, (1, LANES), 1)
    chunk = mvt_ref.shape[2]
    per_sub = SUB_ROWS // chunk
    for cc in range(x_ref.shape[0] // SUB_ROWS):
        rows = slice(cc * SUB_ROWS, (cc + 1) * SUB_ROWS)
        h = (_rms(x_ref[rows, :], nw_ref[...]) * (1.0 + sc_ref[...]) + sh_ref[...]).astype(BF16)
        y_all = _dot(h, w_ref[...])

        def proj(a, b):
            return y_all[:, a:b]

        ckv = _rms(proj(OC_CKV, OC_CKV + C_KV_LORA), kvn_ref[...]).astype(BF16)
        g2 = proj(OC_G2, OC_G2 + LANES)
        g2_ref[rows, :] = g2
        if lat:
            g2 = _rope_slab(g2, cosk_ref[rows, :], sink_ref[rows, :], C_ROPE // 2, lane < C_ROPE // 2)
        rope_lanes = (lane >= C_NOPE) & (lane < C_NOPE + C_ROPE)
        kr = jnp.where(rope_lanes, pltpu.roll(g2, C_NOPE, axis=1), 0.0)
        k_nope = _dot(ckv, wk_ref[...])
        for hd in range(C_HEADS):
            sl = slice(hd * LANES, (hd + 1) * LANES)
            kf_ref[rows, sl] = (k_nope[:, sl] + kr).astype(BF16)
        vt_ref[:, rows] = _nt(wvt_ref[...], ckv).astype(BF16)
        mk_ref[rows, :] = (proj(OC_MK, OC_MK + M_QKW) * (M_QK ** -0.5)).astype(BF16)
        mvt = _nt(wmvt_ref[...], h).astype(BF16)
        for s in range(per_sub):
            mvt_ref[cc * per_sub + s] = mvt[:, s * chunk:(s + 1) * chunk]
        if lat:
            cq = _rms(proj(OC_CQ, OC_CQ + C_Q_LORA), qn_ref[...]).astype(BF16)
            qf = _dot(cq, wq_ref[...])
            cos, sin = cosq_ref[rows, :], sinq_ref[rows, :]
            take_up = (lane >= C_NOPE) & (lane < C_NOPE + C_ROPE // 2)
            scale = (C_NOPE + C_ROPE) ** -0.5 * LOG2E
            for hd in range(C_HEADS):
                sl = slice(hd * LANES, (hd + 1) * LANES)
                qf_ref[rows, sl] = (_rope_slab(qf[:, sl], cos, sin, C_ROPE // 2, take_up) * scale).astype(BF16)
            mq_ref[rows, :] = proj(OC_MQ, OC_MQ + M_QKW).astype(BF16)
            mo_ref[rows, :] = jax.nn.sigmoid(proj(OC_MO, OC_MO + M_WIDTH)).astype(BF16)
            z_ref[rows, :] = _silu(proj(OC_Z, OC_Z + O_GATE)).astype(BF16)


def _odd_in(x, sh, sc, nw, w, kvn, wk, wv, wmvt, lat_args, *, tm, chunk):
    lat = lat_args is not None
    bsz, t, d = x.shape
    nb = sh.shape[0]
    mod_map = (lambda b, i: (b, 0, 0)) if nb > 1 else (lambda b, i: (0, 0, 0))
    tok = lambda width: (pl.BlockSpec((None, tm, width), lambda b, i: (b, i, 0)), (bsz, t, width))
    const = lambda shape: pl.BlockSpec(shape, lambda b, i: (0,) * len(shape))
    tab = pl.BlockSpec((tm, LANES), lambda b, i: (i, 0))
    specs = [tok(d)[0], pl.BlockSpec((None, 1, d), mod_map), pl.BlockSpec((None, 1, d), mod_map),
             const((1, d)), _resident(w.shape), const(kvn.shape), _resident(wk.shape), _resident(wv.shape),
             _resident(wmvt.shape)]
    args = [x, sh, sc, nw, w, kvn, wk, wv, wmvt]
    vt = (pl.BlockSpec((None, C_WIDTH, tm), lambda b, i: (b, 0, i)), (bsz, C_WIDTH, t))
    mvt = (pl.BlockSpec((None, tm // chunk, M_WIDTH, chunk), lambda b, i: (b, i, 0, 0)),
           (bsz, t // chunk, M_WIDTH, chunk))
    outs = [(tok(C_HEADS * LANES), BF16), (vt, BF16), (tok(M_QKW), BF16), (mvt, BF16), (tok(LANES), F32)]
    if lat:
        cosk, sink, qn, wq, cosq, sinq = lat_args
        specs += [tab, tab, const(qn.shape), _resident(wq.shape), tab, tab]
        args += [cosk, sink, qn, wq, cosq, sinq]
        outs += [(tok(C_HEADS * LANES), BF16), (tok(M_QKW), BF16), (tok(M_WIDTH), BF16), (tok(O_GATE), BF16)]
    return pl.pallas_call(
        functools.partial(_odd_in_kernel, lat=lat),
        grid=(bsz, t // tm),
        in_specs=specs,
        out_specs=[o[0][0] for o in outs],
        out_shape=[jax.ShapeDtypeStruct(o[0][1], o[1]) for o in outs],
        compiler_params=_cparams(("parallel", "parallel")),
        name="odd_in_lat" if lat else "odd_in_ctx",
    )(*args)


def _mla_kernel(q_ref, kc_ref, vtc_ref, *rest):
    n_pairs = C_HEADS // 2
    kx_refs, vtx_refs, (o_ref, s_ref) = rest[:n_pairs], rest[n_pairs:2 * n_pairs], rest[2 * n_pairs:]
    nq, kc = ATT_Q, ATT_KEYS
    n_ctx_chunks = kc_ref.shape[0] // kc
    n_chunks = n_ctx_chunks + kx_refs[0].shape[0] // kc

    def stage(qb, hd):
        sl = slice(hd * LANES, (hd + 1) * LANES)
        half = hd % 2

        def k(ci):
            if ci < n_ctx_chunks:
                return kc_ref[ci * kc:(ci + 1) * kc, sl]
            r0 = (ci - n_ctx_chunks) * kc
            return kx_refs[hd // 2][r0:r0 + kc, half * LANES:(half + 1) * LANES]

        def vt(ci):
            if ci < n_ctx_chunks:
                return vtc_ref[hd * C_VDIM:(hd + 1) * C_VDIM, ci * kc:(ci + 1) * kc]
            r0 = (ci - n_ctx_chunks) * kc
            return vtx_refs[hd // 2][half * C_VDIM:(half + 1) * C_VDIM, r0:r0 + kc]

        return AttnStage(q=lambda: q_ref[qb * nq:(qb + 1) * nq, sl], k=k, vt=vt, mask=lambda ci: None, sink=None)

    def write_pair(p, o_t):
        qb, j = divmod(p, C_HEADS // 2)
        o_ref[qb * nq:(qb + 1) * nq, j * LANES:(j + 1) * LANES] = jnp.transpose(o_t).astype(BF16)

    stages = [stage(qb, hd) for qb in range(q_ref.shape[0] // nq) for hd in range(C_HEADS)]
    _attention_pipeline(stages, n_chunks, s_ref, write_pair)


def _mla(qf, kfc, kfx, vtc, vtx):
    bsz, t, wq = qf.shape
    n_ctx = kfc.shape[1]
    blk = MLA_BLK
    n_pairs = C_HEADS // 2
    pair_k = lambda j: pl.BlockSpec((None, t, 2 * LANES), lambda b, i: (b, 0, j))
    pair_vt = lambda j: pl.BlockSpec((None, 2 * C_VDIM, t), lambda b, i: (b, j, 0))
    return pl.pallas_call(
        _mla_kernel,
        grid=(bsz, t // blk),
        in_specs=[
            pl.BlockSpec((None, blk, wq), lambda b, i: (b, i, 0)),
            pl.BlockSpec((None, n_ctx, wq), lambda b, i: (b, 0, 0)),
            pl.BlockSpec((None, C_WIDTH, n_ctx), lambda b, i: (b, 0, 0)),
        ] + [pair_k(j) for j in range(n_pairs)] + [pair_vt(j) for j in range(n_pairs)],
        out_specs=pl.BlockSpec((None, blk, C_WIDTH), lambda b, i: (b, i, 0)),
        out_shape=jax.ShapeDtypeStruct((bsz, t, C_WIDTH), BF16),
        scratch_shapes=[pltpu.VMEM((SCORE_SLOTS, n_ctx + t, ATT_Q), F32)],
        compiler_params=_cparams(("parallel", "arbitrary")),
        name="mla_attention",
    )(qf, kfc, vtc, *([kfx] * n_pairs), *([vtx] * n_pairs))


def _mlstm_kernel(mq_ref, mk_ref, mvt_ref, g_ref, mo_ref, mkc_ref, mvtc_ref, gc_ref, bias_ref, hnw_ref,
                  out_ref, s_ref, m_ref, hf_ref, hr_ref):
    nc, ncc, L = mvt_ref.shape[0], mvtc_ref.shape[0], mvt_ref.shape[2]
    row = lax.broadcasted_iota(jnp.int32, (L, L), 0)
    col = lax.broadcasted_iota(jnp.int32, (L, L), 1)
    vis = (row <= col, row >= col)
    tri_ones = (col <= row).astype(BF16)
    lane = lax.broadcasted_iota(jnp.int32, (1, LANES), 1)
    lo = lane < M_QK
    ones_rows = jnp.ones((M_ONES, L), BF16)
    fwd_lanes = lane < GATE_LANE0 + 2 * M_HEADS
    pair = lambda hd: slice((hd // 2) * LANES, (hd // 2 + 1) * LANES)
    vrows = lambda hd: slice(hd * M_V, (hd + 1) * M_V)

    def gates(g):
        gb = g + bias_ref[...]
        ls = jnp.minimum(gb, 0.0) - jnp.log1p(jnp.exp(-jnp.abs(gb)))
        hi = ls.astype(BF16)
        lo = (ls - hi.astype(F32)).astype(BF16)
        cf = _dot(tri_ones, hi) + _dot(tri_ones, lo)
        tot = cf[L - 1:L, :]
        b = jnp.where(fwd_lanes, cf, tot - cf + ls)
        li = pltpu.roll(gb, M_HEADS, axis=1)
        return b - li, tot, jnp.transpose(b), jnp.transpose(li)

    def head_lanes(hd, pair_slab):
        keep = lo if hd % 2 == 0 else jnp.logical_not(lo)
        return jnp.where(keep, pair_slab, jnp.zeros_like(pair_slab))

    def scores(u, kh, qh):
        return _nt(jnp.concatenate([kh, s_ref[u].astype(BF16)], axis=0), qh)

    def finish(d, hd, gq, kh, vt_h, big):
        r, tot, b_t, li_t = gq
        u = d * M_HEADS + hd
        fl = GATE_LANE0 + d * 2 * M_HEADS + M_HEADS + hd
        b_row, li_row = b_t[fl:fl + 1, :], li_t[fl:fl + 1, :]
        g, r_col = tot[:, fl:fl + 1], r[:, fl:fl + 1]
        vaug = jnp.concatenate([vt_h, ones_rows], axis=0)
        s_in = s_ref[u]
        m_in = m_ref[u][0:1, 0:1]
        h_t = None
        if big is not None:
            inter = b_row + m_in
            dm = jnp.where(vis[d], b_row - r_col, NEG)
            m_t = jnp.maximum(inter, jnp.max(dm, axis=0, keepdims=True))
            sc = big[0:L] * jnp.exp(dm - m_t)
            res = jnp.exp(inter - m_t) * big[L:] + _dot(vaug, sc.astype(BF16))
            den = jnp.maximum(jnp.abs(res[M_V:M_V + 1]), jnp.exp(-m_t))
            h_t = res[0:M_V] / den
        a_row = g - b_row + li_row
        m_loc = jnp.max(a_row, axis=1, keepdims=True)
        wv = (jnp.exp(a_row - m_loc) * vaug.astype(F32)).astype(BF16)
        s_loc = _dot(wv, kh)
        m_new = jnp.maximum(g + m_in, m_loc)
        s_ref[u] = jnp.exp(g + m_in - m_new) * s_in + jnp.exp(m_loc - m_new) * s_loc
        m_ref[u] = jnp.broadcast_to(m_new, (8, LANES))
        return h_t

    s_ref[...] = jnp.zeros_like(s_ref)
    m_ref[...] = jnp.zeros_like(m_ref)
    ctx_gates = {}
    for d in range(2):
        for cc in range(ncc):
            jj = cc if d == 0 else ncc - 1 - cc
            if jj not in ctx_gates:
                ctx_gates[jj] = gates(gc_ref[jj * L:(jj + 1) * L, :])
            gq = ctx_gates[jj]
            for hd in range(M_HEADS):
                kh = head_lanes(hd, mkc_ref[jj * L:(jj + 1) * L, pair(hd)])
                finish(d, hd, gq, kh, mvtc_ref[jj, vrows(hd), :], None)

    def body(j, carry):
        chunk_of = (j, nc - 1 - j)
        rows = [pl.ds(pl.multiple_of(jj * L, L), L) for jj in chunk_of]
        gq = [gates(g_ref[rows[d], :]) for d in range(2)]
        pending = None
        for d in range(2):
            for hd in range(M_HEADS):
                kh = head_lanes(hd, mk_ref[rows[d], pair(hd)])
                big = scores(d * M_HEADS + hd, kh, head_lanes(hd, mq_ref[rows[d], pair(hd)]))
                if pending is not None:
                    pending()
                dst = hf_ref if d == 0 else hr_ref

                def pending(d=d, hd=hd, kh=kh, big=big, dst=dst):
                    dst[chunk_of[d], hd] = finish(d, hd, gq[d], kh, mvt_ref[chunk_of[d], vrows(hd), :], big)
        pending()
        return carry

    lax.fori_loop(0, nc, body, 0)

    def readout(j, carry):
        rows = pl.ds(pl.multiple_of(j * L, L), L)
        for hd in range(M_HEADS):
            hs_t = hf_ref[j, hd] + hr_ref[j, hd]
            ms = jnp.mean(hs_t * hs_t, axis=0, keepdims=True)
            hs = jnp.transpose(hs_t * lax.rsqrt(ms + NORM_EPS)) * hnw_ref[:, vrows(hd)]
            out_ref[rows, vrows(hd)] = (mo_ref[rows, vrows(hd)].astype(F32) * hs).astype(BF16)
        return carry

    lax.fori_loop(0, nc, readout, 0)


def _mlstm(mq, mk, mvt, g2, mo, mkc, mvtc, g2c, bias_row, hnw):
    bsz, t, _ = mq.shape
    n_ctx = mkc.shape[1]
    nc, _, chunk = mvt.shape[1:]
    ncc = mvtc.shape[1]
    per_b = lambda rows, width: pl.BlockSpec((None, rows, width), lambda b: (b, 0, 0))
    chunked = lambda n: pl.BlockSpec((None, n, M_WIDTH, chunk), lambda b: (b, 0, 0, 0))
    const = lambda shape: pl.BlockSpec(shape, lambda b: (0,) * len(shape))
    return pl.pallas_call(
        _mlstm_kernel,
        grid=(bsz,),
        in_specs=[per_b(t, M_QKW), per_b(t, M_QKW), chunked(nc), per_b(t, LANES), per_b(t, M_WIDTH),
                  per_b(n_ctx, M_QKW), chunked(ncc), per_b(n_ctx, LANES),
                  const((1, LANES)), const((1, M_WIDTH))],
        out_specs=per_b(t, M_WIDTH),
        out_shape=jax.ShapeDtypeStruct((bsz, t, M_WIDTH), BF16),
        scratch_shapes=[pltpu.VMEM((2 * M_HEADS, M_V + M_ONES, LANES), F32),
                        pltpu.VMEM((2 * M_HEADS, 8, LANES), F32),
                        pltpu.VMEM((nc, M_HEADS, M_V, chunk), F32),
                        pltpu.VMEM((nc, M_HEADS, M_V, chunk), F32)],
        compiler_params=_cparams(("parallel",)),
        name="mlstm",
    )(mq, mk, mvt, g2, mo, mkc, mvtc, g2c, bias_row, hnw)


def _axial_angles(n_lat, rot_dim):
    rows = n_lat // GRID_W
    row = jnp.repeat(jnp.arange(rows), GRID_W).astype(F32)
    col = jnp.tile(jnp.arange(GRID_W), rows).astype(F32)
    n_freq = rot_dim // 4
    inv = ROPE_THETA ** (-jnp.arange(n_freq, dtype=F32) / n_freq)
    ang = jnp.concatenate([row[:, None] * inv, col[:, None] * inv], axis=-1)
    return jnp.cos(ang), jnp.sin(ang)


def _rope_tables(n_lat):
    cos, sin = _axial_angles(n_lat, A_HEAD_DIM)
    cos_a = jnp.tile(jnp.concatenate([cos, cos], axis=-1), (1, 2))
    sin_a = jnp.tile(jnp.concatenate([-sin, sin], axis=-1), (1, 2))
    cos, sin = _axial_angles(n_lat, C_ROPE)
    one = lambda w: jnp.ones((n_lat, w), F32)
    zero = lambda w: jnp.zeros((n_lat, w), F32)
    cos_k = jnp.concatenate([cos, cos, one(LANES - C_ROPE)], axis=-1)
    sin_k = jnp.concatenate([-sin, sin, zero(LANES - C_ROPE)], axis=-1)
    cos_q = jnp.concatenate([one(C_NOPE), cos, cos, one(LANES - C_NOPE - C_ROPE)], axis=-1)
    sin_q = jnp.concatenate([zero(C_NOPE), -sin, sin, zero(LANES - C_NOPE - C_ROPE)], axis=-1)
    return (cos_a, sin_a), (cos_k, sin_k), (cos_q, sin_q)


_A_HEAD_ORDER = (0, 4, 1, 5, 2, 6, 3, 7)


def _perm_heads(w, axis):
    shape = w.shape
    grp = A_HEADS // A_KV_HEADS
    w = w.reshape(shape[:axis] + (A_KV_HEADS, grp, A_HEAD_DIM) + shape[axis + 1:])
    return jnp.swapaxes(w, axis, axis + 1).reshape(shape)


def _even_weights(w_in, w_out):
    offs = np.cumsum((0,) + E_COLS)
    parts = [w_in[:, offs[i]:offs[i + 1]] for i in range(len(E_COLS))]
    parts[2] = _perm_heads(parts[2], 1)
    z = parts[6]
    parts[6] = jnp.concatenate([_perm_heads(z[:, :A_WIDTH], 1), z[:, A_WIDTH:]], axis=1)
    w_out_p = jnp.concatenate([_perm_heads(w_out[:A_WIDTH], 0), w_out[A_WIDTH:]], axis=0)
    return jnp.concatenate(parts, axis=1).astype(BF16), w_out_p.astype(BF16)


def _odd_weights(w_in, w_uq, w_ukv):
    d = w_in.shape[0]
    offs = np.cumsum((0,) + O_COLS)
    ckv, kr, mk, mv, mg, cq, mq, mo, z = [w_in[:, offs[i]:offs[i + 1]] for i in range(len(O_COLS))]
    g2 = jnp.concatenate([kr, mg, jnp.zeros((d, LANES - C_ROPE - 4 * M_HEADS), w_in.dtype)], axis=1)
    w_p = jnp.concatenate([ckv, g2, mk, cq, mq, mo, z], axis=1).astype(BF16)
    wmvt = jnp.transpose(mv).astype(BF16)
    ukv = w_ukv.reshape(C_KV_LORA, C_HEADS, C_NOPE + C_VDIM)
    wk = jnp.pad(ukv[:, :, :C_NOPE], ((0, 0), (0, 0), (0, LANES - C_NOPE)))
    wk = wk.reshape(C_KV_LORA, C_HEADS * LANES).astype(BF16)
    wv = jnp.transpose(ukv[:, :, C_NOPE:].reshape(C_KV_LORA, C_WIDTH)).astype(BF16)
    uq = w_uq.reshape(C_Q_LORA, C_HEADS, C_NOPE + C_ROPE)
    wq = jnp.pad(uq, ((0, 0), (0, 0), (0, LANES - C_NOPE - C_ROPE))).reshape(C_Q_LORA, C_HEADS * LANES).astype(BF16)
    return w_p, wk, wv, wmvt, wq


def kernel(x, c, ctx, c_ctx, mod_w, mod_b, pre_norm_w, post_norm_w, e_w_in, e_sink, e_conv_w, e_w_out,
           o_w_in, o_q_norm_w, o_kv_norm_w, o_w_uq, o_w_ukv, o_i_bias, o_f_bias, o_head_norm_w, o_w_out):
    bsz, n_lat, d = x.shape
    n_ctx = ctx.shape[1]
    assert mod_w.shape[0] == 2 and d == D_MODEL, "built for one even + one odd layer"
    assert n_lat % 1024 == 0 and n_ctx % 256 == 0
    tm_in, tm_ctx = 1024, 256

    pad = (-(bsz + 1)) % 8
    cc = jnp.concatenate([c, c_ctx[None, :], jnp.zeros((pad, d), F32)], axis=0)
    mod = _modulation(cc, mod_w, mod_b)
    split = lambda l, r0, r1: [mod[l, r0:r1, k * d:(k + 1) * d][:, None, :] for k in range(3)]
    (rope_a, rope_k, rope_q) = _rope_tables(n_lat)
    dummy_tab = jnp.zeros((n_ctx, LANES), F32)

    sh_x, sc_x, g_x = split(0, 0, bsz)
    sh_c, sc_c, g_c = split(0, bsz, bsz + 1)
    w_in, w_out = _even_weights(e_w_in[0], e_w_out[0])
    nw, pw = pre_norm_w[0][None, :], post_norm_w[0][None, :]
    k, vt, q, bb, u, z = _even_in(x, sh_x, sc_x, nw, w_in, *rope_a, rope=True, tm=tm_in)
    k_c, vt_c, q_c, bb_c, u_c, z_c = _even_in(ctx, sh_c, sc_c, nw, w_in, dummy_tab, dummy_tab,
                                              rope=False, tm=tm_ctx)
    a = _even_attn(e_sink[0], q, k, vt, k_c, vt_c, window=True)
    a_c = _even_attn(e_sink[0], q_c, None, None, k_c, vt_c, window=False)
    x1 = _out_proj((a, bb, u), z, e_conv_w[0], w_out, pw, g_x, x, tm=tm_in)
    ctx1 = _out_proj((a_c, bb_c, u_c), z_c, e_conv_w[0], w_out, pw, g_c, ctx, tm=tm_ctx)

    sh_x, sc_x, g_x = split(1, 0, bsz)
    sh_c, sc_c, _ = split(1, bsz, bsz + 1)
    w_p, wk, wv, wmvt, wq = _odd_weights(o_w_in[0], o_w_uq[0], o_w_ukv[0])
    nw, pw = pre_norm_w[1][None, :], post_norm_w[1][None, :]
    kvn, qn = o_kv_norm_w[0][None, :], o_q_norm_w[0][None, :]
    chunk = min(MLSTM_CHUNK, n_ctx)
    kf, vt, mk, mvt, g2, qf, mq, mo, z = _odd_in(x1, sh_x, sc_x, nw, w_p, kvn, wk, wv, wmvt,
                                                 (*rope_k, qn, wq, *rope_q), tm=tm_in, chunk=chunk)
    kf_c, vt_c, mk_c, mvt_c, g2_c = _odd_in(ctx1, sh_c, sc_c, nw, w_p[:, :OC_CTX_END], kvn, wk, wv, wmvt, None,
                                            tm=tm_ctx, chunk=chunk)
    c_out = _mla(qf, kf_c, kf, vt_c, vt)
    gate_bias = jnp.stack([o_i_bias[0], o_f_bias[0]], axis=1).reshape(1, 4 * M_HEADS)
    bias_row = jnp.pad(gate_bias, ((0, 0), (GATE_LANE0, LANES - GATE_LANE0 - 4 * M_HEADS)))
    m_out = _mlstm(mq, mk, mvt, g2, mo, mk_c, mvt_c, g2_c, bias_row, o_head_norm_w[0][None, :])
    return _out_proj((c_out, m_out), z, None, o_w_out[0].astype(BF16), pw, g_x, x1, tm=tm_in)
```

```python
import collections
import functools

import numpy as np
import jax
import jax.numpy as jnp
from jax import lax
from jax.experimental import pallas as pl
from jax.experimental.pallas import tpu as pltpu

F32 = jnp.float32
BF16 = jnp.bfloat16
HIGHEST = lax.Precision.HIGHEST

LANES = 128
VMEM_LIMIT = 56 * 1024 * 1024

D_MODEL = 1024
GRID_W = 64
ROPE_THETA = 10000.0
NORM_EPS = 1e-6
NEG = -1e30
LOG2E = 1.4426950408889634

A_HEADS, A_KV_HEADS, A_HEAD_DIM = 8, 2, 64
A_WIDTH = A_HEADS * A_HEAD_DIM
WINDOW = 128
B_WIDTH = 512
CONV_W = 3
C_HEADS, C_NOPE, C_ROPE, C_VDIM = 8, 64, 32, 64
C_KV_LORA, C_Q_LORA = 256, 768
C_WIDTH = C_HEADS * C_VDIM
M_HEADS, M_QK, M_V = 4, 64, 128
M_WIDTH = M_HEADS * M_V
E_GATE = A_WIDTH + B_WIDTH
O_GATE = C_WIDTH + M_WIDTH
E_COLS = (128, 128, A_WIDTH, B_WIDTH, B_WIDTH, B_WIDTH, E_GATE)
E_IN = sum(E_COLS)
O_COLS = (C_KV_LORA, C_ROPE, M_HEADS * M_QK, M_WIDTH, 4 * M_HEADS, C_Q_LORA, M_HEADS * M_QK, M_WIDTH, O_GATE)

EC_K, EC_V, EC_Q, EC_BB, EC_BC, EC_BX, EC_Z, EC_END = 0, 128, 256, 768, 1280, 1792, 2304, 3328
OC_CKV = 0
OC_G2 = 256
OC_MK = 384
OC_CTX_END = 640
OC_CQ = 640
OC_MQ = 1408
OC_MO = 1664
OC_Z = 2176
OC_END = 3200
M_QKW = M_HEADS * M_QK
M_ONES = 16
GATE_LANE0 = 32
SUB_ROWS = 256
MLSTM_CHUNK = 256
ATT_Q = 256
ATT_KEYS = 256
SCORE_SLOTS = 3
SCORE_ROWS = 128
ATT_VDIM = 64
MLA_BLK = 512
ATT_STEP = 1024


def _nt(a, b):
    return lax.dot_general(a, b, (((1,), (1,)), ((), ())), preferred_element_type=F32)


def _dot(a, b):
    return jnp.dot(a, b, preferred_element_type=F32)


def _silu(z):
    return z * jax.nn.sigmoid(z)


def _rms(x, w):
    ms = jnp.mean(x * x, axis=-1, keepdims=True)
    return x * lax.rsqrt(ms + NORM_EPS) * w


def _rope_slab(x, cos, sin_signed, half, take_up):
    up = pltpu.roll(x, LANES - half, axis=1)
    dn = pltpu.roll(x, half, axis=1)
    return x * cos + jnp.where(take_up, up, dn) * sin_signed


def _cparams(sem):
    return pltpu.CompilerParams(dimension_semantics=sem, vmem_limit_bytes=VMEM_LIMIT)


def _resident(shape):
    return pl.BlockSpec(shape, lambda b, i: (0,) * len(shape), pipeline_mode=pl.Buffered(1))


def _mod_kernel(cc_ref, w_ref, b_ref, o_ref):
    s = _silu(cc_ref[...])
    o_ref[...] = jnp.dot(s, w_ref[...], preferred_element_type=F32, precision=HIGHEST) + b_ref[...]


def _modulation(cc, mod_w, mod_b):
    depth, d, d3 = mod_w.shape
    r = cc.shape[0]
    return pl.pallas_call(
        _mod_kernel,
        grid=(depth, d3 // d),
        in_specs=[
            pl.BlockSpec((r, d), lambda l, j: (0, 0)),
            pl.BlockSpec((None, d, d), lambda l, j: (l, 0, j)),
            pl.BlockSpec((None, 1, d), lambda l, j: (l, 0, j)),
        ],
        out_specs=pl.BlockSpec((None, r, d), lambda l, j: (l, 0, j)),
        out_shape=jax.ShapeDtypeStruct((depth, r, d3), F32),
        compiler_params=_cparams(("arbitrary", "arbitrary")),
        name="modulation",
    )(cc, mod_w, mod_b.reshape(depth, 1, d3))


def _even_in_kernel(x_ref, sh_ref, sc_ref, nw_ref, w_ref, cos_ref, sin_ref,
                    k_ref, vt_ref, q_ref, bb_ref, u_ref, z_ref, *, rope):
    lane = lax.broadcasted_iota(jnp.int32, (1, LANES), 1)
    take_up = (lane % A_HEAD_DIM) < (A_HEAD_DIM // 2)
    for r0 in range(0, x_ref.shape[0], SUB_ROWS):
        rows = slice(r0, r0 + SUB_ROWS)
        h = (_rms(x_ref[rows, :], nw_ref[...]) * (1.0 + sc_ref[...]) + sh_ref[...]).astype(BF16)
        y_all = _dot(h, w_ref[...])

        def proj(a, b):
            return y_all[:, a:b]

        if rope:
            cos, sin = cos_ref[rows, :], sin_ref[rows, :]
            rot = lambda t: _rope_slab(t, cos, sin, A_HEAD_DIM // 2, take_up)
        else:
            rot = lambda t: t
        k_ref[rows, :] = rot(proj(EC_K, EC_V)).astype(BF16)
        vt_ref[:, rows] = jnp.transpose(proj(EC_V, EC_Q)).astype(BF16)
        q = proj(EC_Q, EC_BB)
        for j in range(A_WIDTH // LANES):
            sl = slice(j * LANES, (j + 1) * LANES)
            q_ref[rows, sl] = (rot(q[:, sl]) * (A_HEAD_DIM ** -0.5 * LOG2E)).astype(BF16)
        bb_ref[rows, :] = proj(EC_BB, EC_BC).astype(BF16)
        u_ref[rows, :] = (proj(EC_BC, EC_BX) * proj(EC_BX, EC_Z)).astype(BF16)
        z_ref[rows, :] = _silu(proj(EC_Z, EC_END)).astype(BF16)


def _even_in(x, sh, sc, nw, w, cos, sin, *, rope, tm):
    bsz, t, d = x.shape
    nb = sh.shape[0]
    mod_map = (lambda b, i: (b, 0, 0)) if nb > 1 else (lambda b, i: (0, 0, 0))
    tok = lambda width: pl.BlockSpec((None, tm, width), lambda b, i: (b, i, 0))
    const = lambda shape: pl.BlockSpec(shape, lambda b, i: (0,) * len(shape))
    widths = (LANES, None, A_WIDTH, B_WIDTH, B_WIDTH, E_GATE)
    vt_spec = pl.BlockSpec((None, LANES, tm), lambda b, i: (b, 0, i))
    vt_shape = jax.ShapeDtypeStruct((bsz, LANES, t), BF16)
    return pl.pallas_call(
        functools.partial(_even_in_kernel, rope=rope),
        grid=(bsz, t // tm),
        in_specs=[
            tok(d),
            pl.BlockSpec((None, 1, d), mod_map),
            pl.BlockSpec((None, 1, d), mod_map),
            const((1, d)),
            _resident(w.shape),
            pl.BlockSpec((tm, LANES), lambda b, i: (i, 0)),
            pl.BlockSpec((tm, LANES), lambda b, i: (i, 0)),
        ],
        out_specs=[vt_spec if wd is None else tok(wd) for wd in widths],
        out_shape=[vt_shape if wd is None else jax.ShapeDtypeStruct((bsz, t, wd), BF16) for wd in widths],
        compiler_params=_cparams(("parallel", "parallel")),
        name="even_in_rope" if rope else "even_in",
    )(x, sh, sc, nw, w, cos, sin)


AttnStage = collections.namedtuple("AttnStage", "q k vt mask sink")


def _tree_max(parts):
    while len(parts) > 1:
        parts = [jnp.maximum(parts[i], parts[i + 1]) if i + 1 < len(parts) else parts[i]
                 for i in range(0, len(parts), 2)]
    return parts[0]


def _attention_pipeline(stages, n_chunks, s_ref, write_pair):
    kc, sr = ATT_KEYS, SCORE_ROWS
    slots, n_keys = s_ref.shape[0], n_chunks * kc
    lag = slots - 1
    ones = jnp.ones((16, n_keys), BF16)
    m8 = {}
    halves = []
    for i in range(len(stages) + lag):
        if i < len(stages):
            cur = stages[i]
            q = cur.q()
            parts = []
            for c in range(n_chunks):
                kch, mask = cur.k(c), cur.mask(c)
                for r in range(0, kc, sr):
                    st = _nt(kch[r:r + sr], q)
                    if mask is not None:
                        st = jnp.where(mask[r:r + sr], st, NEG)
                    s_ref[i % slots, c * kc + r:c * kc + r + sr, :] = st
                    parts += [st[t:t + 8, :] for t in range(0, sr, 8)]
            m8[i] = _tree_max(parts)
        if i >= lag:
            prev = stages[i - lag]
            m_prev = jnp.max(m8.pop(i - lag), axis=0, keepdims=True)
            if prev.sink is not None:
                m_prev = jnp.maximum(m_prev, prev.sink)
            pt = jnp.exp2(s_ref[(i - lag) % slots, 0:n_keys, :] - m_prev).astype(BF16)
            vta = jnp.concatenate([prev.vt(c) for c in range(n_chunks)], axis=1)
            acc = _dot(jnp.concatenate([vta, ones], axis=0), pt)
            denom = acc[ATT_VDIM:ATT_VDIM + 1]
            if prev.sink is not None:
                denom = denom + jnp.exp2(prev.sink - m_prev)
            halves.append(acc[0:ATT_VDIM] / denom)
            if len(halves) == 2:
                write_pair((i - lag) // 2, jnp.concatenate(halves, axis=0))
                halves = []


def _even_attn_kernel(*refs, window, n_lat):
    if window:
        (sink_ref, q_ref, kp_ref, km_ref, kn_ref, kc_ref, vtp_ref, vtm_ref, vtn_ref, vtc_ref,
         o_ref, s_ref) = refs
    else:
        sink_ref, q_ref, kc_ref, vtc_ref, o_ref, s_ref = refs
    nq, kc = ATT_Q, ATT_KEYS
    step, half = q_ref.shape[0], WINDOW
    lane = lax.broadcasted_iota(jnp.int32, (1, LANES), 1)
    lo = lane < A_HEAD_DIM
    ctx_k = [kc_ref[r:r + kc, :] for r in range(0, kc_ref.shape[0], kc)]
    ctx_vt = [vtc_ref[:, r:r + kc] for r in range(0, kc_ref.shape[0], kc)]
    r = lax.broadcasted_iota(jnp.int32, (kc, nq), 0)
    c = lax.broadcasted_iota(jnp.int32, (kc, nq), 1)

    def sub_block(qb):
        k_chunks, vt_chunks, masks = list(ctx_k), list(ctx_vt), [None] * len(ctx_k)
        if window:
            base, q0 = qb * nq, pl.program_id(1) * step + qb * nq
            if qb == 0:
                k1 = jnp.concatenate([kp_ref[...], km_ref[0:half, :]], axis=0)
                vt1 = jnp.concatenate([vtp_ref[...], vtm_ref[:, 0:half]], axis=1)
            else:
                k1, vt1 = km_ref[base - half:base + half, :], vtm_ref[:, base - half:base + half]
            if base + nq == step:
                k2 = jnp.concatenate([km_ref[base + half:step, :], kn_ref[...]], axis=0)
                vt2 = jnp.concatenate([vtm_ref[:, base + half:step], vtn_ref[...]], axis=1)
            else:
                k2, vt2 = km_ref[base + half:base + half + kc, :], vtm_ref[:, base + half:base + half + kc]
            k_chunks += [k1, k2]
            vt_chunks += [vt1, vt2]
            masks += [(jnp.abs(r - half - c) <= WINDOW) & (r + (q0 - half) >= 0),
                      (jnp.abs(r + half - c) <= WINDOW) & (r + (q0 + half) < n_lat)]
        return k_chunks, vt_chunks, masks

    def stage(qb, head, chunks):
        k_chunks, vt_chunks, masks = chunks
        j, grp = head % (A_HEADS // 2), head // (A_HEADS // 2)
        keep = lo if grp == 0 else jnp.logical_not(lo)

        def q():
            qs = q_ref[qb * nq:(qb + 1) * nq, j * LANES:(j + 1) * LANES]
            return jnp.where(keep, qs, jnp.zeros_like(qs))

        return AttnStage(q=q, k=lambda ci: k_chunks[ci],
                         vt=lambda ci: vt_chunks[ci][grp * A_HEAD_DIM:(grp + 1) * A_HEAD_DIM, :],
                         mask=lambda ci: masks[ci], sink=sink_ref[head] * LOG2E)

    def write_pair(p, o_t):
        qb, j = divmod(p, A_HEADS // 2)
        o_ref[qb * nq:(qb + 1) * nq, j * LANES:(j + 1) * LANES] = jnp.transpose(o_t).astype(BF16)

    stages = []
    for qb in range(step // nq):
        chunks = sub_block(qb)
        stages += [stage(qb, h, chunks) for h in _A_HEAD_ORDER]
    _attention_pipeline(stages, len(ctx_k) + (2 if window else 0), s_ref, write_pair)


def _even_attn(sink, q, k, vt, kc, vtc, *, window):
    bsz, t, _ = q.shape
    n_ctx = kc.shape[1]
    nq, keys, half = ATT_Q, ATT_KEYS, WINDOW
    step = min(ATT_STEP, t)
    assert nq == keys == 2 * half and t % step == 0 and n_ctx % keys == 0
    per_step, last = step // half, t // half - 1
    smem = pl.BlockSpec(memory_space=pltpu.SMEM)
    qspec = pl.BlockSpec((None, step, A_WIDTH), lambda b, i: (b, i, 0))
    kcspec = pl.BlockSpec((None, n_ctx, LANES), lambda b, i: (b, 0, 0))
    vtcspec = pl.BlockSpec((None, LANES, n_ctx), lambda b, i: (b, 0, 0))
    if window:
        prev = lambda i: jnp.maximum(per_step * i - 1, 0)
        nxt = lambda i: jnp.minimum(per_step * (i + 1), last)
        specs = [smem, qspec,
                 pl.BlockSpec((None, half, LANES), lambda b, i: (b, prev(i), 0)),
                 pl.BlockSpec((None, step, LANES), lambda b, i: (b, i, 0)),
                 pl.BlockSpec((None, half, LANES), lambda b, i: (b, nxt(i), 0)),
                 kcspec,
                 pl.BlockSpec((None, LANES, half), lambda b, i: (b, 0, prev(i))),
                 pl.BlockSpec((None, LANES, step), lambda b, i: (b, 0, i)),
                 pl.BlockSpec((None, LANES, half), lambda b, i: (b, 0, nxt(i))),
                 vtcspec]
        args = (sink, q, k, k, k, kc, vt, vt, vt, vtc)
    else:
        specs = [smem, qspec, kcspec, vtcspec]
        args = (sink, q, kc, vtc)
    n_chunks = n_ctx // keys + (2 if window else 0)
    return pl.pallas_call(
        functools.partial(_even_attn_kernel, window=window, n_lat=t),
        grid=(bsz, t // step),
        in_specs=specs,
        out_specs=qspec,
        out_shape=jax.ShapeDtypeStruct((bsz, t, A_WIDTH), BF16),
        scratch_shapes=[pltpu.VMEM((SCORE_SLOTS, n_chunks * keys, nq), F32)],
        compiler_params=_cparams(("parallel", "parallel")),
        name="even_attn_window" if window else "even_attn_ctx",
    )(*args)


def _out_kernel(*refs, conv):
    if conv:
        a_ref, bb_ref, u_ref, up_ref, un_ref, z_ref, cw_ref, w_ref, pw_ref, g_ref, x_ref, o_ref = refs
        tm = u_ref.shape[0]
        i = pl.program_id(1)
        u = u_ref[...].astype(F32)
        prev_row = jnp.where(i > 0, up_ref[15:16, :].astype(F32), 0.0)
        next_row = jnp.where(i < pl.num_programs(1) - 1, un_ref[0:1, :].astype(F32), 0.0)
        row = lax.broadcasted_iota(jnp.int32, (tm, 1), 0)
        u_m1 = jnp.where(row == 0, prev_row, pltpu.roll(u, 1, axis=0))
        u_p1 = jnp.where(row == tm - 1, next_row, pltpu.roll(u, tm - 1, axis=0))
        cw = cw_ref[...]
        mix_b = (bb_ref[...].astype(F32) * (u_m1 * cw[0:1, :] + u * cw[1:2, :] + u_p1 * cw[2:3, :])).astype(BF16)
    else:
        a_ref, b_ref, z_ref, w_ref, pw_ref, g_ref, x_ref, o_ref = refs
        mix_b = b_ref[...]
    wa = a_ref.shape[1]
    for r0 in range(0, x_ref.shape[0], SUB_ROWS):
        rows = slice(r0, r0 + SUB_ROWS)
        gated = jnp.concatenate([a_ref[rows, :] * z_ref[rows, 0:wa], mix_b[rows, :] * z_ref[rows, wa:]], axis=1)
        o_ref[rows, :] = x_ref[rows, :] + g_ref[...] * _rms(_dot(gated, w_ref[...]), pw_ref[...])


def _out_proj(mix, z, cw, w, pw, g, x, *, tm):
    conv = cw is not None
    bsz, t, d = x.shape
    nb = g.shape[0]
    mod_map = (lambda b, i: (b, 0, 0)) if nb > 1 else (lambda b, i: (0, 0, 0))
    tok = lambda width: pl.BlockSpec((None, tm, width), lambda b, i: (b, i, 0))
    const = lambda shape: pl.BlockSpec(shape, lambda b, i: (0,) * len(shape))
    r16 = tm // 16
    if conv:
        a, bb, u = mix
        specs = [tok(a.shape[2]), tok(bb.shape[2]), tok(u.shape[2]),
                 pl.BlockSpec((None, 16, u.shape[2]), lambda b, i: (b, jnp.maximum(i * r16 - 1, 0), 0)),
                 pl.BlockSpec((None, 16, u.shape[2]), lambda b, i: (b, jnp.minimum((i + 1) * r16, t // 16 - 1), 0)),
                 tok(z.shape[2]), const(cw.shape)]
        args = (a, bb, u, u, u, z, cw)
    else:
        a, b2 = mix
        specs = [tok(a.shape[2]), tok(b2.shape[2]), tok(z.shape[2])]
        args = (a, b2, z)
    specs += [_resident(w.shape), const((1, d)), pl.BlockSpec((None, 1, d), mod_map), tok(d)]
    return pl.pallas_call(
        functools.partial(_out_kernel, conv=conv),
        grid=(bsz, t // tm),
        in_specs=specs,
        out_specs=tok(d),
        out_shape=jax.ShapeDtypeStruct((bsz, t, d), F32),
        compiler_params=_cparams(("parallel", "parallel")),
        name="out_proj_conv" if conv else "out_proj",
    )(*args, w, pw, g, x)


def _odd_in_kernel(*refs, lat):
    if lat:
        (x_ref, sh_ref, sc_ref, nw_ref, w_ref, kvn_ref, wk_ref, wvt_ref, wmvt_ref, cosk_ref, sink_ref,
         qn_ref, wq_ref, cosq_ref, sinq_ref,
         kf_ref, vt_ref, mk_ref, mvt_ref, g2_ref, qf_ref, mq_ref, mo_ref, z_ref) = refs
    else:
        (x_ref, sh_ref, sc_ref, nw_ref, w_ref, kvn_ref, wk_ref, wvt_ref, wmvt_ref,
         kf_ref, vt_ref, mk_ref, mvt_ref, g2_ref) = refs
    lane = lax.broadcasted_iota(jnp.int32, (1, LANES), 1)
    chunk = mvt_ref.shape[2]
    per_sub = SUB_ROWS // chunk
    for cc in range(x_ref.shape[0] // SUB_ROWS):
        rows = slice(cc * SUB_ROWS, (cc + 1) * SUB_ROWS)
        h = (_rms(x_ref[rows, :], nw_ref[...]) * (1.0 + sc_ref[...]) + sh_ref[...]).astype(BF16)
        y_all = _dot(h, w_ref[...])

        def proj(a, b):
            return y_all[:, a:b]

        ckv = _rms(proj(OC_CKV, OC_CKV + C_KV_LORA), kvn_ref[...]).astype(BF16)
        g2 = proj(OC_G2, OC_G2 + LANES)
        g2_ref[rows, :] = g2
        if lat:
            g2 = _rope_slab(g2, cosk_ref[rows, :], sink_ref[rows, :], C_ROPE // 2, lane < C_ROPE // 2)
        rope_lanes = (lane >= C_NOPE) & (lane < C_NOPE + C_ROPE)
        kr = jnp.where(rope_lanes, pltpu.roll(g2, C_NOPE, axis=1), 0.0)
        k_nope = _dot(ckv, wk_ref[...])
        for hd in range(C_HEADS):
            sl = slice(hd * LANES, (hd + 1) * LANES)
            kf_ref[rows, sl] = (k_nope[:, sl] + kr).astype(BF16)
        vt_ref[:, rows] = _nt(wvt_ref[...], ckv).astype(BF16)
        mk_ref[rows, :] = (proj(OC_MK, OC_MK + M_QKW) * (M_QK ** -0.5)).astype(BF16)
        mvt = _nt(wmvt_ref[...], h).astype(BF16)
        for s in range(per_sub):
            mvt_ref[cc * per_sub + s] = mvt[:, s * chunk:(s + 1) * chunk]
        if lat:
            cq = _rms(proj(OC_CQ, OC_CQ + C_Q_LORA), qn_ref[...]).astype(BF16)
            qf = _dot(cq, wq_ref[...])
            cos, sin = cosq_ref[rows, :], sinq_ref[rows, :]
            take_up = (lane >= C_NOPE) & (lane < C_NOPE + C_ROPE // 2)
            scale = (C_NOPE + C_ROPE) ** -0.5 * LOG2E
            for hd in range(C_HEADS):
                sl = slice(hd * LANES, (hd + 1) * LANES)
                qf_ref[rows, sl] = (_rope_slab(qf[:, sl], cos, sin, C_ROPE // 2, take_up) * scale).astype(BF16)
            mq_ref[rows, :] = proj(OC_MQ, OC_MQ + M_QKW).astype(BF16)
            mo_ref[rows, :] = jax.nn.sigmoid(proj(OC_MO, OC_MO + M_WIDTH)).astype(BF16)
            z_ref[rows, :] = _silu(proj(OC_Z, OC_Z + O_GATE)).astype(BF16)


def _odd_in(x, sh, sc, nw, w, kvn, wk, wv, wmvt, lat_args, *, tm, chunk):
    lat = lat_args is not None
    bsz, t, d = x.shape
    nb = sh.shape[0]
    mod_map = (lambda b, i: (b, 0, 0)) if nb > 1 else (lambda b, i: (0, 0, 0))
    tok = lambda width: (pl.BlockSpec((None, tm, width), lambda b, i: (b, i, 0)), (bsz, t, width))
    const = lambda shape: pl.BlockSpec(shape, lambda b, i: (0,) * len(shape))
    tab = pl.BlockSpec((tm, LANES), lambda b, i: (i, 0))
    specs = [tok(d)[0], pl.BlockSpec((None, 1, d), mod_map), pl.BlockSpec((None, 1, d), mod_map),
             const((1, d)), _resident(w.shape), const(kvn.shape), _resident(wk.shape), _resident(wv.shape),
             _resident(wmvt.shape)]
    args = [x, sh, sc, nw, w, kvn, wk, wv, wmvt]
    vt = (pl.BlockSpec((None, C_WIDTH, tm), lambda b, i: (b, 0, i)), (bsz, C_WIDTH, t))
    mvt = (pl.BlockSpec((None, tm // chunk, M_WIDTH, chunk), lambda b, i: (b, i, 0, 0)),
           (bsz, t // chunk, M_WIDTH, chunk))
    outs = [(tok(C_HEADS * LANES), BF16), (vt, BF16), (tok(M_QKW), BF16), (mvt, BF16), (tok(LANES), F32)]
    if lat:
        cosk, sink, qn, wq, cosq, sinq = lat_args
        specs += [tab, tab, const(qn.shape), _resident(wq.shape), tab, tab]
        args += [cosk, sink, qn, wq, cosq, sinq]
        outs += [(tok(C_HEADS * LANES), BF16), (tok(M_QKW), BF16), (tok(M_WIDTH), BF16), (tok(O_GATE), BF16)]
    return pl.pallas_call(
        functools.partial(_odd_in_kernel, lat=lat),
        grid=(bsz, t // tm),
        in_specs=specs,
        out_specs=[o[0][0] for o in outs],
        out_shape=[jax.ShapeDtypeStruct(o[0][1], o[1]) for o in outs],
        compiler_params=_cparams(("parallel", "parallel")),
        name="odd_in_lat" if lat else "odd_in_ctx",
    )(*args)


def _mla_kernel(q_ref, kc_ref, vtc_ref, *rest):
    n_pairs = C_HEADS // 2
    kx_refs, vtx_refs, (o_ref, s_ref) = rest[:n_pairs], rest[n_pairs:2 * n_pairs], rest[2 * n_pairs:]
    nq, kc = ATT_Q, ATT_KEYS
    n_ctx_chunks = kc_ref.shape[0] // kc
    n_chunks = n_ctx_chunks + kx_refs[0].shape[0] // kc

    def stage(qb, hd):
        sl = slice(hd * LANES, (hd + 1) * LANES)
        half = hd % 2

        def k(ci):
            if ci < n_ctx_chunks:
                return kc_ref[ci * kc:(ci + 1) * kc, sl]
            r0 = (ci - n_ctx_chunks) * kc
            return kx_refs[hd // 2][r0:r0 + kc, half * LANES:(half + 1) * LANES]

        def vt(ci):
            if ci < n_ctx_chunks:
                return vtc_ref[hd * C_VDIM:(hd + 1) * C_VDIM, ci * kc:(ci + 1) * kc]
            r0 = (ci - n_ctx_chunks) * kc
            return vtx_refs[hd // 2][half * C_VDIM:(half + 1) * C_VDIM, r0:r0 + kc]

        return AttnStage(q=lambda: q_ref[qb * nq:(qb + 1) * nq, sl], k=k, vt=vt, mask=lambda ci: None, sink=None)

    def write_pair(p, o_t):
        qb, j = divmod(p, C_HEADS // 2)
        o_ref[qb * nq:(qb + 1) * nq, j * LANES:(j + 1) * LANES] = jnp.transpose(o_t).astype(BF16)

    stages = [stage(qb, hd) for qb in range(q_ref.shape[0] // nq) for hd in range(C_HEADS)]
    _attention_pipeline(stages, n_chunks, s_ref, write_pair)


def _mla(qf, kfc, kfx, vtc, vtx):
    bsz, t, wq = qf.shape
    n_ctx = kfc.shape[1]
    blk = MLA_BLK
    n_pairs = C_HEADS // 2
    pair_k = lambda j: pl.BlockSpec((None, t, 2 * LANES), lambda b, i: (b, 0, j))
    pair_vt = lambda j: pl.BlockSpec((None, 2 * C_VDIM, t), lambda b, i: (b, j, 0))
    return pl.pallas_call(
        _mla_kernel,
        grid=(bsz, t // blk),
        in_specs=[
            pl.BlockSpec((None, blk, wq), lambda b, i: (b, i, 0)),
            pl.BlockSpec((None, n_ctx, wq), lambda b, i: (b, 0, 0)),
            pl.BlockSpec((None, C_WIDTH, n_ctx), lambda b, i: (b, 0, 0)),
        ] + [pair_k(j) for j in range(n_pairs)] + [pair_vt(j) for j in range(n_pairs)],
        out_specs=pl.BlockSpec((None, blk, C_WIDTH), lambda b, i: (b, i, 0)),
        out_shape=jax.ShapeDtypeStruct((bsz, t, C_WIDTH), BF16),
        scratch_shapes=[pltpu.VMEM((SCORE_SLOTS, n_ctx + t, ATT_Q), F32)],
        compiler_params=_cparams(("parallel", "arbitrary")),
        name="mla_attention",
    )(qf, kfc, vtc, *([kfx] * n_pairs), *([vtx] * n_pairs))


def _mlstm_kernel(mq_ref, mk_ref, mvt_ref, g_ref, mo_ref, mkc_ref, mvtc_ref, gc_ref, bias_ref, hnw_ref,
                  out_ref, s_ref, m_ref, hf_ref, hr_ref):
    nc, ncc, L = mvt_ref.shape[0], mvtc_ref.shape[0], mvt_ref.shape[2]
    row = lax.broadcasted_iota(jnp.int32, (L, L), 0)
    col = lax.broadcasted_iota(jnp.int32, (L, L), 1)
    vis = (row <= col, row >= col)
    tri_ones = (col <= row).astype(BF16)
    lane = lax.broadcasted_iota(jnp.int32, (1, LANES), 1)
    lo = lane < M_QK
    ones_rows = jnp.ones((M_ONES, L), BF16)
    fwd_lanes = lane < GATE_LANE0 + 2 * M_HEADS
    tok16 = lax.broadcasted_iota(jnp.int32, (4 * M_HEADS, L), 1)
    fwd_rows16 = lax.broadcasted_iota(jnp.int32, (4 * M_HEADS, L), 0) < 2 * M_HEADS
    pair = lambda hd: slice((hd // 2) * LANES, (hd // 2 + 1) * LANES)
    vrows = lambda hd: slice(hd * M_V, (hd + 1) * M_V)

    def gates(g, with_max=True):
        gb = g + bias_ref[...]
        ls = jnp.minimum(gb, 0.0) - jnp.log1p(jnp.exp(-jnp.abs(gb)))
        hi = ls.astype(BF16)
        lo = (ls - hi.astype(F32)).astype(BF16)
        cf = _dot(tri_ones, hi) + _dot(tri_ones, lo)
        tot = cf[L - 1:L, :]
        b = jnp.where(fwd_lanes, cf, tot - cf + ls)
        li = pltpu.roll(gb, M_HEADS, axis=1)
        gate_rows = slice(GATE_LANE0, GATE_LANE0 + 4 * M_HEADS)
        b_t, li_t = jnp.transpose(b)[gate_rows], jnp.transpose(li)[gate_rows]
        cm = li_t - b_t
        sh = 1 if with_max else L
        while sh < L:
            earlier = jnp.where(tok16 >= sh, pltpu.roll(cm, sh, axis=1), NEG)
            later = jnp.where(tok16 < L - sh, pltpu.roll(cm, L - sh, axis=1), NEG)
            cm = jnp.maximum(cm, jnp.where(fwd_rows16, earlier, later))
            sh *= 2
        return b - li, tot, b_t, li_t, cm

    def head_lanes(hd, pair_slab):
        keep = lo if hd % 2 == 0 else jnp.logical_not(lo)
        return jnp.where(keep, pair_slab, jnp.zeros_like(pair_slab))

    def scores(u, kh, qh):
        return _nt(jnp.concatenate([kh, s_ref[u].astype(BF16)], axis=0), qh)

    def finish(d, hd, gq, kh, vt_h, big):
        r, tot, b_t, li_t, cm = gq
        u = d * M_HEADS + hd
        fl = GATE_LANE0 + d * 2 * M_HEADS + M_HEADS + hd
        gr = slice(fl - GATE_LANE0, fl - GATE_LANE0 + 1)
        b_row, li_row, cm_row = b_t[gr, :], li_t[gr, :], cm[gr, :]
        g, r_col = tot[:, fl:fl + 1], r[:, fl:fl + 1]
        vaug = jnp.concatenate([vt_h, ones_rows], axis=0)
        s_in = s_ref[u]
        m_in = m_ref[u][0:1, 0:1]
        h_t = None
        if big is not None:
            inter = b_row + m_in
            m_t = jnp.maximum(inter, b_row + cm_row)
            sc = big[0:L] * jnp.exp(jnp.where(vis[d], (b_row - m_t) - r_col, NEG))
            res = jnp.exp(inter - m_t) * big[L:] + _dot(vaug, sc.astype(BF16))
            den = jnp.maximum(jnp.abs(res[M_V:M_V + 1]), jnp.exp(-m_t))
            h_t = res[0:M_V] / den
        a_row = g - b_row + li_row
        m_loc = jnp.max(a_row, axis=1, keepdims=True)
        wv = (jnp.exp(a_row - m_loc) * vaug.astype(F32)).astype(BF16)
        s_loc = _dot(wv, kh)
        m_new = jnp.maximum(g + m_in, m_loc)
        s_ref[u] = jnp.exp(g + m_in - m_new) * s_in + jnp.exp(m_loc - m_new) * s_loc
        m_ref[u] = jnp.broadcast_to(m_new, (8, LANES))
        return h_t

    def chunk_gates(jj):
        return gates(g_ref[pl.ds(pl.multiple_of(jj * L, L), L), :])

    first_gates = (chunk_gates(0), chunk_gates(nc - 1))
    s_ref[...] = jnp.zeros_like(s_ref)
    m_ref[...] = jnp.zeros_like(m_ref)
    ctx_gates = {}
    for d in range(2):
        for cc in range(ncc):
            jj = cc if d == 0 else ncc - 1 - cc
            if jj not in ctx_gates:
                ctx_gates[jj] = gates(gc_ref[jj * L:(jj + 1) * L, :], with_max=False)
            gq = ctx_gates[jj]
            for hd in range(M_HEADS):
                kh = head_lanes(hd, mkc_ref[jj * L:(jj + 1) * L, pair(hd)])
                finish(d, hd, gq, kh, mvtc_ref[jj, vrows(hd), :], None)

    def body(j, gq):
        chunk_of = (j, nc - 1 - j)
        rows = [pl.ds(pl.multiple_of(jj * L, L), L) for jj in chunk_of]
        gq_next = (chunk_gates(jnp.minimum(j + 1, nc - 1)), chunk_gates(jnp.maximum(nc - 2 - j, 0)))
        pending = None
        for d in range(2):
            for hd in range(M_HEADS):
                kh = head_lanes(hd, mk_ref[rows[d], pair(hd)])
                big = scores(d * M_HEADS + hd, kh, head_lanes(hd, mq_ref[rows[d], pair(hd)]))
                if pending is not None:
                    pending()
                dst = hf_ref if d == 0 else hr_ref

                def pending(d=d, hd=hd, kh=kh, big=big, dst=dst):
                    dst[chunk_of[d], hd] = finish(d, hd, gq[d], kh, mvt_ref[chunk_of[d], vrows(hd), :], big)
        pending()
        return gq_next

    lax.fori_loop(0, nc, body, first_gates)

    def readout(j, carry):
        rows = pl.ds(pl.multiple_of(j * L, L), L)
        for hd in range(M_HEADS):
            hs_t = hf_ref[j, hd] + hr_ref[j, hd]
            ms = jnp.mean(hs_t * hs_t, axis=0, keepdims=True)
            hs = jnp.transpose(hs_t * lax.rsqrt(ms + NORM_EPS)) * hnw_ref[:, vrows(hd)]
            out_ref[rows, vrows(hd)] = (mo_ref[rows, vrows(hd)].astype(F32) * hs).astype(BF16)
        return carry

    lax.fori_loop(0, nc, readout, 0, unroll=4)


def _mlstm(mq, mk, mvt, g2, mo, mkc, mvtc, g2c, bias_row, hnw):
    bsz, t, _ = mq.shape
    n_ctx = mkc.shape[1]
    nc, _, chunk = mvt.shape[1:]
    ncc = mvtc.shape[1]
    per_b = lambda rows, width: pl.BlockSpec((None, rows, width), lambda b: (b, 0, 0))
    chunked = lambda n: pl.BlockSpec((None, n, M_WIDTH, chunk), lambda b: (b, 0, 0, 0))
    const = lambda shape: pl.BlockSpec(shape, lambda b: (0,) * len(shape))
    return pl.pallas_call(
        _mlstm_kernel,
        grid=(bsz,),
        in_specs=[per_b(t, M_QKW), per_b(t, M_QKW), chunked(nc), per_b(t, LANES), per_b(t, M_WIDTH),
                  per_b(n_ctx, M_QKW), chunked(ncc), per_b(n_ctx, LANES),
                  const((1, LANES)), const((1, M_WIDTH))],
        out_specs=per_b(t, M_WIDTH),
        out_shape=jax.ShapeDtypeStruct((bsz, t, M_WIDTH), BF16),
        scratch_shapes=[pltpu.VMEM((2 * M_HEADS, M_V + M_ONES, LANES), F32),
                        pltpu.VMEM((2 * M_HEADS, 8, LANES), F32),
                        pltpu.VMEM((nc, M_HEADS, M_V, chunk), F32),
                        pltpu.VMEM((nc, M_HEADS, M_V, chunk), F32)],
        compiler_params=_cparams(("parallel",)),
        name="mlstm",
    )(mq, mk, mvt, g2, mo, mkc, mvtc, g2c, bias_row, hnw)


def _axial_angles(n_lat, rot_dim):
    rows = n_lat // GRID_W
    row = jnp.repeat(jnp.arange(rows), GRID_W).astype(F32)
    col = jnp.tile(jnp.arange(GRID_W), rows).astype(F32)
    n_freq = rot_dim // 4
    inv = ROPE_THETA ** (-jnp.arange(n_freq, dtype=F32) / n_freq)
    ang = jnp.concatenate([row[:, None] * inv, col[:, None] * inv], axis=-1)
    return jnp.cos(ang), jnp.sin(ang)


def _rope_tables(n_lat):
    cos, sin = _axial_angles(n_lat, A_HEAD_DIM)
    cos_a = jnp.tile(jnp.concatenate([cos, cos], axis=-1), (1, 2))
    sin_a = jnp.tile(jnp.concatenate([-sin, sin], axis=-1), (1, 2))
    cos, sin = _axial_angles(n_lat, C_ROPE)
    one = lambda w: jnp.ones((n_lat, w), F32)
    zero = lambda w: jnp.zeros((n_lat, w), F32)
    cos_k = jnp.concatenate([cos, cos, one(LANES - C_ROPE)], axis=-1)
    sin_k = jnp.concatenate([-sin, sin, zero(LANES - C_ROPE)], axis=-1)
    cos_q = jnp.concatenate([one(C_NOPE), cos, cos, one(LANES - C_NOPE - C_ROPE)], axis=-1)
    sin_q = jnp.concatenate([zero(C_NOPE), -sin, sin, zero(LANES - C_NOPE - C_ROPE)], axis=-1)
    return (cos_a, sin_a), (cos_k, sin_k), (cos_q, sin_q)


_A_HEAD_ORDER = (0, 4, 1, 5, 2, 6, 3, 7)


def _perm_heads(w, axis):
    shape = w.shape
    grp = A_HEADS // A_KV_HEADS
    w = w.reshape(shape[:axis] + (A_KV_HEADS, grp, A_HEAD_DIM) + shape[axis + 1:])
    return jnp.swapaxes(w, axis, axis + 1).reshape(shape)


def _even_weights(w_in, w_out):
    offs = np.cumsum((0,) + E_COLS)
    parts = [w_in[:, offs[i]:offs[i + 1]] for i in range(len(E_COLS))]
    parts[2] = _perm_heads(parts[2], 1)
    z = parts[6]
    parts[6] = jnp.concatenate([_perm_heads(z[:, :A_WIDTH], 1), z[:, A_WIDTH:]], axis=1)
    w_out_p = jnp.concatenate([_perm_heads(w_out[:A_WIDTH], 0), w_out[A_WIDTH:]], axis=0)
    return jnp.concatenate(parts, axis=1).astype(BF16), w_out_p.astype(BF16)


def _odd_weights(w_in, w_uq, w_ukv):
    d = w_in.shape[0]
    offs = np.cumsum((0,) + O_COLS)
    ckv, kr, mk, mv, mg, cq, mq, mo, z = [w_in[:, offs[i]:offs[i + 1]] for i in range(len(O_COLS))]
    g2 = jnp.concatenate([kr, mg, jnp.zeros((d, LANES - C_ROPE - 4 * M_HEADS), w_in.dtype)], axis=1)
    w_p = jnp.concatenate([ckv, g2, mk, cq, mq, mo, z], axis=1).astype(BF16)
    wmvt = jnp.transpose(mv).astype(BF16)
    ukv = w_ukv.reshape(C_KV_LORA, C_HEADS, C_NOPE + C_VDIM)
    wk = jnp.pad(ukv[:, :, :C_NOPE], ((0, 0), (0, 0), (0, LANES - C_NOPE)))
    wk = wk.reshape(C_KV_LORA, C_HEADS * LANES).astype(BF16)
    wv = jnp.transpose(ukv[:, :, C_NOPE:].reshape(C_KV_LORA, C_WIDTH)).astype(BF16)
    uq = w_uq.reshape(C_Q_LORA, C_HEADS, C_NOPE + C_ROPE)
    wq = jnp.pad(uq, ((0, 0), (0, 0), (0, LANES - C_NOPE - C_ROPE))).reshape(C_Q_LORA, C_HEADS * LANES).astype(BF16)
    return w_p, wk, wv, wmvt, wq


def kernel(x, c, ctx, c_ctx, mod_w, mod_b, pre_norm_w, post_norm_w, e_w_in, e_sink, e_conv_w, e_w_out,
           o_w_in, o_q_norm_w, o_kv_norm_w, o_w_uq, o_w_ukv, o_i_bias, o_f_bias, o_head_norm_w, o_w_out):
    bsz, n_lat, d = x.shape
    n_ctx = ctx.shape[1]
    assert mod_w.shape[0] == 2 and d == D_MODEL, "built for one even + one odd layer"
    assert n_lat % 1024 == 0 and n_ctx % 256 == 0
    tm_in, tm_ctx = 1024, 256

    pad = (-(bsz + 1)) % 8
    cc = jnp.concatenate([c, c_ctx[None, :], jnp.zeros((pad, d), F32)], axis=0)
    mod = _modulation(cc, mod_w, mod_b)
    split = lambda l, r0, r1: [mod[l, r0:r1, k * d:(k + 1) * d][:, None, :] for k in range(3)]
    (rope_a, rope_k, rope_q) = _rope_tables(n_lat)
    dummy_tab = jnp.zeros((n_ctx, LANES), F32)

    sh_x, sc_x, g_x = split(0, 0, bsz)
    sh_c, sc_c, g_c = split(0, bsz, bsz + 1)
    w_in, w_out = _even_weights(e_w_in[0], e_w_out[0])
    nw, pw = pre_norm_w[0][None, :], post_norm_w[0][None, :]
    k, vt, q, bb, u, z = _even_in(x, sh_x, sc_x, nw, w_in, *rope_a, rope=True, tm=tm_in)
    k_c, vt_c, q_c, bb_c, u_c, z_c = _even_in(ctx, sh_c, sc_c, nw, w_in, dummy_tab, dummy_tab,
                                              rope=False, tm=tm_ctx)
    a = _even_attn(e_sink[0], q, k, vt, k_c, vt_c, window=True)
    a_c = _even_attn(e_sink[0], q_c, None, None, k_c, vt_c, window=False)
    x1 = _out_proj((a, bb, u), z, e_conv_w[0], w_out, pw, g_x, x, tm=tm_in)
    ctx1 = _out_proj((a_c, bb_c, u_c), z_c, e_conv_w[0], w_out, pw, g_c, ctx, tm=tm_ctx)

    sh_x, sc_x, g_x = split(1, 0, bsz)
    sh_c, sc_c, _ = split(1, bsz, bsz + 1)
    w_p, wk, wv, wmvt, wq = _odd_weights(o_w_in[0], o_w_uq[0], o_w_ukv[0])
    nw, pw = pre_norm_w[1][None, :], post_norm_w[1][None, :]
    kvn, qn = o_kv_norm_w[0][None, :], o_q_norm_w[0][None, :]
    chunk = min(MLSTM_CHUNK, n_ctx)
    kf, vt, mk, mvt, g2, qf, mq, mo, z = _odd_in(x1, sh_x, sc_x, nw, w_p, kvn, wk, wv, wmvt,
                                                 (*rope_k, qn, wq, *rope_q), tm=tm_in, chunk=chunk)
    kf_c, vt_c, mk_c, mvt_c, g2_c = _odd_in(ctx1, sh_c, sc_c, nw, w_p[:, :OC_CTX_END], kvn, wk, wv, wmvt, None,
                                            tm=tm_ctx, chunk=chunk)
    c_out = _mla(qf, kf_c, kf, vt_c, vt)
    gate_bias = jnp.stack([o_i_bias[0], o_f_bias[0]], axis=1).reshape(1, 4 * M_HEADS)
    bias_row = jnp.pad(gate_bias, ((0, 0), (GATE_LANE0, LANES - GATE_LANE0 - 4 * M_HEADS)))
    m_out = _mlstm(mq, mk, mvt, g2, mo, mk_c, mvt_c, g2_c, bias_row, o_head_norm_w[0][None, :])
    return _out_proj((c_out, m_out), z, None, o_w_out[0].astype(BF16), pw, g_x, x1, tm=tm_in)
```

```python
import collections
import functools

import numpy as np
import jax
import jax.numpy as jnp
from jax import lax
from jax.experimental import pallas as pl
from jax.experimental.pallas import tpu as pltpu

F32 = jnp.float32
BF16 = jnp.bfloat16
HIGHEST = lax.Precision.HIGHEST

LANES = 128
BF16_ROWS = 16
VMEM_LIMIT = 56 * 1024 * 1024

D_MODEL = 1024
GRID_W = 64
ROPE_THETA = 10000.0
NORM_EPS = 1e-6
NEG = -1e30
LOG2E = 1.4426950408889634

A_HEADS, A_KV_HEADS, A_HEAD_DIM = 8, 2, 64
A_WIDTH = A_HEADS * A_HEAD_DIM
WINDOW = 128
B_WIDTH = 512
C_HEADS, C_NOPE, C_ROPE, C_VDIM = 8, 64, 32, 64
C_KV_LORA, C_Q_LORA = 256, 768
C_WIDTH = C_HEADS * C_VDIM
M_HEADS, M_QK, M_V = 4, 64, 128
M_WIDTH = M_HEADS * M_V
E_GATE = A_WIDTH + B_WIDTH
O_GATE = C_WIDTH + M_WIDTH
E_COLS = (128, 128, A_WIDTH, B_WIDTH, B_WIDTH, B_WIDTH, E_GATE)
O_COLS = (C_KV_LORA, C_ROPE, M_HEADS * M_QK, M_WIDTH, 4 * M_HEADS, C_Q_LORA, M_HEADS * M_QK, M_WIDTH, O_GATE)

EC_K, EC_V, EC_Q, EC_BB, EC_BC, EC_BX, EC_Z, EC_END = 0, 128, 256, 768, 1280, 1792, 2304, 3328
OC_CKV = 0
OC_G2 = 256
OC_MK = 384
OC_CTX_END = 640
OC_CQ = 640
OC_MQ = 1408
OC_MO = 1664
OC_Z = 2176
M_QKW = M_HEADS * M_QK
M_ONES = BF16_ROWS
GATE_LANE0 = 32
SUB_ROWS = 256
MLSTM_CHUNK = 256
ATT_Q = 256
ATT_KEYS = 256
SCORE_SLOTS = 3
SCORE_ROWS = 128
ATT_VDIM = 64
MLA_BLK = 512
ATT_STEP = 1024


def _nt(a, b):
    return lax.dot_general(a, b, (((1,), (1,)), ((), ())), preferred_element_type=F32)


def _dot(a, b):
    return jnp.dot(a, b, preferred_element_type=F32)


def _silu(z):
    return z * jax.nn.sigmoid(z)


def _rms(x, w):
    ms = jnp.mean(x * x, axis=-1, keepdims=True)
    return x * lax.rsqrt(ms + NORM_EPS) * w


def _rope_slab(x, cos, sin_signed, half, take_up):
    up = pltpu.roll(x, LANES - half, axis=1)
    dn = pltpu.roll(x, half, axis=1)
    return x * cos + jnp.where(take_up, up, dn) * sin_signed


def _cparams(sem):
    return pltpu.CompilerParams(dimension_semantics=sem, vmem_limit_bytes=VMEM_LIMIT)


def _resident(shape):
    return pl.BlockSpec(shape, lambda b, i: (0,) * len(shape), pipeline_mode=pl.Buffered(1))


def _mod_kernel(cc_ref, w_ref, b_ref, o_ref):
    s = _silu(cc_ref[...])
    o_ref[...] = jnp.dot(s, w_ref[...], preferred_element_type=F32, precision=HIGHEST) + b_ref[...]


def _modulation(cc, mod_w, mod_b):
    depth, d, d3 = mod_w.shape
    r = cc.shape[0]
    return pl.pallas_call(
        _mod_kernel,
        grid=(depth, d3 // d),
        in_specs=[
            pl.BlockSpec((r, d), lambda l, j: (0, 0)),
            pl.BlockSpec((None, d, d), lambda l, j: (l, 0, j)),
            pl.BlockSpec((None, 1, d), lambda l, j: (l, 0, j)),
        ],
        out_specs=pl.BlockSpec((None, r, d), lambda l, j: (l, 0, j)),
        out_shape=jax.ShapeDtypeStruct((depth, r, d3), F32),
        compiler_params=_cparams(("arbitrary", "arbitrary")),
        name="modulation",
    )(cc, mod_w, mod_b.reshape(depth, 1, d3))


def _even_in_kernel(x_ref, sh_ref, sc_ref, nw_ref, w_ref, cos_ref, sin_ref,
                    k_ref, vt_ref, q_ref, bb_ref, u_ref, z_ref, *, rope):
    lane = lax.broadcasted_iota(jnp.int32, (1, LANES), 1)
    take_up = (lane % A_HEAD_DIM) < (A_HEAD_DIM // 2)
    for r0 in range(0, x_ref.shape[0], SUB_ROWS):
        rows = slice(r0, r0 + SUB_ROWS)
        h = (_rms(x_ref[rows, :], nw_ref[...]) * (1.0 + sc_ref[...]) + sh_ref[...]).astype(BF16)
        y_all = _dot(h, w_ref[...])

        def proj(a, b):
            return y_all[:, a:b]

        if rope:
            cos, sin = cos_ref[rows, :], sin_ref[rows, :]
            rot = lambda t: _rope_slab(t, cos, sin, A_HEAD_DIM // 2, take_up)
        else:
            rot = lambda t: t
        k_ref[rows, :] = rot(proj(EC_K, EC_V)).astype(BF16)
        vt_ref[:, rows] = jnp.transpose(proj(EC_V, EC_Q)).astype(BF16)
        q = proj(EC_Q, EC_BB)
        for j in range(A_WIDTH // LANES):
            sl = slice(j * LANES, (j + 1) * LANES)
            q_ref[rows, sl] = (rot(q[:, sl]) * (A_HEAD_DIM ** -0.5 * LOG2E)).astype(BF16)
        bb_ref[rows, :] = proj(EC_BB, EC_BC).astype(BF16)
        u_ref[rows, :] = (proj(EC_BC, EC_BX) * proj(EC_BX, EC_Z)).astype(BF16)
        z_ref[rows, :] = _silu(proj(EC_Z, EC_END)).astype(BF16)


def _even_in(x, sh, sc, nw, w, cos, sin, *, rope, tm):
    bsz, t, d = x.shape
    nb = sh.shape[0]
    mod_map = (lambda b, i: (b, 0, 0)) if nb > 1 else (lambda b, i: (0, 0, 0))
    tok = lambda width: pl.BlockSpec((None, tm, width), lambda b, i: (b, i, 0))
    const = lambda shape: pl.BlockSpec(shape, lambda b, i: (0,) * len(shape))
    widths = (LANES, None, A_WIDTH, B_WIDTH, B_WIDTH, E_GATE)
    vt_spec = pl.BlockSpec((None, LANES, tm), lambda b, i: (b, 0, i))
    vt_shape = jax.ShapeDtypeStruct((bsz, LANES, t), BF16)
    return pl.pallas_call(
        functools.partial(_even_in_kernel, rope=rope),
        grid=(bsz, t // tm),
        in_specs=[
            tok(d),
            pl.BlockSpec((None, 1, d), mod_map),
            pl.BlockSpec((None, 1, d), mod_map),
            const((1, d)),
            _resident(w.shape),
            pl.BlockSpec((tm, LANES), lambda b, i: (i, 0)),
            pl.BlockSpec((tm, LANES), lambda b, i: (i, 0)),
        ],
        out_specs=[vt_spec if wd is None else tok(wd) for wd in widths],
        out_shape=[vt_shape if wd is None else jax.ShapeDtypeStruct((bsz, t, wd), BF16) for wd in widths],
        compiler_params=_cparams(("parallel", "parallel")),
        name="even_in_rope" if rope else "even_in",
    )(x, sh, sc, nw, w, cos, sin)


AttnStage = collections.namedtuple("AttnStage", "q k vt mask sink")


def _tree_max(parts):
    while len(parts) > 1:
        parts = [jnp.maximum(parts[i], parts[i + 1]) if i + 1 < len(parts) else parts[i]
                 for i in range(0, len(parts), 2)]
    return parts[0]


def _attention_pipeline(stages, n_chunks, s_ref, write_pair):
    kc, sr = ATT_KEYS, SCORE_ROWS
    slots, n_keys = s_ref.shape[0], n_chunks * kc
    lag = slots - 1
    ones = jnp.ones((BF16_ROWS, n_keys), BF16)
    m8 = {}
    halves = []
    for i in range(len(stages) + lag):
        if i < len(stages):
            cur = stages[i]
            q = cur.q()
            parts = []
            for c in range(n_chunks):
                kch, mask = cur.k(c), cur.mask(c)
                for r in range(0, kc, sr):
                    st = _nt(kch[r:r + sr], q)
                    if mask is not None:
                        st = jnp.where(mask[r:r + sr], st, NEG)
                    s_ref[i % slots, c * kc + r:c * kc + r + sr, :] = st
                    parts += [st[t:t + 8, :] for t in range(0, sr, 8)]
            m8[i] = _tree_max(parts)
        if i >= lag:
            prev = stages[i - lag]
            m_prev = jnp.max(m8.pop(i - lag), axis=0, keepdims=True)
            if prev.sink is not None:
                m_prev = jnp.maximum(m_prev, prev.sink)
            pt = jnp.exp2(s_ref[(i - lag) % slots, 0:n_keys, :] - m_prev).astype(BF16)
            vta = jnp.concatenate([prev.vt(c) for c in range(n_chunks)], axis=1)
            acc = _dot(jnp.concatenate([vta, ones], axis=0), pt)
            denom = acc[ATT_VDIM:ATT_VDIM + 1]
            if prev.sink is not None:
                denom = denom + jnp.exp2(prev.sink - m_prev)
            halves.append(acc[0:ATT_VDIM] / denom)
            if len(halves) == 2:
                write_pair((i - lag) // 2, jnp.concatenate(halves, axis=0))
                halves = []


def _even_attn_kernel(*refs, window, n_lat):
    if window:
        (sink_ref, q_ref, kp_ref, km_ref, kn_ref, kc_ref, vtp_ref, vtm_ref, vtn_ref, vtc_ref,
         o_ref, s_ref) = refs
    else:
        sink_ref, q_ref, kc_ref, vtc_ref, o_ref, s_ref = refs
    nq, kc = ATT_Q, ATT_KEYS
    step, half = q_ref.shape[0], WINDOW
    lane = lax.broadcasted_iota(jnp.int32, (1, LANES), 1)
    lo = lane < A_HEAD_DIM
    ctx_k = [kc_ref[r:r + kc, :] for r in range(0, kc_ref.shape[0], kc)]
    ctx_vt = [vtc_ref[:, r:r + kc] for r in range(0, kc_ref.shape[0], kc)]
    r = lax.broadcasted_iota(jnp.int32, (kc, nq), 0)
    c = lax.broadcasted_iota(jnp.int32, (kc, nq), 1)

    def sub_block(qb):
        k_chunks, vt_chunks, masks = list(ctx_k), list(ctx_vt), [None] * len(ctx_k)
        if window:
            base, q0 = qb * nq, pl.program_id(1) * step + qb * nq
            if qb == 0:
                k1 = jnp.concatenate([kp_ref[...], km_ref[0:half, :]], axis=0)
                vt1 = jnp.concatenate([vtp_ref[...], vtm_ref[:, 0:half]], axis=1)
            else:
                k1, vt1 = km_ref[base - half:base + half, :], vtm_ref[:, base - half:base + half]
            if base + nq == step:
                k2 = jnp.concatenate([km_ref[base + half:step, :], kn_ref[...]], axis=0)
                vt2 = jnp.concatenate([vtm_ref[:, base + half:step], vtn_ref[...]], axis=1)
            else:
                k2, vt2 = km_ref[base + half:base + half + kc, :], vtm_ref[:, base + half:base + half + kc]
            k_chunks += [k1, k2]
            vt_chunks += [vt1, vt2]
            masks += [(jnp.abs(r - half - c) <= WINDOW) & (r + (q0 - half) >= 0),
                      (jnp.abs(r + half - c) <= WINDOW) & (r + (q0 + half) < n_lat)]
        return k_chunks, vt_chunks, masks

    def stage(qb, head, chunks):
        k_chunks, vt_chunks, masks = chunks
        j, grp = head % (A_HEADS // 2), head // (A_HEADS // 2)
        keep = lo if grp == 0 else jnp.logical_not(lo)

        def q():
            qs = q_ref[qb * nq:(qb + 1) * nq, j * LANES:(j + 1) * LANES]
            return jnp.where(keep, qs, jnp.zeros_like(qs))

        return AttnStage(q=q, k=lambda ci: k_chunks[ci],
                         vt=lambda ci: vt_chunks[ci][grp * A_HEAD_DIM:(grp + 1) * A_HEAD_DIM, :],
                         mask=lambda ci: masks[ci], sink=sink_ref[head] * LOG2E)

    def write_pair(p, o_t):
        qb, j = divmod(p, A_HEADS // 2)
        o_ref[qb * nq:(qb + 1) * nq, j * LANES:(j + 1) * LANES] = jnp.transpose(o_t).astype(BF16)

    stages = []
    for qb in range(step // nq):
        chunks = sub_block(qb)
        stages += [stage(qb, h, chunks) for h in _A_HEAD_ORDER]
    _attention_pipeline(stages, len(ctx_k) + (2 if window else 0), s_ref, write_pair)


def _even_attn(sink, q, k, vt, kc, vtc, *, window):
    bsz, t, _ = q.shape
    n_ctx = kc.shape[1]
    nq, keys, half = ATT_Q, ATT_KEYS, WINDOW
    step = min(ATT_STEP, t)
    assert nq == keys == 2 * half and t % step == 0 and n_ctx % keys == 0
    per_step, last = step // half, t // half - 1
    smem = pl.BlockSpec(memory_space=pltpu.SMEM)
    qspec = pl.BlockSpec((None, step, A_WIDTH), lambda b, i: (b, i, 0))
    kcspec = pl.BlockSpec((None, n_ctx, LANES), lambda b, i: (b, 0, 0))
    vtcspec = pl.BlockSpec((None, LANES, n_ctx), lambda b, i: (b, 0, 0))
    if window:
        prev = lambda i: jnp.maximum(per_step * i - 1, 0)
        nxt = lambda i: jnp.minimum(per_step * (i + 1), last)
        specs = [smem, qspec,
                 pl.BlockSpec((None, half, LANES), lambda b, i: (b, prev(i), 0)),
                 pl.BlockSpec((None, step, LANES), lambda b, i: (b, i, 0)),
                 pl.BlockSpec((None, half, LANES), lambda b, i: (b, nxt(i), 0)),
                 kcspec,
                 pl.BlockSpec((None, LANES, half), lambda b, i: (b, 0, prev(i))),
                 pl.BlockSpec((None, LANES, step), lambda b, i: (b, 0, i)),
                 pl.BlockSpec((None, LANES, half), lambda b, i: (b, 0, nxt(i))),
                 vtcspec]
        args = (sink, q, k, k, k, kc, vt, vt, vt, vtc)
    else:
        specs = [smem, qspec, kcspec, vtcspec]
        args = (sink, q, kc, vtc)
    n_chunks = n_ctx // keys + (2 if window else 0)
    return pl.pallas_call(
        functools.partial(_even_attn_kernel, window=window, n_lat=t),
        grid=(bsz, t // step),
        in_specs=specs,
        out_specs=qspec,
        out_shape=jax.ShapeDtypeStruct((bsz, t, A_WIDTH), BF16),
        scratch_shapes=[pltpu.VMEM((SCORE_SLOTS, n_chunks * keys, nq), F32)],
        compiler_params=_cparams(("parallel", "parallel")),
        name="even_attn_window" if window else "even_attn_ctx",
    )(*args)


def _out_kernel(*refs, conv):
    if conv:
        a_ref, bb_ref, u_ref, up_ref, un_ref, z_ref, cw_ref, w_ref, pw_ref, g_ref, x_ref, o_ref = refs
        tm = u_ref.shape[0]
        i = pl.program_id(1)
        u = u_ref[...].astype(F32)
        prev_row = jnp.where(i > 0, up_ref[BF16_ROWS - 1:BF16_ROWS, :].astype(F32), 0.0)
        next_row = jnp.where(i < pl.num_programs(1) - 1, un_ref[0:1, :].astype(F32), 0.0)
        row = lax.broadcasted_iota(jnp.int32, (tm, 1), 0)
        u_m1 = jnp.where(row == 0, prev_row, pltpu.roll(u, 1, axis=0))
        u_p1 = jnp.where(row == tm - 1, next_row, pltpu.roll(u, tm - 1, axis=0))
        cw = cw_ref[...]
        mix_b = (bb_ref[...].astype(F32) * (u_m1 * cw[0:1, :] + u * cw[1:2, :] + u_p1 * cw[2:3, :])).astype(BF16)
    else:
        a_ref, b_ref, z_ref, w_ref, pw_ref, g_ref, x_ref, o_ref = refs
        mix_b = b_ref[...]
    wa = a_ref.shape[1]
    for r0 in range(0, x_ref.shape[0], SUB_ROWS):
        rows = slice(r0, r0 + SUB_ROWS)
        gated = jnp.concatenate([a_ref[rows, :] * z_ref[rows, 0:wa], mix_b[rows, :] * z_ref[rows, wa:]], axis=1)
        o_ref[rows, :] = x_ref[rows, :] + g_ref[...] * _rms(_dot(gated, w_ref[...]), pw_ref[...])


def _out_proj(mix, z, cw, w, pw, g, x, *, tm):
    conv = cw is not None
    bsz, t, d = x.shape
    nb = g.shape[0]
    mod_map = (lambda b, i: (b, 0, 0)) if nb > 1 else (lambda b, i: (0, 0, 0))
    tok = lambda width: pl.BlockSpec((None, tm, width), lambda b, i: (b, i, 0))
    const = lambda shape: pl.BlockSpec(shape, lambda b, i: (0,) * len(shape))
    hb = BF16_ROWS
    per_tile, last = tm // hb, t // hb - 1
    if conv:
        a, bb, u = mix
        specs = [tok(a.shape[2]), tok(bb.shape[2]), tok(u.shape[2]),
                 pl.BlockSpec((None, hb, u.shape[2]), lambda b, i: (b, jnp.maximum(i * per_tile - 1, 0), 0)),
                 pl.BlockSpec((None, hb, u.shape[2]), lambda b, i: (b, jnp.minimum((i + 1) * per_tile, last), 0)),
                 tok(z.shape[2]), const(cw.shape)]
        args = (a, bb, u, u, u, z, cw)
    else:
        a, b2 = mix
        specs = [tok(a.shape[2]), tok(b2.shape[2]), tok(z.shape[2])]
        args = (a, b2, z)
    specs += [_resident(w.shape), const((1, d)), pl.BlockSpec((None, 1, d), mod_map), tok(d)]
    return pl.pallas_call(
        functools.partial(_out_kernel, conv=conv),
        grid=(bsz, t // tm),
        in_specs=specs,
        out_specs=tok(d),
        out_shape=jax.ShapeDtypeStruct((bsz, t, d), F32),
        compiler_params=_cparams(("parallel", "parallel")),
        name="out_proj_conv" if conv else "out_proj",
    )(*args, w, pw, g, x)


def _odd_in_kernel(*refs, lat):
    if lat:
        (x_ref, sh_ref, sc_ref, nw_ref, w_ref, kvn_ref, wk_ref, wvt_ref, wmvt_ref, cosk_ref, sink_ref,
         qn_ref, wq_ref, cosq_ref, sinq_ref,
         kf_ref, vt_ref, mk_ref, mvt_ref, g2_ref, qf_ref, mq_ref, mo_ref, z_ref) = refs
    else:
        (x_ref, sh_ref, sc_ref, nw_ref, w_ref, kvn_ref, wk_ref, wvt_ref, wmvt_ref,
         kf_ref, vt_ref, mk_ref, mvt_ref, g2_ref) = refs
    lane = lax.broadcasted_iota(jnp.int32, (1, LANES), 1)
    chunk = mvt_ref.shape[2]
    per_sub = SUB_ROWS // chunk
    for cc in range(x_ref.shape[0] // SUB_ROWS):
        rows = slice(cc * SUB_ROWS, (cc + 1) * SUB_ROWS)
        h = (_rms(x_ref[rows, :], nw_ref[...]) * (1.0 + sc_ref[...]) + sh_ref[...]).astype(BF16)
        y_all = _dot(h, w_ref[...])

        def proj(a, b):
            return y_all[:, a:b]

        ckv = _rms(proj(OC_CKV, OC_CKV + C_KV_LORA), kvn_ref[...]).astype(BF16)
        g2 = proj(OC_G2, OC_G2 + LANES)
        g2_ref[rows, :] = g2
        if lat:
            g2 = _rope_slab(g2, cosk_ref[rows, :], sink_ref[rows, :], C_ROPE // 2, lane < C_ROPE // 2)
        rope_lanes = (lane >= C_NOPE) & (lane < C_NOPE + C_ROPE)
        kr = jnp.where(rope_lanes, pltpu.roll(g2, C_NOPE, axis=1), 0.0)
        k_nope = _dot(ckv, wk_ref[...])
        for hd in range(C_HEADS):
            sl = slice(hd * LANES, (hd + 1) * LANES)
            kf_ref[rows, sl] = (k_nope[:, sl] + kr).astype(BF16)
        vt_ref[:, rows] = _nt(wvt_ref[...], ckv).astype(BF16)
        mk_ref[rows, :] = (proj(OC_MK, OC_MK + M_QKW) * (M_QK ** -0.5)).astype(BF16)
        mvt = _nt(wmvt_ref[...], h).astype(BF16)
        for s in range(per_sub):
            mvt_ref[cc * per_sub + s] = mvt[:, s * chunk:(s + 1) * chunk]
        if lat:
            cq = _rms(proj(OC_CQ, OC_CQ + C_Q_LORA), qn_ref[...]).astype(BF16)
            qf = _dot(cq, wq_ref[...])
            cos, sin = cosq_ref[rows, :], sinq_ref[rows, :]
            take_up = (lane >= C_NOPE) & (lane < C_NOPE + C_ROPE // 2)
            scale = (C_NOPE + C_ROPE) ** -0.5 * LOG2E
            for hd in range(C_HEADS):
                sl = slice(hd * LANES, (hd + 1) * LANES)
                qf_ref[rows, sl] = (_rope_slab(qf[:, sl], cos, sin, C_ROPE // 2, take_up) * scale).astype(BF16)
            mq_ref[rows, :] = proj(OC_MQ, OC_MQ + M_QKW).astype(BF16)
            mo_ref[rows, :] = jax.nn.sigmoid(proj(OC_MO, OC_MO + M_WIDTH)).astype(BF16)
            z_ref[rows, :] = _silu(proj(OC_Z, OC_Z + O_GATE)).astype(BF16)


def _odd_in(x, sh, sc, nw, w, kvn, wk, wv, wmvt, lat_args, *, tm, chunk):
    lat = lat_args is not None
    bsz, t, d = x.shape
    nb = sh.shape[0]
    mod_map = (lambda b, i: (b, 0, 0)) if nb > 1 else (lambda b, i: (0, 0, 0))
    tok = lambda width: (pl.BlockSpec((None, tm, width), lambda b, i: (b, i, 0)), (bsz, t, width))
    const = lambda shape: pl.BlockSpec(shape, lambda b, i: (0,) * len(shape))
    tab = pl.BlockSpec((tm, LANES), lambda b, i: (i, 0))
    specs = [tok(d)[0], pl.BlockSpec((None, 1, d), mod_map), pl.BlockSpec((None, 1, d), mod_map),
             const((1, d)), _resident(w.shape), const(kvn.shape), _resident(wk.shape), _resident(wv.shape),
             _resident(wmvt.shape)]
    args = [x, sh, sc, nw, w, kvn, wk, wv, wmvt]
    vt = (pl.BlockSpec((None, C_WIDTH, tm), lambda b, i: (b, 0, i)), (bsz, C_WIDTH, t))
    mvt = (pl.BlockSpec((None, tm // chunk, M_WIDTH, chunk), lambda b, i: (b, i, 0, 0)),
           (bsz, t // chunk, M_WIDTH, chunk))
    outs = [(tok(C_HEADS * LANES), BF16), (vt, BF16), (tok(M_QKW), BF16), (mvt, BF16), (tok(LANES), F32)]
    if lat:
        cosk, sink, qn, wq, cosq, sinq = lat_args
        specs += [tab, tab, const(qn.shape), _resident(wq.shape), tab, tab]
        args += [cosk, sink, qn, wq, cosq, sinq]
        outs += [(tok(C_HEADS * LANES), BF16), (tok(M_QKW), BF16), (tok(M_WIDTH), BF16), (tok(O_GATE), BF16)]
    return pl.pallas_call(
        functools.partial(_odd_in_kernel, lat=lat),
        grid=(bsz, t // tm),
        in_specs=specs,
        out_specs=[o[0][0] for o in outs],
        out_shape=[jax.ShapeDtypeStruct(o[0][1], o[1]) for o in outs],
        compiler_params=_cparams(("parallel", "parallel")),
        name="odd_in_lat" if lat else "odd_in_ctx",
    )(*args)


def _mla_kernel(q_ref, kc_ref, vtc_ref, *rest):
    n_pairs = C_HEADS // 2
    kx_refs, vtx_refs, (o_ref, s_ref) = rest[:n_pairs], rest[n_pairs:2 * n_pairs], rest[2 * n_pairs:]
    nq, kc = ATT_Q, ATT_KEYS
    n_ctx_chunks = kc_ref.shape[0] // kc
    n_chunks = n_ctx_chunks + kx_refs[0].shape[0] // kc

    def stage(qb, hd):
        sl = slice(hd * LANES, (hd + 1) * LANES)
        half = hd % 2

        def k(ci):
            if ci < n_ctx_chunks:
                return kc_ref[ci * kc:(ci + 1) * kc, sl]
            r0 = (ci - n_ctx_chunks) * kc
            return kx_refs[hd // 2][r0:r0 + kc, half * LANES:(half + 1) * LANES]

        def vt(ci):
            if ci < n_ctx_chunks:
                return vtc_ref[hd * C_VDIM:(hd + 1) * C_VDIM, ci * kc:(ci + 1) * kc]
            r0 = (ci - n_ctx_chunks) * kc
            return vtx_refs[hd // 2][half * C_VDIM:(half + 1) * C_VDIM, r0:r0 + kc]

        return AttnStage(q=lambda: q_ref[qb * nq:(qb + 1) * nq, sl], k=k, vt=vt, mask=lambda ci: None, sink=None)

    def write_pair(p, o_t):
        qb, j = divmod(p, C_HEADS // 2)
        o_ref[qb * nq:(qb + 1) * nq, j * LANES:(j + 1) * LANES] = jnp.transpose(o_t).astype(BF16)

    stages = [stage(qb, hd) for qb in range(q_ref.shape[0] // nq) for hd in range(C_HEADS)]
    _attention_pipeline(stages, n_chunks, s_ref, write_pair)


def _mla(qf, kfc, kfx, vtc, vtx):
    bsz, t, wq = qf.shape
    n_ctx = kfc.shape[1]
    blk = MLA_BLK
    n_pairs = C_HEADS // 2
    pair_k = lambda j: pl.BlockSpec((None, t, 2 * LANES), lambda b, i: (b, 0, j))
    pair_vt = lambda j: pl.BlockSpec((None, 2 * C_VDIM, t), lambda b, i: (b, j, 0))
    return pl.pallas_call(
        _mla_kernel,
        grid=(bsz, t // blk),
        in_specs=[
            pl.BlockSpec((None, blk, wq), lambda b, i: (b, i, 0)),
            pl.BlockSpec((None, n_ctx, wq), lambda b, i: (b, 0, 0)),
            pl.BlockSpec((None, C_WIDTH, n_ctx), lambda b, i: (b, 0, 0)),
        ] + [pair_k(j) for j in range(n_pairs)] + [pair_vt(j) for j in range(n_pairs)],
        out_specs=pl.BlockSpec((None, blk, C_WIDTH), lambda b, i: (b, i, 0)),
        out_shape=jax.ShapeDtypeStruct((bsz, t, C_WIDTH), BF16),
        scratch_shapes=[pltpu.VMEM((SCORE_SLOTS, n_ctx + t, ATT_Q), F32)],
        compiler_params=_cparams(("parallel", "arbitrary")),
        name="mla_attention",
    )(qf, kfc, vtc, *([kfx] * n_pairs), *([vtx] * n_pairs))


def _mlstm_kernel(mq_ref, mk_ref, mvt_ref, g_ref, mo_ref, mkc_ref, mvtc_ref, gc_ref, bias_ref, hnw_ref,
                  out_ref, s_ref, m_ref, hf_ref, hr_ref):
    nc, ncc, L = mvt_ref.shape[0], mvtc_ref.shape[0], mvt_ref.shape[2]
    row = lax.broadcasted_iota(jnp.int32, (L, L), 0)
    col = lax.broadcasted_iota(jnp.int32, (L, L), 1)
    vis = (row <= col, row >= col)
    tri_ones = (col <= row).astype(BF16)
    lane = lax.broadcasted_iota(jnp.int32, (1, LANES), 1)
    lo = lane < M_QK
    ones_rows = jnp.ones((M_ONES, L), BF16)
    fwd_lanes = lane < GATE_LANE0 + 2 * M_HEADS
    tok16 = lax.broadcasted_iota(jnp.int32, (4 * M_HEADS, L), 1)
    fwd_rows16 = lax.broadcasted_iota(jnp.int32, (4 * M_HEADS, L), 0) < 2 * M_HEADS
    pair = lambda hd: slice((hd // 2) * LANES, (hd // 2 + 1) * LANES)
    vrows = lambda hd: slice(hd * M_V, (hd + 1) * M_V)

    def gates(g, with_max=True):
        gb = g + bias_ref[...]
        ls = jnp.minimum(gb, 0.0) - jnp.log1p(jnp.exp(-jnp.abs(gb)))
        hi = ls.astype(BF16)
        lo = (ls - hi.astype(F32)).astype(BF16)
        cf = _dot(tri_ones, hi) + _dot(tri_ones, lo)
        tot = cf[L - 1:L, :]
        b = jnp.where(fwd_lanes, cf, tot - cf + ls)
        li = pltpu.roll(gb, M_HEADS, axis=1)
        gate_rows = slice(GATE_LANE0, GATE_LANE0 + 4 * M_HEADS)
        b_t, li_t = jnp.transpose(b)[gate_rows], jnp.transpose(li)[gate_rows]
        cm = li_t - b_t
        sh = 1 if with_max else L
        while sh < L:
            earlier = jnp.where(tok16 >= sh, pltpu.roll(cm, sh, axis=1), NEG)
            later = jnp.where(tok16 < L - sh, pltpu.roll(cm, L - sh, axis=1), NEG)
            cm = jnp.maximum(cm, jnp.where(fwd_rows16, earlier, later))
            sh *= 2
        return b - li, tot, b_t, li_t, cm

    def head_lanes(hd, pair_slab):
        keep = lo if hd % 2 == 0 else jnp.logical_not(lo)
        return jnp.where(keep, pair_slab, jnp.zeros_like(pair_slab))

    def scores(u, kh, qh):
        return _nt(jnp.concatenate([kh, s_ref[u].astype(BF16)], axis=0), qh)

    def finish(d, hd, gq, kh, vt_h, big):
        r, tot, b_t, li_t, cm = gq
        u = d * M_HEADS + hd
        fl = GATE_LANE0 + d * 2 * M_HEADS + M_HEADS + hd
        gr = slice(fl - GATE_LANE0, fl - GATE_LANE0 + 1)
        b_row, li_row, cm_row = b_t[gr, :], li_t[gr, :], cm[gr, :]
        g, r_col = tot[:, fl:fl + 1], r[:, fl:fl + 1]
        vaug = jnp.concatenate([vt_h, ones_rows], axis=0)
        s_in = s_ref[u]
        m_in = m_ref[u][0:1, 0:1]
        h_t = None
        if big is not None:
            inter = b_row + m_in
            m_t = jnp.maximum(inter, b_row + cm_row)
            sc = big[0:L] * jnp.exp(jnp.where(vis[d], (b_row - m_t) - r_col, NEG))
            res = jnp.exp(inter - m_t) * big[L:] + _dot(vaug, sc.astype(BF16))
            den = jnp.maximum(jnp.abs(res[M_V:M_V + 1]), jnp.exp(-m_t))
            h_t = res[0:M_V] / den
        a_row = g - b_row + li_row
        m_loc = jnp.max(a_row, axis=1, keepdims=True)
        wv = (jnp.exp(a_row - m_loc) * vaug.astype(F32)).astype(BF16)
        s_loc = _dot(wv, kh)
        m_new = jnp.maximum(g + m_in, m_loc)
        s_ref[u] = jnp.exp(g + m_in - m_new) * s_in + jnp.exp(m_loc - m_new) * s_loc
        m_ref[u] = jnp.broadcast_to(m_new, (8, LANES))
        return h_t

    def chunk_gates(jj):
        return gates(g_ref[pl.ds(pl.multiple_of(jj * L, L), L), :])

    first_gates = (chunk_gates(0), chunk_gates(nc - 1))
    s_ref[...] = jnp.zeros_like(s_ref)
    m_ref[...] = jnp.zeros_like(m_ref)
    ctx_gates = {}
    for d in range(2):
        for cc in range(ncc):
            jj = cc if d == 0 else ncc - 1 - cc
            if jj not in ctx_gates:
                ctx_gates[jj] = gates(gc_ref[jj * L:(jj + 1) * L, :], with_max=False)
            gq = ctx_gates[jj]
            for hd in range(M_HEADS):
                kh = head_lanes(hd, mkc_ref[jj * L:(jj + 1) * L, pair(hd)])
                finish(d, hd, gq, kh, mvtc_ref[jj, vrows(hd), :], None)

    def body(j, gq):
        chunk_of = (j, nc - 1 - j)
        rows = [pl.ds(pl.multiple_of(jj * L, L), L) for jj in chunk_of]
        gq_next = (chunk_gates(jnp.minimum(j + 1, nc - 1)), chunk_gates(jnp.maximum(nc - 2 - j, 0)))
        pending = None
        for d in range(2):
            for hd in range(M_HEADS):
                kh = head_lanes(hd, mk_ref[rows[d], pair(hd)])
                big = scores(d * M_HEADS + hd, kh, head_lanes(hd, mq_ref[rows[d], pair(hd)]))
                if pending is not None:
                    pending()
                dst = hf_ref if d == 0 else hr_ref

                def pending(d=d, hd=hd, kh=kh, big=big, dst=dst):
                    dst[chunk_of[d], hd] = finish(d, hd, gq[d], kh, mvt_ref[chunk_of[d], vrows(hd), :], big)
        pending()
        return gq_next

    lax.fori_loop(0, nc, body, first_gates)

    def readout(j, carry):
        rows = pl.ds(pl.multiple_of(j * L, L), L)
        for hd in range(M_HEADS):
            hs_t = hf_ref[j, hd] + hr_ref[j, hd]
            ms = jnp.mean(hs_t * hs_t, axis=0, keepdims=True)
            hs = jnp.transpose(hs_t * lax.rsqrt(ms + NORM_EPS)) * hnw_ref[:, vrows(hd)]
            out_ref[rows, vrows(hd)] = (mo_ref[rows, vrows(hd)].astype(F32) * hs).astype(BF16)
        return carry

    lax.fori_loop(0, nc, readout, 0, unroll=4)


def _mlstm(mq, mk, mvt, g2, mo, mkc, mvtc, g2c, bias_row, hnw):
    bsz, t, _ = mq.shape
    n_ctx = mkc.shape[1]
    nc, _, chunk = mvt.shape[1:]
    ncc = mvtc.shape[1]
    per_b = lambda rows, width: pl.BlockSpec((None, rows, width), lambda b: (b, 0, 0))
    chunked = lambda n: pl.BlockSpec((None, n, M_WIDTH, chunk), lambda b: (b, 0, 0, 0))
    const = lambda shape: pl.BlockSpec(shape, lambda b: (0,) * len(shape))
    return pl.pallas_call(
        _mlstm_kernel,
        grid=(bsz,),
        in_specs=[per_b(t, M_QKW), per_b(t, M_QKW), chunked(nc), per_b(t, LANES), per_b(t, M_WIDTH),
                  per_b(n_ctx, M_QKW), chunked(ncc), per_b(n_ctx, LANES),
                  const((1, LANES)), const((1, M_WIDTH))],
        out_specs=per_b(t, M_WIDTH),
        out_shape=jax.ShapeDtypeStruct((bsz, t, M_WIDTH), BF16),
        scratch_shapes=[pltpu.VMEM((2 * M_HEADS, M_V + M_ONES, LANES), F32),
                        pltpu.VMEM((2 * M_HEADS, 8, LANES), F32),
                        pltpu.VMEM((nc, M_HEADS, M_V, chunk), F32),
                        pltpu.VMEM((nc, M_HEADS, M_V, chunk), F32)],
        compiler_params=_cparams(("parallel",)),
        name="mlstm",
    )(mq, mk, mvt, g2, mo, mkc, mvtc, g2c, bias_row, hnw)


def _axial_angles(n_lat, rot_dim):
    rows = n_lat // GRID_W
    row = jnp.repeat(jnp.arange(rows), GRID_W).astype(F32)
    col = jnp.tile(jnp.arange(GRID_W), rows).astype(F32)
    n_freq = rot_dim // 4
    inv = ROPE_THETA ** (-jnp.arange(n_freq, dtype=F32) / n_freq)
    ang_t = jnp.concatenate([inv[:, None] * row[None, :], inv[:, None] * col[None, :]], axis=0)
    return jnp.transpose(jnp.cos(ang_t)), jnp.transpose(jnp.sin(ang_t))


def _rope_tables(n_lat):
    cos, sin = _axial_angles(n_lat, A_HEAD_DIM)
    cos_a = jnp.tile(jnp.concatenate([cos, cos], axis=-1), (1, 2))
    sin_a = jnp.tile(jnp.concatenate([-sin, sin], axis=-1), (1, 2))
    cos, sin = _axial_angles(n_lat, C_ROPE)
    one = lambda w: jnp.ones((n_lat, w), F32)
    zero = lambda w: jnp.zeros((n_lat, w), F32)
    cos_k = jnp.concatenate([cos, cos, one(LANES - C_ROPE)], axis=-1)
    sin_k = jnp.concatenate([-sin, sin, zero(LANES - C_ROPE)], axis=-1)
    cos_q = jnp.concatenate([one(C_NOPE), cos, cos, one(LANES - C_NOPE - C_ROPE)], axis=-1)
    sin_q = jnp.concatenate([zero(C_NOPE), -sin, sin, zero(LANES - C_NOPE - C_ROPE)], axis=-1)
    return (cos_a, sin_a), (cos_k, sin_k), (cos_q, sin_q)


_A_HEAD_ORDER = (0, 4, 1, 5, 2, 6, 3, 7)


def _perm_heads(w, axis):
    shape = w.shape
    grp = A_HEADS // A_KV_HEADS
    w = w.reshape(shape[:axis] + (A_KV_HEADS, grp, A_HEAD_DIM) + shape[axis + 1:])
    return jnp.swapaxes(w, axis, axis + 1).reshape(shape)


def _even_weights(w_in, w_out):
    offs = np.cumsum((0,) + E_COLS)
    parts = [w_in[:, offs[i]:offs[i + 1]] for i in range(len(E_COLS))]
    parts[2] = _perm_heads(parts[2], 1)
    z = parts[6]
    parts[6] = jnp.concatenate([_perm_heads(z[:, :A_WIDTH], 1), z[:, A_WIDTH:]], axis=1)
    w_out_p = jnp.concatenate([_perm_heads(w_out[:A_WIDTH], 0), w_out[A_WIDTH:]], axis=0)
    return jnp.concatenate(parts, axis=1).astype(BF16), w_out_p.astype(BF16)


def _odd_weights(w_in, w_uq, w_ukv):
    d = w_in.shape[0]
    offs = np.cumsum((0,) + O_COLS)
    ckv, kr, mk, mv, mg, cq, mq, mo, z = [w_in[:, offs[i]:offs[i + 1]] for i in range(len(O_COLS))]
    g2 = jnp.concatenate([kr, mg, jnp.zeros((d, LANES - C_ROPE - 4 * M_HEADS), w_in.dtype)], axis=1)
    w_p = jnp.concatenate([ckv, g2, mk, cq, mq, mo, z], axis=1).astype(BF16)
    wmvt = jnp.transpose(mv).astype(BF16)
    ukv = w_ukv.reshape(C_KV_LORA, C_HEADS, C_NOPE + C_VDIM)
    wk = jnp.pad(ukv[:, :, :C_NOPE], ((0, 0), (0, 0), (0, LANES - C_NOPE)))
    wk = wk.reshape(C_KV_LORA, C_HEADS * LANES).astype(BF16)
    wv = jnp.transpose(ukv[:, :, C_NOPE:].reshape(C_KV_LORA, C_WIDTH)).astype(BF16)
    uq = w_uq.reshape(C_Q_LORA, C_HEADS, C_NOPE + C_ROPE)
    wq = jnp.pad(uq, ((0, 0), (0, 0), (0, LANES - C_NOPE - C_ROPE))).reshape(C_Q_LORA, C_HEADS * LANES).astype(BF16)
    return w_p, wk, wv, wmvt, wq


def kernel(x, c, ctx, c_ctx, mod_w, mod_b, pre_norm_w, post_norm_w, e_w_in, e_sink, e_conv_w, e_w_out,
           o_w_in, o_q_norm_w, o_kv_norm_w, o_w_uq, o_w_ukv, o_i_bias, o_f_bias, o_head_norm_w, o_w_out):
    bsz, n_lat, d = x.shape
    n_ctx = ctx.shape[1]
    assert mod_w.shape[0] == 2 and d == D_MODEL, "built for one even + one odd layer"
    assert n_lat % 1024 == 0 and n_ctx % 256 == 0
    tm_in, tm_ctx = 1024, 256

    pad = (-(bsz + 1)) % 8
    cc = jnp.concatenate([c, c_ctx[None, :], jnp.zeros((pad, d), F32)], axis=0)
    mod = _modulation(cc, mod_w, mod_b)
    split = lambda l, r0, r1: [mod[l, r0:r1, k * d:(k + 1) * d][:, None, :] for k in range(3)]
    (rope_a, rope_k, rope_q) = _rope_tables(n_lat)
    dummy_tab = jnp.zeros((n_ctx, LANES), F32)

    sh_x, sc_x, g_x = split(0, 0, bsz)
    sh_c, sc_c, g_c = split(0, bsz, bsz + 1)
    w_in, w_out = _even_weights(e_w_in[0], e_w_out[0])
    nw, pw = pre_norm_w[0][None, :], post_norm_w[0][None, :]
    k, vt, q, bb, u, z = _even_in(x, sh_x, sc_x, nw, w_in, *rope_a, rope=True, tm=tm_in)
    k_c, vt_c, q_c, bb_c, u_c, z_c = _even_in(ctx, sh_c, sc_c, nw, w_in, dummy_tab, dummy_tab,
                                              rope=False, tm=tm_ctx)
    a = _even_attn(e_sink[0], q, k, vt, k_c, vt_c, window=True)
    a_c = _even_attn(e_sink[0], q_c, None, None, k_c, vt_c, window=False)
    x1 = _out_proj((a, bb, u), z, e_conv_w[0], w_out, pw, g_x, x, tm=tm_in)
    ctx1 = _out_proj((a_c, bb_c, u_c), z_c, e_conv_w[0], w_out, pw, g_c, ctx, tm=tm_ctx)

    sh_x, sc_x, g_x = split(1, 0, bsz)
    sh_c, sc_c, _ = split(1, bsz, bsz + 1)
    w_p, wk, wv, wmvt, wq = _odd_weights(o_w_in[0], o_w_uq[0], o_w_ukv[0])
    nw, pw = pre_norm_w[1][None, :], post_norm_w[1][None, :]
    kvn, qn = o_kv_norm_w[0][None, :], o_q_norm_w[0][None, :]
    chunk = min(MLSTM_CHUNK, n_ctx)
    kf, vt, mk, mvt, g2, qf, mq, mo, z = _odd_in(x1, sh_x, sc_x, nw, w_p, kvn, wk, wv, wmvt,
                                                 (*rope_k, qn, wq, *rope_q), tm=tm_in, chunk=chunk)
    kf_c, vt_c, mk_c, mvt_c, g2_c = _odd_in(ctx1, sh_c, sc_c, nw, w_p[:, :OC_CTX_END], kvn, wk, wv, wmvt, None,
                                            tm=tm_ctx, chunk=chunk)
    c_out = _mla(qf, kf_c, kf, vt_c, vt)
    gate_bias = jnp.stack([o_i_bias[0], o_f_bias[0]], axis=1).reshape(1, 4 * M_HEADS)
    bias_row = jnp.pad(gate_bias, ((0, 0), (GATE_LANE0, LANES - GATE_LANE0 - 4 * M_HEADS)))
    m_out = _mlstm(mq, mk, mvt, g2, mo, mk_c, mvt_c, g2_c, bias_row, o_head_norm_w[0][None, :])
    return _out_proj((c_out, m_out), z, None, o_w_out[0].astype(BF16), pw, g_x, x1, tm=tm_in)
```

```python
import collections
import functools

import numpy as np
import jax
import jax.numpy as jnp
from jax import lax
from jax.experimental import pallas as pl
from jax.experimental.pallas import tpu as pltpu

F32 = jnp.float32
BF16 = jnp.bfloat16
HIGHEST = lax.Precision.HIGHEST

LANES = 128
BF16_ROWS = 16
VMEM_LIMIT = 56 * 1024 * 1024

D_MODEL = 1024
GRID_W = 64
ROPE_THETA = 10000.0
NORM_EPS = 1e-6
NEG = -1e30
LOG2E = 1.4426950408889634

A_HEADS, A_KV_HEADS, A_HEAD_DIM = 8, 2, 64
A_WIDTH = A_HEADS * A_HEAD_DIM
WINDOW = 128
B_WIDTH = 512
C_HEADS, C_NOPE, C_ROPE, C_VDIM = 8, 64, 32, 64
C_KV_LORA, C_Q_LORA = 256, 768
C_WIDTH = C_HEADS * C_VDIM
M_HEADS, M_QK, M_V = 4, 64, 128
M_WIDTH = M_HEADS * M_V
E_GATE = A_WIDTH + B_WIDTH
O_GATE = C_WIDTH + M_WIDTH
E_COLS = (128, 128, A_WIDTH, B_WIDTH, B_WIDTH, B_WIDTH, E_GATE)
O_COLS = (C_KV_LORA, C_ROPE, M_HEADS * M_QK, M_WIDTH, 4 * M_HEADS, C_Q_LORA, M_HEADS * M_QK, M_WIDTH, O_GATE)

EC_K, EC_V, EC_Q, EC_BB, EC_BC, EC_BX, EC_Z, EC_END = 0, 128, 256, 768, 1280, 1792, 2304, 3328
OC_CKV = 0
OC_G2 = 256
OC_MK = 384
OC_CTX_END = 640
OC_CQ = 640
OC_MQ = 1408
OC_MO = 1664
OC_Z = 2176
M_QKW = M_HEADS * M_QK
M_ONES = BF16_ROWS
GATE_LANE0 = 32
SUB_ROWS = 256
MLSTM_CHUNK = 256
ATT_Q = 256
ATT_KEYS = 256
SCORE_SLOTS = 3
SCORE_ROWS = 128
ATT_VDIM = 64
MLA_BLK = 512
ATT_STEP = 1024


def _nt(a, b):
    return lax.dot_general(a, b, (((1,), (1,)), ((), ())), preferred_element_type=F32)


def _dot(a, b):
    return jnp.dot(a, b, preferred_element_type=F32)


def _silu(z):
    return z * jax.nn.sigmoid(z)


def _rms(x, w):
    ms = jnp.mean(x * x, axis=-1, keepdims=True)
    return x * lax.rsqrt(ms + NORM_EPS) * w


def _rope_slab(x, cos, sin_signed, half, take_up):
    up = pltpu.roll(x, LANES - half, axis=1)
    dn = pltpu.roll(x, half, axis=1)
    return x * cos + jnp.where(take_up, up, dn) * sin_signed


def _cparams(sem):
    return pltpu.CompilerParams(dimension_semantics=sem, vmem_limit_bytes=VMEM_LIMIT)


def _resident(shape):
    return pl.BlockSpec(shape, lambda b, i: (0,) * len(shape), pipeline_mode=pl.Buffered(1))


def _mod_kernel(cc_ref, w_ref, b_ref, o_ref):
    s = _silu(cc_ref[...])
    o_ref[...] = jnp.dot(s, w_ref[...], preferred_element_type=F32, precision=HIGHEST) + b_ref[...]


def _modulation(cc, mod_w, mod_b):
    depth, d, d3 = mod_w.shape
    r = cc.shape[0]
    return pl.pallas_call(
        _mod_kernel,
        grid=(depth, d3 // d),
        in_specs=[
            pl.BlockSpec((r, d), lambda l, j: (0, 0)),
            pl.BlockSpec((None, d, d), lambda l, j: (l, 0, j)),
            pl.BlockSpec((None, 1, d), lambda l, j: (l, 0, j)),
        ],
        out_specs=pl.BlockSpec((None, r, d), lambda l, j: (l, 0, j)),
        out_shape=jax.ShapeDtypeStruct((depth, r, d3), F32),
        compiler_params=_cparams(("arbitrary", "arbitrary")),
        name="modulation",
    )(cc, mod_w, mod_b.reshape(depth, 1, d3))


def _even_in_kernel(x_ref, sh_ref, sc_ref, nw_ref, w_ref, cos_ref, sin_ref,
                    k_ref, vt_ref, q_ref, bb_ref, u_ref, z_ref, *, rope):
    lane = lax.broadcasted_iota(jnp.int32, (1, LANES), 1)
    take_up = (lane % A_HEAD_DIM) < (A_HEAD_DIM // 2)
    for r0 in range(0, x_ref.shape[0], SUB_ROWS):
        rows = slice(r0, r0 + SUB_ROWS)
        h = (_rms(x_ref[rows, :], nw_ref[...]) * (1.0 + sc_ref[...]) + sh_ref[...]).astype(BF16)
        y_all = _dot(h, w_ref[...])

        def proj(a, b):
            return y_all[:, a:b]

        if rope:
            cos, sin = cos_ref[rows, :], sin_ref[rows, :]
            rot = lambda t: _rope_slab(t, cos, sin, A_HEAD_DIM // 2, take_up)
        else:
            rot = lambda t: t
        k_ref[rows, :] = rot(proj(EC_K, EC_V)).astype(BF16)
        vt_ref[:, rows] = jnp.transpose(proj(EC_V, EC_Q)).astype(BF16)
        q = proj(EC_Q, EC_BB)
        for j in range(A_WIDTH // LANES):
            sl = slice(j * LANES, (j + 1) * LANES)
            q_ref[rows, sl] = (rot(q[:, sl]) * (A_HEAD_DIM ** -0.5 * LOG2E)).astype(BF16)
        bb_ref[rows, :] = proj(EC_BB, EC_BC).astype(BF16)
        u_ref[rows, :] = (proj(EC_BC, EC_BX) * proj(EC_BX, EC_Z)).astype(BF16)
        z_ref[rows, :] = _silu(proj(EC_Z, EC_END)).astype(BF16)


def _even_in(x, sh, sc, nw, w, cos, sin, *, rope, tm):
    bsz, t, d = x.shape
    nb = sh.shape[0]
    mod_map = (lambda b, i: (b, 0, 0)) if nb > 1 else (lambda b, i: (0, 0, 0))
    tok = lambda width: pl.BlockSpec((None, tm, width), lambda b, i: (b, i, 0))
    const = lambda shape: pl.BlockSpec(shape, lambda b, i: (0,) * len(shape))
    widths = (LANES, None, A_WIDTH, B_WIDTH, B_WIDTH, E_GATE)
    vt_spec = pl.BlockSpec((None, LANES, tm), lambda b, i: (b, 0, i))
    vt_shape = jax.ShapeDtypeStruct((bsz, LANES, t), BF16)
    return pl.pallas_call(
        functools.partial(_even_in_kernel, rope=rope),
        grid=(bsz, t // tm),
        in_specs=[
            tok(d),
            pl.BlockSpec((None, 1, d), mod_map),
            pl.BlockSpec((None, 1, d), mod_map),
            const((1, d)),
            _resident(w.shape),
            pl.BlockSpec((tm, LANES), lambda b, i: (i, 0)),
            pl.BlockSpec((tm, LANES), lambda b, i: (i, 0)),
        ],
        out_specs=[vt_spec if wd is None else tok(wd) for wd in widths],
        out_shape=[vt_shape if wd is None else jax.ShapeDtypeStruct((bsz, t, wd), BF16) for wd in widths],
        compiler_params=_cparams(("parallel", "parallel")),
        name="even_in_rope" if rope else "even_in",
    )(x, sh, sc, nw, w, cos, sin)


AttnStage = collections.namedtuple("AttnStage", "q k vt mask sink")


def _tree_max(parts):
    while len(parts) > 1:
        parts = [jnp.maximum(parts[i], parts[i + 1]) if i + 1 < len(parts) else parts[i]
                 for i in range(0, len(parts), 2)]
    return parts[0]


def _attention_pipeline(stages, n_chunks, s_ref, write_pair):
    kc, sr = ATT_KEYS, SCORE_ROWS
    slots, n_keys = s_ref.shape[0], n_chunks * kc
    lag = slots - 1
    ones = jnp.ones((BF16_ROWS, n_keys), BF16)
    m8 = {}
    halves = []
    for i in range(len(stages) + lag):
        if i < len(stages):
            cur = stages[i]
            q = cur.q()
            parts = []
            for c in range(n_chunks):
                kch, mask = cur.k(c), cur.mask(c)
                for r in range(0, kc, sr):
                    st = _nt(kch[r:r + sr], q)
                    if mask is not None:
                        st = jnp.where(mask[r:r + sr], st, NEG)
                    s_ref[i % slots, c * kc + r:c * kc + r + sr, :] = st
                    parts += [st[t:t + 8, :] for t in range(0, sr, 8)]
            m8[i] = _tree_max(parts)
        if i >= lag:
            prev = stages[i - lag]
            m_prev = jnp.max(m8.pop(i - lag), axis=0, keepdims=True)
            if prev.sink is not None:
                m_prev = jnp.maximum(m_prev, prev.sink)
            pt = jnp.exp2(s_ref[(i - lag) % slots, 0:n_keys, :] - m_prev).astype(BF16)
            vta = jnp.concatenate([prev.vt(c) for c in range(n_chunks)], axis=1)
            acc = _dot(jnp.concatenate([vta, ones], axis=0), pt)
            denom = acc[ATT_VDIM:ATT_VDIM + 1]
            if prev.sink is not None:
                denom = denom + jnp.exp2(prev.sink - m_prev)
            halves.append(acc[0:ATT_VDIM] / denom)
            if len(halves) == 2:
                write_pair((i - lag) // 2, jnp.concatenate(halves, axis=0))
                halves = []


def _even_attn_kernel(*refs, window, n_lat):
    if window:
        (sink_ref, q_ref, z_ref, kp_ref, km_ref, kn_ref, kc_ref, vtp_ref, vtm_ref, vtn_ref, vtc_ref,
         o_ref, s_ref) = refs
    else:
        sink_ref, q_ref, z_ref, kc_ref, vtc_ref, o_ref, s_ref = refs
    nq, kc = ATT_Q, ATT_KEYS
    step, half = q_ref.shape[0], WINDOW
    lane = lax.broadcasted_iota(jnp.int32, (1, LANES), 1)
    lo = lane < A_HEAD_DIM
    ctx_k = [kc_ref[r:r + kc, :] for r in range(0, kc_ref.shape[0], kc)]
    ctx_vt = [vtc_ref[:, r:r + kc] for r in range(0, kc_ref.shape[0], kc)]
    r = lax.broadcasted_iota(jnp.int32, (kc, nq), 0)
    c = lax.broadcasted_iota(jnp.int32, (kc, nq), 1)

    def sub_block(qb):
        k_chunks, vt_chunks, masks = list(ctx_k), list(ctx_vt), [None] * len(ctx_k)
        if window:
            base, q0 = qb * nq, pl.program_id(1) * step + qb * nq
            if qb == 0:
                k1 = jnp.concatenate([kp_ref[...], km_ref[0:half, :]], axis=0)
                vt1 = jnp.concatenate([vtp_ref[...], vtm_ref[:, 0:half]], axis=1)
            else:
                k1, vt1 = km_ref[base - half:base + half, :], vtm_ref[:, base - half:base + half]
            if base + nq == step:
                k2 = jnp.concatenate([km_ref[base + half:step, :], kn_ref[...]], axis=0)
                vt2 = jnp.concatenate([vtm_ref[:, base + half:step], vtn_ref[...]], axis=1)
            else:
                k2, vt2 = km_ref[base + half:base + half + kc, :], vtm_ref[:, base + half:base + half + kc]
            k_chunks += [k1, k2]
            vt_chunks += [vt1, vt2]
            masks += [(jnp.abs(r - half - c) <= WINDOW) & (r + (q0 - half) >= 0),
                      (jnp.abs(r + half - c) <= WINDOW) & (r + (q0 + half) < n_lat)]
        return k_chunks, vt_chunks, masks

    def stage(qb, head, chunks):
        k_chunks, vt_chunks, masks = chunks
        j, grp = head % (A_HEADS // 2), head // (A_HEADS // 2)
        keep = lo if grp == 0 else jnp.logical_not(lo)

        def q():
            qs = q_ref[qb * nq:(qb + 1) * nq, j * LANES:(j + 1) * LANES]
            return jnp.where(keep, qs, jnp.zeros_like(qs))

        return AttnStage(q=q, k=lambda ci: k_chunks[ci],
                         vt=lambda ci: vt_chunks[ci][grp * A_HEAD_DIM:(grp + 1) * A_HEAD_DIM, :],
                         mask=lambda ci: masks[ci], sink=sink_ref[head] * LOG2E)

    def write_pair(p, o_t):
        qb, j = divmod(p, A_HEADS // 2)
        blk = (slice(qb * nq, (qb + 1) * nq), slice(j * LANES, (j + 1) * LANES))
        o_ref[blk] = jnp.transpose(o_t).astype(BF16) * z_ref[blk]

    stages = []
    for qb in range(step // nq):
        chunks = sub_block(qb)
        stages += [stage(qb, h, chunks) for h in _A_HEAD_ORDER]
    _attention_pipeline(stages, len(ctx_k) + (2 if window else 0), s_ref, write_pair)


def _even_attn(sink, q, z, k, vt, kc, vtc, *, window):
    bsz, t, _ = q.shape
    n_ctx = kc.shape[1]
    nq, keys, half = ATT_Q, ATT_KEYS, WINDOW
    step = min(ATT_STEP, t)
    assert nq == keys == 2 * half and t % step == 0 and n_ctx % keys == 0
    per_step, last = step // half, t // half - 1
    smem = pl.BlockSpec(memory_space=pltpu.SMEM)
    qspec = pl.BlockSpec((None, step, A_WIDTH), lambda b, i: (b, i, 0))
    kcspec = pl.BlockSpec((None, n_ctx, LANES), lambda b, i: (b, 0, 0))
    vtcspec = pl.BlockSpec((None, LANES, n_ctx), lambda b, i: (b, 0, 0))
    if window:
        prev = lambda i: jnp.maximum(per_step * i - 1, 0)
        nxt = lambda i: jnp.minimum(per_step * (i + 1), last)
        specs = [smem, qspec, qspec,
                 pl.BlockSpec((None, half, LANES), lambda b, i: (b, prev(i), 0)),
                 pl.BlockSpec((None, step, LANES), lambda b, i: (b, i, 0)),
                 pl.BlockSpec((None, half, LANES), lambda b, i: (b, nxt(i), 0)),
                 kcspec,
                 pl.BlockSpec((None, LANES, half), lambda b, i: (b, 0, prev(i))),
                 pl.BlockSpec((None, LANES, step), lambda b, i: (b, 0, i)),
                 pl.BlockSpec((None, LANES, half), lambda b, i: (b, 0, nxt(i))),
                 vtcspec]
        args = (sink, q, z, k, k, k, kc, vt, vt, vt, vtc)
    else:
        specs = [smem, qspec, qspec, kcspec, vtcspec]
        args = (sink, q, z, kc, vtc)
    n_chunks = n_ctx // keys + (2 if window else 0)
    return pl.pallas_call(
        functools.partial(_even_attn_kernel, window=window, n_lat=t),
        grid=(bsz, t // step),
        in_specs=specs,
        out_specs=qspec,
        out_shape=jax.ShapeDtypeStruct((bsz, t, A_WIDTH), BF16),
        scratch_shapes=[pltpu.VMEM((SCORE_SLOTS, n_chunks * keys, nq), F32)],
        compiler_params=_cparams(("parallel", "parallel")),
        name="even_attn_window" if window else "even_attn_ctx",
    )(*args)


def _out_kernel(*refs, conv):
    if conv:
        a_ref, bb_ref, u_ref, up_ref, un_ref, z_ref, cw_ref, w_ref, pw_ref, g_ref, x_ref, o_ref = refs
        tm = u_ref.shape[0]
        i = pl.program_id(1)
        u = u_ref[...].astype(F32)
        prev_row = jnp.where(i > 0, up_ref[BF16_ROWS - 1:BF16_ROWS, :].astype(F32), 0.0)
        next_row = jnp.where(i < pl.num_programs(1) - 1, un_ref[0:1, :].astype(F32), 0.0)
        row = lax.broadcasted_iota(jnp.int32, (tm, 1), 0)
        u_m1 = jnp.where(row == 0, prev_row, pltpu.roll(u, 1, axis=0))
        u_p1 = jnp.where(row == tm - 1, next_row, pltpu.roll(u, tm - 1, axis=0))
        cw = cw_ref[...]
        mix_b = (bb_ref[...].astype(F32) * (u_m1 * cw[0:1, :] + u * cw[1:2, :] + u_p1 * cw[2:3, :])).astype(BF16)
        mix_b = mix_b * z_ref[...]
    else:
        a_ref, b_ref, w_ref, pw_ref, g_ref, x_ref, o_ref = refs
        mix_b = b_ref
    for r0 in range(0, x_ref.shape[0], SUB_ROWS):
        rows = slice(r0, r0 + SUB_ROWS)
        gated = jnp.concatenate([a_ref[rows, :], mix_b[rows, :]], axis=1)
        o_ref[rows, :] = x_ref[rows, :] + g_ref[...] * _rms(_dot(gated, w_ref[...]), pw_ref[...])


def _out_proj(mix, z, cw, w, pw, g, x, *, tm):
    conv = cw is not None
    bsz, t, d = x.shape
    nb = g.shape[0]
    mod_map = (lambda b, i: (b, 0, 0)) if nb > 1 else (lambda b, i: (0, 0, 0))
    tok = lambda width: pl.BlockSpec((None, tm, width), lambda b, i: (b, i, 0))
    const = lambda shape: pl.BlockSpec(shape, lambda b, i: (0,) * len(shape))
    hb = BF16_ROWS
    per_tile, last = tm // hb, t // hb - 1
    if conv:
        a, bb, u = mix
        specs = [tok(a.shape[2]), tok(bb.shape[2]), tok(u.shape[2]),
                 pl.BlockSpec((None, hb, u.shape[2]), lambda b, i: (b, jnp.maximum(i * per_tile - 1, 0), 0)),
                 pl.BlockSpec((None, hb, u.shape[2]), lambda b, i: (b, jnp.minimum((i + 1) * per_tile, last), 0)),
                 pl.BlockSpec((None, tm, bb.shape[2]), lambda b, i: (b, i, z.shape[2] // bb.shape[2] - 1)),
                 const(cw.shape)]
        args = (a, bb, u, u, u, z, cw)
    else:
        a, b2 = mix
        specs = [tok(a.shape[2]), tok(b2.shape[2])]
        args = (a, b2)
    specs += [_resident(w.shape), const((1, d)), pl.BlockSpec((None, 1, d), mod_map), tok(d)]
    return pl.pallas_call(
        functools.partial(_out_kernel, conv=conv),
        grid=(bsz, t // tm),
        in_specs=specs,
        out_specs=tok(d),
        out_shape=jax.ShapeDtypeStruct((bsz, t, d), F32),
        compiler_params=_cparams(("parallel", "parallel")),
        name="out_proj_conv" if conv else "out_proj",
    )(*args, w, pw, g, x)


def _odd_in_kernel(*refs, lat):
    if lat:
        (x_ref, sh_ref, sc_ref, nw_ref, w_ref, kvn_ref, wk_ref, wvt_ref, wmvt_ref, cosk_ref, sink_ref,
         qn_ref, wq_ref, cosq_ref, sinq_ref,
         kf_ref, vt_ref, mk_ref, mvt_ref, g2_ref, qf_ref, mq_ref, mo_ref, z_ref) = refs
    else:
        (x_ref, sh_ref, sc_ref, nw_ref, w_ref, kvn_ref, wk_ref, wvt_ref, wmvt_ref,
         kf_ref, vt_ref, mk_ref, mvt_ref, g2_ref) = refs
    lane = lax.broadcasted_iota(jnp.int32, (1, LANES), 1)
    chunk = mvt_ref.shape[2]
    per_sub = SUB_ROWS // chunk
    for cc in range(x_ref.shape[0] // SUB_ROWS):
        rows = slice(cc * SUB_ROWS, (cc + 1) * SUB_ROWS)
        h = (_rms(x_ref[rows, :], nw_ref[...]) * (1.0 + sc_ref[...]) + sh_ref[...]).astype(BF16)
        y_all = _dot(h, w_ref[...])

        def proj(a, b):
            return y_all[:, a:b]

        ckv = _rms(proj(OC_CKV, OC_CKV + C_KV_LORA), kvn_ref[...]).astype(BF16)
        g2 = proj(OC_G2, OC_G2 + LANES)
        g2_ref[rows, :] = g2
        if lat:
            g2 = _rope_slab(g2, cosk_ref[rows, :], sink_ref[rows, :], C_ROPE // 2, lane < C_ROPE // 2)
        rope_lanes = (lane >= C_NOPE) & (lane < C_NOPE + C_ROPE)
        kr = jnp.where(rope_lanes, pltpu.roll(g2, C_NOPE, axis=1), 0.0)
        k_nope = _dot(ckv, wk_ref[...])
        for hd in range(C_HEADS):
            sl = slice(hd * LANES, (hd + 1) * LANES)
            kf_ref[rows, sl] = (k_nope[:, sl] + kr).astype(BF16)
        vt_ref[:, rows] = _nt(wvt_ref[...], ckv).astype(BF16)
        mk_ref[rows, :] = (proj(OC_MK, OC_MK + M_QKW) * (M_QK ** -0.5)).astype(BF16)
        mvt = _nt(wmvt_ref[...], h).astype(BF16)
        for s in range(per_sub):
            mvt_ref[cc * per_sub + s] = mvt[:, s * chunk:(s + 1) * chunk]
        if lat:
            cq = _rms(proj(OC_CQ, OC_CQ + C_Q_LORA), qn_ref[...]).astype(BF16)
            qf = _dot(cq, wq_ref[...])
            cos, sin = cosq_ref[rows, :], sinq_ref[rows, :]
            take_up = (lane >= C_NOPE) & (lane < C_NOPE + C_ROPE // 2)
            scale = (C_NOPE + C_ROPE) ** -0.5 * LOG2E
            for hd in range(C_HEADS):
                sl = slice(hd * LANES, (hd + 1) * LANES)
                qf_ref[rows, sl] = (_rope_slab(qf[:, sl], cos, sin, C_ROPE // 2, take_up) * scale).astype(BF16)
            mq_ref[rows, :] = proj(OC_MQ, OC_MQ + M_QKW).astype(BF16)
            mo_ref[rows, :] = jax.nn.sigmoid(proj(OC_MO, OC_MO + M_WIDTH)).astype(BF16)
            z_ref[rows, :] = _silu(proj(OC_Z, OC_Z + O_GATE)).astype(BF16)


def _odd_in(x, sh, sc, nw, w, kvn, wk, wv, wmvt, lat_args, *, tm, chunk):
    lat = lat_args is not None
    bsz, t, d = x.shape
    nb = sh.shape[0]
    mod_map = (lambda b, i: (b, 0, 0)) if nb > 1 else (lambda b, i: (0, 0, 0))
    tok = lambda width: (pl.BlockSpec((None, tm, width), lambda b, i: (b, i, 0)), (bsz, t, width))
    const = lambda shape: pl.BlockSpec(shape, lambda b, i: (0,) * len(shape))
    tab = pl.BlockSpec((tm, LANES), lambda b, i: (i, 0))
    specs = [tok(d)[0], pl.BlockSpec((None, 1, d), mod_map), pl.BlockSpec((None, 1, d), mod_map),
             const((1, d)), _resident(w.shape), const(kvn.shape), _resident(wk.shape), _resident(wv.shape),
             _resident(wmvt.shape)]
    args = [x, sh, sc, nw, w, kvn, wk, wv, wmvt]
    vt = (pl.BlockSpec((None, C_WIDTH, tm), lambda b, i: (b, 0, i)), (bsz, C_WIDTH, t))
    mvt = (pl.BlockSpec((None, tm // chunk, M_WIDTH, chunk), lambda b, i: (b, i, 0, 0)),
           (bsz, t // chunk, M_WIDTH, chunk))
    outs = [(tok(C_HEADS * LANES), BF16), (vt, BF16), (tok(M_QKW), BF16), (mvt, BF16), (tok(LANES), F32)]
    if lat:
        cosk, sink, qn, wq, cosq, sinq = lat_args
        specs += [tab, tab, const(qn.shape), _resident(wq.shape), tab, tab]
        args += [cosk, sink, qn, wq, cosq, sinq]
        outs += [(tok(C_HEADS * LANES), BF16), (tok(M_QKW), BF16), (tok(M_WIDTH), BF16), (tok(O_GATE), BF16)]
    return pl.pallas_call(
        functools.partial(_odd_in_kernel, lat=lat),
        grid=(bsz, t // tm),
        in_specs=specs,
        out_specs=[o[0][0] for o in outs],
        out_shape=[jax.ShapeDtypeStruct(o[0][1], o[1]) for o in outs],
        compiler_params=_cparams(("parallel", "parallel")),
        name="odd_in_lat" if lat else "odd_in_ctx",
    )(*args)


def _mla_kernel(q_ref, z_ref, kc_ref, vtc_ref, *rest):
    n_pairs = C_HEADS // 2
    kx_refs, vtx_refs, (o_ref, s_ref) = rest[:n_pairs], rest[n_pairs:2 * n_pairs], rest[2 * n_pairs:]
    nq, kc = ATT_Q, ATT_KEYS
    n_ctx_chunks = kc_ref.shape[0] // kc
    n_chunks = n_ctx_chunks + kx_refs[0].shape[0] // kc

    def stage(qb, hd):
        sl = slice(hd * LANES, (hd + 1) * LANES)
        half = hd % 2

        def k(ci):
            if ci < n_ctx_chunks:
                return kc_ref[ci * kc:(ci + 1) * kc, sl]
            r0 = (ci - n_ctx_chunks) * kc
            return kx_refs[hd // 2][r0:r0 + kc, half * LANES:(half + 1) * LANES]

        def vt(ci):
            if ci < n_ctx_chunks:
                return vtc_ref[hd * C_VDIM:(hd + 1) * C_VDIM, ci * kc:(ci + 1) * kc]
            r0 = (ci - n_ctx_chunks) * kc
            return vtx_refs[hd // 2][half * C_VDIM:(half + 1) * C_VDIM, r0:r0 + kc]

        return AttnStage(q=lambda: q_ref[qb * nq:(qb + 1) * nq, sl], k=k, vt=vt, mask=lambda ci: None, sink=None)

    def write_pair(p, o_t):
        qb, j = divmod(p, C_HEADS // 2)
        blk = (slice(qb * nq, (qb + 1) * nq), slice(j * LANES, (j + 1) * LANES))
        o_ref[blk] = jnp.transpose(o_t).astype(BF16) * z_ref[blk]

    stages = [stage(qb, hd) for qb in range(q_ref.shape[0] // nq) for hd in range(C_HEADS)]
    _attention_pipeline(stages, n_chunks, s_ref, write_pair)


def _mla(qf, z, kfc, kfx, vtc, vtx):
    bsz, t, wq = qf.shape
    n_ctx = kfc.shape[1]
    blk = MLA_BLK
    n_pairs = C_HEADS // 2
    pair_k = lambda j: pl.BlockSpec((None, t, 2 * LANES), lambda b, i: (b, 0, j))
    pair_vt = lambda j: pl.BlockSpec((None, 2 * C_VDIM, t), lambda b, i: (b, j, 0))
    return pl.pallas_call(
        _mla_kernel,
        grid=(bsz, t // blk),
        in_specs=[
            pl.BlockSpec((None, blk, wq), lambda b, i: (b, i, 0)),
            pl.BlockSpec((None, blk, C_WIDTH), lambda b, i: (b, i, 0)),
            pl.BlockSpec((None, n_ctx, wq), lambda b, i: (b, 0, 0)),
            pl.BlockSpec((None, C_WIDTH, n_ctx), lambda b, i: (b, 0, 0)),
        ] + [pair_k(j) for j in range(n_pairs)] + [pair_vt(j) for j in range(n_pairs)],
        out_specs=pl.BlockSpec((None, blk, C_WIDTH), lambda b, i: (b, i, 0)),
        out_shape=jax.ShapeDtypeStruct((bsz, t, C_WIDTH), BF16),
        scratch_shapes=[pltpu.VMEM((SCORE_SLOTS, n_ctx + t, ATT_Q), F32)],
        compiler_params=_cparams(("parallel", "arbitrary")),
        name="mla_attention",
    )(qf, z, kfc, vtc, *([kfx] * n_pairs), *([vtx] * n_pairs))


def _mlstm_kernel(mq_ref, mk_ref, mvt_ref, g_ref, mo_ref, z_ref, mkc_ref, mvtc_ref, gc_ref, bias_ref, hnw_ref,
                  out_ref, s_ref, m_ref, hf_ref, hr_ref):
    nc, ncc, L = mvt_ref.shape[0], mvtc_ref.shape[0], mvt_ref.shape[2]
    row = lax.broadcasted_iota(jnp.int32, (L, L), 0)
    col = lax.broadcasted_iota(jnp.int32, (L, L), 1)
    vis = (row <= col, row >= col)
    tri_ones = (col <= row).astype(BF16)
    lane = lax.broadcasted_iota(jnp.int32, (1, LANES), 1)
    lo = lane < M_QK
    ones_rows = jnp.ones((M_ONES, L), BF16)
    fwd_lanes = lane < GATE_LANE0 + 2 * M_HEADS
    tok16 = lax.broadcasted_iota(jnp.int32, (4 * M_HEADS, L), 1)
    fwd_rows16 = lax.broadcasted_iota(jnp.int32, (4 * M_HEADS, L), 0) < 2 * M_HEADS
    pair = lambda hd: slice((hd // 2) * LANES, (hd // 2 + 1) * LANES)
    vrows = lambda hd: slice(hd * M_V, (hd + 1) * M_V)

    def gates(g, with_max=True):
        gb = g + bias_ref[...]
        ls = jnp.minimum(gb, 0.0) - jnp.log1p(jnp.exp(-jnp.abs(gb)))
        hi = ls.astype(BF16)
        lo = (ls - hi.astype(F32)).astype(BF16)
        cf = _dot(tri_ones, hi) + _dot(tri_ones, lo)
        tot = cf[L - 1:L, :]
        b = jnp.where(fwd_lanes, cf, tot - cf + ls)
        li = pltpu.roll(gb, M_HEADS, axis=1)
        gate_rows = slice(GATE_LANE0, GATE_LANE0 + 4 * M_HEADS)
        b_t, li_t = jnp.transpose(b)[gate_rows], jnp.transpose(li)[gate_rows]
        cm = li_t - b_t
        sh = 1 if with_max else L
        while sh < L:
            earlier = jnp.where(tok16 >= sh, pltpu.roll(cm, sh, axis=1), NEG)
            later = jnp.where(tok16 < L - sh, pltpu.roll(cm, L - sh, axis=1), NEG)
            cm = jnp.maximum(cm, jnp.where(fwd_rows16, earlier, later))
            sh *= 2
        return b - li, tot, b_t, li_t, cm

    def head_lanes(hd, pair_slab):
        keep = lo if hd % 2 == 0 else jnp.logical_not(lo)
        return jnp.where(keep, pair_slab, jnp.zeros_like(pair_slab))

    def scores(u, kh, qh):
        return _nt(jnp.concatenate([kh, s_ref[u].astype(BF16)], axis=0), qh)

    def finish(d, hd, gq, kh, vt_h, big):
        r, tot, b_t, li_t, cm = gq
        u = d * M_HEADS + hd
        fl = GATE_LANE0 + d * 2 * M_HEADS + M_HEADS + hd
        gr = slice(fl - GATE_LANE0, fl - GATE_LANE0 + 1)
        b_row, li_row, cm_row = b_t[gr, :], li_t[gr, :], cm[gr, :]
        g, r_col = tot[:, fl:fl + 1], r[:, fl:fl + 1]
        vaug = jnp.concatenate([vt_h, ones_rows], axis=0)
        s_in = s_ref[u]
        m_in = m_ref[u][0:1, 0:1]
        h_t = None
        if big is not None:
            inter = b_row + m_in
            m_t = jnp.maximum(inter, b_row + cm_row)
            sc = big[0:L] * jnp.exp(jnp.where(vis[d], (b_row - m_t) - r_col, NEG))
            res = jnp.exp(inter - m_t) * big[L:] + _dot(vaug, sc.astype(BF16))
            den = jnp.maximum(jnp.abs(res[M_V:M_V + 1]), jnp.exp(-m_t))
            h_t = res[0:M_V] / den
        a_row = g - b_row + li_row
        m_loc = jnp.max(a_row, axis=1, keepdims=True)
        wv = (jnp.exp(a_row - m_loc) * vaug.astype(F32)).astype(BF16)
        s_loc = _dot(wv, kh)
        m_new = jnp.maximum(g + m_in, m_loc)
        s_ref[u] = jnp.exp(g + m_in - m_new) * s_in + jnp.exp(m_loc - m_new) * s_loc
        m_ref[u] = jnp.broadcast_to(m_new, (8, LANES))
        return h_t

    def chunk_gates(jj):
        return gates(g_ref[pl.ds(pl.multiple_of(jj * L, L), L), :])

    first_gates = (chunk_gates(0), chunk_gates(nc - 1))
    s_ref[...] = jnp.zeros_like(s_ref)
    m_ref[...] = jnp.zeros_like(m_ref)
    ctx_gates = {}
    for d in range(2):
        for cc in range(ncc):
            jj = cc if d == 0 else ncc - 1 - cc
            if jj not in ctx_gates:
                ctx_gates[jj] = gates(gc_ref[jj * L:(jj + 1) * L, :], with_max=False)
            gq = ctx_gates[jj]
            for hd in range(M_HEADS):
                kh = head_lanes(hd, mkc_ref[jj * L:(jj + 1) * L, pair(hd)])
                finish(d, hd, gq, kh, mvtc_ref[jj, vrows(hd), :], None)

    def body(j, gq):
        chunk_of = (j, nc - 1 - j)
        rows = [pl.ds(pl.multiple_of(jj * L, L), L) for jj in chunk_of]
        gq_next = (chunk_gates(jnp.minimum(j + 1, nc - 1)), chunk_gates(jnp.maximum(nc - 2 - j, 0)))
        pending = None
        for d in range(2):
            for hd in range(M_HEADS):
                kh = head_lanes(hd, mk_ref[rows[d], pair(hd)])
                big = scores(d * M_HEADS + hd, kh, head_lanes(hd, mq_ref[rows[d], pair(hd)]))
                if pending is not None:
                    pending()
                dst = hf_ref if d == 0 else hr_ref

                def pending(d=d, hd=hd, kh=kh, big=big, dst=dst):
                    dst[chunk_of[d], hd] = finish(d, hd, gq[d], kh, mvt_ref[chunk_of[d], vrows(hd), :], big)
        pending()
        return gq_next

    lax.fori_loop(0, nc, body, first_gates)

    def readout(j, carry):
        rows = pl.ds(pl.multiple_of(j * L, L), L)
        for hd in range(M_HEADS):
            hs_t = hf_ref[j, hd] + hr_ref[j, hd]
            ms = jnp.mean(hs_t * hs_t, axis=0, keepdims=True)
            hs = jnp.transpose(hs_t * lax.rsqrt(ms + NORM_EPS)) * hnw_ref[:, vrows(hd)]
            m_out = (mo_ref[rows, vrows(hd)].astype(F32) * hs).astype(BF16)
            out_ref[rows, vrows(hd)] = m_out * z_ref[rows, vrows(hd)]
        return carry

    lax.fori_loop(0, nc, readout, 0, unroll=4)


def _mlstm(mq, mk, mvt, g2, mo, z, mkc, mvtc, g2c, bias_row, hnw):
    bsz, t, _ = mq.shape
    n_ctx = mkc.shape[1]
    nc, _, chunk = mvt.shape[1:]
    ncc = mvtc.shape[1]
    per_b = lambda rows, width: pl.BlockSpec((None, rows, width), lambda b: (b, 0, 0))
    chunked = lambda n: pl.BlockSpec((None, n, M_WIDTH, chunk), lambda b: (b, 0, 0, 0))
    const = lambda shape: pl.BlockSpec(shape, lambda b: (0,) * len(shape))
    return pl.pallas_call(
        _mlstm_kernel,
        grid=(bsz,),
        in_specs=[per_b(t, M_QKW), per_b(t, M_QKW), chunked(nc), per_b(t, LANES), per_b(t, M_WIDTH),
                  pl.BlockSpec((None, t, M_WIDTH), lambda b: (b, 0, z.shape[2] // M_WIDTH - 1)),
                  per_b(n_ctx, M_QKW), chunked(ncc), per_b(n_ctx, LANES),
                  const((1, LANES)), const((1, M_WIDTH))],
        out_specs=per_b(t, M_WIDTH),
        out_shape=jax.ShapeDtypeStruct((bsz, t, M_WIDTH), BF16),
        scratch_shapes=[pltpu.VMEM((2 * M_HEADS, M_V + M_ONES, LANES), F32),
                        pltpu.VMEM((2 * M_HEADS, 8, LANES), F32),
                        pltpu.VMEM((nc, M_HEADS, M_V, chunk), F32),
                        pltpu.VMEM((nc, M_HEADS, M_V, chunk), F32)],
        compiler_params=_cparams(("parallel",)),
        name="mlstm",
    )(mq, mk, mvt, g2, mo, z, mkc, mvtc, g2c, bias_row, hnw)


def _axial_angles(n_lat, rot_dim):
    rows = n_lat // GRID_W
    row = jnp.repeat(jnp.arange(rows), GRID_W).astype(F32)
    col = jnp.tile(jnp.arange(GRID_W), rows).astype(F32)
    n_freq = rot_dim // 4
    inv = ROPE_THETA ** (-jnp.arange(n_freq, dtype=F32) / n_freq)
    ang_t = jnp.concatenate([inv[:, None] * row[None, :], inv[:, None] * col[None, :]], axis=0)
    return jnp.transpose(jnp.cos(ang_t)), jnp.transpose(jnp.sin(ang_t))


def _rope_tables(n_lat):
    cos, sin = _axial_angles(n_lat, A_HEAD_DIM)
    cos_a = jnp.tile(jnp.concatenate([cos, cos], axis=-1), (1, 2))
    sin_a = jnp.tile(jnp.concatenate([-sin, sin], axis=-1), (1, 2))
    cos, sin = _axial_angles(n_lat, C_ROPE)
    one = lambda w: jnp.ones((n_lat, w), F32)
    zero = lambda w: jnp.zeros((n_lat, w), F32)
    cos_k = jnp.concatenate([cos, cos, one(LANES - C_ROPE)], axis=-1)
    sin_k = jnp.concatenate([-sin, sin, zero(LANES - C_ROPE)], axis=-1)
    cos_q = jnp.concatenate([one(C_NOPE), cos, cos, one(LANES - C_NOPE - C_ROPE)], axis=-1)
    sin_q = jnp.concatenate([zero(C_NOPE), -sin, sin, zero(LANES - C_NOPE - C_ROPE)], axis=-1)
    return (cos_a, sin_a), (cos_k, sin_k), (cos_q, sin_q)


_A_HEAD_ORDER = (0, 4, 1, 5, 2, 6, 3, 7)


def _perm_heads(w, axis):
    shape = w.shape
    grp = A_HEADS // A_KV_HEADS
    w = w.reshape(shape[:axis] + (A_KV_HEADS, grp, A_HEAD_DIM) + shape[axis + 1:])
    return jnp.swapaxes(w, axis, axis + 1).reshape(shape)


def _even_weights(w_in, w_out):
    offs = np.cumsum((0,) + E_COLS)
    parts = [w_in[:, offs[i]:offs[i + 1]] for i in range(len(E_COLS))]
    parts[2] = _perm_heads(parts[2], 1)
    z = parts[6]
    parts[6] = jnp.concatenate([_perm_heads(z[:, :A_WIDTH], 1), z[:, A_WIDTH:]], axis=1)
    w_out_p = jnp.concatenate([_perm_heads(w_out[:A_WIDTH], 0), w_out[A_WIDTH:]], axis=0)
    return jnp.concatenate(parts, axis=1).astype(BF16), w_out_p.astype(BF16)


def _odd_weights(w_in, w_uq, w_ukv):
    d = w_in.shape[0]
    offs = np.cumsum((0,) + O_COLS)
    ckv, kr, mk, mv, mg, cq, mq, mo, z = [w_in[:, offs[i]:offs[i + 1]] for i in range(len(O_COLS))]
    g2 = jnp.concatenate([kr, mg, jnp.zeros((d, LANES - C_ROPE - 4 * M_HEADS), w_in.dtype)], axis=1)
    w_p = jnp.concatenate([ckv, g2, mk, cq, mq, mo, z], axis=1).astype(BF16)
    wmvt = jnp.transpose(mv).astype(BF16)
    ukv = w_ukv.reshape(C_KV_LORA, C_HEADS, C_NOPE + C_VDIM)
    wk = jnp.pad(ukv[:, :, :C_NOPE], ((0, 0), (0, 0), (0, LANES - C_NOPE)))
    wk = wk.reshape(C_KV_LORA, C_HEADS * LANES).astype(BF16)
    wv = jnp.transpose(ukv[:, :, C_NOPE:].reshape(C_KV_LORA, C_WIDTH)).astype(BF16)
    uq = w_uq.reshape(C_Q_LORA, C_HEADS, C_NOPE + C_ROPE)
    wq = jnp.pad(uq, ((0, 0), (0, 0), (0, LANES - C_NOPE - C_ROPE))).reshape(C_Q_LORA, C_HEADS * LANES).astype(BF16)
    return w_p, wk, wv, wmvt, wq


def kernel(x, c, ctx, c_ctx, mod_w, mod_b, pre_norm_w, post_norm_w, e_w_in, e_sink, e_conv_w, e_w_out,
           o_w_in, o_q_norm_w, o_kv_norm_w, o_w_uq, o_w_ukv, o_i_bias, o_f_bias, o_head_norm_w, o_w_out):
    bsz, n_lat, d = x.shape
    n_ctx = ctx.shape[1]
    assert mod_w.shape[0] == 2 and d == D_MODEL, "built for one even + one odd layer"
    assert n_lat % 1024 == 0 and n_ctx % 256 == 0
    tm_in, tm_ctx = 1024, 256

    pad = (-(bsz + 1)) % 8
    cc = jnp.concatenate([c, c_ctx[None, :], jnp.zeros((pad, d), F32)], axis=0)
    mod = _modulation(cc, mod_w, mod_b)
    split = lambda l, r0, r1: [mod[l, r0:r1, k * d:(k + 1) * d][:, None, :] for k in range(3)]
    (rope_a, rope_k, rope_q) = _rope_tables(n_lat)
    dummy_tab = jnp.zeros((n_ctx, LANES), F32)

    sh_x, sc_x, g_x = split(0, 0, bsz)
    sh_c, sc_c, g_c = split(0, bsz, bsz + 1)
    w_in, w_out = _even_weights(e_w_in[0], e_w_out[0])
    nw, pw = pre_norm_w[0][None, :], post_norm_w[0][None, :]
    k, vt, q, bb, u, z = _even_in(x, sh_x, sc_x, nw, w_in, *rope_a, rope=True, tm=tm_in)
    k_c, vt_c, q_c, bb_c, u_c, z_c = _even_in(ctx, sh_c, sc_c, nw, w_in, dummy_tab, dummy_tab,
                                              rope=False, tm=tm_ctx)
    a = _even_attn(e_sink[0], q, z, k, vt, k_c, vt_c, window=True)
    a_c = _even_attn(e_sink[0], q_c, z_c, None, None, k_c, vt_c, window=False)
    x1 = _out_proj((a, bb, u), z, e_conv_w[0], w_out, pw, g_x, x, tm=tm_in)
    ctx1 = _out_proj((a_c, bb_c, u_c), z_c, e_conv_w[0], w_out, pw, g_c, ctx, tm=tm_ctx)

    sh_x, sc_x, g_x = split(1, 0, bsz)
    sh_c, sc_c, _ = split(1, bsz, bsz + 1)
    w_p, wk, wv, wmvt, wq = _odd_weights(o_w_in[0], o_w_uq[0], o_w_ukv[0])
    nw, pw = pre_norm_w[1][None, :], post_norm_w[1][None, :]
    kvn, qn = o_kv_norm_w[0][None, :], o_q_norm_w[0][None, :]
    chunk = min(MLSTM_CHUNK, n_ctx)
    kf, vt, mk, mvt, g2, qf, mq, mo, z = _odd_in(x1, sh_x, sc_x, nw, w_p, kvn, wk, wv, wmvt,
                                                 (*rope_k, qn, wq, *rope_q), tm=tm_in, chunk=chunk)
    kf_c, vt_c, mk_c, mvt_c, g2_c = _odd_in(ctx1, sh_c, sc_c, nw, w_p[:, :OC_CTX_END], kvn, wk, wv, wmvt, None,
                                            tm=tm_ctx, chunk=chunk)
    c_out = _mla(qf, z, kf_c, kf, vt_c, vt)
    gate_bias = jnp.stack([o_i_bias[0], o_f_bias[0]], axis=1).reshape(1, 4 * M_HEADS)
    bias_row = jnp.pad(gate_bias, ((0, 0), (GATE_LANE0, LANES - GATE_LANE0 - 4 * M_HEADS)))
    m_out = _mlstm(mq, mk, mvt, g2, mo, z, mk_c, mvt_c, g2_c, bias_row, o_head_norm_w[0][None, :])
    return _out_proj((c_out, m_out), None, None, o_w_out[0].astype(BF16), pw, g_x, x1, tm=tm_in)
```

```python
import collections
import functools

import numpy as np
import jax
import jax.numpy as jnp
from jax import lax
from jax.experimental import pallas as pl
from jax.experimental.pallas import tpu as pltpu

F32 = jnp.float32
BF16 = jnp.bfloat16
HIGHEST = lax.Precision.HIGHEST

LANES = 128
BF16_ROWS = 16
VMEM_LIMIT = 56 * 1024 * 1024

D_MODEL = 1024
GRID_W = 64
ROPE_THETA = 10000.0
NORM_EPS = 1e-6
NEG = -1e30
LOG2E = 1.4426950408889634

A_HEADS, A_KV_HEADS, A_HEAD_DIM = 8, 2, 64
A_WIDTH = A_HEADS * A_HEAD_DIM
WINDOW = 128
B_WIDTH = 512
C_HEADS, C_NOPE, C_ROPE, C_VDIM = 8, 64, 32, 64
C_KV_LORA, C_Q_LORA = 256, 768
C_WIDTH = C_HEADS * C_VDIM
M_HEADS, M_QK, M_V = 4, 64, 128
M_WIDTH = M_HEADS * M_V
E_GATE = A_WIDTH + B_WIDTH
O_GATE = C_WIDTH + M_WIDTH
E_COLS = (128, 128, A_WIDTH, B_WIDTH, B_WIDTH, B_WIDTH, E_GATE)
O_COLS = (C_KV_LORA, C_ROPE, M_HEADS * M_QK, M_WIDTH, 4 * M_HEADS, C_Q_LORA, M_HEADS * M_QK, M_WIDTH, O_GATE)

EC_K, EC_V, EC_Q, EC_BB, EC_BC, EC_BX, EC_Z, EC_END = 0, 128, 256, 768, 1280, 1792, 2304, 3328
OC_CKV = 0
OC_G2 = 256
OC_MK = 384
OC_CTX_END = 640
OC_CQ = 640
OC_MQ = 1408
OC_MO = 1664
OC_Z = 2176
M_QKW = M_HEADS * M_QK
M_ONES = BF16_ROWS
GATE_LANE0 = 32
SUB_ROWS = 256
MLSTM_CHUNK = 256
ATT_Q = 256
ATT_KEYS = 256
SCORE_SLOTS = 3
SCORE_ROWS = 128
ATT_VDIM = 64
MLA_BLK = 512
ATT_STEP = 1024


def _nt(a, b):
    return lax.dot_general(a, b, (((1,), (1,)), ((), ())), preferred_element_type=F32)


def _dot(a, b):
    return jnp.dot(a, b, preferred_element_type=F32)


def _silu(z):
    return z * jax.nn.sigmoid(z)


def _rms(x, w):
    ms = jnp.mean(x * x, axis=-1, keepdims=True)
    return x * lax.rsqrt(ms + NORM_EPS) * w


def _rope_slab(x, cos, sin_signed, half, take_up):
    up = pltpu.roll(x, LANES - half, axis=1)
    dn = pltpu.roll(x, half, axis=1)
    return x * cos + jnp.where(take_up, up, dn) * sin_signed


def _cparams(sem):
    return pltpu.CompilerParams(dimension_semantics=sem, vmem_limit_bytes=VMEM_LIMIT)


def _resident(shape):
    return pl.BlockSpec(shape, lambda b, i: (0,) * len(shape), pipeline_mode=pl.Buffered(1))


def _mod_kernel(cc_ref, w_ref, b_ref, o_ref):
    s = _silu(cc_ref[...])
    o_ref[...] = jnp.dot(s, w_ref[...], preferred_element_type=F32, precision=HIGHEST) + b_ref[...]


def _modulation(cc, mod_w, mod_b):
    depth, d, d3 = mod_w.shape
    r = cc.shape[0]
    return pl.pallas_call(
        _mod_kernel,
        grid=(depth, d3 // d),
        in_specs=[
            pl.BlockSpec((r, d), lambda l, j: (0, 0)),
            pl.BlockSpec((None, d, d), lambda l, j: (l, 0, j)),
            pl.BlockSpec((None, 1, d), lambda l, j: (l, 0, j)),
        ],
        out_specs=pl.BlockSpec((None, r, d), lambda l, j: (l, 0, j)),
        out_shape=jax.ShapeDtypeStruct((depth, r, d3), F32),
        compiler_params=_cparams(("arbitrary", "arbitrary")),
        name="modulation",
    )(cc, mod_w, mod_b.reshape(depth, 1, d3))


def _even_in_kernel(x_ref, sh_ref, sc_ref, nw_ref, w_ref, cos_ref, sin_ref,
                    k_ref, vt_ref, q_ref, bb_ref, u_ref, z_ref, *, rope):
    lane = lax.broadcasted_iota(jnp.int32, (1, LANES), 1)
    take_up = (lane % A_HEAD_DIM) < (A_HEAD_DIM // 2)
    for r0 in range(0, x_ref.shape[0], SUB_ROWS):
        rows = slice(r0, r0 + SUB_ROWS)
        h = (_rms(x_ref[rows, :], nw_ref[...]) * (1.0 + sc_ref[...]) + sh_ref[...]).astype(BF16)
        y_all = _dot(h, w_ref[...])

        def proj(a, b):
            return y_all[:, a:b]

        if rope:
            cos, sin = cos_ref[rows, :], sin_ref[rows, :]
            rot = lambda t: _rope_slab(t, cos, sin, A_HEAD_DIM // 2, take_up)
        else:
            rot = lambda t: t
        k_ref[rows, :] = rot(proj(EC_K, EC_V)).astype(BF16)
        vt_ref[:, rows] = jnp.transpose(proj(EC_V, EC_Q)).astype(BF16)
        q = proj(EC_Q, EC_BB)
        for j in range(A_WIDTH // LANES):
            sl = slice(j * LANES, (j + 1) * LANES)
            q_ref[rows, sl] = (rot(q[:, sl]) * (A_HEAD_DIM ** -0.5 * LOG2E)).astype(BF16)
        bb_ref[rows, :] = proj(EC_BB, EC_BC).astype(BF16)
        u_ref[rows, :] = (proj(EC_BC, EC_BX) * proj(EC_BX, EC_Z)).astype(BF16)
        z_ref[rows, :] = _silu(proj(EC_Z, EC_END)).astype(BF16)


def _even_in(x, sh, sc, nw, w, cos, sin, *, rope, tm):
    bsz, t, d = x.shape
    assert t % tm == 0 and tm % SUB_ROWS == 0
    nb = sh.shape[0]
    mod_map = (lambda b, i: (b, 0, 0)) if nb > 1 else (lambda b, i: (0, 0, 0))
    tok = lambda width: pl.BlockSpec((None, tm, width), lambda b, i: (b, i, 0))
    const = lambda shape: pl.BlockSpec(shape, lambda b, i: (0,) * len(shape))
    widths = (LANES, None, A_WIDTH, B_WIDTH, B_WIDTH, E_GATE)
    vt_spec = pl.BlockSpec((None, LANES, tm), lambda b, i: (b, 0, i))
    vt_shape = jax.ShapeDtypeStruct((bsz, LANES, t), BF16)
    return pl.pallas_call(
        functools.partial(_even_in_kernel, rope=rope),
        grid=(bsz, t // tm),
        in_specs=[
            tok(d),
            pl.BlockSpec((None, 1, d), mod_map),
            pl.BlockSpec((None, 1, d), mod_map),
            const((1, d)),
            _resident(w.shape),
            pl.BlockSpec((tm, LANES), lambda b, i: (i, 0)),
            pl.BlockSpec((tm, LANES), lambda b, i: (i, 0)),
        ],
        out_specs=[vt_spec if wd is None else tok(wd) for wd in widths],
        out_shape=[vt_shape if wd is None else jax.ShapeDtypeStruct((bsz, t, wd), BF16) for wd in widths],
        compiler_params=_cparams(("parallel", "parallel")),
        name="even_in_rope" if rope else "even_in",
    )(x, sh, sc, nw, w, cos, sin)


AttnStage = collections.namedtuple("AttnStage", "q k vt mask sink")


def _tree_max(parts):
    while len(parts) > 1:
        parts = [jnp.maximum(parts[i], parts[i + 1]) if i + 1 < len(parts) else parts[i]
                 for i in range(0, len(parts), 2)]
    return parts[0]


def _attention_pipeline(stages, n_chunks, s_ref, write_pair):
    kc, sr = ATT_KEYS, SCORE_ROWS
    slots, n_keys = s_ref.shape[0], n_chunks * kc
    lag = slots - 1
    ones = jnp.ones((BF16_ROWS, n_keys), BF16)
    m8 = {}
    halves = []
    for i in range(len(stages) + lag):
        if i < len(stages):
            cur = stages[i]
            q = cur.q()
            parts = []
            for c in range(n_chunks):
                kch, mask = cur.k(c), cur.mask(c)
                for r in range(0, kc, sr):
                    st = _nt(kch[r:r + sr], q)
                    if mask is not None:
                        st = jnp.where(mask[r:r + sr], st, NEG)
                    s_ref[i % slots, c * kc + r:c * kc + r + sr, :] = st
                    parts += [st[t:t + 8, :] for t in range(0, sr, 8)]
            m8[i] = _tree_max(parts)
        if i >= lag:
            prev = stages[i - lag]
            m_prev = jnp.max(m8.pop(i - lag), axis=0, keepdims=True)
            if prev.sink is not None:
                m_prev = jnp.maximum(m_prev, prev.sink)
            pt = jnp.exp2(s_ref[(i - lag) % slots, 0:n_keys, :] - m_prev).astype(BF16)
            vta = jnp.concatenate([prev.vt(c) for c in range(n_chunks)], axis=1)
            acc = _dot(jnp.concatenate([vta, ones], axis=0), pt)
            denom = acc[ATT_VDIM:ATT_VDIM + 1]
            if prev.sink is not None:
                denom = denom + jnp.exp2(prev.sink - m_prev)
            halves.append(acc[0:ATT_VDIM] / denom)
            if len(halves) == 2:
                write_pair((i - lag) // 2, jnp.concatenate(halves, axis=0))
                halves = []


def _even_attn_kernel(*refs, window, n_lat):
    if window:
        (sink_ref, q_ref, z_ref, kp_ref, km_ref, kn_ref, kc_ref, vtp_ref, vtm_ref, vtn_ref, vtc_ref,
         o_ref, s_ref) = refs
    else:
        sink_ref, q_ref, z_ref, kc_ref, vtc_ref, o_ref, s_ref = refs
    nq, kc = ATT_Q, ATT_KEYS
    step, half = q_ref.shape[0], WINDOW
    lane = lax.broadcasted_iota(jnp.int32, (1, LANES), 1)
    lo = lane < A_HEAD_DIM
    ctx_k = [kc_ref[r:r + kc, :] for r in range(0, kc_ref.shape[0], kc)]
    ctx_vt = [vtc_ref[:, r:r + kc] for r in range(0, kc_ref.shape[0], kc)]
    r = lax.broadcasted_iota(jnp.int32, (kc, nq), 0)
    c = lax.broadcasted_iota(jnp.int32, (kc, nq), 1)

    def sub_block(qb):
        k_chunks, vt_chunks, masks = list(ctx_k), list(ctx_vt), [None] * len(ctx_k)
        if window:
            base, q0 = qb * nq, pl.program_id(1) * step + qb * nq
            if qb == 0:
                k1 = jnp.concatenate([kp_ref[...], km_ref[0:half, :]], axis=0)
                vt1 = jnp.concatenate([vtp_ref[...], vtm_ref[:, 0:half]], axis=1)
            else:
                k1, vt1 = km_ref[base - half:base + half, :], vtm_ref[:, base - half:base + half]
            if base + nq == step:
                k2 = jnp.concatenate([km_ref[base + half:step, :], kn_ref[...]], axis=0)
                vt2 = jnp.concatenate([vtm_ref[:, base + half:step], vtn_ref[...]], axis=1)
            else:
                k2, vt2 = km_ref[base + half:base + half + kc, :], vtm_ref[:, base + half:base + half + kc]
            k_chunks += [k1, k2]
            vt_chunks += [vt1, vt2]
            masks += [(jnp.abs(r - half - c) <= WINDOW) & (r + (q0 - half) >= 0),
                      (jnp.abs(r + half - c) <= WINDOW) & (r + (q0 + half) < n_lat)]
        return k_chunks, vt_chunks, masks

    def stage(qb, head, chunks):
        k_chunks, vt_chunks, masks = chunks
        j, grp = head % (A_HEADS // 2), head // (A_HEADS // 2)
        keep = lo if grp == 0 else jnp.logical_not(lo)

        def q():
            qs = q_ref[qb * nq:(qb + 1) * nq, j * LANES:(j + 1) * LANES]
            return jnp.where(keep, qs, jnp.zeros_like(qs))

        return AttnStage(q=q, k=lambda ci: k_chunks[ci],
                         vt=lambda ci: vt_chunks[ci][grp * A_HEAD_DIM:(grp + 1) * A_HEAD_DIM, :],
                         mask=lambda ci: masks[ci], sink=sink_ref[head] * LOG2E)

    def write_pair(p, o_t):
        qb, j = divmod(p, A_HEADS // 2)
        blk = (slice(qb * nq, (qb + 1) * nq), slice(j * LANES, (j + 1) * LANES))
        o_ref[blk] = jnp.transpose(o_t).astype(BF16) * z_ref[blk]

    stages = []
    for qb in range(step // nq):
        chunks = sub_block(qb)
        stages += [stage(qb, h, chunks) for h in _A_HEAD_ORDER]
    _attention_pipeline(stages, len(ctx_k) + (2 if window else 0), s_ref, write_pair)


def _even_attn(sink, q, z, k, vt, kc, vtc, *, window):
    bsz, t, _ = q.shape
    n_ctx = kc.shape[1]
    nq, keys, half = ATT_Q, ATT_KEYS, WINDOW
    step = min(ATT_STEP, t)
    assert nq == keys == 2 * half and t % step == 0 and n_ctx % keys == 0
    per_step, last = step // half, t // half - 1
    smem = pl.BlockSpec(memory_space=pltpu.SMEM)
    qspec = pl.BlockSpec((None, step, A_WIDTH), lambda b, i: (b, i, 0))
    kcspec = pl.BlockSpec((None, n_ctx, LANES), lambda b, i: (b, 0, 0))
    vtcspec = pl.BlockSpec((None, LANES, n_ctx), lambda b, i: (b, 0, 0))
    if window:
        prev = lambda i: jnp.maximum(per_step * i - 1, 0)
        nxt = lambda i: jnp.minimum(per_step * (i + 1), last)
        specs = [smem, qspec, qspec,
                 pl.BlockSpec((None, half, LANES), lambda b, i: (b, prev(i), 0)),
                 pl.BlockSpec((None, step, LANES), lambda b, i: (b, i, 0)),
                 pl.BlockSpec((None, half, LANES), lambda b, i: (b, nxt(i), 0)),
                 kcspec,
                 pl.BlockSpec((None, LANES, half), lambda b, i: (b, 0, prev(i))),
                 pl.BlockSpec((None, LANES, step), lambda b, i: (b, 0, i)),
                 pl.BlockSpec((None, LANES, half), lambda b, i: (b, 0, nxt(i))),
                 vtcspec]
        args = (sink, q, z, k, k, k, kc, vt, vt, vt, vtc)
    else:
        specs = [smem, qspec, qspec, kcspec, vtcspec]
        args = (sink, q, z, kc, vtc)
    n_chunks = n_ctx // keys + (2 if window else 0)
    return pl.pallas_call(
        functools.partial(_even_attn_kernel, window=window, n_lat=t),
        grid=(bsz, t // step),
        in_specs=specs,
        out_specs=qspec,
        out_shape=jax.ShapeDtypeStruct((bsz, t, A_WIDTH), BF16),
        scratch_shapes=[pltpu.VMEM((SCORE_SLOTS, n_chunks * keys, nq), F32)],
        compiler_params=_cparams(("parallel", "parallel")),
        name="even_attn_window" if window else "even_attn_ctx",
    )(*args)


def _out_kernel(*refs, conv):
    if conv:
        a_ref, bb_ref, u_ref, up_ref, un_ref, z_ref, cw_ref, w_ref, pw_ref, g_ref, x_ref, o_ref = refs
        tm = u_ref.shape[0]
        i = pl.program_id(1)
        u = u_ref[...].astype(F32)
        prev_row = jnp.where(i > 0, up_ref[BF16_ROWS - 1:BF16_ROWS, :].astype(F32), 0.0)
        next_row = jnp.where(i < pl.num_programs(1) - 1, un_ref[0:1, :].astype(F32), 0.0)
        row = lax.broadcasted_iota(jnp.int32, (tm, 1), 0)
        u_m1 = jnp.where(row == 0, prev_row, pltpu.roll(u, 1, axis=0))
        u_p1 = jnp.where(row == tm - 1, next_row, pltpu.roll(u, tm - 1, axis=0))
        cw = cw_ref[...]
        mix_b = (bb_ref[...].astype(F32) * (u_m1 * cw[0:1, :] + u * cw[1:2, :] + u_p1 * cw[2:3, :])).astype(BF16)
        mix_b = mix_b * z_ref[...]
    else:
        a_ref, b_ref, w_ref, pw_ref, g_ref, x_ref, o_ref = refs
        mix_b = b_ref
    for r0 in range(0, x_ref.shape[0], SUB_ROWS):
        rows = slice(r0, r0 + SUB_ROWS)
        gated = jnp.concatenate([a_ref[rows, :], mix_b[rows, :]], axis=1)
        o_ref[rows, :] = x_ref[rows, :] + g_ref[...] * _rms(_dot(gated, w_ref[...]), pw_ref[...])


def _out_proj(mix, z, cw, w, pw, g, x, *, tm):
    conv = cw is not None
    bsz, t, d = x.shape
    assert t % tm == 0 and tm % SUB_ROWS == 0
    nb = g.shape[0]
    mod_map = (lambda b, i: (b, 0, 0)) if nb > 1 else (lambda b, i: (0, 0, 0))
    tok = lambda width: pl.BlockSpec((None, tm, width), lambda b, i: (b, i, 0))
    const = lambda shape: pl.BlockSpec(shape, lambda b, i: (0,) * len(shape))
    hb = BF16_ROWS
    per_tile, last = tm // hb, t // hb - 1
    if conv:
        a, bb, u = mix
        specs = [tok(a.shape[2]), tok(bb.shape[2]), tok(u.shape[2]),
                 pl.BlockSpec((None, hb, u.shape[2]), lambda b, i: (b, jnp.maximum(i * per_tile - 1, 0), 0)),
                 pl.BlockSpec((None, hb, u.shape[2]), lambda b, i: (b, jnp.minimum((i + 1) * per_tile, last), 0)),
                 pl.BlockSpec((None, tm, bb.shape[2]), lambda b, i: (b, i, z.shape[2] // bb.shape[2] - 1)),
                 const(cw.shape)]
        args = (a, bb, u, u, u, z, cw)
    else:
        a, b2 = mix
        specs = [tok(a.shape[2]), tok(b2.shape[2])]
        args = (a, b2)
    specs += [_resident(w.shape), const((1, d)), pl.BlockSpec((None, 1, d), mod_map), tok(d)]
    return pl.pallas_call(
        functools.partial(_out_kernel, conv=conv),
        grid=(bsz, t // tm),
        in_specs=specs,
        out_specs=tok(d),
        out_shape=jax.ShapeDtypeStruct((bsz, t, d), F32),
        compiler_params=_cparams(("parallel", "parallel")),
        name="out_proj_conv" if conv else "out_proj",
    )(*args, w, pw, g, x)


def _odd_in_kernel(*refs, lat):
    if lat:
        (x_ref, sh_ref, sc_ref, nw_ref, w_ref, kvn_ref, wk_ref, wvt_ref, wmvt_ref, cosk_ref, sink_ref,
         qn_ref, wq_ref, cosq_ref, sinq_ref,
         kf_ref, vt_ref, mk_ref, mvt_ref, g2_ref, qf_ref, mq_ref, mo_ref, z_ref) = refs
    else:
        (x_ref, sh_ref, sc_ref, nw_ref, w_ref, kvn_ref, wk_ref, wvt_ref, wmvt_ref,
         kf_ref, vt_ref, mk_ref, mvt_ref, g2_ref) = refs
    lane = lax.broadcasted_iota(jnp.int32, (1, LANES), 1)
    chunk = mvt_ref.shape[2]
    per_sub = SUB_ROWS // chunk
    for cc in range(x_ref.shape[0] // SUB_ROWS):
        rows = slice(cc * SUB_ROWS, (cc + 1) * SUB_ROWS)
        h = (_rms(x_ref[rows, :], nw_ref[...]) * (1.0 + sc_ref[...]) + sh_ref[...]).astype(BF16)
        y_all = _dot(h, w_ref[...])

        def proj(a, b):
            return y_all[:, a:b]

        ckv = _rms(proj(OC_CKV, OC_CKV + C_KV_LORA), kvn_ref[...]).astype(BF16)
        g2 = proj(OC_G2, OC_G2 + LANES)
        g2_ref[rows, :] = g2
        if lat:
            g2 = _rope_slab(g2, cosk_ref[rows, :], sink_ref[rows, :], C_ROPE // 2, lane < C_ROPE // 2)
        rope_lanes = (lane >= C_NOPE) & (lane < C_NOPE + C_ROPE)
        kr = jnp.where(rope_lanes, pltpu.roll(g2, C_NOPE, axis=1), 0.0)
        k_nope = _dot(ckv, wk_ref[...])
        for hd in range(C_HEADS):
            sl = slice(hd * LANES, (hd + 1) * LANES)
            kf_ref[rows, sl] = (k_nope[:, sl] + kr).astype(BF16)
        vt_ref[:, rows] = _nt(wvt_ref[...], ckv).astype(BF16)
        mk_ref[rows, :] = (proj(OC_MK, OC_MK + M_QKW) * (M_QK ** -0.5)).astype(BF16)
        mvt = _nt(wmvt_ref[...], h).astype(BF16)
        for s in range(per_sub):
            mvt_ref[cc * per_sub + s] = mvt[:, s * chunk:(s + 1) * chunk]
        if lat:
            cq = _rms(proj(OC_CQ, OC_CQ + C_Q_LORA), qn_ref[...]).astype(BF16)
            qf = _dot(cq, wq_ref[...])
            cos, sin = cosq_ref[rows, :], sinq_ref[rows, :]
            take_up = (lane >= C_NOPE) & (lane < C_NOPE + C_ROPE // 2)
            scale = (C_NOPE + C_ROPE) ** -0.5 * LOG2E
            for hd in range(C_HEADS):
                sl = slice(hd * LANES, (hd + 1) * LANES)
                qf_ref[rows, sl] = (_rope_slab(qf[:, sl], cos, sin, C_ROPE // 2, take_up) * scale).astype(BF16)
            mq_ref[rows, :] = proj(OC_MQ, OC_MQ + M_QKW).astype(BF16)
            mo_ref[rows, :] = jax.nn.sigmoid(proj(OC_MO, OC_MO + M_WIDTH)).astype(BF16)
            z_ref[rows, :] = _silu(proj(OC_Z, OC_Z + O_GATE)).astype(BF16)


def _odd_in(x, sh, sc, nw, w, kvn, wk, wv, wmvt, lat_args, *, tm, chunk):
    lat = lat_args is not None
    bsz, t, d = x.shape
    assert t % tm == 0 and tm % SUB_ROWS == 0 and SUB_ROWS % chunk == 0
    nb = sh.shape[0]
    mod_map = (lambda b, i: (b, 0, 0)) if nb > 1 else (lambda b, i: (0, 0, 0))
    tok = lambda width: (pl.BlockSpec((None, tm, width), lambda b, i: (b, i, 0)), (bsz, t, width))
    const = lambda shape: pl.BlockSpec(shape, lambda b, i: (0,) * len(shape))
    tab = pl.BlockSpec((tm, LANES), lambda b, i: (i, 0))
    specs = [tok(d)[0], pl.BlockSpec((None, 1, d), mod_map), pl.BlockSpec((None, 1, d), mod_map),
             const((1, d)), _resident(w.shape), const(kvn.shape), _resident(wk.shape), _resident(wv.shape),
             _resident(wmvt.shape)]
    args = [x, sh, sc, nw, w, kvn, wk, wv, wmvt]
    vt = (pl.BlockSpec((None, C_WIDTH, tm), lambda b, i: (b, 0, i)), (bsz, C_WIDTH, t))
    mvt = (pl.BlockSpec((None, tm // chunk, M_WIDTH, chunk), lambda b, i: (b, i, 0, 0)),
           (bsz, t // chunk, M_WIDTH, chunk))
    outs = [(tok(C_HEADS * LANES), BF16), (vt, BF16), (tok(M_QKW), BF16), (mvt, BF16), (tok(LANES), F32)]
    if lat:
        cosk, sink, qn, wq, cosq, sinq = lat_args
        specs += [tab, tab, const(qn.shape), _resident(wq.shape), tab, tab]
        args += [cosk, sink, qn, wq, cosq, sinq]
        outs += [(tok(C_HEADS * LANES), BF16), (tok(M_QKW), BF16), (tok(M_WIDTH), BF16), (tok(O_GATE), BF16)]
    return pl.pallas_call(
        functools.partial(_odd_in_kernel, lat=lat),
        grid=(bsz, t // tm),
        in_specs=specs,
        out_specs=[o[0][0] for o in outs],
        out_shape=[jax.ShapeDtypeStruct(o[0][1], o[1]) for o in outs],
        compiler_params=_cparams(("parallel", "parallel")),
        name="odd_in_lat" if lat else "odd_in_ctx",
    )(*args)


def _mla_kernel(q_ref, z_ref, kc_ref, vtc_ref, *rest):
    n_pairs = C_HEADS // 2
    kx_refs, vtx_refs, (o_ref, s_ref) = rest[:n_pairs], rest[n_pairs:2 * n_pairs], rest[2 * n_pairs:]
    nq, kc = ATT_Q, ATT_KEYS
    n_ctx_chunks = kc_ref.shape[0] // kc
    n_chunks = n_ctx_chunks + kx_refs[0].shape[0] // kc

    def stage(qb, hd):
        sl = slice(hd * LANES, (hd + 1) * LANES)
        half = hd % 2

        def k(ci):
            if ci < n_ctx_chunks:
                return kc_ref[ci * kc:(ci + 1) * kc, sl]
            r0 = (ci - n_ctx_chunks) * kc
            return kx_refs[hd // 2][r0:r0 + kc, half * LANES:(half + 1) * LANES]

        def vt(ci):
            if ci < n_ctx_chunks:
                return vtc_ref[hd * C_VDIM:(hd + 1) * C_VDIM, ci * kc:(ci + 1) * kc]
            r0 = (ci - n_ctx_chunks) * kc
            return vtx_refs[hd // 2][half * C_VDIM:(half + 1) * C_VDIM, r0:r0 + kc]

        return AttnStage(q=lambda: q_ref[qb * nq:(qb + 1) * nq, sl], k=k, vt=vt, mask=lambda ci: None, sink=None)

    def write_pair(p, o_t):
        qb, j = divmod(p, C_HEADS // 2)
        blk = (slice(qb * nq, (qb + 1) * nq), slice(j * LANES, (j + 1) * LANES))
        o_ref[blk] = jnp.transpose(o_t).astype(BF16) * z_ref[blk]

    stages = [stage(qb, hd) for qb in range(q_ref.shape[0] // nq) for hd in range(C_HEADS)]
    _attention_pipeline(stages, n_chunks, s_ref, write_pair)


def _mla(qf, z, kfc, kfx, vtc, vtx):
    bsz, t, wq = qf.shape
    n_ctx = kfc.shape[1]
    blk = MLA_BLK
    assert t % blk == 0 and blk % ATT_Q == 0 and n_ctx % ATT_KEYS == 0 and t % ATT_KEYS == 0
    n_pairs = C_HEADS // 2
    pair_k = lambda j: pl.BlockSpec((None, t, 2 * LANES), lambda b, i: (b, 0, j))
    pair_vt = lambda j: pl.BlockSpec((None, 2 * C_VDIM, t), lambda b, i: (b, j, 0))
    return pl.pallas_call(
        _mla_kernel,
        grid=(bsz, t // blk),
        in_specs=[
            pl.BlockSpec((None, blk, wq), lambda b, i: (b, i, 0)),
            pl.BlockSpec((None, blk, C_WIDTH), lambda b, i: (b, i, 0)),
            pl.BlockSpec((None, n_ctx, wq), lambda b, i: (b, 0, 0)),
            pl.BlockSpec((None, C_WIDTH, n_ctx), lambda b, i: (b, 0, 0)),
        ] + [pair_k(j) for j in range(n_pairs)] + [pair_vt(j) for j in range(n_pairs)],
        out_specs=pl.BlockSpec((None, blk, C_WIDTH), lambda b, i: (b, i, 0)),
        out_shape=jax.ShapeDtypeStruct((bsz, t, C_WIDTH), BF16),
        scratch_shapes=[pltpu.VMEM((SCORE_SLOTS, n_ctx + t, ATT_Q), F32)],
        compiler_params=_cparams(("parallel", "arbitrary")),
        name="mla_attention",
    )(qf, z, kfc, vtc, *([kfx] * n_pairs), *([vtx] * n_pairs))


def _mlstm_kernel(mq_ref, mk_ref, mvt_ref, g_ref, mo_ref, z_ref, mkc_ref, mvtc_ref, gc_ref, bias_ref, hnw_ref,
                  out_ref, s_ref, m_ref, hf_ref, hr_ref):
    nc, ncc, L = mvt_ref.shape[0], mvtc_ref.shape[0], mvt_ref.shape[2]
    row = lax.broadcasted_iota(jnp.int32, (L, L), 0)
    col = lax.broadcasted_iota(jnp.int32, (L, L), 1)
    vis = (row <= col, row >= col)
    tri_ones = (col <= row).astype(BF16)
    lane = lax.broadcasted_iota(jnp.int32, (1, LANES), 1)
    lo = lane < M_QK
    ones_rows = jnp.ones((M_ONES, L), BF16)
    fwd_lanes = lane < GATE_LANE0 + 2 * M_HEADS
    tok16 = lax.broadcasted_iota(jnp.int32, (4 * M_HEADS, L), 1)
    fwd_rows16 = lax.broadcasted_iota(jnp.int32, (4 * M_HEADS, L), 0) < 2 * M_HEADS
    pair = lambda hd: slice((hd // 2) * LANES, (hd // 2 + 1) * LANES)
    vrows = lambda hd: slice(hd * M_V, (hd + 1) * M_V)

    def gates(g, with_max=True):
        gb = g + bias_ref[...]
        ls = jnp.minimum(gb, 0.0) - jnp.log1p(jnp.exp(-jnp.abs(gb)))
        hi = ls.astype(BF16)
        lo = (ls - hi.astype(F32)).astype(BF16)
        cf = _dot(tri_ones, hi) + _dot(tri_ones, lo)
        tot = cf[L - 1:L, :]
        b = jnp.where(fwd_lanes, cf, tot - cf + ls)
        li = pltpu.roll(gb, M_HEADS, axis=1)
        gate_rows = slice(GATE_LANE0, GATE_LANE0 + 4 * M_HEADS)
        b_t, li_t = jnp.transpose(b)[gate_rows], jnp.transpose(li)[gate_rows]
        cm = li_t - b_t
        sh = 1 if with_max else L
        while sh < L:
            earlier = jnp.where(tok16 >= sh, pltpu.roll(cm, sh, axis=1), NEG)
            later = jnp.where(tok16 < L - sh, pltpu.roll(cm, L - sh, axis=1), NEG)
            cm = jnp.maximum(cm, jnp.where(fwd_rows16, earlier, later))
            sh *= 2
        return b - li, tot, b_t, li_t, cm

    def head_lanes(hd, pair_slab):
        keep = lo if hd % 2 == 0 else jnp.logical_not(lo)
        return jnp.where(keep, pair_slab, jnp.zeros_like(pair_slab))

    def scores(u, kh, qh):
        return _nt(jnp.concatenate([kh, s_ref[u].astype(BF16)], axis=0), qh)

    def finish(d, hd, gq, kh, vt_h, big):
        r, tot, b_t, li_t, cm = gq
        u = d * M_HEADS + hd
        fl = GATE_LANE0 + d * 2 * M_HEADS + M_HEADS + hd
        gr = slice(fl - GATE_LANE0, fl - GATE_LANE0 + 1)
        b_row, li_row, cm_row = b_t[gr, :], li_t[gr, :], cm[gr, :]
        g, r_col = tot[:, fl:fl + 1], r[:, fl:fl + 1]
        vaug = jnp.concatenate([vt_h, ones_rows], axis=0)
        s_in = s_ref[u]
        m_in = m_ref[u][0:1, 0:1]
        h_t = None
        if big is not None:
            inter = b_row + m_in
            m_t = jnp.maximum(inter, b_row + cm_row)
            sc = big[0:L] * jnp.exp(jnp.where(vis[d], (b_row - m_t) - r_col, NEG))
            res = jnp.exp(inter - m_t) * big[L:] + _dot(vaug, sc.astype(BF16))
            den = jnp.maximum(jnp.abs(res[M_V:M_V + 1]), jnp.exp(-m_t))
            h_t = res[0:M_V] / den
        a_row = g - b_row + li_row
        m_loc = jnp.max(a_row, axis=1, keepdims=True)
        wv = (jnp.exp(a_row - m_loc) * vaug.astype(F32)).astype(BF16)
        s_loc = _dot(wv, kh)
        m_new = jnp.maximum(g + m_in, m_loc)
        s_ref[u] = jnp.exp(g + m_in - m_new) * s_in + jnp.exp(m_loc - m_new) * s_loc
        m_ref[u] = jnp.broadcast_to(m_new, (8, LANES))
        return h_t

    def chunk_gates(jj):
        return gates(g_ref[pl.ds(pl.multiple_of(jj * L, L), L), :])

    first_gates = (chunk_gates(0), chunk_gates(nc - 1))
    s_ref[...] = jnp.zeros_like(s_ref)
    m_ref[...] = jnp.zeros_like(m_ref)
    ctx_gates = {}
    for d in range(2):
        for cc in range(ncc):
            jj = cc if d == 0 else ncc - 1 - cc
            if jj not in ctx_gates:
                ctx_gates[jj] = gates(gc_ref[jj * L:(jj + 1) * L, :], with_max=False)
            gq = ctx_gates[jj]
            for hd in range(M_HEADS):
                kh = head_lanes(hd, mkc_ref[jj * L:(jj + 1) * L, pair(hd)])
                finish(d, hd, gq, kh, mvtc_ref[jj, vrows(hd), :], None)

    def body(j, gq):
        chunk_of = (j, nc - 1 - j)
        rows = [pl.ds(pl.multiple_of(jj * L, L), L) for jj in chunk_of]
        gq_next = (chunk_gates(jnp.minimum(j + 1, nc - 1)), chunk_gates(jnp.maximum(nc - 2 - j, 0)))
        pending = None
        for d in range(2):
            for hd in range(M_HEADS):
                kh = head_lanes(hd, mk_ref[rows[d], pair(hd)])
                big = scores(d * M_HEADS + hd, kh, head_lanes(hd, mq_ref[rows[d], pair(hd)]))
                if pending is not None:
                    pending()
                dst = hf_ref if d == 0 else hr_ref

                def pending(d=d, hd=hd, kh=kh, big=big, dst=dst):
                    dst[chunk_of[d], hd] = finish(d, hd, gq[d], kh, mvt_ref[chunk_of[d], vrows(hd), :], big)
        pending()
        return gq_next

    lax.fori_loop(0, nc, body, first_gates)

    def readout(j, carry):
        rows = pl.ds(pl.multiple_of(j * L, L), L)
        for hd in range(M_HEADS):
            hs_t = hf_ref[j, hd] + hr_ref[j, hd]
            ms = jnp.mean(hs_t * hs_t, axis=0, keepdims=True)
            hs = jnp.transpose(hs_t * lax.rsqrt(ms + NORM_EPS)) * hnw_ref[:, vrows(hd)]
            m_out = (mo_ref[rows, vrows(hd)].astype(F32) * hs).astype(BF16)
            out_ref[rows, vrows(hd)] = m_out * z_ref[rows, vrows(hd)]
        return carry

    lax.fori_loop(0, nc, readout, 0, unroll=4)


def _mlstm(mq, mk, mvt, g2, mo, z, mkc, mvtc, g2c, bias_row, hnw):
    bsz, t, _ = mq.shape
    n_ctx = mkc.shape[1]
    nc, _, chunk = mvt.shape[1:]
    ncc = mvtc.shape[1]
    per_b = lambda rows, width: pl.BlockSpec((None, rows, width), lambda b: (b, 0, 0))
    chunked = lambda n: pl.BlockSpec((None, n, M_WIDTH, chunk), lambda b: (b, 0, 0, 0))
    const = lambda shape: pl.BlockSpec(shape, lambda b: (0,) * len(shape))
    return pl.pallas_call(
        _mlstm_kernel,
        grid=(bsz,),
        in_specs=[per_b(t, M_QKW), per_b(t, M_QKW), chunked(nc), per_b(t, LANES), per_b(t, M_WIDTH),
                  pl.BlockSpec((None, t, M_WIDTH), lambda b: (b, 0, z.shape[2] // M_WIDTH - 1)),
                  per_b(n_ctx, M_QKW), chunked(ncc), per_b(n_ctx, LANES),
                  const((1, LANES)), const((1, M_WIDTH))],
        out_specs=per_b(t, M_WIDTH),
        out_shape=jax.ShapeDtypeStruct((bsz, t, M_WIDTH), BF16),
        scratch_shapes=[pltpu.VMEM((2 * M_HEADS, M_V + M_ONES, LANES), F32),
                        pltpu.VMEM((2 * M_HEADS, 8, LANES), F32),
                        pltpu.VMEM((nc, M_HEADS, M_V, chunk), F32),
                        pltpu.VMEM((nc, M_HEADS, M_V, chunk), F32)],
        compiler_params=_cparams(("parallel",)),
        name="mlstm",
    )(mq, mk, mvt, g2, mo, z, mkc, mvtc, g2c, bias_row, hnw)


def _axial_angles(n_lat, rot_dim):
    rows = n_lat // GRID_W
    row = jnp.repeat(jnp.arange(rows), GRID_W).astype(F32)
    col = jnp.tile(jnp.arange(GRID_W), rows).astype(F32)
    n_freq = rot_dim // 4
    inv = ROPE_THETA ** (-jnp.arange(n_freq, dtype=F32) / n_freq)
    ang_t = jnp.concatenate([inv[:, None] * row[None, :], inv[:, None] * col[None, :]], axis=0)
    return jnp.transpose(jnp.cos(ang_t)), jnp.transpose(jnp.sin(ang_t))


def _rope_tables(n_lat):
    cos, sin = _axial_angles(n_lat, A_HEAD_DIM)
    cos_a = jnp.tile(jnp.concatenate([cos, cos], axis=-1), (1, 2))
    sin_a = jnp.tile(jnp.concatenate([-sin, sin], axis=-1), (1, 2))
    cos, sin = _axial_angles(n_lat, C_ROPE)
    one = lambda w: jnp.ones((n_lat, w), F32)
    zero = lambda w: jnp.zeros((n_lat, w), F32)
    cos_k = jnp.concatenate([cos, cos, one(LANES - C_ROPE)], axis=-1)
    sin_k = jnp.concatenate([-sin, sin, zero(LANES - C_ROPE)], axis=-1)
    cos_q = jnp.concatenate([one(C_NOPE), cos, cos, one(LANES - C_NOPE - C_ROPE)], axis=-1)
    sin_q = jnp.concatenate([zero(C_NOPE), -sin, sin, zero(LANES - C_NOPE - C_ROPE)], axis=-1)
    return (cos_a, sin_a), (cos_k, sin_k), (cos_q, sin_q)


_A_HEAD_ORDER = (0, 4, 1, 5, 2, 6, 3, 7)


def _perm_heads(w, axis):
    shape = w.shape
    grp = A_HEADS // A_KV_HEADS
    w = w.reshape(shape[:axis] + (A_KV_HEADS, grp, A_HEAD_DIM) + shape[axis + 1:])
    return jnp.swapaxes(w, axis, axis + 1).reshape(shape)


def _even_weights(w_in, w_out):
    offs = np.cumsum((0,) + E_COLS)
    parts = [w_in[:, offs[i]:offs[i + 1]] for i in range(len(E_COLS))]
    parts[2] = _perm_heads(parts[2], 1)
    z = parts[6]
    parts[6] = jnp.concatenate([_perm_heads(z[:, :A_WIDTH], 1), z[:, A_WIDTH:]], axis=1)
    w_out_p = jnp.concatenate([_perm_heads(w_out[:A_WIDTH], 0), w_out[A_WIDTH:]], axis=0)
    return jnp.concatenate(parts, axis=1).astype(BF16), w_out_p.astype(BF16)


def _odd_weights(w_in, w_uq, w_ukv):
    d = w_in.shape[0]
    offs = np.cumsum((0,) + O_COLS)
    ckv, kr, mk, mv, mg, cq, mq, mo, z = [w_in[:, offs[i]:offs[i + 1]] for i in range(len(O_COLS))]
    g2 = jnp.concatenate([kr, mg, jnp.zeros((d, LANES - C_ROPE - 4 * M_HEADS), w_in.dtype)], axis=1)
    w_p = jnp.concatenate([ckv, g2, mk, cq, mq, mo, z], axis=1).astype(BF16)
    wmvt = jnp.transpose(mv).astype(BF16)
    ukv = w_ukv.reshape(C_KV_LORA, C_HEADS, C_NOPE + C_VDIM)
    wk = jnp.pad(ukv[:, :, :C_NOPE], ((0, 0), (0, 0), (0, LANES - C_NOPE)))
    wk = wk.reshape(C_KV_LORA, C_HEADS * LANES).astype(BF16)
    wv = jnp.transpose(ukv[:, :, C_NOPE:].reshape(C_KV_LORA, C_WIDTH)).astype(BF16)
    uq = w_uq.reshape(C_Q_LORA, C_HEADS, C_NOPE + C_ROPE)
    wq = jnp.pad(uq, ((0, 0), (0, 0), (0, LANES - C_NOPE - C_ROPE))).reshape(C_Q_LORA, C_HEADS * LANES).astype(BF16)
    return w_p, wk, wv, wmvt, wq


def kernel(x, c, ctx, c_ctx, mod_w, mod_b, pre_norm_w, post_norm_w, e_w_in, e_sink, e_conv_w, e_w_out,
           o_w_in, o_q_norm_w, o_kv_norm_w, o_w_uq, o_w_ukv, o_i_bias, o_f_bias, o_head_norm_w, o_w_out):
    bsz, n_lat, d = x.shape
    n_ctx = ctx.shape[1]
    assert mod_w.shape[0] == 2 and d == D_MODEL, "built for one even + one odd layer"
    assert n_lat % 1024 == 0 and n_ctx % 256 == 0
    tm_in, tm_ctx = 1024, 256

    pad = (-(bsz + 1)) % 8
    cc = jnp.concatenate([c, c_ctx[None, :], jnp.zeros((pad, d), F32)], axis=0)
    mod = _modulation(cc, mod_w, mod_b)
    split = lambda l, r0, r1: [mod[l, r0:r1, k * d:(k + 1) * d][:, None, :] for k in range(3)]
    (rope_a, rope_k, rope_q) = _rope_tables(n_lat)
    dummy_tab = jnp.zeros((n_ctx, LANES), F32)

    sh_x, sc_x, g_x = split(0, 0, bsz)
    sh_c, sc_c, g_c = split(0, bsz, bsz + 1)
    w_in, w_out = _even_weights(e_w_in[0], e_w_out[0])
    nw, pw = pre_norm_w[0][None, :], post_norm_w[0][None, :]
    k, vt, q, bb, u, z = _even_in(x, sh_x, sc_x, nw, w_in, *rope_a, rope=True, tm=tm_in)
    k_c, vt_c, q_c, bb_c, u_c, z_c = _even_in(ctx, sh_c, sc_c, nw, w_in, dummy_tab, dummy_tab,
                                              rope=False, tm=tm_ctx)
    a = _even_attn(e_sink[0], q, z, k, vt, k_c, vt_c, window=True)
    a_c = _even_attn(e_sink[0], q_c, z_c, None, None, k_c, vt_c, window=False)
    x1 = _out_proj((a, bb, u), z, e_conv_w[0], w_out, pw, g_x, x, tm=tm_in)
    ctx1 = _out_proj((a_c, bb_c, u_c), z_c, e_conv_w[0], w_out, pw, g_c, ctx, tm=tm_ctx)

    sh_x, sc_x, g_x = split(1, 0, bsz)
    sh_c, sc_c, _ = split(1, bsz, bsz + 1)
    w_p, wk, wv, wmvt, wq = _odd_weights(o_w_in[0], o_w_uq[0], o_w_ukv[0])
    nw, pw = pre_norm_w[1][None, :], post_norm_w[1][None, :]
    kvn, qn = o_kv_norm_w[0][None, :], o_q_norm_w[0][None, :]
    chunk = min(MLSTM_CHUNK, n_ctx)
    kf, vt, mk, mvt, g2, qf, mq, mo, z = _odd_in(x1, sh_x, sc_x, nw, w_p, kvn, wk, wv, wmvt,
                                                 (*rope_k, qn, wq, *rope_q), tm=tm_in, chunk=chunk)
    kf_c, vt_c, mk_c, mvt_c, g2_c = _odd_in(ctx1, sh_c, sc_c, nw, w_p[:, :OC_CTX_END], kvn, wk, wv, wmvt, None,
                                            tm=tm_ctx, chunk=chunk)
    c_out = _mla(qf, z, kf_c, kf, vt_c, vt)
    gate_bias = jnp.stack([o_i_bias[0], o_f_bias[0]], axis=1).reshape(1, 4 * M_HEADS)
    bias_row = jnp.pad(gate_bias, ((0, 0), (GATE_LANE0, LANES - GATE_LANE0 - 4 * M_HEADS)))
    m_out = _mlstm(mq, mk, mvt, g2, mo, z, mk_c, mvt_c, g2_c, bias_row, o_head_norm_w[0][None, :])
    return _out_proj((c_out, m_out), None, None, o_w_out[0].astype(BF16), pw, g_x, x1, tm=tm_in)
```

```python
import collections
import functools

import numpy as np
import jax
import jax.numpy as jnp
from jax import lax
from jax.experimental import pallas as pl
from jax.experimental.pallas import tpu as pltpu

F32 = jnp.float32
BF16 = jnp.bfloat16
HIGHEST = lax.Precision.HIGHEST

LANES = 128
BF16_ROWS = 16
VMEM_LIMIT = 56 * 1024 * 1024

D_MODEL = 1024
GRID_W = 64
ROPE_THETA = 10000.0
NORM_EPS = 1e-6
NEG = -1e30
LOG2E = 1.4426950408889634

A_HEADS, A_KV_HEADS, A_HEAD_DIM = 8, 2, 64
A_WIDTH = A_HEADS * A_HEAD_DIM
WINDOW = 128
B_WIDTH = 512
C_HEADS, C_NOPE, C_ROPE, C_VDIM = 8, 64, 32, 64
C_KV_LORA, C_Q_LORA = 256, 768
C_WIDTH = C_HEADS * C_VDIM
M_HEADS, M_QK, M_V = 4, 64, 128
M_WIDTH = M_HEADS * M_V
E_GATE = A_WIDTH + B_WIDTH
O_GATE = C_WIDTH + M_WIDTH
E_COLS = (128, 128, A_WIDTH, B_WIDTH, B_WIDTH, B_WIDTH, E_GATE)
O_COLS = (C_KV_LORA, C_ROPE, M_HEADS * M_QK, M_WIDTH, 4 * M_HEADS, C_Q_LORA, M_HEADS * M_QK, M_WIDTH, O_GATE)

EC_K, EC_V, EC_Q, EC_BB, EC_BC, EC_BX, EC_Z, EC_END = 0, 128, 256, 768, 1280, 1792, 2304, 3328
OC_CKV = 0
OC_G2 = 256
OC_MK = 384
OC_CTX_END = 640
OC_CQ = 640
OC_MQ = 1408
OC_MO = 1664
OC_Z = 2176
M_QKW = M_HEADS * M_QK
M_ONES = BF16_ROWS
GATE_LANE0 = 32
SUB_ROWS = 256
SCORES_AHEAD = 3
MLSTM_CHUNK = 256
ATT_Q = 256
ATT_KEYS = 256
SCORE_SLOTS = 3
SCORE_ROWS = 128
ATT_VDIM = 64
MLA_BLK = 512
ATT_STEP = 1024


def _nt(a, b):
    return lax.dot_general(a, b, (((1,), (1,)), ((), ())), preferred_element_type=F32)


def _dot(a, b):
    return jnp.dot(a, b, preferred_element_type=F32)


def _silu(z):
    return z * jax.nn.sigmoid(z)


def _rms(x, w):
    ms = jnp.mean(x * x, axis=-1, keepdims=True)
    return x * lax.rsqrt(ms + NORM_EPS) * w


def _rope_slab(x, cos, sin_signed, half, take_up):
    up = pltpu.roll(x, LANES - half, axis=1)
    dn = pltpu.roll(x, half, axis=1)
    return x * cos + jnp.where(take_up, up, dn) * sin_signed


def _cparams(sem):
    return pltpu.CompilerParams(dimension_semantics=sem, vmem_limit_bytes=VMEM_LIMIT)


def _resident(shape):
    return pl.BlockSpec(shape, lambda b, i: (0,) * len(shape), pipeline_mode=pl.Buffered(1))


def _mod_kernel(cc_ref, w_ref, b_ref, o_ref):
    s = _silu(cc_ref[...])
    o_ref[...] = jnp.dot(s, w_ref[...], preferred_element_type=F32, precision=HIGHEST) + b_ref[...]


def _modulation(cc, mod_w, mod_b):
    depth, d, d3 = mod_w.shape
    r = cc.shape[0]
    return pl.pallas_call(
        _mod_kernel,
        grid=(depth, d3 // d),
        in_specs=[
            pl.BlockSpec((r, d), lambda l, j: (0, 0)),
            pl.BlockSpec((None, d, d), lambda l, j: (l, 0, j)),
            pl.BlockSpec((None, 1, d), lambda l, j: (l, 0, j)),
        ],
        out_specs=pl.BlockSpec((None, r, d), lambda l, j: (l, 0, j)),
        out_shape=jax.ShapeDtypeStruct((depth, r, d3), F32),
        compiler_params=_cparams(("arbitrary", "arbitrary")),
        name="modulation",
    )(cc, mod_w, mod_b.reshape(depth, 1, d3))


def _even_in_kernel(x_ref, sh_ref, sc_ref, nw_ref, w_ref, cos_ref, sin_ref,
                    k_ref, vt_ref, q_ref, bb_ref, u_ref, z_ref, *, rope):
    lane = lax.broadcasted_iota(jnp.int32, (1, LANES), 1)
    take_up = (lane % A_HEAD_DIM) < (A_HEAD_DIM // 2)
    for r0 in range(0, x_ref.shape[0], SUB_ROWS):
        rows = slice(r0, r0 + SUB_ROWS)
        h = (_rms(x_ref[rows, :], nw_ref[...]) * (1.0 + sc_ref[...]) + sh_ref[...]).astype(BF16)
        y_all = _dot(h, w_ref[...])

        def proj(a, b):
            return y_all[:, a:b]

        if rope:
            cos, sin = cos_ref[rows, :], sin_ref[rows, :]
            rot = lambda t: _rope_slab(t, cos, sin, A_HEAD_DIM // 2, take_up)
        else:
            rot = lambda t: t
        k_ref[rows, :] = rot(proj(EC_K, EC_V)).astype(BF16)
        vt_ref[:, rows] = jnp.transpose(proj(EC_V, EC_Q)).astype(BF16)
        q = proj(EC_Q, EC_BB)
        for j in range(A_WIDTH // LANES):
            sl = slice(j * LANES, (j + 1) * LANES)
            q_ref[rows, sl] = (rot(q[:, sl]) * (A_HEAD_DIM ** -0.5 * LOG2E)).astype(BF16)
        bb_ref[rows, :] = proj(EC_BB, EC_BC).astype(BF16)
        u_ref[rows, :] = (proj(EC_BC, EC_BX) * proj(EC_BX, EC_Z)).astype(BF16)
        z_ref[rows, :] = _silu(proj(EC_Z, EC_END)).astype(BF16)


def _even_in(x, sh, sc, nw, w, cos, sin, *, rope, tm):
    bsz, t, d = x.shape
    assert t % tm == 0 and tm % SUB_ROWS == 0
    nb = sh.shape[0]
    mod_map = (lambda b, i: (b, 0, 0)) if nb > 1 else (lambda b, i: (0, 0, 0))
    tok = lambda width: pl.BlockSpec((None, tm, width), lambda b, i: (b, i, 0))
    const = lambda shape: pl.BlockSpec(shape, lambda b, i: (0,) * len(shape))
    widths = (LANES, None, A_WIDTH, B_WIDTH, B_WIDTH, E_GATE)
    vt_spec = pl.BlockSpec((None, LANES, tm), lambda b, i: (b, 0, i))
    vt_shape = jax.ShapeDtypeStruct((bsz, LANES, t), BF16)
    return pl.pallas_call(
        functools.partial(_even_in_kernel, rope=rope),
        grid=(bsz, t // tm),
        in_specs=[
            tok(d),
            pl.BlockSpec((None, 1, d), mod_map),
            pl.BlockSpec((None, 1, d), mod_map),
            const((1, d)),
            _resident(w.shape),
            pl.BlockSpec((tm, LANES), lambda b, i: (i, 0)),
            pl.BlockSpec((tm, LANES), lambda b, i: (i, 0)),
        ],
        out_specs=[vt_spec if wd is None else tok(wd) for wd in widths],
        out_shape=[vt_shape if wd is None else jax.ShapeDtypeStruct((bsz, t, wd), BF16) for wd in widths],
        compiler_params=_cparams(("parallel", "parallel")),
        name="even_in_rope" if rope else "even_in",
    )(x, sh, sc, nw, w, cos, sin)


AttnStage = collections.namedtuple("AttnStage", "q k vt mask sink")


def _tree_max(parts):
    while len(parts) > 1:
        parts = [jnp.maximum(parts[i], parts[i + 1]) if i + 1 < len(parts) else parts[i]
                 for i in range(0, len(parts), 2)]
    return parts[0]


def _attention_pipeline(stages, n_chunks, s_ref, write_pair):
    kc, sr = ATT_KEYS, SCORE_ROWS
    slots, n_keys = s_ref.shape[0], n_chunks * kc
    lag = slots - 1
    ones = jnp.ones((BF16_ROWS, n_keys), BF16)
    m8 = {}
    halves = []
    for i in range(len(stages) + lag):
        if i < len(stages):
            cur = stages[i]
            q = cur.q()
            parts = []
            for c in range(n_chunks):
                kch, mask = cur.k(c), cur.mask(c)
                for r in range(0, kc, sr):
                    st = _nt(kch[r:r + sr], q)
                    if mask is not None:
                        st = jnp.where(mask[r:r + sr], st, NEG)
                    s_ref[i % slots, c * kc + r:c * kc + r + sr, :] = st
                    parts += [st[t:t + 8, :] for t in range(0, sr, 8)]
            m8[i] = _tree_max(parts)
        if i >= lag:
            prev = stages[i - lag]
            m_prev = jnp.max(m8.pop(i - lag), axis=0, keepdims=True)
            if prev.sink is not None:
                m_prev = jnp.maximum(m_prev, prev.sink)
            pt = jnp.exp2(s_ref[(i - lag) % slots, 0:n_keys, :] - m_prev).astype(BF16)
            vta = jnp.concatenate([prev.vt(c) for c in range(n_chunks)], axis=1)
            acc = _dot(jnp.concatenate([vta, ones], axis=0), pt)
            denom = acc[ATT_VDIM:ATT_VDIM + 1]
            if prev.sink is not None:
                denom = denom + jnp.exp2(prev.sink - m_prev)
            halves.append(acc[0:ATT_VDIM] / denom)
            if len(halves) == 2:
                write_pair((i - lag) // 2, jnp.concatenate(halves, axis=0))
                halves = []


def _even_attn_kernel(*refs, window, n_lat):
    if window:
        (sink_ref, q_ref, z_ref, kp_ref, km_ref, kn_ref, kc_ref, vtp_ref, vtm_ref, vtn_ref, vtc_ref,
         o_ref, s_ref) = refs
    else:
        sink_ref, q_ref, z_ref, kc_ref, vtc_ref, o_ref, s_ref = refs
    nq, kc = ATT_Q, ATT_KEYS
    step, half = q_ref.shape[0], WINDOW
    lane = lax.broadcasted_iota(jnp.int32, (1, LANES), 1)
    lo = lane < A_HEAD_DIM
    ctx_k = [kc_ref[r:r + kc, :] for r in range(0, kc_ref.shape[0], kc)]
    ctx_vt = [vtc_ref[:, r:r + kc] for r in range(0, kc_ref.shape[0], kc)]
    r = lax.broadcasted_iota(jnp.int32, (kc, nq), 0)
    c = lax.broadcasted_iota(jnp.int32, (kc, nq), 1)

    def sub_block(qb):
        k_chunks, vt_chunks, masks = list(ctx_k), list(ctx_vt), [None] * len(ctx_k)
        if window:
            base, q0 = qb * nq, pl.program_id(1) * step + qb * nq
            if qb == 0:
                k1 = jnp.concatenate([kp_ref[...], km_ref[0:half, :]], axis=0)
                vt1 = jnp.concatenate([vtp_ref[...], vtm_ref[:, 0:half]], axis=1)
            else:
                k1, vt1 = km_ref[base - half:base + half, :], vtm_ref[:, base - half:base + half]
            if base + nq == step:
                k2 = jnp.concatenate([km_ref[base + half:step, :], kn_ref[...]], axis=0)
                vt2 = jnp.concatenate([vtm_ref[:, base + half:step], vtn_ref[...]], axis=1)
            else:
                k2, vt2 = km_ref[base + half:base + half + kc, :], vtm_ref[:, base + half:base + half + kc]
            k_chunks += [k1, k2]
            vt_chunks += [vt1, vt2]
            masks += [(jnp.abs(r - half - c) <= WINDOW) & (r + (q0 - half) >= 0),
                      (jnp.abs(r + half - c) <= WINDOW) & (r + (q0 + half) < n_lat)]
        return k_chunks, vt_chunks, masks

    def stage(qb, head, chunks):
        k_chunks, vt_chunks, masks = chunks
        j, grp = head % (A_HEADS // 2), head // (A_HEADS // 2)
        keep = lo if grp == 0 else jnp.logical_not(lo)

        def q():
            qs = q_ref[qb * nq:(qb + 1) * nq, j * LANES:(j + 1) * LANES]
            return jnp.where(keep, qs, jnp.zeros_like(qs))

        return AttnStage(q=q, k=lambda ci: k_chunks[ci],
                         vt=lambda ci: vt_chunks[ci][grp * A_HEAD_DIM:(grp + 1) * A_HEAD_DIM, :],
                         mask=lambda ci: masks[ci], sink=sink_ref[head] * LOG2E)

    def write_pair(p, o_t):
        qb, j = divmod(p, A_HEADS // 2)
        blk = (slice(qb * nq, (qb + 1) * nq), slice(j * LANES, (j + 1) * LANES))
        o_ref[blk] = jnp.transpose(o_t).astype(BF16) * z_ref[blk]

    stages = []
    for qb in range(step // nq):
        chunks = sub_block(qb)
        stages += [stage(qb, h, chunks) for h in _A_HEAD_ORDER]
    _attention_pipeline(stages, len(ctx_k) + (2 if window else 0), s_ref, write_pair)


def _even_attn(sink, q, z, k, vt, kc, vtc, *, window):
    bsz, t, _ = q.shape
    n_ctx = kc.shape[1]
    nq, keys, half = ATT_Q, ATT_KEYS, WINDOW
    step = min(ATT_STEP, t)
    assert nq == keys == 2 * half and t % step == 0 and n_ctx % keys == 0
    per_step, last = step // half, t // half - 1
    smem = pl.BlockSpec(memory_space=pltpu.SMEM)
    qspec = pl.BlockSpec((None, step, A_WIDTH), lambda b, i: (b, i, 0))
    kcspec = pl.BlockSpec((None, n_ctx, LANES), lambda b, i: (b, 0, 0))
    vtcspec = pl.BlockSpec((None, LANES, n_ctx), lambda b, i: (b, 0, 0))
    if window:
        prev = lambda i: jnp.maximum(per_step * i - 1, 0)
        nxt = lambda i: jnp.minimum(per_step * (i + 1), last)
        specs = [smem, qspec, qspec,
                 pl.BlockSpec((None, half, LANES), lambda b, i: (b, prev(i), 0)),
                 pl.BlockSpec((None, step, LANES), lambda b, i: (b, i, 0)),
                 pl.BlockSpec((None, half, LANES), lambda b, i: (b, nxt(i), 0)),
                 kcspec,
                 pl.BlockSpec((None, LANES, half), lambda b, i: (b, 0, prev(i))),
                 pl.BlockSpec((None, LANES, step), lambda b, i: (b, 0, i)),
                 pl.BlockSpec((None, LANES, half), lambda b, i: (b, 0, nxt(i))),
                 vtcspec]
        args = (sink, q, z, k, k, k, kc, vt, vt, vt, vtc)
    else:
        specs = [smem, qspec, qspec, kcspec, vtcspec]
        args = (sink, q, z, kc, vtc)
    n_chunks = n_ctx // keys + (2 if window else 0)
    return pl.pallas_call(
        functools.partial(_even_attn_kernel, window=window, n_lat=t),
        grid=(bsz, t // step),
        in_specs=specs,
        out_specs=qspec,
        out_shape=jax.ShapeDtypeStruct((bsz, t, A_WIDTH), BF16),
        scratch_shapes=[pltpu.VMEM((SCORE_SLOTS, n_chunks * keys, nq), F32)],
        compiler_params=_cparams(("parallel", "parallel")),
        name="even_attn_window" if window else "even_attn_ctx",
    )(*args)


def _out_kernel(*refs, conv):
    if conv:
        a_ref, bb_ref, u_ref, up_ref, un_ref, z_ref, cw_ref, w_ref, pw_ref, g_ref, x_ref, o_ref = refs
        tm = u_ref.shape[0]
        i = pl.program_id(1)
        u = u_ref[...].astype(F32)
        prev_row = jnp.where(i > 0, up_ref[BF16_ROWS - 1:BF16_ROWS, :].astype(F32), 0.0)
        next_row = jnp.where(i < pl.num_programs(1) - 1, un_ref[0:1, :].astype(F32), 0.0)
        row = lax.broadcasted_iota(jnp.int32, (tm, 1), 0)
        u_m1 = jnp.where(row == 0, prev_row, pltpu.roll(u, 1, axis=0))
        u_p1 = jnp.where(row == tm - 1, next_row, pltpu.roll(u, tm - 1, axis=0))
        cw = cw_ref[...]
        mix_b = (bb_ref[...].astype(F32) * (u_m1 * cw[0:1, :] + u * cw[1:2, :] + u_p1 * cw[2:3, :])).astype(BF16)
        mix_b = mix_b * z_ref[...]
    else:
        a_ref, b_ref, w_ref, pw_ref, g_ref, x_ref, o_ref = refs
        mix_b = b_ref
    for r0 in range(0, x_ref.shape[0], SUB_ROWS):
        rows = slice(r0, r0 + SUB_ROWS)
        gated = jnp.concatenate([a_ref[rows, :], mix_b[rows, :]], axis=1)
        o_ref[rows, :] = x_ref[rows, :] + g_ref[...] * _rms(_dot(gated, w_ref[...]), pw_ref[...])


def _out_proj(mix, z, cw, w, pw, g, x, *, tm):
    conv = cw is not None
    bsz, t, d = x.shape
    assert t % tm == 0 and tm % SUB_ROWS == 0
    nb = g.shape[0]
    mod_map = (lambda b, i: (b, 0, 0)) if nb > 1 else (lambda b, i: (0, 0, 0))
    tok = lambda width: pl.BlockSpec((None, tm, width), lambda b, i: (b, i, 0))
    const = lambda shape: pl.BlockSpec(shape, lambda b, i: (0,) * len(shape))
    hb = BF16_ROWS
    per_tile, last = tm // hb, t // hb - 1
    if conv:
        a, bb, u = mix
        specs = [tok(a.shape[2]), tok(bb.shape[2]), tok(u.shape[2]),
                 pl.BlockSpec((None, hb, u.shape[2]), lambda b, i: (b, jnp.maximum(i * per_tile - 1, 0), 0)),
                 pl.BlockSpec((None, hb, u.shape[2]), lambda b, i: (b, jnp.minimum((i + 1) * per_tile, last), 0)),
                 pl.BlockSpec((None, tm, bb.shape[2]), lambda b, i: (b, i, z.shape[2] // bb.shape[2] - 1)),
                 const(cw.shape)]
        args = (a, bb, u, u, u, z, cw)
    else:
        a, b2 = mix
        specs = [tok(a.shape[2]), tok(b2.shape[2])]
        args = (a, b2)
    specs += [_resident(w.shape), const((1, d)), pl.BlockSpec((None, 1, d), mod_map), tok(d)]
    return pl.pallas_call(
        functools.partial(_out_kernel, conv=conv),
        grid=(bsz, t // tm),
        in_specs=specs,
        out_specs=tok(d),
        out_shape=jax.ShapeDtypeStruct((bsz, t, d), F32),
        compiler_params=_cparams(("parallel", "parallel")),
        name="out_proj_conv" if conv else "out_proj",
    )(*args, w, pw, g, x)


def _odd_in_kernel(*refs, lat):
    if lat:
        (x_ref, sh_ref, sc_ref, nw_ref, w_ref, kvn_ref, wk_ref, wvt_ref, wmvt_ref, cosk_ref, sink_ref,
         qn_ref, wq_ref, cosq_ref, sinq_ref,
         kf_ref, vt_ref, mk_ref, mvt_ref, g2_ref, qf_ref, mq_ref, mo_ref, z_ref) = refs
    else:
        (x_ref, sh_ref, sc_ref, nw_ref, w_ref, kvn_ref, wk_ref, wvt_ref, wmvt_ref,
         kf_ref, vt_ref, mk_ref, mvt_ref, g2_ref) = refs
    lane = lax.broadcasted_iota(jnp.int32, (1, LANES), 1)
    chunk = mvt_ref.shape[2]
    per_sub = SUB_ROWS // chunk
    for cc in range(x_ref.shape[0] // SUB_ROWS):
        rows = slice(cc * SUB_ROWS, (cc + 1) * SUB_ROWS)
        h = (_rms(x_ref[rows, :], nw_ref[...]) * (1.0 + sc_ref[...]) + sh_ref[...]).astype(BF16)
        y_all = _dot(h, w_ref[...])

        def proj(a, b):
            return y_all[:, a:b]

        ckv = _rms(proj(OC_CKV, OC_CKV + C_KV_LORA), kvn_ref[...]).astype(BF16)
        g2 = proj(OC_G2, OC_G2 + LANES)
        g2_ref[rows, :] = g2
        if lat:
            g2 = _rope_slab(g2, cosk_ref[rows, :], sink_ref[rows, :], C_ROPE // 2, lane < C_ROPE // 2)
        rope_lanes = (lane >= C_NOPE) & (lane < C_NOPE + C_ROPE)
        kr = jnp.where(rope_lanes, pltpu.roll(g2, C_NOPE, axis=1), 0.0)
        k_nope = _dot(ckv, wk_ref[...])
        for hd in range(C_HEADS):
            sl = slice(hd * LANES, (hd + 1) * LANES)
            kf_ref[rows, sl] = (k_nope[:, sl] + kr).astype(BF16)
        vt_ref[:, rows] = _nt(wvt_ref[...], ckv).astype(BF16)
        mk_ref[rows, :] = (proj(OC_MK, OC_MK + M_QKW) * (M_QK ** -0.5)).astype(BF16)
        mvt = _nt(wmvt_ref[...], h).astype(BF16)
        for s in range(per_sub):
            mvt_ref[cc * per_sub + s] = mvt[:, s * chunk:(s + 1) * chunk]
        if lat:
            cq = _rms(proj(OC_CQ, OC_CQ + C_Q_LORA), qn_ref[...]).astype(BF16)
            qf = _dot(cq, wq_ref[...])
            cos, sin = cosq_ref[rows, :], sinq_ref[rows, :]
            take_up = (lane >= C_NOPE) & (lane < C_NOPE + C_ROPE // 2)
            scale = (C_NOPE + C_ROPE) ** -0.5 * LOG2E
            for hd in range(C_HEADS):
                sl = slice(hd * LANES, (hd + 1) * LANES)
                qf_ref[rows, sl] = (_rope_slab(qf[:, sl], cos, sin, C_ROPE // 2, take_up) * scale).astype(BF16)
            mq_ref[rows, :] = proj(OC_MQ, OC_MQ + M_QKW).astype(BF16)
            mo_ref[rows, :] = jax.nn.sigmoid(proj(OC_MO, OC_MO + M_WIDTH)).astype(BF16)
            z_ref[rows, :] = _silu(proj(OC_Z, OC_Z + O_GATE)).astype(BF16)


def _odd_in(x, sh, sc, nw, w, kvn, wk, wv, wmvt, lat_args, *, tm, chunk):
    lat = lat_args is not None
    bsz, t, d = x.shape
    assert t % tm == 0 and tm % SUB_ROWS == 0 and SUB_ROWS % chunk == 0
    nb = sh.shape[0]
    mod_map = (lambda b, i: (b, 0, 0)) if nb > 1 else (lambda b, i: (0, 0, 0))
    tok = lambda width: (pl.BlockSpec((None, tm, width), lambda b, i: (b, i, 0)), (bsz, t, width))
    const = lambda shape: pl.BlockSpec(shape, lambda b, i: (0,) * len(shape))
    tab = pl.BlockSpec((tm, LANES), lambda b, i: (i, 0))
    specs = [tok(d)[0], pl.BlockSpec((None, 1, d), mod_map), pl.BlockSpec((None, 1, d), mod_map),
             const((1, d)), _resident(w.shape), const(kvn.shape), _resident(wk.shape), _resident(wv.shape),
             _resident(wmvt.shape)]
    args = [x, sh, sc, nw, w, kvn, wk, wv, wmvt]
    vt = (pl.BlockSpec((None, C_WIDTH, tm), lambda b, i: (b, 0, i)), (bsz, C_WIDTH, t))
    mvt = (pl.BlockSpec((None, tm // chunk, M_WIDTH, chunk), lambda b, i: (b, i, 0, 0)),
           (bsz, t // chunk, M_WIDTH, chunk))
    outs = [(tok(C_HEADS * LANES), BF16), (vt, BF16), (tok(M_QKW), BF16), (mvt, BF16), (tok(LANES), F32)]
    if lat:
        cosk, sink, qn, wq, cosq, sinq = lat_args
        specs += [tab, tab, const(qn.shape), _resident(wq.shape), tab, tab]
        args += [cosk, sink, qn, wq, cosq, sinq]
        outs += [(tok(C_HEADS * LANES), BF16), (tok(M_QKW), BF16), (tok(M_WIDTH), BF16), (tok(O_GATE), BF16)]
    return pl.pallas_call(
        functools.partial(_odd_in_kernel, lat=lat),
        grid=(bsz, t // tm),
        in_specs=specs,
        out_specs=[o[0][0] for o in outs],
        out_shape=[jax.ShapeDtypeStruct(o[0][1], o[1]) for o in outs],
        compiler_params=_cparams(("parallel", "parallel")),
        name="odd_in_lat" if lat else "odd_in_ctx",
    )(*args)


def _mla_kernel(q_ref, z_ref, kc_ref, vtc_ref, *rest):
    n_pairs = C_HEADS // 2
    kx_refs, vtx_refs, (o_ref, s_ref) = rest[:n_pairs], rest[n_pairs:2 * n_pairs], rest[2 * n_pairs:]
    nq, kc = ATT_Q, ATT_KEYS
    n_ctx_chunks = kc_ref.shape[0] // kc
    n_chunks = n_ctx_chunks + kx_refs[0].shape[0] // kc

    def stage(qb, hd):
        sl = slice(hd * LANES, (hd + 1) * LANES)
        half = hd % 2

        def k(ci):
            if ci < n_ctx_chunks:
                return kc_ref[ci * kc:(ci + 1) * kc, sl]
            r0 = (ci - n_ctx_chunks) * kc
            return kx_refs[hd // 2][r0:r0 + kc, half * LANES:(half + 1) * LANES]

        def vt(ci):
            if ci < n_ctx_chunks:
                return vtc_ref[hd * C_VDIM:(hd + 1) * C_VDIM, ci * kc:(ci + 1) * kc]
            r0 = (ci - n_ctx_chunks) * kc
            return vtx_refs[hd // 2][half * C_VDIM:(half + 1) * C_VDIM, r0:r0 + kc]

        return AttnStage(q=lambda: q_ref[qb * nq:(qb + 1) * nq, sl], k=k, vt=vt, mask=lambda ci: None, sink=None)

    def write_pair(p, o_t):
        qb, j = divmod(p, C_HEADS // 2)
        blk = (slice(qb * nq, (qb + 1) * nq), slice(j * LANES, (j + 1) * LANES))
        o_ref[blk] = jnp.transpose(o_t).astype(BF16) * z_ref[blk]

    stages = [stage(qb, hd) for qb in range(q_ref.shape[0] // nq) for hd in range(C_HEADS)]
    _attention_pipeline(stages, n_chunks, s_ref, write_pair)


def _mla(qf, z, kfc, kfx, vtc, vtx):
    bsz, t, wq = qf.shape
    n_ctx = kfc.shape[1]
    blk = MLA_BLK
    assert t % blk == 0 and blk % ATT_Q == 0 and n_ctx % ATT_KEYS == 0 and t % ATT_KEYS == 0
    n_pairs = C_HEADS // 2
    pair_k = lambda j: pl.BlockSpec((None, t, 2 * LANES), lambda b, i: (b, 0, j))
    pair_vt = lambda j: pl.BlockSpec((None, 2 * C_VDIM, t), lambda b, i: (b, j, 0))
    return pl.pallas_call(
        _mla_kernel,
        grid=(bsz, t // blk),
        in_specs=[
            pl.BlockSpec((None, blk, wq), lambda b, i: (b, i, 0)),
            pl.BlockSpec((None, blk, C_WIDTH), lambda b, i: (b, i, 0)),
            pl.BlockSpec((None, n_ctx, wq), lambda b, i: (b, 0, 0)),
            pl.BlockSpec((None, C_WIDTH, n_ctx), lambda b, i: (b, 0, 0)),
        ] + [pair_k(j) for j in range(n_pairs)] + [pair_vt(j) for j in range(n_pairs)],
        out_specs=pl.BlockSpec((None, blk, C_WIDTH), lambda b, i: (b, i, 0)),
        out_shape=jax.ShapeDtypeStruct((bsz, t, C_WIDTH), BF16),
        scratch_shapes=[pltpu.VMEM((SCORE_SLOTS, n_ctx + t, ATT_Q), F32)],
        compiler_params=_cparams(("parallel", "arbitrary")),
        name="mla_attention",
    )(qf, z, kfc, vtc, *([kfx] * n_pairs), *([vtx] * n_pairs))


def _mlstm_kernel(mq_ref, mk_ref, mvt_ref, g_ref, mo_ref, z_ref, mkc_ref, mvtc_ref, gc_ref, bias_ref, hnw_ref,
                  out_ref, s_ref, m_ref, hf_ref, hr_ref):
    nc, ncc, L = mvt_ref.shape[0], mvtc_ref.shape[0], mvt_ref.shape[2]
    row = lax.broadcasted_iota(jnp.int32, (L, L), 0)
    col = lax.broadcasted_iota(jnp.int32, (L, L), 1)
    vis = (row <= col, row >= col)
    tri_ones = (col <= row).astype(BF16)
    lane = lax.broadcasted_iota(jnp.int32, (1, LANES), 1)
    lo = lane < M_QK
    ones_rows = jnp.ones((M_ONES, L), BF16)
    fwd_lanes = lane < GATE_LANE0 + 2 * M_HEADS
    tok16 = lax.broadcasted_iota(jnp.int32, (4 * M_HEADS, L), 1)
    fwd_rows16 = lax.broadcasted_iota(jnp.int32, (4 * M_HEADS, L), 0) < 2 * M_HEADS
    pair = lambda hd: slice((hd // 2) * LANES, (hd // 2 + 1) * LANES)
    vrows = lambda hd: slice(hd * M_V, (hd + 1) * M_V)

    def gates(g, with_max=True):
        gb = g + bias_ref[...]
        ls = jnp.minimum(gb, 0.0) - jnp.log1p(jnp.exp(-jnp.abs(gb)))
        hi = ls.astype(BF16)
        lo = (ls - hi.astype(F32)).astype(BF16)
        cf = _dot(tri_ones, hi) + _dot(tri_ones, lo)
        tot = cf[L - 1:L, :]
        b = jnp.where(fwd_lanes, cf, tot - cf + ls)
        li = pltpu.roll(gb, M_HEADS, axis=1)
        gate_rows = slice(GATE_LANE0, GATE_LANE0 + 4 * M_HEADS)
        b_t, li_t = jnp.transpose(b)[gate_rows], jnp.transpose(li)[gate_rows]
        cm = li_t - b_t
        sh = 1 if with_max else L
        while sh < L:
            earlier = jnp.where(tok16 >= sh, pltpu.roll(cm, sh, axis=1), NEG)
            later = jnp.where(tok16 < L - sh, pltpu.roll(cm, L - sh, axis=1), NEG)
            cm = jnp.maximum(cm, jnp.where(fwd_rows16, earlier, later))
            sh *= 2
        return b - li, tot, b_t, li_t, cm

    def head_lanes(hd, pair_slab):
        keep = lo if hd % 2 == 0 else jnp.logical_not(lo)
        return jnp.where(keep, pair_slab, jnp.zeros_like(pair_slab))

    def scores(u, kh, qh):
        return _nt(jnp.concatenate([kh, s_ref[u].astype(BF16)], axis=0), qh)

    def finish(d, hd, gq, kh, vt_h, big):
        r, tot, b_t, li_t, cm = gq
        u = d * M_HEADS + hd
        fl = GATE_LANE0 + d * 2 * M_HEADS + M_HEADS + hd
        gr = slice(fl - GATE_LANE0, fl - GATE_LANE0 + 1)
        b_row, li_row, cm_row = b_t[gr, :], li_t[gr, :], cm[gr, :]
        g, r_col = tot[:, fl:fl + 1], r[:, fl:fl + 1]
        vaug = jnp.concatenate([vt_h, ones_rows], axis=0)
        s_in = s_ref[u]
        m_in = m_ref[u][0:1, 0:1]
        h_t = None
        if big is not None:
            inter = b_row + m_in
            m_t = jnp.maximum(inter, b_row + cm_row)
            sc = big[0:L] * jnp.exp(jnp.where(vis[d], (b_row - m_t) - r_col, NEG))
            res = jnp.exp(inter - m_t) * big[L:] + _dot(vaug, sc.astype(BF16))
            den = jnp.maximum(jnp.abs(res[M_V:M_V + 1]), jnp.exp(-m_t))
            h_t = res[0:M_V] / den
        a_row = g - b_row + li_row
        m_loc = jnp.max(a_row, axis=1, keepdims=True)
        wv = (jnp.exp(a_row - m_loc) * vaug.astype(F32)).astype(BF16)
        s_loc = _dot(wv, kh)
        m_new = jnp.maximum(g + m_in, m_loc)
        s_ref[u] = jnp.exp(g + m_in - m_new) * s_in + jnp.exp(m_loc - m_new) * s_loc
        m_ref[u] = jnp.broadcast_to(m_new, (8, LANES))
        return h_t

    def chunk_gates(jj):
        return gates(g_ref[pl.ds(pl.multiple_of(jj * L, L), L), :])

    first_gates = (chunk_gates(0), chunk_gates(nc - 1))
    s_ref[...] = jnp.zeros_like(s_ref)
    m_ref[...] = jnp.zeros_like(m_ref)
    ctx_gates = {}
    for d in range(2):
        for cc in range(ncc):
            jj = cc if d == 0 else ncc - 1 - cc
            if jj not in ctx_gates:
                ctx_gates[jj] = gates(gc_ref[jj * L:(jj + 1) * L, :], with_max=False)
            gq = ctx_gates[jj]
            for hd in range(M_HEADS):
                kh = head_lanes(hd, mkc_ref[jj * L:(jj + 1) * L, pair(hd)])
                finish(d, hd, gq, kh, mvtc_ref[jj, vrows(hd), :], None)

    def body(j, gq):
        chunk_of = (j, nc - 1 - j)
        rows = [pl.ds(pl.multiple_of(jj * L, L), L) for jj in chunk_of]
        gq_next = (chunk_gates(jnp.minimum(j + 1, nc - 1)), chunk_gates(jnp.maximum(nc - 2 - j, 0)))
        todo = []
        for d in range(2):
            for hd in range(M_HEADS):
                kh = head_lanes(hd, mk_ref[rows[d], pair(hd)])
                big = scores(d * M_HEADS + hd, kh, head_lanes(hd, mq_ref[rows[d], pair(hd)]))
                dst = hf_ref if d == 0 else hr_ref

                def fin(d=d, hd=hd, kh=kh, big=big, dst=dst):
                    dst[chunk_of[d], hd] = finish(d, hd, gq[d], kh, mvt_ref[chunk_of[d], vrows(hd), :], big)
                todo.append(fin)
                if len(todo) > SCORES_AHEAD:
                    todo.pop(0)()
        for fin in todo:
            fin()
        return gq_next

    lax.fori_loop(0, nc, body, first_gates)

    def readout(j, carry):
        rows = pl.ds(pl.multiple_of(j * L, L), L)
        for hd in range(M_HEADS):
            hs_t = hf_ref[j, hd] + hr_ref[j, hd]
            ms = jnp.mean(hs_t * hs_t, axis=0, keepdims=True)
            hs = jnp.transpose(hs_t * lax.rsqrt(ms + NORM_EPS)) * hnw_ref[:, vrows(hd)]
            m_out = (mo_ref[rows, vrows(hd)].astype(F32) * hs).astype(BF16)
            out_ref[rows, vrows(hd)] = m_out * z_ref[rows, vrows(hd)]
        return carry

    lax.fori_loop(0, nc, readout, 0, unroll=4)


def _mlstm(mq, mk, mvt, g2, mo, z, mkc, mvtc, g2c, bias_row, hnw):
    bsz, t, _ = mq.shape
    n_ctx = mkc.shape[1]
    nc, _, chunk = mvt.shape[1:]
    ncc = mvtc.shape[1]
    per_b = lambda rows, width: pl.BlockSpec((None, rows, width), lambda b: (b, 0, 0))
    chunked = lambda n: pl.BlockSpec((None, n, M_WIDTH, chunk), lambda b: (b, 0, 0, 0))
    const = lambda shape: pl.BlockSpec(shape, lambda b: (0,) * len(shape))
    return pl.pallas_call(
        _mlstm_kernel,
        grid=(bsz,),
        in_specs=[per_b(t, M_QKW), per_b(t, M_QKW), chunked(nc), per_b(t, LANES), per_b(t, M_WIDTH),
                  pl.BlockSpec((None, t, M_WIDTH), lambda b: (b, 0, z.shape[2] // M_WIDTH - 1)),
                  per_b(n_ctx, M_QKW), chunked(ncc), per_b(n_ctx, LANES),
                  const((1, LANES)), const((1, M_WIDTH))],
        out_specs=per_b(t, M_WIDTH),
        out_shape=jax.ShapeDtypeStruct((bsz, t, M_WIDTH), BF16),
        scratch_shapes=[pltpu.VMEM((2 * M_HEADS, M_V + M_ONES, LANES), F32),
                        pltpu.VMEM((2 * M_HEADS, 8, LANES), F32),
                        pltpu.VMEM((nc, M_HEADS, M_V, chunk), F32),
                        pltpu.VMEM((nc, M_HEADS, M_V, chunk), F32)],
        compiler_params=_cparams(("parallel",)),
        name="mlstm",
    )(mq, mk, mvt, g2, mo, z, mkc, mvtc, g2c, bias_row, hnw)


def _axial_angles(n_lat, rot_dim):
    rows = n_lat // GRID_W
    row = jnp.repeat(jnp.arange(rows), GRID_W).astype(F32)
    col = jnp.tile(jnp.arange(GRID_W), rows).astype(F32)
    n_freq = rot_dim // 4
    inv = ROPE_THETA ** (-jnp.arange(n_freq, dtype=F32) / n_freq)
    ang_t = jnp.concatenate([inv[:, None] * row[None, :], inv[:, None] * col[None, :]], axis=0)
    return jnp.transpose(jnp.cos(ang_t)), jnp.transpose(jnp.sin(ang_t))


def _rope_tables(n_lat):
    cos, sin = _axial_angles(n_lat, A_HEAD_DIM)
    cos_a = jnp.tile(jnp.concatenate([cos, cos], axis=-1), (1, 2))
    sin_a = jnp.tile(jnp.concatenate([-sin, sin], axis=-1), (1, 2))
    cos, sin = _axial_angles(n_lat, C_ROPE)
    one = lambda w: jnp.ones((n_lat, w), F32)
    zero = lambda w: jnp.zeros((n_lat, w), F32)
    cos_k = jnp.concatenate([cos, cos, one(LANES - C_ROPE)], axis=-1)
    sin_k = jnp.concatenate([-sin, sin, zero(LANES - C_ROPE)], axis=-1)
    cos_q = jnp.concatenate([one(C_NOPE), cos, cos, one(LANES - C_NOPE - C_ROPE)], axis=-1)
    sin_q = jnp.concatenate([zero(C_NOPE), -sin, sin, zero(LANES - C_NOPE - C_ROPE)], axis=-1)
    return (cos_a, sin_a), (cos_k, sin_k), (cos_q, sin_q)


_A_HEAD_ORDER = (0, 4, 1, 5, 2, 6, 3, 7)


def _perm_heads(w, axis):
    shape = w.shape
    grp = A_HEADS // A_KV_HEADS
    w = w.reshape(shape[:axis] + (A_KV_HEADS, grp, A_HEAD_DIM) + shape[axis + 1:])
    return jnp.swapaxes(w, axis, axis + 1).reshape(shape)


def _even_weights(w_in, w_out):
    offs = np.cumsum((0,) + E_COLS)
    parts = [w_in[:, offs[i]:offs[i + 1]] for i in range(len(E_COLS))]
    parts[2] = _perm_heads(parts[2], 1)
    z = parts[6]
    parts[6] = jnp.concatenate([_perm_heads(z[:, :A_WIDTH], 1), z[:, A_WIDTH:]], axis=1)
    w_out_p = jnp.concatenate([_perm_heads(w_out[:A_WIDTH], 0), w_out[A_WIDTH:]], axis=0)
    return jnp.concatenate(parts, axis=1).astype(BF16), w_out_p.astype(BF16)


def _odd_weights(w_in, w_uq, w_ukv):
    d = w_in.shape[0]
    offs = np.cumsum((0,) + O_COLS)
    ckv, kr, mk, mv, mg, cq, mq, mo, z = [w_in[:, offs[i]:offs[i + 1]] for i in range(len(O_COLS))]
    g2 = jnp.concatenate([kr, mg, jnp.zeros((d, LANES - C_ROPE - 4 * M_HEADS), w_in.dtype)], axis=1)
    w_p = jnp.concatenate([ckv, g2, mk, cq, mq, mo, z], axis=1).astype(BF16)
    wmvt = jnp.transpose(mv).astype(BF16)
    ukv = w_ukv.reshape(C_KV_LORA, C_HEADS, C_NOPE + C_VDIM)
    wk = jnp.pad(ukv[:, :, :C_NOPE], ((0, 0), (0, 0), (0, LANES - C_NOPE)))
    wk = wk.reshape(C_KV_LORA, C_HEADS * LANES).astype(BF16)
    wv = jnp.transpose(ukv[:, :, C_NOPE:].reshape(C_KV_LORA, C_WIDTH)).astype(BF16)
    uq = w_uq.reshape(C_Q_LORA, C_HEADS, C_NOPE + C_ROPE)
    wq = jnp.pad(uq, ((0, 0), (0, 0), (0, LANES - C_NOPE - C_ROPE))).reshape(C_Q_LORA, C_HEADS * LANES).astype(BF16)
    return w_p, wk, wv, wmvt, wq


def kernel(x, c, ctx, c_ctx, mod_w, mod_b, pre_norm_w, post_norm_w, e_w_in, e_sink, e_conv_w, e_w_out,
           o_w_in, o_q_norm_w, o_kv_norm_w, o_w_uq, o_w_ukv, o_i_bias, o_f_bias, o_head_norm_w, o_w_out):
    bsz, n_lat, d = x.shape
    n_ctx = ctx.shape[1]
    assert mod_w.shape[0] == 2 and d == D_MODEL, "built for one even + one odd layer"
    assert n_lat % 1024 == 0 and n_ctx % 256 == 0
    tm_in, tm_ctx = 1024, 256

    pad = (-(bsz + 1)) % 8
    cc = jnp.concatenate([c, c_ctx[None, :], jnp.zeros((pad, d), F32)], axis=0)
    mod = _modulation(cc, mod_w, mod_b)
    split = lambda l, r0, r1: [mod[l, r0:r1, k * d:(k + 1) * d][:, None, :] for k in range(3)]
    (rope_a, rope_k, rope_q) = _rope_tables(n_lat)
    dummy_tab = jnp.zeros((n_ctx, LANES), F32)

    sh_x, sc_x, g_x = split(0, 0, bsz)
    sh_c, sc_c, g_c = split(0, bsz, bsz + 1)
    w_in, w_out = _even_weights(e_w_in[0], e_w_out[0])
    nw, pw = pre_norm_w[0][None, :], post_norm_w[0][None, :]
    k, vt, q, bb, u, z = _even_in(x, sh_x, sc_x, nw, w_in, *rope_a, rope=True, tm=tm_in)
    k_c, vt_c, q_c, bb_c, u_c, z_c = _even_in(ctx, sh_c, sc_c, nw, w_in, dummy_tab, dummy_tab,
                                              rope=False, tm=tm_ctx)
    a = _even_attn(e_sink[0], q, z, k, vt, k_c, vt_c, window=True)
    a_c = _even_attn(e_sink[0], q_c, z_c, None, None, k_c, vt_c, window=False)
    x1 = _out_proj((a, bb, u), z, e_conv_w[0], w_out, pw, g_x, x, tm=tm_in)
    ctx1 = _out_proj((a_c, bb_c, u_c), z_c, e_conv_w[0], w_out, pw, g_c, ctx, tm=tm_ctx)

    sh_x, sc_x, g_x = split(1, 0, bsz)
    sh_c, sc_c, _ = split(1, bsz, bsz + 1)
    w_p, wk, wv, wmvt, wq = _odd_weights(o_w_in[0], o_w_uq[0], o_w_ukv[0])
    nw, pw = pre_norm_w[1][None, :], post_norm_w[1][None, :]
    kvn, qn = o_kv_norm_w[0][None, :], o_q_norm_w[0][None, :]
    chunk = min(MLSTM_CHUNK, n_ctx)
    kf, vt, mk, mvt, g2, qf, mq, mo, z = _odd_in(x1, sh_x, sc_x, nw, w_p, kvn, wk, wv, wmvt,
                                                 (*rope_k, qn, wq, *rope_q), tm=tm_in, chunk=chunk)
    kf_c, vt_c, mk_c, mvt_c, g2_c = _odd_in(ctx1, sh_c, sc_c, nw, w_p[:, :OC_CTX_END], kvn, wk, wv, wmvt, None,
                                            tm=tm_ctx, chunk=chunk)
    c_out = _mla(qf, z, kf_c, kf, vt_c, vt)
    gate_bias = jnp.stack([o_i_bias[0], o_f_bias[0]], axis=1).reshape(1, 4 * M_HEADS)
    bias_row = jnp.pad(gate_bias, ((0, 0), (GATE_LANE0, LANES - GATE_LANE0 - 4 * M_HEADS)))
    m_out = _mlstm(mq, mk, mvt, g2, mo, z, mk_c, mvt_c, g2_c, bias_row, o_head_norm_w[0][None, :])
    return _out_proj((c_out, m_out), None, None, o_w_out[0].astype(BF16), pw, g_x, x1, tm=tm_in)
```
